```python
import math
import jax, jax.numpy as jnp
from jax import lax
import numpy as np

D_MODEL = 1024
BATCH = 16
SEQ = 2048
DEPTH = 1

A_HEADS = 8
A_HEAD_DIM = 64
A_KV_GROUPS = 2
A_HEADS_PER_GROUP = A_HEADS // A_KV_GROUPS
A_WIDTH = A_HEADS * A_HEAD_DIM
A_KV_WIDTH = A_KV_GROUPS * A_HEAD_DIM
CMP_BLOCK = 32
CMP_STRIDE = 16
CMP_HIDDEN = 256
SLC_BLOCK = 64
SLC_TOPN = 16
WINDOW = 512
NSA_Q_BLOCK = 32
B_HEADS = 8
B_HEAD_DIM = 64
B_WIDTH = B_HEADS * B_HEAD_DIM
DECAY_LORA = 64
ICLR_LORA = 64
LNX_EPS = 64e-5
REL_BUCKETS = 32
REL_MAX_EXACT = 16
REL_MAX_DIST = 128
NORM_EPS = 1e-6
NEG_INF = -1e30
FORCE_SCORE = 1e30
NSA_IN = 2 * A_WIDTH + 6 * A_KV_WIDTH + 3 * A_HEADS
SHIFT_WIDTH = 3 * B_WIDTH + DECAY_LORA + ICLR_LORA
REST_IN = B_WIDTH + 2 * D_MODEL
IN_WIDTH = NSA_IN + SHIFT_WIDTH + REST_IN

kernel_name = 'hybrid_nsa_rwkv7_block'


def split_last(t, sizes):
    outs, off = [], 0
    for s in sizes:
        outs.append(t[..., off:off + s])
        off += s
    return outs


def rms_norm(x, gain):
    xf = x.astype(jnp.float32)
    y = xf * lax.rsqrt(jnp.mean(xf * xf, axis=-1, keepdims=True) + NORM_EPS)
    return (y * gain.astype(jnp.float32)).astype(x.dtype)


def masked_softmax(logits, mask):
    logits = jnp.where(mask, logits.astype(jnp.float32), NEG_INF)
    return jnp.where(mask, jax.nn.softmax(logits, axis=-1), 0.0)


def t5_bucket(dist):
    n = jnp.maximum(dist, 0)
    nf = jnp.maximum(n, REL_MAX_EXACT).astype(jnp.float32)
    large = REL_MAX_EXACT + (jnp.log(nf / REL_MAX_EXACT) / math.log(REL_MAX_DIST / REL_MAX_EXACT)
                             * (REL_BUCKETS - REL_MAX_EXACT)).astype(jnp.int32)
    return jnp.where(n < REL_MAX_EXACT, n, jnp.minimum(large, REL_BUCKETS - 1))


def shared_rel_bias(rel_bias, dist):
    nq, nk = dist.shape
    bias = rel_bias[t5_bucket(dist)].astype(jnp.float32)
    return bias.transpose(2, 0, 1).reshape(A_KV_GROUPS, A_HEADS_PER_GROUP, nq, nk)


def token_shift(p, mu):
    prev = jnp.pad(p, ((0, 0), (1, 0), (0, 0)))[:, :-1]
    return p + (prev - p) * mu


def compress_blocks(kv, pos_emb, w1, w2):
    b, s, g, dh = kv.shape
    n_cmp = (s - CMP_BLOCK) // CMP_STRIDE + 1
    idx = jnp.arange(n_cmp)[:, None] * CMP_STRIDE + jnp.arange(CMP_BLOCK)[None, :]
    blocks = kv[:, idx] + pos_emb[:, None, :]
    flat = blocks.transpose(0, 3, 1, 2, 4).reshape(b, g, n_cmp, CMP_BLOCK * dh)
    return jax.nn.gelu(flat @ w1) @ w2


def nsa_mixer(q, k_cmp, v_cmp, k_slc, v_slc, k_win, v_win, gate_logits, rel_bias,
              q_norm_gain, k_norm_gain, cmp_pos_k, cmp_pos_v, cmp_k_w1, cmp_k_w2, cmp_v_w1, cmp_v_w2):
    b, s, _ = q.shape
    g, hpg, dh = A_KV_GROUPS, A_HEADS_PER_GROUP, A_HEAD_DIM
    scale = dh ** -0.5
    heads_kv = lambda t: t.reshape(b, s, g, dh)
    qh = rms_norm(q.reshape(b, s, g, hpg, dh), q_norm_gain).transpose(0, 2, 3, 1, 4)
    kc = rms_norm(compress_blocks(heads_kv(k_cmp), cmp_pos_k, cmp_k_w1, cmp_k_w2), k_norm_gain[0])
    vc = compress_blocks(heads_kv(v_cmp), cmp_pos_v, cmp_v_w1, cmp_v_w2)
    n_cmp = kc.shape[2]
    cmp_end = jnp.arange(n_cmp) * CMP_STRIDE + CMP_BLOCK - 1
    n_slc = s // SLC_BLOCK
    top_n = min(SLC_TOPN, n_slc)
    ks = rms_norm(heads_kv(k_slc), k_norm_gain[1]).transpose(0, 2, 1, 3).reshape(b, g, n_slc, SLC_BLOCK, dh)
    vs = heads_kv(v_slc).transpose(0, 2, 1, 3).reshape(b, g, n_slc, SLC_BLOCK, dh)
    r1, r2 = SLC_BLOCK // CMP_STRIDE, CMP_BLOCK // CMP_STRIDE
    imp_idx = (r1 * jnp.arange(n_slc)[:, None, None] + jnp.arange(r1)[None, :, None]
               - jnp.arange(r2)[None, None, :]).reshape(n_slc, r1 * r2)
    imp_valid = (imp_idx >= 0) & (imp_idx < n_cmp)
    imp_idx = jnp.clip(imp_idx, 0, n_cmp - 1)
    blk = jnp.arange(n_slc)
    pad = ((0, 0), (0, 0), (WINDOW, 0), (0, 0))
    kw = jnp.pad(rms_norm(heads_kv(k_win), k_norm_gain[2]).transpose(0, 2, 1, 3), pad)
    vw = jnp.pad(heads_kv(v_win).transpose(0, 2, 1, 3), pad)
    gates = jax.nn.sigmoid(gate_logits.astype(jnp.float32)).reshape(b, s, 3, g, hpg).transpose(2, 0, 3, 4, 1)
    tbl = rel_bias.reshape(REL_BUCKETS, g, hpg).astype(jnp.float32)
    b_ix = jnp.arange(b)[:, None, None, None]
    g_ix = jnp.arange(g)[None, :, None, None]

    def query_block(i):
        q0 = i * NSA_Q_BLOCK
        t = q0 + jnp.arange(NSA_Q_BLOCK)
        qb = lax.dynamic_slice_in_dim(qh, q0, NSA_Q_BLOCK, axis=3)
        gb = lax.dynamic_slice_in_dim(gates, q0, NSA_Q_BLOCK, axis=4)[..., None].astype(q.dtype)
        dist_c = t[:, None] - cmp_end[None, :]
        s_c = jnp.einsum('bghqd,bgnd->bghqn', qb, kc).astype(jnp.float32) * scale + shared_rel_bias(rel_bias, dist_c)
        p_c = masked_softmax(s_c, dist_c >= 0)
        o_c = jnp.einsum('bghqn,bgnd->bghqd', p_c.astype(vc.dtype), vc)
        p_grp = p_c.sum(axis=2)
        imp = jnp.sum(jnp.where(imp_valid, p_grp[..., imp_idx], 0.0), axis=-1)
        cur = t[:, None] // SLC_BLOCK
        forced = (blk[None, :] == 0) | (blk[None, :] == cur) | (blk[None, :] == cur - 1)
        causal = blk[None, :] * SLC_BLOCK <= t[:, None]
        imp = jnp.where(forced, FORCE_SCORE, jnp.where(causal, imp, NEG_INF))
        _, sel = lax.top_k(imp, top_n)
        k_sel = ks[b_ix, g_ix, sel].reshape(b, g, NSA_Q_BLOCK, top_n * SLC_BLOCK, dh)
        v_sel = vs[b_ix, g_ix, sel].reshape(b, g, NSA_Q_BLOCK, top_n * SLC_BLOCK, dh)
        key_pos = (sel[..., None] * SLC_BLOCK + jnp.arange(SLC_BLOCK)).reshape(b, g, NSA_Q_BLOCK, top_n * SLC_BLOCK)
        dist_s = t[:, None] - key_pos
        bias_s = tbl[t5_bucket(dist_s)[:, :, None], jnp.arange(g)[:, None, None, None], jnp.arange(hpg)[:, None, None]]
        s_s = jnp.einsum('bghqd,bgqkd->bghqk', qb, k_sel).astype(jnp.float32) * scale + bias_s
        p_s = masked_softmax(s_s, (dist_s >= 0)[:, :, None])
        o_s = jnp.einsum('bghqk,bgqkd->bghqd', p_s.astype(v_sel.dtype), v_sel)
        kwb = lax.dynamic_slice_in_dim(kw, q0, WINDOW + NSA_Q_BLOCK, axis=2)
        vwb = lax.dynamic_slice_in_dim(vw, q0, WINDOW + NSA_Q_BLOCK, axis=2)
        kpos = q0 - WINDOW + jnp.arange(WINDOW + NSA_Q_BLOCK)
        dist_w = t[:, None] - kpos[None, :]
        mask_w = (dist_w >= 0) & (dist_w < WINDOW) & (kpos[None, :] >= 0)
        s_w = jnp.einsum('bghqd,bgkd->bghqk', qb, kwb).astype(jnp.float32) * scale + shared_rel_bias(rel_bias, dist_w)
        p_w = masked_softmax(s_w, mask_w)
        o_w = jnp.einsum('bghqk,bgkd->bghqd', p_w.astype(vwb.dtype), vwb)
        return gb[0] * o_c + gb[1] * o_s + gb[2] * o_w

    out = lax.map(query_block, jnp.arange(s // NSA_Q_BLOCK))
    return out.transpose(1, 0, 4, 2, 3, 5).reshape(b, s, A_WIDTH)


def rwkv7_mixer(r, k, v, wd, ad, w0, w_lora_up, a0, a_lora_up, k_k, k_a, r_k, ln_x_w, ln_x_b):
    b, s, _ = r.shape
    h, n = B_HEADS, B_HEAD_DIM
    f32 = jnp.float32
    heads = lambda t: t.astype(f32).reshape(b, s, h, n)
    tm = lambda t: t.transpose(1, 0, 2, 3)
    log_w = -jax.nn.softplus(-(w0 + jnp.tanh(wd) @ w_lora_up).astype(f32)) - 0.5
    decay = jnp.exp(-jnp.exp(log_w))
    a = jax.nn.sigmoid((a0 + ad @ a_lora_up).astype(f32))
    kk = heads(k * k_k)
    kk = kk * lax.rsqrt(jnp.maximum(jnp.sum(kk * kk, axis=-1, keepdims=True), 1e-24))
    k_mod = heads(k.astype(f32) * (1.0 + (a - 1.0) * k_a.astype(f32)))
    rh, vh, ah = heads(r), heads(v), heads(a)

    def step(state, inp):
        r_t, w_t, k_t, v_t, a_t, b_t = inp
        sa = jnp.einsum('bhvk,bhk->bhv', state, a_t)
        state = state * w_t[:, :, None, :] + sa[..., None] * b_t[:, :, None, :] + v_t[..., None] * k_t[:, :, None, :]
        return state, jnp.einsum('bhvk,bhk->bhv', state, r_t)

    state0 = jnp.zeros((b, h, n, n), f32)
    _, y = lax.scan(step, state0, (tm(rh), tm(heads(decay)), tm(k_mod), tm(vh), tm(-kk), tm(kk * ah)))
    y = tm(y)
    mean = jnp.mean(y, axis=-1, keepdims=True)
    var = jnp.mean(jnp.square(y - mean), axis=-1, keepdims=True)
    y = ((y - mean) * lax.rsqrt(var + LNX_EPS)).reshape(b, s, B_WIDTH) * ln_x_w + ln_x_b
    bonus = jnp.sum(rh * k_mod * r_k.astype(f32), axis=-1, keepdims=True) * vh
    return (y + bonus.reshape(b, s, B_WIDTH)).astype(r.dtype)


def hybrid_layer(x, c, rel_bias, w_ada, b_ada, norm_gain, w_in, q_norm_gain, k_norm_gain,
                 cmp_pos_k, cmp_pos_v, cmp_k_w1, cmp_k_w2, cmp_v_w1, cmp_v_w2,
                 shift_mu, w0, w_lora_up, a0, a_lora_up, k_k, k_a, r_k, ln_x_w, ln_x_b,
                 w_out_a, w_out_b, w_o):
    shift, scale, gate = jnp.split(jax.nn.silu(c) @ w_ada + b_ada, 3, axis=-1)
    h = rms_norm(x, norm_gain) * (1.0 + scale[:, None, :]) + shift[:, None, :]
    cols = h @ w_in
    cols_a, cols_shift, cols_rest = split_last(cols, (NSA_IN, SHIFT_WIDTH, REST_IN))
    q, k_cmp, v_cmp, k_slc, v_slc, k_win, v_win, a_gate_logits, a_silu = split_last(
        cols_a, (A_WIDTH,) + (A_KV_WIDTH,) * 6 + (3 * A_HEADS, A_WIDTH))
    r, k, v, wd, ad = split_last(token_shift(cols_shift, shift_mu), (B_WIDTH,) * 3 + (DECAY_LORA, ICLR_LORA))
    b_silu, merge_a, merge_b = split_last(cols_rest, (B_WIDTH, D_MODEL, D_MODEL))
    y_a = nsa_mixer(q, k_cmp, v_cmp, k_slc, v_slc, k_win, v_win, a_gate_logits, rel_bias,
                    q_norm_gain, k_norm_gain, cmp_pos_k, cmp_pos_v, cmp_k_w1, cmp_k_w2, cmp_v_w1, cmp_v_w2)
    y_a = y_a * jax.nn.silu(a_silu)
    y_b = rwkv7_mixer(r, k, v, wd, ad, w0, w_lora_up, a0, a_lora_up, k_k, k_a, r_k, ln_x_w, ln_x_b)
    y_b = y_b * jax.nn.silu(b_silu)
    merged = jax.nn.sigmoid(merge_a) * (y_a @ w_out_a) + jax.nn.sigmoid(merge_b) * (y_b @ w_out_b)
    return x + gate[:, None, :] * (merged @ w_o)


def setup_inputs(seed: int = 0) -> dict:
    key = jax.random.key(seed)
    ks = jax.random.split(key, 32)
    nrm = lambda k, shape, s: jax.random.normal(k, shape, jnp.float32) * s
    L = DEPTH
    fan_cmp = CMP_BLOCK * A_HEAD_DIM
    return {
        'x': nrm(ks[0], (BATCH, SEQ, D_MODEL), 1.0),
        'c': nrm(ks[1], (BATCH, D_MODEL), 1.0),
        'w_ada': nrm(ks[2], (L, D_MODEL, 3 * D_MODEL), 0.2 * D_MODEL ** -0.5),
        'b_ada': nrm(ks[3], (L, 3 * D_MODEL), 0.01),
        'norm_gain': 1.0 + nrm(ks[4], (L, D_MODEL), 0.02),
        'w_in': nrm(ks[5], (L, D_MODEL, IN_WIDTH), D_MODEL ** -0.5),
        'q_norm_gain': 1.0 + nrm(ks[6], (L, A_HEAD_DIM), 0.02),
        'k_norm_gain': 1.0 + nrm(ks[7], (L, 3, A_HEAD_DIM), 0.02),
        'cmp_pos_k': nrm(ks[8], (L, CMP_BLOCK, A_HEAD_DIM), 0.1),
        'cmp_pos_v': nrm(ks[9], (L, CMP_BLOCK, A_HEAD_DIM), 0.1),
        'cmp_k_w1': nrm(ks[10], (L, fan_cmp, CMP_HIDDEN), fan_cmp ** -0.5),
        'cmp_k_w2': nrm(ks[11], (L, CMP_HIDDEN, A_HEAD_DIM), CMP_HIDDEN ** -0.5),
        'cmp_v_w1': nrm(ks[12], (L, fan_cmp, CMP_HIDDEN), fan_cmp ** -0.5),
        'cmp_v_w2': nrm(ks[13], (L, CMP_HIDDEN, A_HEAD_DIM), CMP_HIDDEN ** -0.5),
        'rel_bias': nrm(ks[14], (REL_BUCKETS, A_HEADS), 0.5),
        'shift_mu': jax.random.uniform(ks[15], (L, SHIFT_WIDTH), jnp.float32),
        'w0': (-6.0 + 5.0 * jnp.linspace(0.0, 1.0, B_WIDTH))[None, :] + nrm(ks[16], (L, B_WIDTH), 0.1),
        'w_lora_up': nrm(ks[17], (L, DECAY_LORA, B_WIDTH), 0.5 * DECAY_LORA ** -0.5),
        'a0': nrm(ks[18], (L, B_WIDTH), 0.1),
        'a_lora_up': nrm(ks[19], (L, ICLR_LORA, B_WIDTH), ICLR_LORA ** -0.5),
        'k_k': 0.85 + nrm(ks[20], (L, B_WIDTH), 0.02),
        'k_a': 1.0 + nrm(ks[21], (L, B_WIDTH), 0.02),
        'r_k': nrm(ks[22], (L, B_HEADS, B_HEAD_DIM), 0.1),
        'ln_x_w': 1.0 + nrm(ks[23], (L, B_WIDTH), 0.02),
        'ln_x_b': nrm(ks[24], (L, B_WIDTH), 0.01),
        'w_out_a': nrm(ks[25], (L, A_WIDTH, D_MODEL), A_WIDTH ** -0.5),
        'w_out_b': nrm(ks[26], (L, B_WIDTH, D_MODEL), B_WIDTH ** -0.5),
        'w_o': nrm(ks[27], (L, D_MODEL, D_MODEL), D_MODEL ** -0.5),
    }


def reference(x, c, w_ada, b_ada, norm_gain, w_in, q_norm_gain, k_norm_gain, cmp_pos_k, cmp_pos_v,
              cmp_k_w1, cmp_k_w2, cmp_v_w1, cmp_v_w2, rel_bias, shift_mu, w0, w_lora_up, a0, a_lora_up,
              k_k, k_a, r_k, ln_x_w, ln_x_b, w_out_a, w_out_b, w_o):
    for l in range(DEPTH):
        x = hybrid_layer(x, c, rel_bias, w_ada[l], b_ada[l], norm_gain[l], w_in[l], q_norm_gain[l], k_norm_gain[l],
                         cmp_pos_k[l], cmp_pos_v[l], cmp_k_w1[l], cmp_k_w2[l], cmp_v_w1[l], cmp_v_w2[l],
                         shift_mu[l], w0[l], w_lora_up[l], a0[l], a_lora_up[l], k_k[l], k_a[l], r_k[l],
                         ln_x_w[l], ln_x_b[l], w_out_a[l], w_out_b[l], w_o[l])
    return x
```

```python
import functools
import math

import numpy as np
import jax
import jax.numpy as jnp
from jax import lax
from jax.experimental import pallas as pl
from jax.experimental.pallas import tpu as pltpu

F32 = jnp.float32
BF16 = jnp.bfloat16

D_MODEL = 1024
A_HEADS = 8
A_HEAD_DIM = 64
A_KV_GROUPS = 2
A_HPG = A_HEADS // A_KV_GROUPS
A_WIDTH = A_HEADS * A_HEAD_DIM
A_KV_WIDTH = A_KV_GROUPS * A_HEAD_DIM
CMP_BLOCK = 32
CMP_STRIDE = 16
CMP_HIDDEN = 256
SLC_BLOCK = 64
SLC_TOPN = 16
WINDOW = 512
B_HEADS = 8
B_HEAD_DIM = 64
B_WIDTH = B_HEADS * B_HEAD_DIM
DECAY_LORA = 64
ICLR_LORA = 64
LNX_EPS = 64e-5
REL_BUCKETS = 32
REL_MAX_EXACT = 16
REL_MAX_DIST = 128
NORM_EPS = 1e-6
NEG_INF = -1e30
FORCE_SCORE = 1e30

LANES = 128
TQ = 128
CHUNK = 64
GATE_PAD = 2 * LANES
NSA_COLS = A_WIDTH + 6 * A_KV_WIDTH + GATE_PAD
FIN_COLS = A_WIDTH + B_WIDTH + 2 * D_MODEL
RWKV_COLS = 3 * B_WIDTH + DECAY_LORA + ICLR_LORA
VMEM_LIMIT = 56 * 1024 * 1024


def _dot(a, b):
    return jnp.dot(a.astype(BF16), b.astype(BF16), preferred_element_type=F32)


def _dot_nt(a, b):
    return lax.dot_general(a.astype(BF16), b.astype(BF16), (((1,), (1,)), ((), ())),
                           preferred_element_type=F32)


def _dot_tn(a, b):
    return lax.dot_general(a.astype(BF16), b.astype(BF16), (((0,), (0,)), ((), ())),
                           preferred_element_type=F32)


def _split2(x):
    hi = x.astype(BF16)
    lo = (x - hi.astype(F32)).astype(BF16)
    return hi, lo


def _split3(x):
    h1 = x.astype(BF16)
    r1 = x - h1.astype(F32)
    h2 = r1.astype(BF16)
    h3 = (r1 - h2.astype(F32)).astype(BF16)
    return h1, h2, h3


def _sigmoid(x):
    return 1.0 / (1.0 + jnp.exp(-x))


def _bucket_thresholds():
    n = np.arange(0, 4096)
    nf = np.maximum(n, REL_MAX_EXACT).astype(np.float64)
    val = np.log(nf / REL_MAX_EXACT) / math.log(REL_MAX_DIST / REL_MAX_EXACT) * (REL_BUCKETS - REL_MAX_EXACT)
    frac = np.abs(val - np.round(val))
    assert np.all((frac > 1e-4) | (n <= REL_MAX_EXACT) | (n >= REL_MAX_DIST))
    large = REL_MAX_EXACT + np.floor(val + 1e-9).astype(np.int64)
    bucket = np.where(n < REL_MAX_EXACT, n, np.minimum(large, REL_BUCKETS - 1))
    return [int(np.argmax(bucket >= j)) for j in range(REL_BUCKETS)]


_BUCKET_TH = _bucket_thresholds()


def _bias_from_dist(dist, tbl_ref, head):
    val = jnp.full(dist.shape, tbl_ref[0, head], F32)
    for j in range(1, REL_BUCKETS):
        val = jnp.where(dist >= _BUCKET_TH[j], tbl_ref[j, head], val)
    return val


def _ada_kernel(c_ref, w_ref, b_ref, o_ref):
    c = c_ref[...]
    o_ref[...] = _dot(c * _sigmoid(c), w_ref[...]) + b_ref[...]


def _ada(c, w_ada, b_ada):
    bsz = c.shape[0]
    return pl.pallas_call(
        _ada_kernel,
        grid=(3,),
        in_specs=[pl.BlockSpec((bsz, D_MODEL), lambda j: (0, 0)),
                  pl.BlockSpec((D_MODEL, D_MODEL), lambda j: (0, j)),
                  pl.BlockSpec((1, D_MODEL), lambda j: (0, j))],
        out_specs=pl.BlockSpec((bsz, D_MODEL), lambda j: (0, j)),
        out_shape=jax.ShapeDtypeStruct((bsz, 3 * D_MODEL), F32),
        name="ada",
    )(c, w_ada, b_ada.reshape(1, 3 * D_MODEL))


def _proj_kernel(x_ref, mod_ref, g_ref, wn_ref, wf_ref, wr_ref, on_ref, of_ref, or_ref):
    x = x_ref[0]
    ms = jnp.mean(x * x, axis=-1, keepdims=True)
    y = x * lax.rsqrt(ms + NORM_EPS) * g_ref[...]
    mod = mod_ref[0]
    h = (y * (1.0 + mod[:, D_MODEL:2 * D_MODEL]) + mod[:, :D_MODEL]).astype(BF16)
    on_ref[0] = jnp.dot(h, wn_ref[...], preferred_element_type=F32)
    of_ref[0] = jnp.dot(h, wf_ref[...], preferred_element_type=F32)
    or_ref[0] = jnp.dot(h, wr_ref[...], preferred_element_type=F32)


def _proj(x, mod, norm_gain, w_nsa, w_fin, w_rwkv, tm=256):
    bsz, s, _ = x.shape
    const = lambda b, i: (0, 0)
    return pl.pallas_call(
        _proj_kernel,
        grid=(bsz, s // tm),
        in_specs=[pl.BlockSpec((1, tm, D_MODEL), lambda b, i: (b, i, 0)),
                  pl.BlockSpec((1, 1, 3 * D_MODEL), lambda b, i: (b, 0, 0)),
                  pl.BlockSpec((1, D_MODEL), const),
                  pl.BlockSpec((D_MODEL, NSA_COLS), const),
                  pl.BlockSpec((D_MODEL, FIN_COLS), const),
                  pl.BlockSpec((D_MODEL, RWKV_COLS), const)],
        out_specs=[pl.BlockSpec((1, tm, NSA_COLS), lambda b, i: (b, i, 0)),
                   pl.BlockSpec((1, tm, FIN_COLS), lambda b, i: (b, i, 0)),
                   pl.BlockSpec((1, tm, RWKV_COLS), lambda b, i: (b, i, 0))],
        out_shape=[jax.ShapeDtypeStruct((bsz, s, NSA_COLS), F32),
                   jax.ShapeDtypeStruct((bsz, s, FIN_COLS), F32),
                   jax.ShapeDtypeStruct((bsz, s, RWKV_COLS), F32)],
        compiler_params=pltpu.CompilerParams(dimension_semantics=("parallel", "parallel"),
                                             vmem_limit_bytes=VMEM_LIMIT),
        name="proj",
    )(x, mod.reshape(bsz, 1, 3 * D_MODEL), norm_gain.reshape(1, D_MODEL), w_nsa, w_fin, w_rwkv)


def _seg_mean(x2, bd):
    hi, lo = _split2(x2)
    return (jnp.dot(hi, bd, preferred_element_type=F32) + jnp.dot(lo, bd, preferred_element_type=F32))


def _nsaprep_kernel(c_ref, qg_ref, ksg_ref, kwg_ref, bd_ref, q_ref, ks_ref, vs_ref, kw_ref, vw_ref):
    tq = c_ref.shape[1]
    bd = bd_ref[...]
    q = c_ref[0, :, 0:A_WIDTH]
    qn = (q * lax.rsqrt(_seg_mean(q * q, bd) + NORM_EPS) * qg_ref[...]).astype(BF16)
    off = A_WIDTH + 2 * A_KV_WIDTH
    k_slc = c_ref[0, :, off:off + A_KV_WIDTH]
    v_slc = c_ref[0, :, off + A_KV_WIDTH:off + 2 * A_KV_WIDTH]
    k_win = c_ref[0, :, off + 2 * A_KV_WIDTH:off + 3 * A_KV_WIDTH]
    v_win = c_ref[0, :, off + 3 * A_KV_WIDTH:off + 4 * A_KV_WIDTH]
    bd_kv = bd[:A_KV_WIDTH, :A_KV_WIDTH]
    ksn = (k_slc * lax.rsqrt(_seg_mean(k_slc * k_slc, bd_kv) + NORM_EPS) * ksg_ref[...]).astype(BF16)
    kwn = (k_win * lax.rsqrt(_seg_mean(k_win * k_win, bd_kv) + NORM_EPS) * kwg_ref[...]).astype(BF16)
    vsb = v_slc.astype(BF16)
    vwb = v_win.astype(BF16)
    for g in range(A_KV_GROUPS):
        for h in range(A_HPG):
            c0 = (g * A_HPG + h) * A_HEAD_DIM
            q_ref[0, g, 0, h * tq:(h + 1) * tq, :] = qn[:, c0:c0 + A_HEAD_DIM]
        sl = slice(g * A_HEAD_DIM, (g + 1) * A_HEAD_DIM)
        ks_ref[0, g] = ksn[:, sl]
        vs_ref[0, g] = vsb[:, sl]
        kw_ref[0, g] = kwn[:, sl]
        vw_ref[0, g] = vwb[:, sl]


def _nsaprep(cols_nsa, qg, ksg, kwg, bd):
    bsz, s, _ = cols_nsa.shape
    nt = s // TQ
    const = lambda b, i: (0, 0)
    kv_spec = pl.BlockSpec((1, A_KV_GROUPS, TQ, A_HEAD_DIM), lambda b, i: (b, 0, i, 0))
    kv_shape = jax.ShapeDtypeStruct((bsz, A_KV_GROUPS, s, A_HEAD_DIM), BF16)
    return pl.pallas_call(
        _nsaprep_kernel,
        grid=(bsz, nt),
        in_specs=[pl.BlockSpec((1, TQ, NSA_COLS), lambda b, i: (b, i, 0)),
                  pl.BlockSpec((1, A_WIDTH), const),
                  pl.BlockSpec((1, A_KV_WIDTH), const),
                  pl.BlockSpec((1, A_KV_WIDTH), const),
                  pl.BlockSpec((A_WIDTH, A_WIDTH), const)],
        out_specs=[pl.BlockSpec((1, A_KV_GROUPS, 1, A_HPG * TQ, A_HEAD_DIM), lambda b, i: (b, 0, i, 0, 0)),
                   kv_spec, kv_spec, kv_spec, kv_spec],
        out_shape=[jax.ShapeDtypeStruct((bsz, A_KV_GROUPS, nt, A_HPG * TQ, A_HEAD_DIM), BF16),
                   kv_shape, kv_shape, kv_shape, kv_shape],
        compiler_params=pltpu.CompilerParams(dimension_semantics=("parallel", "parallel")),
        name="nsaprep",
    )(cols_nsa, qg, ksg, kwg, bd)


def _compress_kernel(zk_ref, zv_ref, pk_ref, pv_ref, w1k_ref, w2k_ref, w1v_ref, w2v_ref, kg_ref, kc_ref, vc_ref):
    half = (CMP_BLOCK // 2) * A_HEAD_DIM
    n16 = zk_ref.shape[2]

    def mlp(z, pos_ref, w1_ref, w2_ref):
        top = _dot(z + pos_ref[0:1, :], w1_ref[0:half, :])
        bot = _dot(z + pos_ref[1:2, :], w1_ref[half:2 * half, :])
        hid = top + pltpu.roll(bot, n16 - 1, axis=0)
        return _dot(jax.nn.gelu(hid, approximate=True), w2_ref[...])

    for g in range(A_KV_GROUPS):
        kc = mlp(zk_ref[0, g], pk_ref, w1k_ref, w2k_ref)
        ms = jnp.mean(kc * kc, axis=-1, keepdims=True)
        kc_ref[0, g] = (kc * lax.rsqrt(ms + NORM_EPS) * kg_ref[...]).astype(BF16)
        vc_ref[0, g] = mlp(zv_ref[0, g], pv_ref, w1v_ref, w2v_ref).astype(BF16)


def _compress(zk, zv, pk, pv, w1k, w2k, w1v, w2v, kg):
    bsz, _, n16, zw = zk.shape
    const = lambda b: (0, 0)
    z_spec = pl.BlockSpec((1, A_KV_GROUPS, n16, zw), lambda b: (b, 0, 0, 0))
    o_spec = pl.BlockSpec((1, A_KV_GROUPS, n16, A_HEAD_DIM), lambda b: (b, 0, 0, 0))
    o_shape = jax.ShapeDtypeStruct((bsz, A_KV_GROUPS, n16, A_HEAD_DIM), BF16)
    return pl.pallas_call(
        _compress_kernel,
        grid=(bsz,),
        in_specs=[z_spec, z_spec,
                  pl.BlockSpec((2, zw), const), pl.BlockSpec((2, zw), const),
                  pl.BlockSpec((2 * zw, CMP_HIDDEN), const), pl.BlockSpec((CMP_HIDDEN, A_HEAD_DIM), const),
                  pl.BlockSpec((2 * zw, CMP_HIDDEN), const), pl.BlockSpec((CMP_HIDDEN, A_HEAD_DIM), const),
                  pl.BlockSpec((1, A_HEAD_DIM), const)],
        out_specs=[o_spec, o_spec],
        out_shape=[o_shape, o_shape],
        compiler_params=pltpu.CompilerParams(dimension_semantics=("parallel",)),
        name="compress",
    )(zk, zv, pk, pv, w1k, w2k, w1v, w2v, kg)


def _bias_cmp_kernel(tbl_ref, o_ref):
    head = pl.program_id(0)
    rows, n_cmp = o_ref.shape[1], o_ref.shape[2]
    t = pl.program_id(1) * rows + lax.broadcasted_iota(jnp.int32, (rows, n_cmp), 0)
    n = lax.broadcasted_iota(jnp.int32, (rows, n_cmp), 1)
    dist = t - (n * CMP_STRIDE + CMP_BLOCK - 1)
    o_ref[0] = _bias_from_dist(dist, tbl_ref, head)


def _bias_toeplitz_kernel(tbl_ref, o_ref):
    g = pl.program_id(0)
    r = pl.program_id(1)
    q = lax.broadcasted_iota(jnp.int32, (TQ, TQ), 0)
    k = lax.broadcasted_iota(jnp.int32, (TQ, TQ), 1)
    dist = r * TQ + q - k
    for h in range(A_HPG):
        o_ref[0, 0, h * TQ:(h + 1) * TQ, :] = _bias_from_dist(dist, tbl_ref, g * A_HPG + h)


def _bias_tables(rel_bias, s, n_cmp):
    rows = 256
    smem = pl.BlockSpec(memory_space=pltpu.SMEM)
    bias_c = pl.pallas_call(
        _bias_cmp_kernel,
        grid=(A_HEADS, s // rows),
        in_specs=[smem],
        out_specs=pl.BlockSpec((1, rows, n_cmp), lambda h, i: (h, i, 0)),
        out_shape=jax.ShapeDtypeStruct((A_HEADS, s, n_cmp), F32),
        name="bias_cmp",
    )(rel_bias)
    assert _BUCKET_TH[REL_BUCKETS - 1] <= TQ + 1
    bias_d = pl.pallas_call(
        _bias_toeplitz_kernel,
        grid=(A_KV_GROUPS, 3),
        in_specs=[smem],
        out_specs=pl.BlockSpec((1, 1, A_HPG * TQ, TQ), lambda g, r: (g, r, 0, 0)),
        out_shape=jax.ShapeDtypeStruct((A_KV_GROUPS, 3, A_HPG * TQ, TQ), F32),
        name="bias_toeplitz",
    )(rel_bias)
    return bias_c, bias_d


def _attn_kernel(q_ref, kc_ref, vc_ref, ks_ref, vs_ref, kw_ref, vw_ref, bc_ref, bd_ref, gl_ref, o_ref,
                 m_ref, l_ref, acc_ref):
    i = pl.program_id(2)
    tq = TQ
    n_cmp = kc_ref.shape[2]
    n_slc = ks_ref.shape[2] // SLC_BLOCK
    t0 = i * tq
    gates = _sigmoid(gl_ref[0])

    qrow = t0 + lax.broadcasted_iota(jnp.int32, (tq, n_cmp), 0)
    ncol = lax.broadcasted_iota(jnp.int32, (tq, n_cmp), 1)
    valid_c = (ncol * CMP_STRIDE + CMP_BLOCK - 1) <= qrow
    kc = kc_ref[0, 0]
    vc = vc_ref[0, 0]
    p_grp = jnp.zeros((tq, n_cmp), F32)
    out_heads = []
    for h in range(A_HPG):
        qh = q_ref[0, 0, 0, h * tq:(h + 1) * tq, :]
        s = _dot_nt(qh, kc) + bc_ref[h]
        s = jnp.where(valid_c, s, NEG_INF)
        m = jnp.max(s, axis=-1, keepdims=True)
        e = jnp.where(valid_c, jnp.exp(s - m), 0.0)
        l = jnp.sum(e, axis=-1, keepdims=True)
        p = e / jnp.where(l > 0.0, l, 1.0)
        p_grp = p_grp + p
        out_heads.append(gates[:, h:h + 1] * _dot(p, vc))

    r1, r2 = SLC_BLOCK // CMP_STRIDE, CMP_BLOCK // CMP_STRIDE
    nn = lax.broadcasted_iota(jnp.int32, (n_cmp, n_slc), 0)
    jj = lax.broadcasted_iota(jnp.int32, (n_cmp, n_slc), 1)
    d = nn - r1 * jj
    cnt = jnp.zeros((n_cmp, n_slc), F32)
    for a in range(r1):
        for c in range(r2):
            cnt = cnt + jnp.where(d == a - c, 1.0, 0.0)
    cnt = cnt.astype(BF16)
    imp = sum(jnp.dot(part, cnt, preferred_element_type=F32) for part in _split3(p_grp))
    tpos = t0 + lax.broadcasted_iota(jnp.int32, (tq, n_slc), 0)
    blk = lax.broadcasted_iota(jnp.int32, (tq, n_slc), 1)
    cur = tpos // SLC_BLOCK
    forced = (blk == 0) | (blk == cur) | (blk == cur - 1)
    causal = blk * SLC_BLOCK <= tpos
    imp = jnp.where(forced, FORCE_SCORE, jnp.where(causal, imp, NEG_INF))
    rank = jnp.zeros((tq, n_slc), F32)
    for c in range(n_slc):
        col = imp[:, c:c + 1]
        ahead = (col > imp) | ((col == imp) & (blk > c))
        rank = rank + jnp.where(ahead, 1.0, 0.0)
    sel = jnp.where(rank < float(min(SLC_TOPN, n_slc)), 1.0, 0.0).astype(BF16)

    qloc = lax.broadcasted_iota(jnp.int32, (tq, tq), 0)
    kloc = lax.broadcasted_iota(jnp.int32, (tq, tq), 1)

    def flash(k_ref, v_ref, lo, hi, mask_fn):
        m_ref[...] = jnp.full(m_ref.shape, NEG_INF, F32)
        l_ref[...] = jnp.zeros(l_ref.shape, F32)
        acc_ref[...] = jnp.zeros(acc_ref.shape, F32)

        def body(kt, carry):
            k0 = pl.multiple_of(kt * tq, tq)
            k = k_ref[0, 0, pl.ds(k0, tq), :]
            v = v_ref[0, 0, pl.ds(k0, tq), :]
            r = i - kt
            mask = mask_fn(kt, r)
            rb = jnp.minimum(r, 2)
            for h in range(A_HPG):
                rows = slice(h * tq, (h + 1) * tq)
                qh = q_ref[0, 0, 0, rows, :]
                s = _dot_nt(qh, k) + bd_ref[0, rb, rows, :]
                s = jnp.where(mask, s, NEG_INF)
                m_old = m_ref[rows, :]
                m_new = jnp.maximum(m_old, jnp.max(s, axis=-1, keepdims=True))
                alpha = jnp.exp(m_old - m_new)
                p = jnp.where(mask, jnp.exp(s - m_new), 0.0)
                l_ref[rows, :] = alpha * l_ref[rows, :] + jnp.sum(p, axis=-1, keepdims=True)
                acc_ref[rows, :] = alpha * acc_ref[rows, :] + _dot(p, v)
                m_ref[rows, :] = m_new
            return carry

        lax.fori_loop(lo, hi, body, 0)

    def mask_sel(kt, r):
        kb = (kt * tq + lax.broadcasted_iota(jnp.int32, (n_slc, tq), 1)) // SLC_BLOCK
        expand = jnp.where(kb == lax.broadcasted_iota(jnp.int32, (n_slc, tq), 0), 1.0, 0.0).astype(BF16)
        chosen = jnp.dot(sel, expand, preferred_element_type=F32) > 0.5
        return chosen & (r * tq + qloc - kloc >= 0)

    flash(ks_ref, vs_ref, 0, i + 1, mask_sel)
    for h in range(A_HPG):
        rows = slice(h * tq, (h + 1) * tq)
        out_heads[h] = out_heads[h] + gates[:, A_HPG + h:A_HPG + h + 1] * (acc_ref[rows, :] / l_ref[rows, :])

    wt = WINDOW // tq

    def mask_win(kt, r):
        dist = r * tq + qloc - kloc
        return (dist >= 0) & (dist < WINDOW)

    flash(kw_ref, vw_ref, jnp.maximum(i - wt, 0), i + 1, mask_win)
    for h in range(A_HPG):
        rows = slice(h * tq, (h + 1) * tq)
        out_heads[h] = out_heads[h] + gates[:, 2 * A_HPG + h:2 * A_HPG + h + 1] * (acc_ref[rows, :] / l_ref[rows, :])

    o_ref[0] = jnp.concatenate(out_heads, axis=-1)


def _attention(q_hs, kc, vc, ks, vs, kw, vw, bias_c, bias_d, cols_nsa):
    bsz, _, nt, _, _ = q_hs.shape
    s = ks.shape[2]
    n_cmp = kc.shape[2]
    gate_blk0 = (A_WIDTH + 6 * A_KV_WIDTH) // LANES
    full_kv = pl.BlockSpec((1, 1, s, A_HEAD_DIM), lambda b, g, i: (b, g, 0, 0))
    cmp_kv = pl.BlockSpec((1, 1, n_cmp, A_HEAD_DIM), lambda b, g, i: (b, g, 0, 0))
    return pl.pallas_call(
        _attn_kernel,
        grid=(bsz, A_KV_GROUPS, nt),
        in_specs=[pl.BlockSpec((1, 1, 1, A_HPG * TQ, A_HEAD_DIM), lambda b, g, i: (b, g, i, 0, 0)),
                  cmp_kv, cmp_kv, full_kv, full_kv, full_kv, full_kv,
                  pl.BlockSpec((A_HPG, TQ, n_cmp), lambda b, g, i: (g, i, 0)),
                  pl.BlockSpec((1, 3, A_HPG * TQ, TQ), lambda b, g, i: (g, 0, 0, 0)),
                  pl.BlockSpec((1, TQ, LANES), lambda b, g, i: (b, i, gate_blk0 + g))],
        out_specs=pl.BlockSpec((1, TQ, A_HPG * A_HEAD_DIM), lambda b, g, i: (b, i, g)),
        out_shape=jax.ShapeDtypeStruct((bsz, s, A_WIDTH), F32),
        scratch_shapes=[pltpu.VMEM((A_HPG * TQ, 1), F32),
                        pltpu.VMEM((A_HPG * TQ, 1), F32),
                        pltpu.VMEM((A_HPG * TQ, A_HEAD_DIM), F32)],
        compiler_params=pltpu.CompilerParams(dimension_semantics=("parallel", "parallel", "parallel")),
        name="attn",
    )(q_hs, kc, vc, ks, vs, kw, vw, bias_c, bias_d, cols_nsa)


def _rwkv_kernel(c_ref, mu_ref, w0_ref, wl_ref, a0_ref, al_ref, kk_ref, ka_ref, rk_ref, lw_ref, lb_ref,
                 o_ref, state_ref, prev_ref):
    cc = pl.program_id(1)
    n = B_HEAD_DIM
    csz = c_ref.shape[1]

    @pl.when(cc == 0)
    def _():
        state_ref[...] = jnp.zeros(state_ref.shape, F32)
        prev_ref[...] = jnp.zeros(prev_ref.shape, F32)

    p = c_ref[0]
    row = lax.broadcasted_iota(jnp.int32, p.shape, 0)
    prev = jnp.where(row == 0, prev_ref[0:1, :], pltpu.roll(p, 1, axis=0))
    prev_ref[0:1, :] = p[csz - 1:csz, :]
    x = p + (prev - p) * mu_ref[...]
    r = x[:, 0:B_WIDTH]
    k = x[:, B_WIDTH:2 * B_WIDTH]
    v = x[:, 2 * B_WIDTH:3 * B_WIDTH]
    wd = x[:, 3 * B_WIDTH:3 * B_WIDTH + DECAY_LORA]
    ad = x[:, 3 * B_WIDTH + DECAY_LORA:3 * B_WIDTH + DECAY_LORA + ICLR_LORA]

    z = -(w0_ref[...] + _dot(jnp.tanh(wd), wl_ref[...]))
    softplus = jnp.maximum(z, 0.0) + jnp.log(1.0 + jnp.exp(-jnp.abs(z)))
    ld = -jnp.exp(-softplus - 0.5)
    a = _sigmoid(a0_ref[...] + _dot(ad, al_ref[...]))
    kk = k * kk_ref[...]
    k_mod = k * (1.0 + (a - 1.0) * ka_ref[...])
    rkr = r * k_mod * rk_ref[...]

    ti = lax.broadcasted_iota(jnp.int32, (csz, csz), 0)
    si = lax.broadcasted_iota(jnp.int32, (csz, csz), 1)
    lower = si <= ti
    strict = si < ti
    tri = jnp.where(lower, 1.0, 0.0).astype(BF16)
    ld_hi, ld_lo = _split2(ld)
    cum = jnp.dot(tri, ld_hi, preferred_element_type=F32) + jnp.dot(tri, ld_lo, preferred_element_type=F32)
    g_inc = jnp.exp(cum)
    g_exc = jnp.exp(cum - ld)
    g_inv = jnp.exp(-cum)
    g_end = jnp.exp(cum[csz - 1:csz, :] - cum)
    g_all = g_inc[csz - 1:csz, :]

    eye = jnp.where(si == ti, 1.0, 0.0)
    outs = []
    for h in range(B_HEADS):
        sl = slice(h * n, (h + 1) * n)
        kk_h = kk[:, sl]
        kk_h = kk_h * lax.rsqrt(jnp.maximum(jnp.sum(kk_h * kk_h, axis=-1, keepdims=True), 1e-24))
        b_h = kk_h * a[:, sl]
        v_h = v[:, sl]
        lhs = jnp.concatenate([-kk_h * g_exc[:, sl], r[:, sl] * g_inc[:, sl]], axis=0)
        bt = b_h * g_inv[:, sl]
        kt = k_mod[:, sl] * g_inv[:, sl]
        xb = _dot_nt(lhs, bt)
        xk = _dot_nt(lhs, kt)
        a_ab = jnp.where(strict, xb[:csz], 0.0)
        a_ak = jnp.where(strict, xk[:csz], 0.0)
        m_rb = jnp.where(lower, xb[csz:], 0.0)
        m_rk = jnp.where(lower, xk[csz:], 0.0)
        tinv = eye + a_ab
        pw = a_ab
        for _ in range(int(math.log2(csz)) - 1):
            pw = _dot(pw, pw)
            tinv = tinv + _dot(tinv, pw)
        s0 = state_ref[h]
        as0 = _dot_nt(lhs, s0)
        u = _dot(tinv, as0[:csz] + _dot(a_ak, v_h))
        y = as0[csz:] + _dot(m_rb, u) + _dot(m_rk, v_h)
        uv = jnp.concatenate([u, v_h], axis=0)
        bk = jnp.concatenate([b_h * g_end[:, sl], k_mod[:, sl] * g_end[:, sl]], axis=0)
        state_ref[h] = s0 * g_all[:, sl] + _dot_tn(uv, bk)
        mean = jnp.mean(y, axis=-1, keepdims=True)
        yc = y - mean
        var = jnp.mean(yc * yc, axis=-1, keepdims=True)
        bonus = jnp.sum(rkr[:, sl], axis=-1, keepdims=True) * v_h
        outs.append((yc * lax.rsqrt(var + LNX_EPS), bonus))
    yn = jnp.concatenate([o[0] for o in outs], axis=-1)
    bonus = jnp.concatenate([o[1] for o in outs], axis=-1)
    o_ref[0] = yn * lw_ref[...] + lb_ref[...] + bonus


def _rwkv(cols_rwkv, mu, w0, wl, a0, al, k_k, k_a, r_k, ln_w, ln_b):
    bsz, s, _ = cols_rwkv.shape
    const = lambda b, c: (0, 0)
    vec = pl.BlockSpec((1, B_WIDTH), const)
    return pl.pallas_call(
        _rwkv_kernel,
        grid=(bsz, s // CHUNK),
        in_specs=[pl.BlockSpec((1, CHUNK, RWKV_COLS), lambda b, c: (b, c, 0)),
                  pl.BlockSpec((1, RWKV_COLS), const),
                  vec, pl.BlockSpec((DECAY_LORA, B_WIDTH), const),
                  vec, pl.BlockSpec((ICLR_LORA, B_WIDTH), const),
                  vec, vec, vec, vec, vec],
        out_specs=pl.BlockSpec((1, CHUNK, B_WIDTH), lambda b, c: (b, c, 0)),
        out_shape=jax.ShapeDtypeStruct((bsz, s, B_WIDTH), F32),
        scratch_shapes=[pltpu.VMEM((B_HEADS, B_HEAD_DIM, B_HEAD_DIM), F32),
                        pltpu.VMEM((8, RWKV_COLS), F32)],
        compiler_params=pltpu.CompilerParams(dimension_semantics=("parallel", "arbitrary")),
        name="rwkv",
    )(cols_rwkv, mu, w0, wl, a0, al, k_k, k_a, r_k, ln_w, ln_b)


def _final_kernel(x_ref, ya_ref, yb_ref, cf_ref, gate_ref, wa_ref, wb_ref, wo_ref, o_ref):
    a_silu = cf_ref[0, :, 0:A_WIDTH]
    b_silu = cf_ref[0, :, A_WIDTH:A_WIDTH + B_WIDTH]
    merge_a = cf_ref[0, :, A_WIDTH + B_WIDTH:A_WIDTH + B_WIDTH + D_MODEL]
    merge_b = cf_ref[0, :, A_WIDTH + B_WIDTH + D_MODEL:A_WIDTH + B_WIDTH + 2 * D_MODEL]
    ya = ya_ref[0] * (a_silu * _sigmoid(a_silu))
    yb = yb_ref[0] * (b_silu * _sigmoid(b_silu))
    merged = _sigmoid(merge_a) * _dot(ya, wa_ref[...]) + _sigmoid(merge_b) * _dot(yb, wb_ref[...])
    o_ref[0] = x_ref[0] + gate_ref[0] * _dot(merged, wo_ref[...])


def _final(x, y_a, y_b, cols_fin, gate, w_out_a, w_out_b, w_o, tm=256):
    bsz, s, _ = x.shape
    const = lambda b, i: (0, 0)
    row = lambda w: pl.BlockSpec((1, tm, w), lambda b, i: (b, i, 0))
    return pl.pallas_call(
        _final_kernel,
        grid=(bsz, s // tm),
        in_specs=[row(D_MODEL), row(A_WIDTH), row(B_WIDTH), row(FIN_COLS),
                  pl.BlockSpec((1, 1, D_MODEL), lambda b, i: (b, 0, 0)),
                  pl.BlockSpec((A_WIDTH, D_MODEL), const),
                  pl.BlockSpec((B_WIDTH, D_MODEL), const),
                  pl.BlockSpec((D_MODEL, D_MODEL), const)],
        out_specs=row(D_MODEL),
        out_shape=jax.ShapeDtypeStruct((bsz, s, D_MODEL), F32),
        compiler_params=pltpu.CompilerParams(dimension_semantics=("parallel", "parallel"),
                                             vmem_limit_bytes=VMEM_LIMIT),
        name="final",
    )(x, y_a, y_b, cols_fin, gate, w_out_a, w_out_b, w_o)


def _split_w_in(w_in):
    nsa_in = 2 * A_WIDTH + 6 * A_KV_WIDTH + 3 * A_HEADS
    o_gate = A_WIDTH + 6 * A_KV_WIDTH
    o_asilu = o_gate + 3 * A_HEADS
    o_shift = nsa_in
    o_rest = nsa_in + RWKV_COLS
    gate_w = w_in[:, o_gate:o_asilu].reshape(D_MODEL, 3, A_KV_GROUPS, A_HPG)
    gate_w = gate_w.transpose(0, 2, 1, 3).reshape(D_MODEL, A_KV_GROUPS, 3 * A_HPG)
    gate_w = jnp.pad(gate_w, ((0, 0), (0, 0), (0, LANES - 3 * A_HPG))).reshape(D_MODEL, GATE_PAD)
    w_nsa = jnp.concatenate([w_in[:, :o_gate], gate_w], axis=1)
    w_fin = jnp.concatenate([w_in[:, o_asilu:o_shift], w_in[:, o_rest:]], axis=1)
    w_rwkv = w_in[:, o_shift:o_rest]
    return w_nsa.astype(BF16), w_fin.astype(BF16), w_rwkv.astype(BF16)


def _layer(x, c, rel_bias, w_ada, b_ada, norm_gain, w_in, q_norm_gain, k_norm_gain,
           cmp_pos_k, cmp_pos_v, cmp_k_w1, cmp_k_w2, cmp_v_w1, cmp_v_w2,
           shift_mu, w0, w_lora_up, a0, a_lora_up, k_k, k_a, r_k, ln_x_w, ln_x_b,
           w_out_a, w_out_b, w_o):
    bsz, s, _ = x.shape
    assert s % (2 * TQ) == 0 and s // CMP_STRIDE == LANES
    n16 = s // CMP_STRIDE
    mod = _ada(c, w_ada, b_ada)
    w_nsa, w_fin, w_rwkv = _split_w_in(w_in)
    cols_nsa, cols_fin, cols_rwkv = _proj(x, mod, norm_gain, w_nsa, w_fin, w_rwkv)

    scale = A_HEAD_DIM ** -0.5
    qg = (jnp.tile(q_norm_gain, A_HEADS) * scale).reshape(1, A_WIDTH)
    ksg = jnp.tile(k_norm_gain[1], A_KV_GROUPS).reshape(1, A_KV_WIDTH)
    kwg = jnp.tile(k_norm_gain[2], A_KV_GROUPS).reshape(1, A_KV_WIDTH)
    seg = np.arange(A_WIDTH) // A_HEAD_DIM
    bd = jnp.asarray((seg[:, None] == seg[None, :]).astype(np.float32) / A_HEAD_DIM, BF16)
    q_hs, ks, vs, kw, vw = _nsaprep(cols_nsa, qg, ksg, kwg, bd)

    def blocks16(t):
        t = t.reshape(bsz, n16, CMP_STRIDE, A_KV_GROUPS, A_HEAD_DIM).transpose(0, 3, 1, 2, 4)
        return t.reshape(bsz, A_KV_GROUPS, n16, CMP_STRIDE * A_HEAD_DIM)

    zk = blocks16(cols_nsa[:, :, A_WIDTH:A_WIDTH + A_KV_WIDTH])
    zv = blocks16(cols_nsa[:, :, A_WIDTH + A_KV_WIDTH:A_WIDTH + 2 * A_KV_WIDTH])
    kc, vc = _compress(zk, zv,
                       cmp_pos_k.reshape(2, CMP_STRIDE * A_HEAD_DIM), cmp_pos_v.reshape(2, CMP_STRIDE * A_HEAD_DIM),
                       cmp_k_w1.astype(BF16), cmp_k_w2.astype(BF16), cmp_v_w1.astype(BF16), cmp_v_w2.astype(BF16),
                       k_norm_gain[0].reshape(1, A_HEAD_DIM))
    bias_c, bias_d = _bias_tables(rel_bias, s, n16)
    y_a = _attention(q_hs, kc, vc, ks, vs, kw, vw, bias_c, bias_d, cols_nsa)

    vec = lambda t: t.reshape(1, -1)
    y_b = _rwkv(cols_rwkv, vec(shift_mu), vec(w0), w_lora_up.astype(BF16), vec(a0), a_lora_up.astype(BF16),
                vec(k_k), vec(k_a), vec(r_k), vec(ln_x_w), vec(ln_x_b))

    gate = mod[:, 2 * D_MODEL:].reshape(bsz, 1, D_MODEL)
    return _final(x, y_a, y_b, cols_fin, gate, w_out_a.astype(BF16), w_out_b.astype(BF16), w_o.astype(BF16))


def kernel(x, c, w_ada, b_ada, norm_gain, w_in, q_norm_gain, k_norm_gain, cmp_pos_k, cmp_pos_v, cmp_k_w1, cmp_k_w2, cmp_v_w1, cmp_v_w2, rel_bias, shift_mu, w0, w_lora_up, a0, a_lora_up, k_k, k_a, r_k, ln_x_w, ln_x_b, w_out_a, w_out_b, w_o):
    for l in range(w_in.shape[0]):
        x = _layer(x, c, rel_bias, w_ada[l], b_ada[l], norm_gain[l], w_in[l], q_norm_gain[l], k_norm_gain[l],
                   cmp_pos_k[l], cmp_pos_v[l], cmp_k_w1[l], cmp_k_w2[l], cmp_v_w1[l], cmp_v_w2[l],
                   shift_mu[l], w0[l], w_lora_up[l], a0[l], a_lora_up[l], k_k[l], k_a[l], r_k[l],
                   ln_x_w[l], ln_x_b[l], w_out_a[l], w_out_b[l], w_o[l])
    return x
```

```python
import functools
import math

import numpy as np
import jax
import jax.numpy as jnp
from jax import lax
from jax.experimental import pallas as pl
from jax.experimental.pallas import tpu as pltpu

F32 = jnp.float32
BF16 = jnp.bfloat16

D_MODEL = 1024
A_HEADS = 8
A_HEAD_DIM = 64
A_KV_GROUPS = 2
A_HPG = A_HEADS // A_KV_GROUPS
A_WIDTH = A_HEADS * A_HEAD_DIM
A_KV_WIDTH = A_KV_GROUPS * A_HEAD_DIM
CMP_BLOCK = 32
CMP_STRIDE = 16
CMP_HIDDEN = 256
SLC_BLOCK = 64
SLC_TOPN = 16
WINDOW = 512
B_HEADS = 8
B_HEAD_DIM = 64
B_WIDTH = B_HEADS * B_HEAD_DIM
DECAY_LORA = 64
ICLR_LORA = 64
LNX_EPS = 64e-5
REL_BUCKETS = 32
REL_MAX_EXACT = 16
REL_MAX_DIST = 128
NORM_EPS = 1e-6
NEG_INF = -1e30
FORCE_SCORE = 1e30

LANES = 128
TQ = 128
CHUNK = 64
GATE_PAD = 2 * LANES
GATE_ROWS = 16
NSA_COLS = A_WIDTH + 6 * A_KV_WIDTH + GATE_PAD
FIN_COLS = A_WIDTH + B_WIDTH + 2 * D_MODEL
RWKV_COLS = 3 * B_WIDTH + DECAY_LORA + ICLR_LORA
VMEM_LIMIT = 56 * 1024 * 1024


def _dot(a, b):
    return jnp.dot(a.astype(BF16), b.astype(BF16), preferred_element_type=F32)


def _dot_nt(a, b):
    return lax.dot_general(a.astype(BF16), b.astype(BF16), (((1,), (1,)), ((), ())),
                           preferred_element_type=F32)


def _dot_tn(a, b):
    return lax.dot_general(a.astype(BF16), b.astype(BF16), (((0,), (0,)), ((), ())),
                           preferred_element_type=F32)


def _split2(x):
    hi = x.astype(BF16)
    lo = (x - hi.astype(F32)).astype(BF16)
    return hi, lo


def _split3(x):
    h1 = x.astype(BF16)
    r1 = x - h1.astype(F32)
    h2 = r1.astype(BF16)
    h3 = (r1 - h2.astype(F32)).astype(BF16)
    return h1, h2, h3


def _sigmoid(x):
    return 1.0 / (1.0 + jnp.exp(-x))


def _bucket_thresholds():
    n = np.arange(0, 4096)
    nf = np.maximum(n, REL_MAX_EXACT).astype(np.float64)
    val = np.log(nf / REL_MAX_EXACT) / math.log(REL_MAX_DIST / REL_MAX_EXACT) * (REL_BUCKETS - REL_MAX_EXACT)
    frac = np.abs(val - np.round(val))
    assert np.all((frac > 1e-4) | (n <= REL_MAX_EXACT) | (n >= REL_MAX_DIST))
    large = REL_MAX_EXACT + np.floor(val + 1e-9).astype(np.int64)
    bucket = np.where(n < REL_MAX_EXACT, n, np.minimum(large, REL_BUCKETS - 1))
    return [int(np.argmax(bucket >= j)) for j in range(REL_BUCKETS)]


_BUCKET_TH = _bucket_thresholds()


def _bias_from_dist(dist, tbl_ref, head):
    val = jnp.full(dist.shape, tbl_ref[0, head], F32)
    for j in range(1, REL_BUCKETS):
        val = jnp.where(dist >= _BUCKET_TH[j], tbl_ref[j, head], val)
    return val


def _ada_kernel(c_ref, w_ref, b_ref, o_ref):
    c = c_ref[...]
    o_ref[...] = _dot(c * _sigmoid(c), w_ref[...]) + b_ref[...]


def _ada(c, w_ada, b_ada):
    bsz = c.shape[0]
    return pl.pallas_call(
        _ada_kernel,
        grid=(3,),
        in_specs=[pl.BlockSpec((bsz, D_MODEL), lambda j: (0, 0)),
                  pl.BlockSpec((D_MODEL, D_MODEL), lambda j: (0, j)),
                  pl.BlockSpec((1, D_MODEL), lambda j: (0, j))],
        out_specs=pl.BlockSpec((bsz, D_MODEL), lambda j: (0, j)),
        out_shape=jax.ShapeDtypeStruct((bsz, 3 * D_MODEL), F32),
        name="ada",
    )(c, w_ada, b_ada.reshape(1, 3 * D_MODEL))


def _proj_kernel(x_ref, mod_ref, g_ref, wn_ref, wf_ref, wr_ref, on_ref, of_ref, or_ref):
    x = x_ref[0]
    ms = jnp.mean(x * x, axis=-1, keepdims=True)
    y = x * lax.rsqrt(ms + NORM_EPS) * g_ref[...]
    mod = mod_ref[0]
    h = (y * (1.0 + mod[:, D_MODEL:2 * D_MODEL]) + mod[:, :D_MODEL]).astype(BF16)
    on_ref[0] = jnp.dot(h, wn_ref[...], preferred_element_type=F32)
    of_ref[0] = jnp.dot(h, wf_ref[...], preferred_element_type=F32)
    or_ref[0] = jnp.dot(h, wr_ref[...], preferred_element_type=F32)


def _proj(x, mod, norm_gain, w_nsa, w_fin, w_rwkv, tm=256):
    bsz, s, _ = x.shape
    const = lambda b, i: (0, 0)
    return pl.pallas_call(
        _proj_kernel,
        grid=(bsz, s // tm),
        in_specs=[pl.BlockSpec((1, tm, D_MODEL), lambda b, i: (b, i, 0)),
                  pl.BlockSpec((1, 1, 3 * D_MODEL), lambda b, i: (b, 0, 0)),
                  pl.BlockSpec((1, D_MODEL), const),
                  pl.BlockSpec((D_MODEL, NSA_COLS), const),
                  pl.BlockSpec((D_MODEL, FIN_COLS), const),
                  pl.BlockSpec((D_MODEL, RWKV_COLS), const)],
        out_specs=[pl.BlockSpec((1, tm, NSA_COLS), lambda b, i: (b, i, 0)),
                   pl.BlockSpec((1, tm, FIN_COLS), lambda b, i: (b, i, 0)),
                   pl.BlockSpec((1, tm, RWKV_COLS), lambda b, i: (b, i, 0))],
        out_shape=[jax.ShapeDtypeStruct((bsz, s, NSA_COLS), F32),
                   jax.ShapeDtypeStruct((bsz, s, FIN_COLS), F32),
                   jax.ShapeDtypeStruct((bsz, s, RWKV_COLS), F32)],
        compiler_params=pltpu.CompilerParams(dimension_semantics=("parallel", "parallel"),
                                             vmem_limit_bytes=VMEM_LIMIT),
        name="proj",
    )(x, mod.reshape(bsz, 1, 3 * D_MODEL), norm_gain.reshape(1, D_MODEL), w_nsa, w_fin, w_rwkv)


def _seg_mean(x2, bd):
    hi, lo = _split2(x2)
    return (jnp.dot(hi, bd, preferred_element_type=F32) + jnp.dot(lo, bd, preferred_element_type=F32))


def _nsaprep_kernel(c_ref, qg_ref, ksg_ref, kwg_ref, bd_ref, q_ref, ks_ref, vs_ref, kw_ref, vw_ref, gt_ref):
    bd = bd_ref[...]
    q = c_ref[0, :, 0:A_WIDTH]
    qn_t = (q * lax.rsqrt(_seg_mean(q * q, bd) + NORM_EPS) * qg_ref[...]).T
    off = A_WIDTH + 2 * A_KV_WIDTH
    k_slc = c_ref[0, :, off:off + A_KV_WIDTH]
    v_slc = c_ref[0, :, off + A_KV_WIDTH:off + 2 * A_KV_WIDTH]
    k_win = c_ref[0, :, off + 2 * A_KV_WIDTH:off + 3 * A_KV_WIDTH]
    v_win = c_ref[0, :, off + 3 * A_KV_WIDTH:off + 4 * A_KV_WIDTH]
    bd_kv = bd[:A_KV_WIDTH, :A_KV_WIDTH]
    ksn = (k_slc * lax.rsqrt(_seg_mean(k_slc * k_slc, bd_kv) + NORM_EPS) * ksg_ref[...]).astype(BF16)
    kwn = (k_win * lax.rsqrt(_seg_mean(k_win * k_win, bd_kv) + NORM_EPS) * kwg_ref[...]).astype(BF16)
    vs_t = v_slc.T.astype(BF16)
    vw_t = v_win.T.astype(BF16)
    gate0 = A_WIDTH + 6 * A_KV_WIDTH
    for g in range(A_KV_GROUPS):
        heads = [qn_t[(g * A_HPG + h) * A_HEAD_DIM:(g * A_HPG + h + 1) * A_HEAD_DIM, :] for h in range(A_HPG)]
        q_ref[0, g, 0] = jnp.concatenate(heads, axis=1).astype(BF16)
        sl = slice(g * A_HEAD_DIM, (g + 1) * A_HEAD_DIM)
        ks_ref[0, g] = ksn[:, sl]
        kw_ref[0, g] = kwn[:, sl]
        vs_ref[0, g, 0] = vs_t[sl, :]
        vw_ref[0, g, 0] = vw_t[sl, :]
        gates_t = _sigmoid(c_ref[0, :, gate0 + g * LANES:gate0 + (g + 1) * LANES]).T
        gt_ref[0, g, 0] = gates_t[0:GATE_ROWS, :]


def _nsaprep(cols_nsa, qg, ksg, kwg, bd):
    bsz, s, _ = cols_nsa.shape
    nt = s // TQ
    const = lambda b, i: (0, 0)
    k_spec = pl.BlockSpec((1, A_KV_GROUPS, TQ, A_HEAD_DIM), lambda b, i: (b, 0, i, 0))
    k_shape = jax.ShapeDtypeStruct((bsz, A_KV_GROUPS, s, A_HEAD_DIM), BF16)
    vt_spec = pl.BlockSpec((1, A_KV_GROUPS, 1, A_HEAD_DIM, TQ), lambda b, i: (b, 0, i, 0, 0))
    vt_shape = jax.ShapeDtypeStruct((bsz, A_KV_GROUPS, nt, A_HEAD_DIM, TQ), BF16)
    return pl.pallas_call(
        _nsaprep_kernel,
        grid=(bsz, nt),
        in_specs=[pl.BlockSpec((1, TQ, NSA_COLS), lambda b, i: (b, i, 0)),
                  pl.BlockSpec((1, A_WIDTH), const),
                  pl.BlockSpec((1, A_KV_WIDTH), const),
                  pl.BlockSpec((1, A_KV_WIDTH), const),
                  pl.BlockSpec((A_WIDTH, A_WIDTH), const)],
        out_specs=[pl.BlockSpec((1, A_KV_GROUPS, 1, A_HEAD_DIM, A_HPG * TQ), lambda b, i: (b, 0, i, 0, 0)),
                   k_spec, vt_spec, k_spec, vt_spec,
                   pl.BlockSpec((1, A_KV_GROUPS, 1, GATE_ROWS, TQ), lambda b, i: (b, 0, i, 0, 0))],
        out_shape=[jax.ShapeDtypeStruct((bsz, A_KV_GROUPS, nt, A_HEAD_DIM, A_HPG * TQ), BF16),
                   k_shape, vt_shape, k_shape, vt_shape,
                   jax.ShapeDtypeStruct((bsz, A_KV_GROUPS, nt, GATE_ROWS, TQ), F32)],
        compiler_params=pltpu.CompilerParams(dimension_semantics=("parallel", "parallel")),
        name="nsaprep",
    )(cols_nsa, qg, ksg, kwg, bd)


def _compress_kernel(zk_ref, zv_ref, pk_ref, pv_ref, w1k_ref, w2k_ref, w1v_ref, w2v_ref, kg_ref, kc_ref, vc_ref):
    half = (CMP_BLOCK // 2) * A_HEAD_DIM
    n16 = zk_ref.shape[2]

    def hidden(z, pos_ref, w1_ref):
        top = _dot(z + pos_ref[0:1, :], w1_ref[0:half, :])
        bot = _dot(z + pos_ref[1:2, :], w1_ref[half:2 * half, :])
        return jax.nn.gelu(top + pltpu.roll(bot, n16 - 1, axis=0), approximate=True)

    for g in range(A_KV_GROUPS):
        kc = _dot(hidden(zk_ref[0, g], pk_ref, w1k_ref), w2k_ref[...])
        ms = jnp.mean(kc * kc, axis=-1, keepdims=True)
        kc_ref[0, g] = (kc * lax.rsqrt(ms + NORM_EPS) * kg_ref[...]).astype(BF16)
        vc_ref[0, g] = _dot_nt(w2v_ref[...], hidden(zv_ref[0, g], pv_ref, w1v_ref)).astype(BF16)


def _compress(zk, zv, pk, pv, w1k, w2k, w1v, w2v_t, kg):
    bsz, _, n16, zw = zk.shape
    const = lambda b: (0, 0)
    z_spec = pl.BlockSpec((1, A_KV_GROUPS, n16, zw), lambda b: (b, 0, 0, 0))
    return pl.pallas_call(
        _compress_kernel,
        grid=(bsz,),
        in_specs=[z_spec, z_spec,
                  pl.BlockSpec((2, zw), const), pl.BlockSpec((2, zw), const),
                  pl.BlockSpec((2 * zw, CMP_HIDDEN), const), pl.BlockSpec((CMP_HIDDEN, A_HEAD_DIM), const),
                  pl.BlockSpec((2 * zw, CMP_HIDDEN), const), pl.BlockSpec((A_HEAD_DIM, CMP_HIDDEN), const),
                  pl.BlockSpec((1, A_HEAD_DIM), const)],
        out_specs=[pl.BlockSpec((1, A_KV_GROUPS, n16, A_HEAD_DIM), lambda b: (b, 0, 0, 0)),
                   pl.BlockSpec((1, A_KV_GROUPS, A_HEAD_DIM, n16), lambda b: (b, 0, 0, 0))],
        out_shape=[jax.ShapeDtypeStruct((bsz, A_KV_GROUPS, n16, A_HEAD_DIM), BF16),
                   jax.ShapeDtypeStruct((bsz, A_KV_GROUPS, A_HEAD_DIM, n16), BF16)],
        compiler_params=pltpu.CompilerParams(dimension_semantics=("parallel",)),
        name="compress",
    )(zk, zv, pk, pv, w1k, w2k, w1v, w2v_t, kg)


N_BIAS_TILES = 4


def _bias_cmp_kernel(tbl_ref, o_ref):
    i = pl.program_id(0)
    g = pl.program_id(1)
    n_cmp = o_ref.shape[2]
    n = lax.broadcasted_iota(jnp.int32, (n_cmp, TQ), 0)
    q = lax.broadcasted_iota(jnp.int32, (n_cmp, TQ), 1)
    dist = i * TQ + q - (n * CMP_STRIDE + CMP_BLOCK - 1)
    for h in range(A_HPG):
        bias = _bias_from_dist(dist, tbl_ref, g * A_HPG + h)
        o_ref[0, 0, :, h * TQ:(h + 1) * TQ] = jnp.where(dist >= 0, bias, NEG_INF)


def _bias_toeplitz_kernel(tbl_ref, o_ref):
    g = pl.program_id(0)
    r = pl.program_id(1)
    off = jnp.where(r == N_BIAS_TILES - 1, WINDOW // TQ, r)
    k = lax.broadcasted_iota(jnp.int32, (TQ, TQ), 0)
    q = lax.broadcasted_iota(jnp.int32, (TQ, TQ), 1)
    dist = off * TQ + q - k
    valid = (dist >= 0) & (dist < WINDOW)
    for h in range(A_HPG):
        bias = _bias_from_dist(dist, tbl_ref, g * A_HPG + h)
        o_ref[0, 0, :, h * TQ:(h + 1) * TQ] = jnp.where(valid, bias, NEG_INF)


def _bias_tables(rel_bias, s, n_cmp):
    smem = pl.BlockSpec(memory_space=pltpu.SMEM)
    nt = s // TQ
    bias_c = pl.pallas_call(
        _bias_cmp_kernel,
        grid=(nt, A_KV_GROUPS),
        in_specs=[smem],
        out_specs=pl.BlockSpec((1, 1, n_cmp, A_HPG * TQ), lambda i, g: (i, g, 0, 0)),
        out_shape=jax.ShapeDtypeStruct((nt, A_KV_GROUPS, n_cmp, A_HPG * TQ), F32),
        name="bias_cmp",
    )(rel_bias)
    assert _BUCKET_TH[REL_BUCKETS - 1] <= TQ + 1 and WINDOW // TQ >= 3
    bias_d = pl.pallas_call(
        _bias_toeplitz_kernel,
        grid=(A_KV_GROUPS, N_BIAS_TILES),
        in_specs=[smem],
        out_specs=pl.BlockSpec((1, 1, TQ, A_HPG * TQ), lambda g, r: (g, r, 0, 0)),
        out_shape=jax.ShapeDtypeStruct((A_KV_GROUPS, N_BIAS_TILES, TQ, A_HPG * TQ), F32),
        name="bias_toeplitz",
    )(rel_bias)
    return bias_c, bias_d


def _attn_kernel(q_ref, kc_ref, vc_ref, ks_ref, vs_ref, kw_ref, vw_ref, bc_ref, bd_ref, gt_ref, o_ref,
                 m_ref, l_ref, acc_ref):
    i = pl.program_id(2)
    tq = TQ
    n_cmp = kc_ref.shape[2]
    n_slc = ks_ref.shape[2] // SLC_BLOCK
    t0 = i * tq
    q_t = q_ref[0, 0, 0]
    gates = gt_ref[0, 0, 0]

    bias = bc_ref[0, 0]
    valid = bias > 0.5 * NEG_INF
    s = jnp.dot(kc_ref[0, 0], q_t, preferred_element_type=F32) + bias
    m = jnp.max(s, axis=0, keepdims=True)
    e = jnp.where(valid, jnp.exp(s - m), 0.0)
    l = jnp.sum(e, axis=0, keepdims=True)
    p = e * (1.0 / jnp.where(l > 0.0, l, 1.0))
    out_c = jnp.dot(vc_ref[0, 0], p.astype(BF16), preferred_element_type=F32)
    p_grp = sum(p[:, h * tq:(h + 1) * tq] for h in range(A_HPG))

    r1, r2 = SLC_BLOCK // CMP_STRIDE, CMP_BLOCK // CMP_STRIDE
    jj = lax.broadcasted_iota(jnp.int32, (n_slc, n_cmp), 0)
    nn = lax.broadcasted_iota(jnp.int32, (n_slc, n_cmp), 1)
    d = nn - r1 * jj
    cnt = jnp.zeros((n_slc, n_cmp), F32)
    for a in range(r1):
        for c in range(r2):
            cnt = cnt + jnp.where(d == a - c, 1.0, 0.0)
    cnt = cnt.astype(BF16)
    imp = sum(jnp.dot(cnt, part, preferred_element_type=F32) for part in _split3(p_grp))
    blk = lax.broadcasted_iota(jnp.int32, (n_slc, tq), 0)
    tpos = t0 + lax.broadcasted_iota(jnp.int32, (n_slc, tq), 1)
    cur = tpos // SLC_BLOCK
    forced = (blk == 0) | (blk == cur) | (blk == cur - 1)
    causal = blk * SLC_BLOCK <= tpos
    imp = jnp.where(forced, FORCE_SCORE, jnp.where(causal, imp, NEG_INF))
    rank = jnp.zeros((n_slc, tq), F32)
    for c in range(n_slc):
        row = imp[c:c + 1, :]
        ahead = (row > imp) | ((row == imp) & (blk > c))
        rank = rank + jnp.where(ahead, 1.0, 0.0)
    sel = jnp.where(rank < float(min(SLC_TOPN, n_slc)), 1.0, 0.0).astype(BF16)

    def flash(k_ref, v_ref, n_tiles, tile_of, penalty_of):
        m_ref[...] = jnp.full(m_ref.shape, NEG_INF, F32)
        l_ref[...] = jnp.zeros(l_ref.shape, F32)
        acc_ref[...] = jnp.zeros(acc_ref.shape, F32)

        def body(j, carry):
            kt = i - j
            k0 = pl.multiple_of(kt * tq, tq)
            s = jnp.dot(k_ref[0, 0, pl.ds(k0, tq), :], q_t, preferred_element_type=F32)
            s = s + bd_ref[0, tile_of(j)]
            pen = penalty_of(kt)
            if pen is not None:
                s = s + jnp.concatenate([pen] * A_HPG, axis=1)
            m_old = m_ref[...]
            m_new = jnp.maximum(m_old, jnp.max(s, axis=0, keepdims=True))
            alpha = jnp.exp(m_old - m_new)
            p = jnp.exp(s - m_new)
            l_ref[...] = alpha * l_ref[...] + jnp.sum(p, axis=0, keepdims=True)
            acc_ref[...] = alpha * acc_ref[...] + jnp.dot(v_ref[0, 0, kt], p.astype(BF16),
                                                          preferred_element_type=F32)
            m_ref[...] = m_new
            return carry

        lax.fori_loop(0, n_tiles, body, 0)
        return acc_ref[...] * (1.0 / l_ref[...])

    def sel_penalty(kt):
        kb = (kt * tq + lax.broadcasted_iota(jnp.int32, (tq, n_slc), 0)) // SLC_BLOCK
        expand = jnp.where(kb == lax.broadcasted_iota(jnp.int32, (tq, n_slc), 1), 1.0, 0.0).astype(BF16)
        chosen = jnp.dot(expand, sel, preferred_element_type=F32)
        return (chosen - 1.0) * FORCE_SCORE

    out_s = flash(ks_ref, vs_ref, i + 1, lambda j: jnp.minimum(j, 2), sel_penalty)

    wt = WINDOW // tq
    out_w = flash(kw_ref, vw_ref, jnp.minimum(i, wt) + 1,
                  lambda j: jnp.where(j == wt, N_BIAS_TILES - 1, jnp.minimum(j, 2)), lambda kt: None)

    blocks = []
    for h in range(A_HPG):
        cols = slice(h * tq, (h + 1) * tq)
        blocks.append(gates[h:h + 1, :] * out_c[:, cols]
                      + gates[A_HPG + h:A_HPG + h + 1, :] * out_s[:, cols]
                      + gates[2 * A_HPG + h:2 * A_HPG + h + 1, :] * out_w[:, cols])
    o_ref[0] = jnp.concatenate(blocks, axis=0).T


def _attention(q_t, kc, vc_t, ks, vs_t, kw, vw_t, bias_c, bias_d, gates_t):
    bsz, _, nt, _, _ = q_t.shape
    s = ks.shape[2]
    n_cmp = kc.shape[2]
    k_spec = pl.BlockSpec((1, 1, s, A_HEAD_DIM), lambda b, g, i: (b, g, 0, 0))
    vt_spec = pl.BlockSpec((1, 1, nt, A_HEAD_DIM, TQ), lambda b, g, i: (b, g, 0, 0, 0))
    return pl.pallas_call(
        _attn_kernel,
        grid=(bsz, A_KV_GROUPS, nt),
        in_specs=[pl.BlockSpec((1, 1, 1, A_HEAD_DIM, A_HPG * TQ), lambda b, g, i: (b, g, i, 0, 0)),
                  pl.BlockSpec((1, 1, n_cmp, A_HEAD_DIM), lambda b, g, i: (b, g, 0, 0)),
                  pl.BlockSpec((1, 1, A_HEAD_DIM, n_cmp), lambda b, g, i: (b, g, 0, 0)),
                  k_spec, vt_spec, k_spec, vt_spec,
                  pl.BlockSpec((1, 1, n_cmp, A_HPG * TQ), lambda b, g, i: (i, g, 0, 0)),
                  pl.BlockSpec((1, N_BIAS_TILES, TQ, A_HPG * TQ), lambda b, g, i: (g, 0, 0, 0)),
                  pl.BlockSpec((1, 1, 1, GATE_ROWS, TQ), lambda b, g, i: (b, g, i, 0, 0))],
        out_specs=pl.BlockSpec((1, TQ, A_HPG * A_HEAD_DIM), lambda b, g, i: (b, i, g)),
        out_shape=jax.ShapeDtypeStruct((bsz, s, A_WIDTH), F32),
        scratch_shapes=[pltpu.VMEM((1, A_HPG * TQ), F32),
                        pltpu.VMEM((1, A_HPG * TQ), F32),
                        pltpu.VMEM((A_HEAD_DIM, A_HPG * TQ), F32)],
        compiler_params=pltpu.CompilerParams(dimension_semantics=("parallel", "parallel", "parallel")),
        name="attn",
    )(q_t, kc, vc_t, ks, vs_t, kw, vw_t, bias_c, bias_d, gates_t)


def _rwkv_kernel(c_ref, mu_ref, w0_ref, wl_ref, a0_ref, al_ref, kk_ref, ka_ref, rk_ref, lw_ref, lb_ref,
                 o_ref, state_ref, prev_ref):
    cc = pl.program_id(1)
    n = B_HEAD_DIM
    csz = c_ref.shape[1]

    @pl.when(cc == 0)
    def _():
        state_ref[...] = jnp.zeros(state_ref.shape, F32)
        prev_ref[...] = jnp.zeros(prev_ref.shape, F32)

    p = c_ref[0]
    row = lax.broadcasted_iota(jnp.int32, p.shape, 0)
    prev = jnp.where(row == 0, prev_ref[0:1, :], pltpu.roll(p, 1, axis=0))
    prev_ref[0:1, :] = p[csz - 1:csz, :]
    x = p + (prev - p) * mu_ref[...]
    r = x[:, 0:B_WIDTH]
    k = x[:, B_WIDTH:2 * B_WIDTH]
    v = x[:, 2 * B_WIDTH:3 * B_WIDTH]
    wd = x[:, 3 * B_WIDTH:3 * B_WIDTH + DECAY_LORA]
    ad = x[:, 3 * B_WIDTH + DECAY_LORA:3 * B_WIDTH + DECAY_LORA + ICLR_LORA]

    z = -(w0_ref[...] + _dot(jnp.tanh(wd), wl_ref[...]))
    softplus = jnp.maximum(z, 0.0) + jnp.log(1.0 + jnp.exp(-jnp.abs(z)))
    ld = -jnp.exp(-softplus - 0.5)
    a = _sigmoid(a0_ref[...] + _dot(ad, al_ref[...]))
    kk = k * kk_ref[...]
    k_mod = k * (1.0 + (a - 1.0) * ka_ref[...])
    rkr = r * k_mod * rk_ref[...]

    ti = lax.broadcasted_iota(jnp.int32, (csz, csz), 0)
    si = lax.broadcasted_iota(jnp.int32, (csz, csz), 1)
    lower = si <= ti
    strict = si < ti
    tri = jnp.where(lower, 1.0, 0.0).astype(BF16)
    ld_hi, ld_lo = _split2(ld)
    cum = jnp.dot(tri, ld_hi, preferred_element_type=F32) + jnp.dot(tri, ld_lo, preferred_element_type=F32)
    g_inc = jnp.exp(cum)
    g_exc = jnp.exp(cum - ld)
    g_inv = jnp.exp(-cum)
    g_end = jnp.exp(cum[csz - 1:csz, :] - cum)
    g_all = g_inc[csz - 1:csz, :]

    eye = jnp.where(si == ti, 1.0, 0.0)
    outs = []
    for h in range(B_HEADS):
        sl = slice(h * n, (h + 1) * n)
        kk_h = kk[:, sl]
        kk_h = kk_h * lax.rsqrt(jnp.maximum(jnp.sum(kk_h * kk_h, axis=-1, keepdims=True), 1e-24))
        b_h = kk_h * a[:, sl]
        v_h = v[:, sl]
        lhs = jnp.concatenate([-kk_h * g_exc[:, sl], r[:, sl] * g_inc[:, sl]], axis=0)
        bt = b_h * g_inv[:, sl]
        kt = k_mod[:, sl] * g_inv[:, sl]
        xb = _dot_nt(lhs, bt)
        xk = _dot_nt(lhs, kt)
        a_ab = jnp.where(strict, xb[:csz], 0.0)
        a_ak = jnp.where(strict, xk[:csz], 0.0)
        m_rb = jnp.where(lower, xb[csz:], 0.0)
        m_rk = jnp.where(lower, xk[csz:], 0.0)
        tinv = eye + a_ab
        pw = a_ab
        for _ in range(int(math.log2(csz)) - 1):
            pw = _dot(pw, pw)
            tinv = tinv + _dot(tinv, pw)
        s0 = state_ref[h]
        as0 = _dot_nt(lhs, s0)
        u = _dot(tinv, as0[:csz] + _dot(a_ak, v_h))
        y = as0[csz:] + _dot(m_rb, u) + _dot(m_rk, v_h)
        uv = jnp.concatenate([u, v_h], axis=0)
        bk = jnp.concatenate([b_h * g_end[:, sl], k_mod[:, sl] * g_end[:, sl]], axis=0)
        state_ref[h] = s0 * g_all[:, sl] + _dot_tn(uv, bk)
        mean = jnp.mean(y, axis=-1, keepdims=True)
        yc = y - mean
        var = jnp.mean(yc * yc, axis=-1, keepdims=True)
        bonus = jnp.sum(rkr[:, sl], axis=-1, keepdims=True) * v_h
        outs.append((yc * lax.rsqrt(var + LNX_EPS), bonus))
    yn = jnp.concatenate([o[0] for o in outs], axis=-1)
    bonus = jnp.concatenate([o[1] for o in outs], axis=-1)
    o_ref[0] = yn * lw_ref[...] + lb_ref[...] + bonus


def _rwkv(cols_rwkv, mu, w0, wl, a0, al, k_k, k_a, r_k, ln_w, ln_b):
    bsz, s, _ = cols_rwkv.shape
    const = lambda b, c: (0, 0)
    vec = pl.BlockSpec((1, B_WIDTH), const)
    return pl.pallas_call(
        _rwkv_kernel,
        grid=(bsz, s // CHUNK),
        in_specs=[pl.BlockSpec((1, CHUNK, RWKV_COLS), lambda b, c: (b, c, 0)),
                  pl.BlockSpec((1, RWKV_COLS), const),
                  vec, pl.BlockSpec((DECAY_LORA, B_WIDTH), const),
                  vec, pl.BlockSpec((ICLR_LORA, B_WIDTH), const),
                  vec, vec, vec, vec, vec],
        out_specs=pl.BlockSpec((1, CHUNK, B_WIDTH), lambda b, c: (b, c, 0)),
        out_shape=jax.ShapeDtypeStruct((bsz, s, B_WIDTH), F32),
        scratch_shapes=[pltpu.VMEM((B_HEADS, B_HEAD_DIM, B_HEAD_DIM), F32),
                        pltpu.VMEM((8, RWKV_COLS), F32)],
        compiler_params=pltpu.CompilerParams(dimension_semantics=("parallel", "arbitrary")),
        name="rwkv",
    )(cols_rwkv, mu, w0, wl, a0, al, k_k, k_a, r_k, ln_w, ln_b)


def _final_kernel(x_ref, ya_ref, yb_ref, cf_ref, gate_ref, wa_ref, wb_ref, wo_ref, o_ref):
    a_silu = cf_ref[0, :, 0:A_WIDTH]
    b_silu = cf_ref[0, :, A_WIDTH:A_WIDTH + B_WIDTH]
    merge_a = cf_ref[0, :, A_WIDTH + B_WIDTH:A_WIDTH + B_WIDTH + D_MODEL]
    merge_b = cf_ref[0, :, A_WIDTH + B_WIDTH + D_MODEL:A_WIDTH + B_WIDTH + 2 * D_MODEL]
    ya = ya_ref[0] * (a_silu * _sigmoid(a_silu))
    yb = yb_ref[0] * (b_silu * _sigmoid(b_silu))
    merged = _sigmoid(merge_a) * _dot(ya, wa_ref[...]) + _sigmoid(merge_b) * _dot(yb, wb_ref[...])
    o_ref[0] = x_ref[0] + gate_ref[0] * _dot(merged, wo_ref[...])


def _final(x, y_a, y_b, cols_fin, gate, w_out_a, w_out_b, w_o, tm=256):
    bsz, s, _ = x.shape
    const = lambda b, i: (0, 0)
    row = lambda w: pl.BlockSpec((1, tm, w), lambda b, i: (b, i, 0))
    return pl.pallas_call(
        _final_kernel,
        grid=(bsz, s // tm),
        in_specs=[row(D_MODEL), row(A_WIDTH), row(B_WIDTH), row(FIN_COLS),
                  pl.BlockSpec((1, 1, D_MODEL), lambda b, i: (b, 0, 0)),
                  pl.BlockSpec((A_WIDTH, D_MODEL), const),
                  pl.BlockSpec((B_WIDTH, D_MODEL), const),
                  pl.BlockSpec((D_MODEL, D_MODEL), const)],
        out_specs=row(D_MODEL),
        out_shape=jax.ShapeDtypeStruct((bsz, s, D_MODEL), F32),
        compiler_params=pltpu.CompilerParams(dimension_semantics=("parallel", "parallel"),
                                             vmem_limit_bytes=VMEM_LIMIT),
        name="final",
    )(x, y_a, y_b, cols_fin, gate, w_out_a, w_out_b, w_o)


def _split_w_in(w_in):
    nsa_in = 2 * A_WIDTH + 6 * A_KV_WIDTH + 3 * A_HEADS
    o_gate = A_WIDTH + 6 * A_KV_WIDTH
    o_asilu = o_gate + 3 * A_HEADS
    o_shift = nsa_in
    o_rest = nsa_in + RWKV_COLS
    gate_w = w_in[:, o_gate:o_asilu].reshape(D_MODEL, 3, A_KV_GROUPS, A_HPG)
    gate_w = gate_w.transpose(0, 2, 1, 3).reshape(D_MODEL, A_KV_GROUPS, 3 * A_HPG)
    gate_w = jnp.pad(gate_w, ((0, 0), (0, 0), (0, LANES - 3 * A_HPG))).reshape(D_MODEL, GATE_PAD)
    w_nsa = jnp.concatenate([w_in[:, :o_gate], gate_w], axis=1)
    w_fin = jnp.concatenate([w_in[:, o_asilu:o_shift], w_in[:, o_rest:]], axis=1)
    w_rwkv = w_in[:, o_shift:o_rest]
    return w_nsa.astype(BF16), w_fin.astype(BF16), w_rwkv.astype(BF16)


def _layer(x, c, rel_bias, w_ada, b_ada, norm_gain, w_in, q_norm_gain, k_norm_gain,
           cmp_pos_k, cmp_pos_v, cmp_k_w1, cmp_k_w2, cmp_v_w1, cmp_v_w2,
           shift_mu, w0, w_lora_up, a0, a_lora_up, k_k, k_a, r_k, ln_x_w, ln_x_b,
           w_out_a, w_out_b, w_o):
    bsz, s, _ = x.shape
    assert s % (2 * TQ) == 0 and s // CMP_STRIDE == LANES
    n16 = s // CMP_STRIDE
    mod = _ada(c, w_ada, b_ada)
    w_nsa, w_fin, w_rwkv = _split_w_in(w_in)
    cols_nsa, cols_fin, cols_rwkv = _proj(x, mod, norm_gain, w_nsa, w_fin, w_rwkv)

    scale = A_HEAD_DIM ** -0.5
    qg = (jnp.tile(q_norm_gain, A_HEADS) * scale).reshape(1, A_WIDTH)
    ksg = jnp.tile(k_norm_gain[1], A_KV_GROUPS).reshape(1, A_KV_WIDTH)
    kwg = jnp.tile(k_norm_gain[2], A_KV_GROUPS).reshape(1, A_KV_WIDTH)
    seg = np.arange(A_WIDTH) // A_HEAD_DIM
    bd = jnp.asarray((seg[:, None] == seg[None, :]).astype(np.float32) / A_HEAD_DIM, BF16)
    q_t, ks, vs_t, kw, vw_t, gates_t = _nsaprep(cols_nsa, qg, ksg, kwg, bd)

    def blocks16(t):
        t = t.reshape(bsz, n16, CMP_STRIDE, A_KV_GROUPS, A_HEAD_DIM).transpose(0, 3, 1, 2, 4)
        return t.reshape(bsz, A_KV_GROUPS, n16, CMP_STRIDE * A_HEAD_DIM)

    zk = blocks16(cols_nsa[:, :, A_WIDTH:A_WIDTH + A_KV_WIDTH])
    zv = blocks16(cols_nsa[:, :, A_WIDTH + A_KV_WIDTH:A_WIDTH + 2 * A_KV_WIDTH])
    kc, vc_t = _compress(zk, zv,
                         cmp_pos_k.reshape(2, CMP_STRIDE * A_HEAD_DIM), cmp_pos_v.reshape(2, CMP_STRIDE * A_HEAD_DIM),
                         cmp_k_w1.astype(BF16), cmp_k_w2.astype(BF16), cmp_v_w1.astype(BF16), cmp_v_w2.T.astype(BF16),
                         k_norm_gain[0].reshape(1, A_HEAD_DIM))
    bias_c, bias_d = _bias_tables(rel_bias, s, n16)
    y_a = _attention(q_t, kc, vc_t, ks, vs_t, kw, vw_t, bias_c, bias_d, gates_t)

    vec = lambda t: t.reshape(1, -1)
    y_b = _rwkv(cols_rwkv, vec(shift_mu), vec(w0), w_lora_up.astype(BF16), vec(a0), a_lora_up.astype(BF16),
                vec(k_k), vec(k_a), vec(r_k), vec(ln_x_w), vec(ln_x_b))

    gate = mod[:, 2 * D_MODEL:].reshape(bsz, 1, D_MODEL)
    return _final(x, y_a, y_b, cols_fin, gate, w_out_a.astype(BF16), w_out_b.astype(BF16), w_o.astype(BF16))


def kernel(x, c, w_ada, b_ada, norm_gain, w_in, q_norm_gain, k_norm_gain, cmp_pos_k, cmp_pos_v, cmp_k_w1, cmp_k_w2, cmp_v_w1, cmp_v_w2, rel_bias, shift_mu, w0, w_lora_up, a0, a_lora_up, k_k, k_a, r_k, ln_x_w, ln_x_b, w_out_a, w_out_b, w_o):
    for l in range(w_in.shape[0]):
        x = _layer(x, c, rel_bias, w_ada[l], b_ada[l], norm_gain[l], w_in[l], q_norm_gain[l], k_norm_gain[l],
                   cmp_pos_k[l], cmp_pos_v[l], cmp_k_w1[l], cmp_k_w2[l], cmp_v_w1[l], cmp_v_w2[l],
                   shift_mu[l], w0[l], w_lora_up[l], a0[l], a_lora_up[l], k_k[l], k_a[l], r_k[l],
                   ln_x_w[l], ln_x_b[l], w_out_a[l], w_out_b[l], w_o[l])
    return x
```

```python
import functools
import math

import numpy as np
import jax
import jax.numpy as jnp
from jax import lax
from jax.experimental import pallas as pl
from jax.experimental.pallas import tpu as pltpu

F32 = jnp.float32
BF16 = jnp.bfloat16

D_MODEL = 1024
A_HEADS = 8
A_HEAD_DIM = 64
A_KV_GROUPS = 2
A_HPG = A_HEADS // A_KV_GROUPS
A_WIDTH = A_HEADS * A_HEAD_DIM
A_KV_WIDTH = A_KV_GROUPS * A_HEAD_DIM
CMP_BLOCK = 32
CMP_STRIDE = 16
CMP_HIDDEN = 256
SLC_BLOCK = 64
SLC_TOPN = 16
WINDOW = 512
B_HEADS = 8
B_HEAD_DIM = 64
B_WIDTH = B_HEADS * B_HEAD_DIM
DECAY_LORA = 64
ICLR_LORA = 64
LNX_EPS = 64e-5
REL_BUCKETS = 32
REL_MAX_EXACT = 16
REL_MAX_DIST = 128
NORM_EPS = 1e-6
NEG_INF = -1e30
FORCE_SCORE = 1e30

LANES = 128
TQ = 128
CHUNK = 64
GATE_PAD = 2 * LANES
GATE_ROWS = 16
NSA_COLS = A_WIDTH + 6 * A_KV_WIDTH + GATE_PAD
FIN_COLS = A_WIDTH + B_WIDTH + 2 * D_MODEL
RWKV_COLS = 3 * B_WIDTH + DECAY_LORA + ICLR_LORA
VMEM_LIMIT = 56 * 1024 * 1024


def _dot(a, b):
    return jnp.dot(a.astype(BF16), b.astype(BF16), preferred_element_type=F32)


def _dot_nt(a, b):
    return lax.dot_general(a.astype(BF16), b.astype(BF16), (((1,), (1,)), ((), ())),
                           preferred_element_type=F32)


def _dot_tn(a, b):
    return lax.dot_general(a.astype(BF16), b.astype(BF16), (((0,), (0,)), ((), ())),
                           preferred_element_type=F32)


def _split2(x):
    hi = x.astype(BF16)
    lo = (x - hi.astype(F32)).astype(BF16)
    return hi, lo


def _split3(x):
    h1 = x.astype(BF16)
    r1 = x - h1.astype(F32)
    h2 = r1.astype(BF16)
    h3 = (r1 - h2.astype(F32)).astype(BF16)
    return h1, h2, h3


def _sigmoid(x):
    return 1.0 / (1.0 + jnp.exp(-x))


def _bucket_thresholds():
    n = np.arange(0, 4096)
    nf = np.maximum(n, REL_MAX_EXACT).astype(np.float64)
    val = np.log(nf / REL_MAX_EXACT) / math.log(REL_MAX_DIST / REL_MAX_EXACT) * (REL_BUCKETS - REL_MAX_EXACT)
    frac = np.abs(val - np.round(val))
    assert np.all((frac > 1e-4) | (n <= REL_MAX_EXACT) | (n >= REL_MAX_DIST))
    large = REL_MAX_EXACT + np.floor(val + 1e-9).astype(np.int64)
    bucket = np.where(n < REL_MAX_EXACT, n, np.minimum(large, REL_BUCKETS - 1))
    return [int(np.argmax(bucket >= j)) for j in range(REL_BUCKETS)]


_BUCKET_TH = _bucket_thresholds()


def _bias_from_dist(dist, tbl_ref, head):
    val = jnp.full(dist.shape, tbl_ref[0, head], F32)
    for j in range(1, REL_BUCKETS):
        val = jnp.where(dist >= _BUCKET_TH[j], tbl_ref[j, head], val)
    return val


def _ada_kernel(c_ref, w_ref, b_ref, o_ref):
    c = c_ref[...]
    o_ref[...] = _dot(c * _sigmoid(c), w_ref[...]) + b_ref[...]


def _ada(c, w_ada, b_ada):
    bsz = c.shape[0]
    return pl.pallas_call(
        _ada_kernel,
        grid=(3,),
        in_specs=[pl.BlockSpec((bsz, D_MODEL), lambda j: (0, 0)),
                  pl.BlockSpec((D_MODEL, D_MODEL), lambda j: (0, j)),
                  pl.BlockSpec((1, D_MODEL), lambda j: (0, j))],
        out_specs=pl.BlockSpec((bsz, D_MODEL), lambda j: (0, j)),
        out_shape=jax.ShapeDtypeStruct((bsz, 3 * D_MODEL), F32),
        name="ada",
    )(c, w_ada, b_ada.reshape(1, 3 * D_MODEL))


def _proj_kernel(x_ref, mod_ref, g_ref, wn_ref, wf_ref, wr_ref, on_ref, of_ref, or_ref):
    x = x_ref[0]
    ms = jnp.mean(x * x, axis=-1, keepdims=True)
    y = x * lax.rsqrt(ms + NORM_EPS) * g_ref[...]
    mod = mod_ref[0]
    h = (y * (1.0 + mod[:, D_MODEL:2 * D_MODEL]) + mod[:, :D_MODEL]).astype(BF16)
    on_ref[0] = jnp.dot(h, wn_ref[...], preferred_element_type=F32)
    of_ref[0] = jnp.dot(h, wf_ref[...], preferred_element_type=F32)
    or_ref[0] = jnp.dot(h, wr_ref[...], preferred_element_type=F32)


def _proj(x, mod, norm_gain, w_nsa, w_fin, w_rwkv, tm=256):
    bsz, s, _ = x.shape
    const = lambda b, i: (0, 0)
    return pl.pallas_call(
        _proj_kernel,
        grid=(bsz, s // tm),
        in_specs=[pl.BlockSpec((1, tm, D_MODEL), lambda b, i: (b, i, 0)),
                  pl.BlockSpec((1, 1, 3 * D_MODEL), lambda b, i: (b, 0, 0)),
                  pl.BlockSpec((1, D_MODEL), const),
                  pl.BlockSpec((D_MODEL, NSA_COLS), const),
                  pl.BlockSpec((D_MODEL, FIN_COLS), const),
                  pl.BlockSpec((D_MODEL, RWKV_COLS), const)],
        out_specs=[pl.BlockSpec((1, tm, NSA_COLS), lambda b, i: (b, i, 0)),
                   pl.BlockSpec((1, tm, FIN_COLS), lambda b, i: (b, i, 0)),
                   pl.BlockSpec((1, tm, RWKV_COLS), lambda b, i: (b, i, 0))],
        out_shape=[jax.ShapeDtypeStruct((bsz, s, NSA_COLS), F32),
                   jax.ShapeDtypeStruct((bsz, s, FIN_COLS), F32),
                   jax.ShapeDtypeStruct((bsz, s, RWKV_COLS), F32)],
        compiler_params=pltpu.CompilerParams(dimension_semantics=("parallel", "parallel"),
                                             vmem_limit_bytes=VMEM_LIMIT),
        name="proj",
    )(x, mod.reshape(bsz, 1, 3 * D_MODEL), norm_gain.reshape(1, D_MODEL), w_nsa, w_fin, w_rwkv)


def _seg_mean(x2, bd):
    hi, lo = _split2(x2)
    return (jnp.dot(hi, bd, preferred_element_type=F32) + jnp.dot(lo, bd, preferred_element_type=F32))


def _nsaprep_kernel(c_ref, qg_ref, ksg_ref, kwg_ref, bd_ref, q_ref, ks_ref, vs_ref, kw_ref, vw_ref, gt_ref):
    bd = bd_ref[...]
    q = c_ref[0, :, 0:A_WIDTH]
    qn_t = (q * lax.rsqrt(_seg_mean(q * q, bd) + NORM_EPS) * qg_ref[...]).T
    off = A_WIDTH + 2 * A_KV_WIDTH
    k_slc = c_ref[0, :, off:off + A_KV_WIDTH]
    v_slc = c_ref[0, :, off + A_KV_WIDTH:off + 2 * A_KV_WIDTH]
    k_win = c_ref[0, :, off + 2 * A_KV_WIDTH:off + 3 * A_KV_WIDTH]
    v_win = c_ref[0, :, off + 3 * A_KV_WIDTH:off + 4 * A_KV_WIDTH]
    bd_kv = bd[:A_KV_WIDTH, :A_KV_WIDTH]
    ksn = (k_slc * lax.rsqrt(_seg_mean(k_slc * k_slc, bd_kv) + NORM_EPS) * ksg_ref[...]).astype(BF16)
    kwn = (k_win * lax.rsqrt(_seg_mean(k_win * k_win, bd_kv) + NORM_EPS) * kwg_ref[...]).astype(BF16)
    vs_t = v_slc.T.astype(BF16)
    vw_t = v_win.T.astype(BF16)
    gate0 = A_WIDTH + 6 * A_KV_WIDTH
    for g in range(A_KV_GROUPS):
        heads = [qn_t[(g * A_HPG + h) * A_HEAD_DIM:(g * A_HPG + h + 1) * A_HEAD_DIM, :] for h in range(A_HPG)]
        q_ref[0, g, 0] = jnp.concatenate(heads, axis=1).astype(BF16)
        sl = slice(g * A_HEAD_DIM, (g + 1) * A_HEAD_DIM)
        ks_ref[0, g] = ksn[:, sl]
        kw_ref[0, g] = kwn[:, sl]
        vs_ref[0, g, 0] = vs_t[sl, :]
        vw_ref[0, g, 0] = vw_t[sl, :]
        gates_t = _sigmoid(c_ref[0, :, gate0 + g * LANES:gate0 + (g + 1) * LANES]).T
        gt_ref[0, g, 0] = gates_t[0:GATE_ROWS, :]


def _nsaprep(cols_nsa, qg, ksg, kwg, bd):
    bsz, s, _ = cols_nsa.shape
    nt = s // TQ
    const = lambda b, i: (0, 0)
    k_spec = pl.BlockSpec((1, A_KV_GROUPS, TQ, A_HEAD_DIM), lambda b, i: (b, 0, i, 0))
    k_shape = jax.ShapeDtypeStruct((bsz, A_KV_GROUPS, s, A_HEAD_DIM), BF16)
    vt_spec = pl.BlockSpec((1, A_KV_GROUPS, 1, A_HEAD_DIM, TQ), lambda b, i: (b, 0, i, 0, 0))
    vt_shape = jax.ShapeDtypeStruct((bsz, A_KV_GROUPS, nt, A_HEAD_DIM, TQ), BF16)
    return pl.pallas_call(
        _nsaprep_kernel,
        grid=(bsz, nt),
        in_specs=[pl.BlockSpec((1, TQ, NSA_COLS), lambda b, i: (b, i, 0)),
                  pl.BlockSpec((1, A_WIDTH), const),
                  pl.BlockSpec((1, A_KV_WIDTH), const),
                  pl.BlockSpec((1, A_KV_WIDTH), const),
                  pl.BlockSpec((A_WIDTH, A_WIDTH), const)],
        out_specs=[pl.BlockSpec((1, A_KV_GROUPS, 1, A_HEAD_DIM, A_HPG * TQ), lambda b, i: (b, 0, i, 0, 0)),
                   k_spec, vt_spec, k_spec, vt_spec,
                   pl.BlockSpec((1, A_KV_GROUPS, 1, GATE_ROWS, TQ), lambda b, i: (b, 0, i, 0, 0))],
        out_shape=[jax.ShapeDtypeStruct((bsz, A_KV_GROUPS, nt, A_HEAD_DIM, A_HPG * TQ), BF16),
                   k_shape, vt_shape, k_shape, vt_shape,
                   jax.ShapeDtypeStruct((bsz, A_KV_GROUPS, nt, GATE_ROWS, TQ), F32)],
        compiler_params=pltpu.CompilerParams(dimension_semantics=("parallel", "parallel")),
        name="nsaprep",
    )(cols_nsa, qg, ksg, kwg, bd)


def _compress_kernel(zk_ref, zv_ref, pk_ref, pv_ref, w1k_ref, w2k_ref, w1v_ref, w2v_ref, kg_ref, kc_ref, vc_ref):
    half = (CMP_BLOCK // 2) * A_HEAD_DIM
    n16 = zk_ref.shape[2]

    def hidden(z, pos_ref, w1_ref):
        top = _dot(z + pos_ref[0:1, :], w1_ref[0:half, :])
        bot = _dot(z + pos_ref[1:2, :], w1_ref[half:2 * half, :])
        return jax.nn.gelu(top + pltpu.roll(bot, n16 - 1, axis=0), approximate=True)

    for g in range(A_KV_GROUPS):
        kc = _dot(hidden(zk_ref[0, g], pk_ref, w1k_ref), w2k_ref[...])
        ms = jnp.mean(kc * kc, axis=-1, keepdims=True)
        kc_ref[0, g] = (kc * lax.rsqrt(ms + NORM_EPS) * kg_ref[...]).astype(BF16)
        vc_ref[0, g] = _dot_nt(w2v_ref[...], hidden(zv_ref[0, g], pv_ref, w1v_ref)).astype(BF16)


def _compress(zk, zv, pk, pv, w1k, w2k, w1v, w2v_t, kg):
    bsz, _, n16, zw = zk.shape
    const = lambda b: (0, 0)
    z_spec = pl.BlockSpec((1, A_KV_GROUPS, n16, zw), lambda b: (b, 0, 0, 0))
    return pl.pallas_call(
        _compress_kernel,
        grid=(bsz,),
        in_specs=[z_spec, z_spec,
                  pl.BlockSpec((2, zw), const), pl.BlockSpec((2, zw), const),
                  pl.BlockSpec((2 * zw, CMP_HIDDEN), const), pl.BlockSpec((CMP_HIDDEN, A_HEAD_DIM), const),
                  pl.BlockSpec((2 * zw, CMP_HIDDEN), const), pl.BlockSpec((A_HEAD_DIM, CMP_HIDDEN), const),
                  pl.BlockSpec((1, A_HEAD_DIM), const)],
        out_specs=[pl.BlockSpec((1, A_KV_GROUPS, n16, A_HEAD_DIM), lambda b: (b, 0, 0, 0)),
                   pl.BlockSpec((1, A_KV_GROUPS, A_HEAD_DIM, n16), lambda b: (b, 0, 0, 0))],
        out_shape=[jax.ShapeDtypeStruct((bsz, A_KV_GROUPS, n16, A_HEAD_DIM), BF16),
                   jax.ShapeDtypeStruct((bsz, A_KV_GROUPS, A_HEAD_DIM, n16), BF16)],
        compiler_params=pltpu.CompilerParams(dimension_semantics=("parallel",)),
        name="compress",
    )(zk, zv, pk, pv, w1k, w2k, w1v, w2v_t, kg)


N_BIAS_TILES = 4


def _bias_cmp_kernel(tbl_ref, o_ref):
    i = pl.program_id(0)
    g = pl.program_id(1)
    n_cmp = o_ref.shape[2]
    n = lax.broadcasted_iota(jnp.int32, (n_cmp, TQ), 0)
    q = lax.broadcasted_iota(jnp.int32, (n_cmp, TQ), 1)
    dist = i * TQ + q - (n * CMP_STRIDE + CMP_BLOCK - 1)
    for h in range(A_HPG):
        bias = _bias_from_dist(dist, tbl_ref, g * A_HPG + h)
        o_ref[0, 0, :, h * TQ:(h + 1) * TQ] = jnp.where(dist >= 0, bias, NEG_INF)


def _bias_toeplitz_kernel(tbl_ref, o_ref):
    g = pl.program_id(0)
    r = pl.program_id(1)
    off = jnp.where(r == N_BIAS_TILES - 1, WINDOW // TQ, r)
    k = lax.broadcasted_iota(jnp.int32, (TQ, TQ), 0)
    q = lax.broadcasted_iota(jnp.int32, (TQ, TQ), 1)
    dist = off * TQ + q - k
    valid = (dist >= 0) & (dist < WINDOW)
    for h in range(A_HPG):
        bias = _bias_from_dist(dist, tbl_ref, g * A_HPG + h)
        o_ref[0, 0, :, h * TQ:(h + 1) * TQ] = jnp.where(valid, bias, NEG_INF)


def _bias_tables(rel_bias, s, n_cmp):
    smem = pl.BlockSpec(memory_space=pltpu.SMEM)
    nt = s // TQ
    bias_c = pl.pallas_call(
        _bias_cmp_kernel,
        grid=(nt, A_KV_GROUPS),
        in_specs=[smem],
        out_specs=pl.BlockSpec((1, 1, n_cmp, A_HPG * TQ), lambda i, g: (i, g, 0, 0)),
        out_shape=jax.ShapeDtypeStruct((nt, A_KV_GROUPS, n_cmp, A_HPG * TQ), F32),
        name="bias_cmp",
    )(rel_bias)
    assert _BUCKET_TH[REL_BUCKETS - 1] <= TQ + 1 and WINDOW // TQ >= 3
    bias_d = pl.pallas_call(
        _bias_toeplitz_kernel,
        grid=(A_KV_GROUPS, N_BIAS_TILES),
        in_specs=[smem],
        out_specs=pl.BlockSpec((1, 1, TQ, A_HPG * TQ), lambda g, r: (g, r, 0, 0)),
        out_shape=jax.ShapeDtypeStruct((A_KV_GROUPS, N_BIAS_TILES, TQ, A_HPG * TQ), F32),
        name="bias_toeplitz",
    )(rel_bias)
    return bias_c, bias_d


def _attn_kernel(q_ref, kc_ref, vc_ref, ks_ref, vs_ref, kw_ref, vw_ref, bc_ref, bd_ref, gt_ref, o_ref,
                 m_ref, l_ref, acc_ref):
    i = pl.program_id(2)
    tq = TQ
    n_cmp = kc_ref.shape[2]
    n_slc = ks_ref.shape[2] // SLC_BLOCK
    t0 = i * tq
    q_t = q_ref[0, 0, 0]
    gates = gt_ref[0, 0, 0]

    bias = bc_ref[0, 0]
    valid = bias > 0.5 * NEG_INF
    s = jnp.dot(kc_ref[0, 0], q_t, preferred_element_type=F32) + bias
    m = jnp.max(s, axis=0, keepdims=True)
    e = jnp.where(valid, jnp.exp(s - m), 0.0)
    l = jnp.sum(e, axis=0, keepdims=True)
    p = e * (1.0 / jnp.where(l > 0.0, l, 1.0))
    out_c = jnp.dot(vc_ref[0, 0], p.astype(BF16), preferred_element_type=F32)
    p_grp = sum(p[:, h * tq:(h + 1) * tq] for h in range(A_HPG))

    r1, r2 = SLC_BLOCK // CMP_STRIDE, CMP_BLOCK // CMP_STRIDE
    jj = lax.broadcasted_iota(jnp.int32, (n_slc, n_cmp), 0)
    nn = lax.broadcasted_iota(jnp.int32, (n_slc, n_cmp), 1)
    d = nn - r1 * jj
    cnt = jnp.zeros((n_slc, n_cmp), F32)
    for a in range(r1):
        for c in range(r2):
            cnt = cnt + jnp.where(d == a - c, 1.0, 0.0)
    cnt = cnt.astype(BF16)
    imp = sum(jnp.dot(cnt, part, preferred_element_type=F32) for part in _split3(p_grp))
    blk = lax.broadcasted_iota(jnp.int32, (n_slc, tq), 0)
    tpos = t0 + lax.broadcasted_iota(jnp.int32, (n_slc, tq), 1)
    cur = tpos // SLC_BLOCK
    forced = (blk == 0) | (blk == cur) | (blk == cur - 1)
    causal = blk * SLC_BLOCK <= tpos
    imp = jnp.where(forced, FORCE_SCORE, jnp.where(causal, imp, NEG_INF))
    rank = jnp.zeros((n_slc, tq), F32)
    for c in range(n_slc):
        row = imp[c:c + 1, :]
        ahead = (row > imp) | ((row == imp) & (blk > c))
        rank = rank + jnp.where(ahead, 1.0, 0.0)
    sel = jnp.where(rank < float(min(SLC_TOPN, n_slc)), 1.0, 0.0).astype(BF16)

    def flash(k_ref, v_ref, n_tiles, tile_of, penalty_of):
        m_ref[...] = jnp.full(m_ref.shape, NEG_INF, F32)
        l_ref[...] = jnp.zeros(l_ref.shape, F32)
        acc_ref[...] = jnp.zeros(acc_ref.shape, F32)

        def body(j, carry):
            kt = i - j
            k0 = pl.multiple_of(kt * tq, tq)
            s = jnp.dot(k_ref[0, 0, pl.ds(k0, tq), :], q_t, preferred_element_type=F32)
            s = s + bd_ref[0, tile_of(j)]
            pen = penalty_of(kt)
            if pen is not None:
                s = s + jnp.concatenate([pen] * A_HPG, axis=1)
            m_old = m_ref[...]
            m_new = jnp.maximum(m_old, jnp.max(s, axis=0, keepdims=True))
            alpha = jnp.exp(m_old - m_new)
            p = jnp.exp(s - m_new)
            l_ref[...] = alpha * l_ref[...] + jnp.sum(p, axis=0, keepdims=True)
            acc_ref[...] = alpha * acc_ref[...] + jnp.dot(v_ref[0, 0, kt], p.astype(BF16),
                                                          preferred_element_type=F32)
            m_ref[...] = m_new
            return carry

        lax.fori_loop(0, n_tiles, body, 0)
        return acc_ref[...] * (1.0 / l_ref[...])

    def sel_penalty(kt):
        kb = (kt * tq + lax.broadcasted_iota(jnp.int32, (tq, n_slc), 0)) // SLC_BLOCK
        expand = jnp.where(kb == lax.broadcasted_iota(jnp.int32, (tq, n_slc), 1), 1.0, 0.0).astype(BF16)
        chosen = jnp.dot(expand, sel, preferred_element_type=F32)
        return (chosen - 1.0) * FORCE_SCORE

    out_s = flash(ks_ref, vs_ref, i + 1, lambda j: jnp.minimum(j, 2), sel_penalty)

    wt = WINDOW // tq
    out_w = flash(kw_ref, vw_ref, jnp.minimum(i, wt) + 1,
                  lambda j: jnp.where(j == wt, N_BIAS_TILES - 1, jnp.minimum(j, 2)), lambda kt: None)

    blocks = []
    for h in range(A_HPG):
        cols = slice(h * tq, (h + 1) * tq)
        blocks.append(gates[h:h + 1, :] * out_c[:, cols]
                      + gates[A_HPG + h:A_HPG + h + 1, :] * out_s[:, cols]
                      + gates[2 * A_HPG + h:2 * A_HPG + h + 1, :] * out_w[:, cols])
    o_ref[0] = jnp.concatenate(blocks, axis=0).T


def _attention(q_t, kc, vc_t, ks, vs_t, kw, vw_t, bias_c, bias_d, gates_t):
    bsz, _, nt, _, _ = q_t.shape
    s = ks.shape[2]
    n_cmp = kc.shape[2]
    k_spec = pl.BlockSpec((1, 1, s, A_HEAD_DIM), lambda b, g, i: (b, g, 0, 0))
    vt_spec = pl.BlockSpec((1, 1, nt, A_HEAD_DIM, TQ), lambda b, g, i: (b, g, 0, 0, 0))
    return pl.pallas_call(
        _attn_kernel,
        grid=(bsz, A_KV_GROUPS, nt),
        in_specs=[pl.BlockSpec((1, 1, 1, A_HEAD_DIM, A_HPG * TQ), lambda b, g, i: (b, g, i, 0, 0)),
                  pl.BlockSpec((1, 1, n_cmp, A_HEAD_DIM), lambda b, g, i: (b, g, 0, 0)),
                  pl.BlockSpec((1, 1, A_HEAD_DIM, n_cmp), lambda b, g, i: (b, g, 0, 0)),
                  k_spec, vt_spec, k_spec, vt_spec,
                  pl.BlockSpec((1, 1, n_cmp, A_HPG * TQ), lambda b, g, i: (i, g, 0, 0)),
                  pl.BlockSpec((1, N_BIAS_TILES, TQ, A_HPG * TQ), lambda b, g, i: (g, 0, 0, 0)),
                  pl.BlockSpec((1, 1, 1, GATE_ROWS, TQ), lambda b, g, i: (b, g, i, 0, 0))],
        out_specs=pl.BlockSpec((1, TQ, A_HPG * A_HEAD_DIM), lambda b, g, i: (b, i, g)),
        out_shape=jax.ShapeDtypeStruct((bsz, s, A_WIDTH), F32),
        scratch_shapes=[pltpu.VMEM((1, A_HPG * TQ), F32),
                        pltpu.VMEM((1, A_HPG * TQ), F32),
                        pltpu.VMEM((A_HEAD_DIM, A_HPG * TQ), F32)],
        compiler_params=pltpu.CompilerParams(dimension_semantics=("parallel", "parallel", "parallel")),
        name="attn",
    )(q_t, kc, vc_t, ks, vs_t, kw, vw_t, bias_c, bias_d, gates_t)


def _rwkv_kernel(c_ref, mu_ref, w0_ref, wl_ref, a0_ref, al_ref, kk_ref, ka_ref, rk_ref, lw_ref, lb_ref,
                 o_ref, state_ref, prev_ref):
    cc = pl.program_id(1)
    n = B_HEAD_DIM
    nb, csz = c_ref.shape[0], c_ref.shape[1]

    @pl.when(cc == 0)
    def _():
        state_ref[...] = jnp.zeros(state_ref.shape, F32)
        prev_ref[...] = jnp.zeros(prev_ref.shape, F32)

    ti = lax.broadcasted_iota(jnp.int32, (csz, csz), 0)
    si = lax.broadcasted_iota(jnp.int32, (csz, csz), 1)
    lower = si <= ti
    strict = si < ti
    tri = jnp.where(lower, 1.0, 0.0).astype(BF16)
    eye = jnp.where(si == ti, 1.0, 0.0)

    chains = []
    for bi in range(nb):
        p = c_ref[bi]
        row = lax.broadcasted_iota(jnp.int32, p.shape, 0)
        prev = jnp.where(row == 0, prev_ref[bi, 0:1, :], pltpu.roll(p, 1, axis=0))
        prev_ref[bi, 0:1, :] = p[csz - 1:csz, :]
        x = p + (prev - p) * mu_ref[...]
        r = x[:, 0:B_WIDTH]
        k = x[:, B_WIDTH:2 * B_WIDTH]
        v = x[:, 2 * B_WIDTH:3 * B_WIDTH]
        wd = x[:, 3 * B_WIDTH:3 * B_WIDTH + DECAY_LORA]
        ad = x[:, 3 * B_WIDTH + DECAY_LORA:3 * B_WIDTH + DECAY_LORA + ICLR_LORA]

        z = -(w0_ref[...] + _dot(jnp.tanh(wd), wl_ref[...]))
        softplus = jnp.maximum(z, 0.0) + jnp.log(1.0 + jnp.exp(-jnp.abs(z)))
        ld = -jnp.exp(-softplus - 0.5)
        a = _sigmoid(a0_ref[...] + _dot(ad, al_ref[...]))
        kk = k * kk_ref[...]
        k_mod = k * (1.0 + (a - 1.0) * ka_ref[...])
        rkr = r * k_mod * rk_ref[...]

        ld_hi, ld_lo = _split2(ld)
        cum = jnp.dot(tri, ld_hi, preferred_element_type=F32) + jnp.dot(tri, ld_lo, preferred_element_type=F32)
        g_inc = jnp.exp(cum)
        g_exc = jnp.exp(cum - ld)
        g_inv = jnp.exp(-cum)
        g_end = jnp.exp(cum[csz - 1:csz, :] - cum)
        g_all = g_inc[csz - 1:csz, :]

        for h in range(B_HEADS):
            sl = slice(h * n, (h + 1) * n)
            kk_h = kk[:, sl]
            kk_h = kk_h * lax.rsqrt(jnp.maximum(jnp.sum(kk_h * kk_h, axis=-1, keepdims=True), 1e-24))
            b_h = kk_h * a[:, sl]
            ch = dict(
                idx=bi * B_HEADS + h,
                v=v[:, sl],
                lhs=jnp.concatenate([-kk_h * g_exc[:, sl], r[:, sl] * g_inc[:, sl]], axis=0).astype(BF16),
                bt=(b_h * g_inv[:, sl]).astype(BF16),
                kt=(k_mod[:, sl] * g_inv[:, sl]).astype(BF16),
                bk=jnp.concatenate([b_h * g_end[:, sl], k_mod[:, sl] * g_end[:, sl]], axis=0).astype(BF16),
                g_all=g_all[:, sl],
                bonus=jnp.sum(rkr[:, sl], axis=-1, keepdims=True) * v[:, sl],
            )
            chains.append(ch)

    for ch in chains:
        xb = _dot_nt(ch["lhs"], ch["bt"])
        xk = _dot_nt(ch["lhs"], ch["kt"])
        ch["a_ab"] = jnp.where(strict, xb[:csz], 0.0)
        ch["a_ak"] = jnp.where(strict, xk[:csz], 0.0).astype(BF16)
        ch["m_rb"] = jnp.where(lower, xb[csz:], 0.0).astype(BF16)
        ch["m_rk"] = jnp.where(lower, xk[csz:], 0.0).astype(BF16)
    for ch in chains:
        vb = ch["v"].astype(BF16)
        ch["akv"] = _dot(ch["a_ak"], vb)
        ch["mrkv"] = _dot(ch["m_rk"], vb)
        ch["tinv"] = eye + ch["a_ab"]
        ch["pw"] = ch["a_ab"].astype(BF16)
    for _ in range(int(math.log2(csz)) - 1):
        for ch in chains:
            pw = _dot(ch["pw"], ch["pw"]).astype(BF16)
            ch["pw"] = pw
        for ch in chains:
            ch["tinv"] = ch["tinv"] + _dot(ch["tinv"], ch["pw"])
    for ch in chains:
        ch["s0"] = state_ref[ch["idx"]]
        ch["as0"] = _dot_nt(ch["lhs"], ch["s0"])
    for ch in chains:
        ch["u"] = _dot(ch["tinv"], ch["as0"][:csz] + ch["akv"])
    outs = []
    for ch in chains:
        u = ch["u"]
        y = ch["as0"][csz:] + _dot(ch["m_rb"], u) + ch["mrkv"]
        uv = jnp.concatenate([u, ch["v"]], axis=0)
        state_ref[ch["idx"]] = ch["s0"] * ch["g_all"] + _dot_tn(uv, ch["bk"])
        mean = jnp.mean(y, axis=-1, keepdims=True)
        yc = y - mean
        var = jnp.mean(yc * yc, axis=-1, keepdims=True)
        outs.append(yc * lax.rsqrt(var + LNX_EPS))
    for bi in range(nb):
        yn = jnp.concatenate(outs[bi * B_HEADS:(bi + 1) * B_HEADS], axis=-1)
        bonus = jnp.concatenate([ch["bonus"] for ch in chains[bi * B_HEADS:(bi + 1) * B_HEADS]], axis=-1)
        o_ref[bi] = yn * lw_ref[...] + lb_ref[...] + bonus


RWKV_NB = 4


def _rwkv(cols_rwkv, mu, w0, wl, a0, al, k_k, k_a, r_k, ln_w, ln_b):
    bsz, s, _ = cols_rwkv.shape
    nb = RWKV_NB if bsz % RWKV_NB == 0 else 1
    const = lambda b, c: (0, 0)
    vec = pl.BlockSpec((1, B_WIDTH), const)
    return pl.pallas_call(
        _rwkv_kernel,
        grid=(bsz // nb, s // CHUNK),
        in_specs=[pl.BlockSpec((nb, CHUNK, RWKV_COLS), lambda b, c: (b, c, 0)),
                  pl.BlockSpec((1, RWKV_COLS), const),
                  vec, pl.BlockSpec((DECAY_LORA, B_WIDTH), const),
                  vec, pl.BlockSpec((ICLR_LORA, B_WIDTH), const),
                  vec, vec, vec, vec, vec],
        out_specs=pl.BlockSpec((nb, CHUNK, B_WIDTH), lambda b, c: (b, c, 0)),
        out_shape=jax.ShapeDtypeStruct((bsz, s, B_WIDTH), F32),
        scratch_shapes=[pltpu.VMEM((nb * B_HEADS, B_HEAD_DIM, B_HEAD_DIM), F32),
                        pltpu.VMEM((nb, 8, RWKV_COLS), F32)],
        compiler_params=pltpu.CompilerParams(dimension_semantics=("parallel", "arbitrary")),
        name="rwkv",
    )(cols_rwkv, mu, w0, wl, a0, al, k_k, k_a, r_k, ln_w, ln_b)


def _final_kernel(x_ref, ya_ref, yb_ref, cf_ref, gate_ref, wa_ref, wb_ref, wo_ref, o_ref):
    a_silu = cf_ref[0, :, 0:A_WIDTH]
    b_silu = cf_ref[0, :, A_WIDTH:A_WIDTH + B_WIDTH]
    merge_a = cf_ref[0, :, A_WIDTH + B_WIDTH:A_WIDTH + B_WIDTH + D_MODEL]
    merge_b = cf_ref[0, :, A_WIDTH + B_WIDTH + D_MODEL:A_WIDTH + B_WIDTH + 2 * D_MODEL]
    ya = ya_ref[0] * (a_silu * _sigmoid(a_silu))
    yb = yb_ref[0] * (b_silu * _sigmoid(b_silu))
    merged = _sigmoid(merge_a) * _dot(ya, wa_ref[...]) + _sigmoid(merge_b) * _dot(yb, wb_ref[...])
    o_ref[0] = x_ref[0] + gate_ref[0] * _dot(merged, wo_ref[...])


def _final(x, y_a, y_b, cols_fin, gate, w_out_a, w_out_b, w_o, tm=256):
    bsz, s, _ = x.shape
    const = lambda b, i: (0, 0)
    row = lambda w: pl.BlockSpec((1, tm, w), lambda b, i: (b, i, 0))
    return pl.pallas_call(
        _final_kernel,
        grid=(bsz, s // tm),
        in_specs=[row(D_MODEL), row(A_WIDTH), row(B_WIDTH), row(FIN_COLS),
                  pl.BlockSpec((1, 1, D_MODEL), lambda b, i: (b, 0, 0)),
                  pl.BlockSpec((A_WIDTH, D_MODEL), const),
                  pl.BlockSpec((B_WIDTH, D_MODEL), const),
                  pl.BlockSpec((D_MODEL, D_MODEL), const)],
        out_specs=row(D_MODEL),
        out_shape=jax.ShapeDtypeStruct((bsz, s, D_MODEL), F32),
        compiler_params=pltpu.CompilerParams(dimension_semantics=("parallel", "parallel"),
                                             vmem_limit_bytes=VMEM_LIMIT),
        name="final",
    )(x, y_a, y_b, cols_fin, gate, w_out_a, w_out_b, w_o)


def _split_w_in(w_in):
    nsa_in = 2 * A_WIDTH + 6 * A_KV_WIDTH + 3 * A_HEADS
    o_gate = A_WIDTH + 6 * A_KV_WIDTH
    o_asilu = o_gate + 3 * A_HEADS
    o_shift = nsa_in
    o_rest = nsa_in + RWKV_COLS
    gate_w = w_in[:, o_gate:o_asilu].reshape(D_MODEL, 3, A_KV_GROUPS, A_HPG)
    gate_w = gate_w.transpose(0, 2, 1, 3).reshape(D_MODEL, A_KV_GROUPS, 3 * A_HPG)
    gate_w = jnp.pad(gate_w, ((0, 0), (0, 0), (0, LANES - 3 * A_HPG))).reshape(D_MODEL, GATE_PAD)
    w_nsa = jnp.concatenate([w_in[:, :o_gate], gate_w], axis=1)
    w_fin = jnp.concatenate([w_in[:, o_asilu:o_shift], w_in[:, o_rest:]], axis=1)
    w_rwkv = w_in[:, o_shift:o_rest]
    return w_nsa.astype(BF16), w_fin.astype(BF16), w_rwkv.astype(BF16)


def _layer(x, c, rel_bias, w_ada, b_ada, norm_gain, w_in, q_norm_gain, k_norm_gain,
           cmp_pos_k, cmp_pos_v, cmp_k_w1, cmp_k_w2, cmp_v_w1, cmp_v_w2,
           shift_mu, w0, w_lora_up, a0, a_lora_up, k_k, k_a, r_k, ln_x_w, ln_x_b,
           w_out_a, w_out_b, w_o):
    bsz, s, _ = x.shape
    assert s % (2 * TQ) == 0 and s // CMP_STRIDE == LANES
    n16 = s // CMP_STRIDE
    mod = _ada(c, w_ada, b_ada)
    w_nsa, w_fin, w_rwkv = _split_w_in(w_in)
    cols_nsa, cols_fin, cols_rwkv = _proj(x, mod, norm_gain, w_nsa, w_fin, w_rwkv)

    scale = A_HEAD_DIM ** -0.5
    qg = (jnp.tile(q_norm_gain, A_HEADS) * scale).reshape(1, A_WIDTH)
    ksg = jnp.tile(k_norm_gain[1], A_KV_GROUPS).reshape(1, A_KV_WIDTH)
    kwg = jnp.tile(k_norm_gain[2], A_KV_GROUPS).reshape(1, A_KV_WIDTH)
    seg = np.arange(A_WIDTH) // A_HEAD_DIM
    bd = jnp.asarray((seg[:, None] == seg[None, :]).astype(np.float32) / A_HEAD_DIM, BF16)
    q_t, ks, vs_t, kw, vw_t, gates_t = _nsaprep(cols_nsa, qg, ksg, kwg, bd)

    def blocks16(t):
        t = t.reshape(bsz, n16, CMP_STRIDE, A_KV_GROUPS, A_HEAD_DIM).transpose(0, 3, 1, 2, 4)
        return t.reshape(bsz, A_KV_GROUPS, n16, CMP_STRIDE * A_HEAD_DIM)

    zk = blocks16(cols_nsa[:, :, A_WIDTH:A_WIDTH + A_KV_WIDTH])
    zv = blocks16(cols_nsa[:, :, A_WIDTH + A_KV_WIDTH:A_WIDTH + 2 * A_KV_WIDTH])
    kc, vc_t = _compress(zk, zv,
                         cmp_pos_k.reshape(2, CMP_STRIDE * A_HEAD_DIM), cmp_pos_v.reshape(2, CMP_STRIDE * A_HEAD_DIM),
                         cmp_k_w1.astype(BF16), cmp_k_w2.astype(BF16), cmp_v_w1.astype(BF16), cmp_v_w2.T.astype(BF16),
                         k_norm_gain[0].reshape(1, A_HEAD_DIM))
    bias_c, bias_d = _bias_tables(rel_bias, s, n16)
    y_a = _attention(q_t, kc, vc_t, ks, vs_t, kw, vw_t, bias_c, bias_d, gates_t)

    vec = lambda t: t.reshape(1, -1)
    y_b = _rwkv(cols_rwkv, vec(shift_mu), vec(w0), w_lora_up.astype(BF16), vec(a0), a_lora_up.astype(BF16),
                vec(k_k), vec(k_a), vec(r_k), vec(ln_x_w), vec(ln_x_b))

    gate = mod[:, 2 * D_MODEL:].reshape(bsz, 1, D_MODEL)
    return _final(x, y_a, y_b, cols_fin, gate, w_out_a.astype(BF16), w_out_b.astype(BF16), w_o.astype(BF16))


def kernel(x, c, w_ada, b_ada, norm_gain, w_in, q_norm_gain, k_norm_gain, cmp_pos_k, cmp_pos_v, cmp_k_w1, cmp_k_w2, cmp_v_w1, cmp_v_w2, rel_bias, shift_mu, w0, w_lora_up, a0, a_lora_up, k_k, k_a, r_k, ln_x_w, ln_x_b, w_out_a, w_out_b, w_o):
    for l in range(w_in.shape[0]):
        x = _layer(x, c, rel_bias, w_ada[l], b_ada[l], norm_gain[l], w_in[l], q_norm_gain[l], k_norm_gain[l],
                   cmp_pos_k[l], cmp_pos_v[l], cmp_k_w1[l], cmp_k_w2[l], cmp_v_w1[l], cmp_v_w2[l],
                   shift_mu[l], w0[l], w_lora_up[l], a0[l], a_lora_up[l], k_k[l], k_a[l], r_k[l],
                   ln_x_w[l], ln_x_b[l], w_out_a[l], w_out_b[l], w_o[l])
    return x
```

```python
import functools
import math

import numpy as np
import jax
import jax.numpy as jnp
from jax import lax
from jax.experimental import pallas as pl
from jax.experimental.pallas import tpu as pltpu

F32 = jnp.float32
BF16 = jnp.bfloat16

D_MODEL = 1024
A_HEADS = 8
A_HEAD_DIM = 64
A_KV_GROUPS = 2
A_HPG = A_HEADS // A_KV_GROUPS
A_WIDTH = A_HEADS * A_HEAD_DIM
A_KV_WIDTH = A_KV_GROUPS * A_HEAD_DIM
CMP_BLOCK = 32
CMP_STRIDE = 16
CMP_HIDDEN = 256
SLC_BLOCK = 64
SLC_TOPN = 16
WINDOW = 512
B_HEADS = 8
B_HEAD_DIM = 64
B_WIDTH = B_HEADS * B_HEAD_DIM
DECAY_LORA = 64
ICLR_LORA = 64
LNX_EPS = 64e-5
REL_BUCKETS = 32
REL_MAX_EXACT = 16
REL_MAX_DIST = 128
NORM_EPS = 1e-6
NEG_INF = -1e30
FORCE_SCORE = 1e30

LANES = 128
TQ = 128
CHUNK = 64
GATE_PAD = 2 * LANES
GATE_ROWS = 16
NSA_COLS = A_WIDTH + 6 * A_KV_WIDTH + GATE_PAD
FIN_COLS = A_WIDTH + B_WIDTH + 2 * D_MODEL
RWKV_COLS = 3 * B_WIDTH + DECAY_LORA + ICLR_LORA
VMEM_LIMIT = 56 * 1024 * 1024


def _dot(a, b):
    return jnp.dot(a.astype(BF16), b.astype(BF16), preferred_element_type=F32)


def _dot_nt(a, b):
    return lax.dot_general(a.astype(BF16), b.astype(BF16), (((1,), (1,)), ((), ())),
                           preferred_element_type=F32)


def _dot_tn(a, b):
    return lax.dot_general(a.astype(BF16), b.astype(BF16), (((0,), (0,)), ((), ())),
                           preferred_element_type=F32)


def _split2(x):
    hi = x.astype(BF16)
    lo = (x - hi.astype(F32)).astype(BF16)
    return hi, lo


def _split3(x):
    h1 = x.astype(BF16)
    r1 = x - h1.astype(F32)
    h2 = r1.astype(BF16)
    h3 = (r1 - h2.astype(F32)).astype(BF16)
    return h1, h2, h3


def _sigmoid(x):
    return 1.0 / (1.0 + jnp.exp(-x))


def _bucket_thresholds():
    n = np.arange(0, 4096)
    nf = np.maximum(n, REL_MAX_EXACT).astype(np.float64)
    val = np.log(nf / REL_MAX_EXACT) / math.log(REL_MAX_DIST / REL_MAX_EXACT) * (REL_BUCKETS - REL_MAX_EXACT)
    frac = np.abs(val - np.round(val))
    assert np.all((frac > 1e-4) | (n <= REL_MAX_EXACT) | (n >= REL_MAX_DIST))
    large = REL_MAX_EXACT + np.floor(val + 1e-9).astype(np.int64)
    bucket = np.where(n < REL_MAX_EXACT, n, np.minimum(large, REL_BUCKETS - 1))
    return [int(np.argmax(bucket >= j)) for j in range(REL_BUCKETS)]


_BUCKET_TH = _bucket_thresholds()


def _bias_from_dist(dist, tbl_ref, head):
    val = jnp.full(dist.shape, tbl_ref[0, head], F32)
    for j in range(1, REL_BUCKETS):
        val = jnp.where(dist >= _BUCKET_TH[j], tbl_ref[j, head], val)
    return val


def _ada_kernel(c_ref, w_ref, b_ref, o_ref):
    c = c_ref[...]
    o_ref[...] = _dot(c * _sigmoid(c), w_ref[...]) + b_ref[...]


def _ada(c, w_ada, b_ada):
    bsz = c.shape[0]
    return pl.pallas_call(
        _ada_kernel,
        grid=(3,),
        in_specs=[pl.BlockSpec((bsz, D_MODEL), lambda j: (0, 0)),
                  pl.BlockSpec((D_MODEL, D_MODEL), lambda j: (0, j)),
                  pl.BlockSpec((1, D_MODEL), lambda j: (0, j))],
        out_specs=pl.BlockSpec((bsz, D_MODEL), lambda j: (0, j)),
        out_shape=jax.ShapeDtypeStruct((bsz, 3 * D_MODEL), F32),
        name="ada",
    )(c, w_ada, b_ada.reshape(1, 3 * D_MODEL))


def _proj_kernel(x_ref, mod_ref, g_ref, wn_ref, wf_ref, wr_ref, on_ref, of_ref, or_ref):
    x = x_ref[0]
    ms = jnp.mean(x * x, axis=-1, keepdims=True)
    y = x * lax.rsqrt(ms + NORM_EPS) * g_ref[...]
    mod = mod_ref[0]
    h = (y * (1.0 + mod[:, D_MODEL:2 * D_MODEL]) + mod[:, :D_MODEL]).astype(BF16)
    on_ref[0] = jnp.dot(h, wn_ref[...], preferred_element_type=F32)
    of_ref[0] = jnp.dot(h, wf_ref[...], preferred_element_type=F32)
    or_ref[0] = jnp.dot(h, wr_ref[...], preferred_element_type=F32)


def _proj(x, mod, norm_gain, w_nsa, w_fin, w_rwkv, tm=256):
    bsz, s, _ = x.shape
    const = lambda b, i: (0, 0)
    return pl.pallas_call(
        _proj_kernel,
        grid=(bsz, s // tm),
        in_specs=[pl.BlockSpec((1, tm, D_MODEL), lambda b, i: (b, i, 0)),
                  pl.BlockSpec((1, 1, 3 * D_MODEL), lambda b, i: (b, 0, 0)),
                  pl.BlockSpec((1, D_MODEL), const),
                  pl.BlockSpec((D_MODEL, NSA_COLS), const),
                  pl.BlockSpec((D_MODEL, FIN_COLS), const),
                  pl.BlockSpec((D_MODEL, RWKV_COLS), const)],
        out_specs=[pl.BlockSpec((1, tm, NSA_COLS), lambda b, i: (b, i, 0)),
                   pl.BlockSpec((1, tm, FIN_COLS), lambda b, i: (b, i, 0)),
                   pl.BlockSpec((1, tm, RWKV_COLS), lambda b, i: (b, i, 0))],
        out_shape=[jax.ShapeDtypeStruct((bsz, s, NSA_COLS), F32),
                   jax.ShapeDtypeStruct((bsz, s, FIN_COLS), F32),
                   jax.ShapeDtypeStruct((bsz, s, RWKV_COLS), F32)],
        compiler_params=pltpu.CompilerParams(dimension_semantics=("parallel", "parallel"),
                                             vmem_limit_bytes=VMEM_LIMIT),
        name="proj",
    )(x, mod.reshape(bsz, 1, 3 * D_MODEL), norm_gain.reshape(1, D_MODEL), w_nsa, w_fin, w_rwkv)


def _seg_mean(x2, bd):
    hi, lo = _split2(x2)
    return (jnp.dot(hi, bd, preferred_element_type=F32) + jnp.dot(lo, bd, preferred_element_type=F32))


def _nsaprep_kernel(c_ref, qg_ref, ksg_ref, kwg_ref, bd_ref, q_ref, ks_ref, vs_ref, kw_ref, vw_ref, gt_ref):
    bd = bd_ref[...]
    q = c_ref[0, :, 0:A_WIDTH]
    qn_t = (q * lax.rsqrt(_seg_mean(q * q, bd) + NORM_EPS) * qg_ref[...]).T
    off = A_WIDTH + 2 * A_KV_WIDTH
    k_slc = c_ref[0, :, off:off + A_KV_WIDTH]
    v_slc = c_ref[0, :, off + A_KV_WIDTH:off + 2 * A_KV_WIDTH]
    k_win = c_ref[0, :, off + 2 * A_KV_WIDTH:off + 3 * A_KV_WIDTH]
    v_win = c_ref[0, :, off + 3 * A_KV_WIDTH:off + 4 * A_KV_WIDTH]
    bd_kv = bd[:A_KV_WIDTH, :A_KV_WIDTH]
    ksn = k_slc * lax.rsqrt(_seg_mean(k_slc * k_slc, bd_kv) + NORM_EPS) * ksg_ref[...]
    kwn = (k_win * lax.rsqrt(_seg_mean(k_win * k_win, bd_kv) + NORM_EPS) * kwg_ref[...]).astype(BF16)
    vs_t = v_slc.T.astype(BF16)
    vw_t = v_win.T.astype(BF16)
    tq = c_ref.shape[1]
    lane = lax.broadcasted_iota(jnp.int32, (tq, LANES), 1)
    blk = (pl.program_id(1) * tq + lax.broadcasted_iota(jnp.int32, (tq, LANES), 0)) // SLC_BLOCK
    onehot = jnp.where(lane - A_HEAD_DIM == blk, 1.0, 0.0)
    gate0 = A_WIDTH + 6 * A_KV_WIDTH
    for g in range(A_KV_GROUPS):
        heads = [qn_t[(g * A_HPG + h) * A_HEAD_DIM:(g * A_HPG + h + 1) * A_HEAD_DIM, :] for h in range(A_HPG)]
        q_ref[0, g, 0] = jnp.concatenate(heads, axis=1).astype(BF16)
        sl = slice(g * A_HEAD_DIM, (g + 1) * A_HEAD_DIM)
        k_g = ksn if g == 0 else pltpu.roll(ksn, A_HEAD_DIM, axis=1)
        ks_ref[0, g] = jnp.where(lane < A_HEAD_DIM, k_g, onehot).astype(BF16)
        kw_ref[0, g] = kwn[:, sl]
        vs_ref[0, g, 0] = vs_t[sl, :]
        vw_ref[0, g, 0] = vw_t[sl, :]
        gates_t = _sigmoid(c_ref[0, :, gate0 + g * LANES:gate0 + (g + 1) * LANES]).T
        gt_ref[0, g, 0] = gates_t[0:GATE_ROWS, :]


def _nsaprep(cols_nsa, qg, ksg, kwg, bd):
    bsz, s, _ = cols_nsa.shape
    nt = s // TQ
    const = lambda b, i: (0, 0)
    assert A_HEAD_DIM + s // SLC_BLOCK <= LANES
    k_spec = pl.BlockSpec((1, A_KV_GROUPS, TQ, A_HEAD_DIM), lambda b, i: (b, 0, i, 0))
    k_shape = jax.ShapeDtypeStruct((bsz, A_KV_GROUPS, s, A_HEAD_DIM), BF16)
    ka_spec = pl.BlockSpec((1, A_KV_GROUPS, TQ, LANES), lambda b, i: (b, 0, i, 0))
    ka_shape = jax.ShapeDtypeStruct((bsz, A_KV_GROUPS, s, LANES), BF16)
    vt_spec = pl.BlockSpec((1, A_KV_GROUPS, 1, A_HEAD_DIM, TQ), lambda b, i: (b, 0, i, 0, 0))
    vt_shape = jax.ShapeDtypeStruct((bsz, A_KV_GROUPS, nt, A_HEAD_DIM, TQ), BF16)
    return pl.pallas_call(
        _nsaprep_kernel,
        grid=(bsz, nt),
        in_specs=[pl.BlockSpec((1, TQ, NSA_COLS), lambda b, i: (b, i, 0)),
                  pl.BlockSpec((1, A_WIDTH), const),
                  pl.BlockSpec((1, A_KV_WIDTH), const),
                  pl.BlockSpec((1, A_KV_WIDTH), const),
                  pl.BlockSpec((A_WIDTH, A_WIDTH), const)],
        out_specs=[pl.BlockSpec((1, A_KV_GROUPS, 1, A_HEAD_DIM, A_HPG * TQ), lambda b, i: (b, 0, i, 0, 0)),
                   ka_spec, vt_spec, k_spec, vt_spec,
                   pl.BlockSpec((1, A_KV_GROUPS, 1, GATE_ROWS, TQ), lambda b, i: (b, 0, i, 0, 0))],
        out_shape=[jax.ShapeDtypeStruct((bsz, A_KV_GROUPS, nt, A_HEAD_DIM, A_HPG * TQ), BF16),
                   ka_shape, vt_shape, k_shape, vt_shape,
                   jax.ShapeDtypeStruct((bsz, A_KV_GROUPS, nt, GATE_ROWS, TQ), F32)],
        compiler_params=pltpu.CompilerParams(dimension_semantics=("parallel", "parallel")),
        name="nsaprep",
    )(cols_nsa, qg, ksg, kwg, bd)


def _compress_kernel(zk_ref, zv_ref, pk_ref, pv_ref, w1k_ref, w2k_ref, w1v_ref, w2v_ref, kg_ref, kc_ref, vc_ref):
    half = (CMP_BLOCK // 2) * A_HEAD_DIM
    n16 = zk_ref.shape[2]

    def hidden(z, pos_ref, w1_ref):
        top = _dot(z + pos_ref[0:1, :], w1_ref[0:half, :])
        bot = _dot(z + pos_ref[1:2, :], w1_ref[half:2 * half, :])
        return jax.nn.gelu(top + pltpu.roll(bot, n16 - 1, axis=0), approximate=True)

    for g in range(A_KV_GROUPS):
        kc = _dot(hidden(zk_ref[0, g], pk_ref, w1k_ref), w2k_ref[...])
        ms = jnp.mean(kc * kc, axis=-1, keepdims=True)
        kc_ref[0, g] = (kc * lax.rsqrt(ms + NORM_EPS) * kg_ref[...]).astype(BF16)
        vc_ref[0, g] = _dot_nt(w2v_ref[...], hidden(zv_ref[0, g], pv_ref, w1v_ref)).astype(BF16)


def _compress(zk, zv, pk, pv, w1k, w2k, w1v, w2v_t, kg):
    bsz, _, n16, zw = zk.shape
    const = lambda b: (0, 0)
    z_spec = pl.BlockSpec((1, A_KV_GROUPS, n16, zw), lambda b: (b, 0, 0, 0))
    return pl.pallas_call(
        _compress_kernel,
        grid=(bsz,),
        in_specs=[z_spec, z_spec,
                  pl.BlockSpec((2, zw), const), pl.BlockSpec((2, zw), const),
                  pl.BlockSpec((2 * zw, CMP_HIDDEN), const), pl.BlockSpec((CMP_HIDDEN, A_HEAD_DIM), const),
                  pl.BlockSpec((2 * zw, CMP_HIDDEN), const), pl.BlockSpec((A_HEAD_DIM, CMP_HIDDEN), const),
                  pl.BlockSpec((1, A_HEAD_DIM), const)],
        out_specs=[pl.BlockSpec((1, A_KV_GROUPS, n16, A_HEAD_DIM), lambda b: (b, 0, 0, 0)),
                   pl.BlockSpec((1, A_KV_GROUPS, A_HEAD_DIM, n16), lambda b: (b, 0, 0, 0))],
        out_shape=[jax.ShapeDtypeStruct((bsz, A_KV_GROUPS, n16, A_HEAD_DIM), BF16),
                   jax.ShapeDtypeStruct((bsz, A_KV_GROUPS, A_HEAD_DIM, n16), BF16)],
        compiler_params=pltpu.CompilerParams(dimension_semantics=("parallel",)),
        name="compress",
    )(zk, zv, pk, pv, w1k, w2k, w1v, w2v_t, kg)


TILE_FAR, TILE_EDGE, TILE_MASKED, N_BIAS_TILES = 2, 3, 4, 5
SUB = 4


def _bias_cmp_kernel(tbl_ref, o_ref):
    i = pl.program_id(0)
    g = pl.program_id(1)
    n_cmp = o_ref.shape[2]
    n = lax.broadcasted_iota(jnp.int32, (n_cmp, TQ), 0)
    q = lax.broadcasted_iota(jnp.int32, (n_cmp, TQ), 1)
    dist = i * TQ + q - (n * CMP_STRIDE + CMP_BLOCK - 1)
    for h in range(A_HPG):
        bias = _bias_from_dist(dist, tbl_ref, g * A_HPG + h)
        o_ref[0, 0, :, h * TQ:(h + 1) * TQ] = jnp.where(dist >= 0, bias, NEG_INF)


def _bias_toeplitz_kernel(tbl_ref, o_ref):
    g = pl.program_id(0)
    r = pl.program_id(1)
    off = jnp.where(r == TILE_EDGE, WINDOW // TQ, jnp.where(r == TILE_MASKED, -2, r))
    k = lax.broadcasted_iota(jnp.int32, (TQ, TQ), 0)
    q = lax.broadcasted_iota(jnp.int32, (TQ, TQ), 1)
    dist = off * TQ + q - k
    valid = (dist >= 0) & (dist < WINDOW)
    for h in range(A_HPG):
        bias = _bias_from_dist(dist, tbl_ref, g * A_HPG + h)
        o_ref[0, 0, :, h * TQ:(h + 1) * TQ] = jnp.where(valid, bias, NEG_INF)


def _bias_tables(rel_bias, s, n_cmp):
    smem = pl.BlockSpec(memory_space=pltpu.SMEM)
    nt = s // TQ
    bias_c = pl.pallas_call(
        _bias_cmp_kernel,
        grid=(nt, A_KV_GROUPS),
        in_specs=[smem],
        out_specs=pl.BlockSpec((1, 1, n_cmp, A_HPG * TQ), lambda i, g: (i, g, 0, 0)),
        out_shape=jax.ShapeDtypeStruct((nt, A_KV_GROUPS, n_cmp, A_HPG * TQ), F32),
        name="bias_cmp",
    )(rel_bias)
    assert _BUCKET_TH[REL_BUCKETS - 1] <= TQ + 1 and WINDOW // TQ >= 3
    bias_d = pl.pallas_call(
        _bias_toeplitz_kernel,
        grid=(A_KV_GROUPS, N_BIAS_TILES),
        in_specs=[smem],
        out_specs=pl.BlockSpec((1, 1, TQ, A_HPG * TQ), lambda g, r: (g, r, 0, 0)),
        out_shape=jax.ShapeDtypeStruct((A_KV_GROUPS, N_BIAS_TILES, TQ, A_HPG * TQ), F32),
        name="bias_toeplitz",
    )(rel_bias)
    return bias_c, bias_d


def _attn_kernel(q_ref, kc_ref, vc_ref, ks_ref, vs_ref, kw_ref, vw_ref, bc_ref, bd_ref, gt_ref, o_ref,
                 m_ref, l_ref, acc_ref, part_ref):
    i = pl.program_id(2)
    tq = TQ
    n_cmp = kc_ref.shape[2]
    n_slc = ks_ref.shape[2] // SLC_BLOCK
    t0 = i * tq
    q_t = q_ref[0, 0, 0]
    gates = gt_ref[0, 0, 0]

    bias = bc_ref[0, 0]
    valid = bias > 0.5 * NEG_INF
    s = jnp.dot(kc_ref[0, 0], q_t, preferred_element_type=F32) + bias
    m = jnp.max(s, axis=0, keepdims=True)
    e = jnp.where(valid, jnp.exp(s - m), 0.0)
    l = jnp.sum(e, axis=0, keepdims=True)
    p = e * (1.0 / jnp.where(l > 0.0, l, 1.0))
    out_c = jnp.dot(vc_ref[0, 0], p.astype(BF16), preferred_element_type=F32)
    p_grp = sum(p[:, h * tq:(h + 1) * tq] for h in range(A_HPG))

    r1, r2 = SLC_BLOCK // CMP_STRIDE, CMP_BLOCK // CMP_STRIDE
    jj = lax.broadcasted_iota(jnp.int32, (n_slc, n_cmp), 0)
    nn = lax.broadcasted_iota(jnp.int32, (n_slc, n_cmp), 1)
    d = nn - r1 * jj
    cnt = jnp.zeros((n_slc, n_cmp), F32)
    for a in range(r1):
        for c in range(r2):
            cnt = cnt + jnp.where(d == a - c, 1.0, 0.0)
    cnt = cnt.astype(BF16)
    imp = sum(jnp.dot(cnt, part, preferred_element_type=F32) for part in _split3(p_grp))
    blk = lax.broadcasted_iota(jnp.int32, (n_slc, tq), 0)
    tpos = t0 + lax.broadcasted_iota(jnp.int32, (n_slc, tq), 1)
    cur = tpos // SLC_BLOCK
    forced = (blk == 0) | (blk == cur) | (blk == cur - 1)
    causal = blk * SLC_BLOCK <= tpos
    imp = jnp.where(forced, FORCE_SCORE, jnp.where(causal, imp, NEG_INF))
    rank = jnp.zeros((n_slc, tq), F32)
    for c in range(n_slc):
        row = imp[c:c + 1, :]
        ahead = (row > imp) | ((row == imp) & (blk > c))
        rank = rank + jnp.where(ahead, 1.0, 0.0)
    pen = jnp.where(rank < float(min(SLC_TOPN, n_slc)), 0.0, -FORCE_SCORE).astype(BF16)

    def scores(k_slab, q_mat, first_tile, n_sub, tile_index):
        s = jnp.dot(k_slab, q_mat, preferred_element_type=F32)
        parts = [s[t * tq:(t + 1) * tq] + bd_ref[0, tile_index(i - (first_tile + t))] for t in range(n_sub)]
        return jnp.concatenate(parts, axis=0)

    def values_t(v_ref, first_tile, n_sub):
        return jnp.concatenate([v_ref[0, 0, first_tile + t] for t in range(n_sub)], axis=1)

    q_aug = jnp.concatenate([q_t, jnp.concatenate([pen] * A_HPG, axis=1),
                             jnp.zeros((LANES - A_HEAD_DIM - n_slc, A_HPG * tq), BF16)], axis=0)
    sel_tile = lambda r: jnp.where(r < 0, TILE_MASKED, jnp.minimum(r, TILE_FAR))
    last = i // SUB

    def chunk(c, m_old, l_old, acc_old):
        first_tile = c * SUB
        k0 = pl.multiple_of(first_tile * tq, SUB * tq)
        s = scores(ks_ref[0, 0, pl.ds(k0, SUB * tq), :], q_aug, first_tile, SUB, sel_tile)
        m_new = jnp.max(s, axis=0, keepdims=True)
        if m_old is not None:
            m_new = jnp.maximum(m_old, m_new)
        p = jnp.exp(s - m_new)
        l_new = jnp.sum(p, axis=0, keepdims=True)
        acc_new = jnp.dot(values_t(vs_ref, first_tile, SUB), p.astype(BF16), preferred_element_type=F32)
        if m_old is not None:
            alpha = jnp.exp(m_old - m_new)
            l_new = alpha * l_old + l_new
            acc_new = alpha * acc_old + acc_new
        return m_new, l_new, acc_new

    m_ref[...], l_ref[...], acc_ref[...] = chunk(last, None, None, None)

    wt = WINDOW // tq
    first_w = jnp.maximum(i - wt, 0)
    win_tile = lambda r: jnp.where(r < 0, TILE_MASKED, jnp.where(r == wt, TILE_EDGE, jnp.minimum(r, TILE_FAR)))
    k0 = pl.multiple_of(first_w * tq, tq)
    s = scores(kw_ref[0, 0, pl.ds(k0, (wt + 1) * tq), :], q_t, first_w, wt + 1, win_tile)
    p = jnp.exp(s - jnp.max(s, axis=0, keepdims=True))
    out_w = (jnp.dot(values_t(vw_ref, first_w, wt + 1), p.astype(BF16), preferred_element_type=F32)
             * (1.0 / jnp.sum(p, axis=0, keepdims=True)))
    for h in range(A_HPG):
        cols = slice(h * tq, (h + 1) * tq)
        part_ref[:, cols] = (gates[h:h + 1, :] * out_c[:, cols]
                             + gates[2 * A_HPG + h:2 * A_HPG + h + 1, :] * out_w[:, cols])

    def body(j, carry):
        m_ref[...], l_ref[...], acc_ref[...] = chunk(last - 1 - j, m_ref[...], l_ref[...], acc_ref[...])
        return carry

    lax.fori_loop(0, last, body, 0)
    out_s = acc_ref[...] * (1.0 / l_ref[...])
    blocks = []
    for h in range(A_HPG):
        cols = slice(h * tq, (h + 1) * tq)
        blocks.append(part_ref[:, cols] + gates[A_HPG + h:A_HPG + h + 1, :] * out_s[:, cols])
    o_ref[0] = jnp.concatenate(blocks, axis=0).T


def _attention(q_t, kc, vc_t, ks, vs_t, kw, vw_t, bias_c, bias_d, gates_t):
    bsz, _, nt, _, _ = q_t.shape
    s = ks.shape[2]
    n_cmp = kc.shape[2]
    assert nt % SUB == 0 and WINDOW // TQ + 1 <= nt
    k_spec = lambda width: pl.BlockSpec((1, 1, s, width), lambda b, g, i: (b, g, 0, 0))
    vt_spec = pl.BlockSpec((1, 1, nt, A_HEAD_DIM, TQ), lambda b, g, i: (b, g, 0, 0, 0))
    return pl.pallas_call(
        _attn_kernel,
        grid=(bsz, A_KV_GROUPS, nt),
        in_specs=[pl.BlockSpec((1, 1, 1, A_HEAD_DIM, A_HPG * TQ), lambda b, g, i: (b, g, i, 0, 0)),
                  pl.BlockSpec((1, 1, n_cmp, A_HEAD_DIM), lambda b, g, i: (b, g, 0, 0)),
                  pl.BlockSpec((1, 1, A_HEAD_DIM, n_cmp), lambda b, g, i: (b, g, 0, 0)),
                  k_spec(LANES), vt_spec, k_spec(A_HEAD_DIM), vt_spec,
                  pl.BlockSpec((1, 1, n_cmp, A_HPG * TQ), lambda b, g, i: (i, g, 0, 0)),
                  pl.BlockSpec((1, N_BIAS_TILES, TQ, A_HPG * TQ), lambda b, g, i: (g, 0, 0, 0)),
                  pl.BlockSpec((1, 1, 1, GATE_ROWS, TQ), lambda b, g, i: (b, g, i, 0, 0))],
        out_specs=pl.BlockSpec((1, TQ, A_HPG * A_HEAD_DIM), lambda b, g, i: (b, i, g)),
        out_shape=jax.ShapeDtypeStruct((bsz, s, A_WIDTH), F32),
        scratch_shapes=[pltpu.VMEM((1, A_HPG * TQ), F32),
                        pltpu.VMEM((1, A_HPG * TQ), F32),
                        pltpu.VMEM((A_HEAD_DIM, A_HPG * TQ), F32),
                        pltpu.VMEM((A_HEAD_DIM, A_HPG * TQ), F32)],
        compiler_params=pltpu.CompilerParams(dimension_semantics=("parallel", "parallel", "parallel")),
        name="attn",
    )(q_t, kc, vc_t, ks, vs_t, kw, vw_t, bias_c, bias_d, gates_t)


def _rwkv_kernel(c_ref, mu_ref, w0_ref, wl_ref, a0_ref, al_ref, kk_ref, ka_ref, rk_ref, lw_ref, lb_ref,
                 o_ref, state_ref, prev_ref):
    cc = pl.program_id(1)
    n = B_HEAD_DIM
    nb, csz = c_ref.shape[0], c_ref.shape[1]

    @pl.when(cc == 0)
    def _():
        state_ref[...] = jnp.zeros(state_ref.shape, F32)
        prev_ref[...] = jnp.zeros(prev_ref.shape, F32)

    ti = lax.broadcasted_iota(jnp.int32, (csz, csz), 0)
    si = lax.broadcasted_iota(jnp.int32, (csz, csz), 1)
    lower = si <= ti
    strict = si < ti
    tri = jnp.where(lower, 1.0, 0.0).astype(BF16)
    eye = jnp.where(si == ti, 1.0, 0.0)

    chains = []
    for bi in range(nb):
        p = c_ref[bi]
        row = lax.broadcasted_iota(jnp.int32, p.shape, 0)
        prev = jnp.where(row == 0, prev_ref[bi, 0:1, :], pltpu.roll(p, 1, axis=0))
        prev_ref[bi, 0:1, :] = p[csz - 1:csz, :]
        x = p + (prev - p) * mu_ref[...]
        r = x[:, 0:B_WIDTH]
        k = x[:, B_WIDTH:2 * B_WIDTH]
        v = x[:, 2 * B_WIDTH:3 * B_WIDTH]
        wd = x[:, 3 * B_WIDTH:3 * B_WIDTH + DECAY_LORA]
        ad = x[:, 3 * B_WIDTH + DECAY_LORA:3 * B_WIDTH + DECAY_LORA + ICLR_LORA]

        z = -(w0_ref[...] + _dot(jnp.tanh(wd), wl_ref[...]))
        softplus = jnp.maximum(z, 0.0) + jnp.log(1.0 + jnp.exp(-jnp.abs(z)))
        ld = -jnp.exp(-softplus - 0.5)
        a = _sigmoid(a0_ref[...] + _dot(ad, al_ref[...]))
        kk = k * kk_ref[...]
        k_mod = k * (1.0 + (a - 1.0) * ka_ref[...])
        rkr = r * k_mod * rk_ref[...]

        ld_hi, ld_lo = _split2(ld)
        cum = jnp.dot(tri, ld_hi, preferred_element_type=F32) + jnp.dot(tri, ld_lo, preferred_element_type=F32)
        g_inc = jnp.exp(cum)
        g_exc = jnp.exp(cum - ld)
        g_inv = jnp.exp(-cum)
        g_end = jnp.exp(cum[csz - 1:csz, :] - cum)
        g_all = g_inc[csz - 1:csz, :]

        for h in range(B_HEADS):
            sl = slice(h * n, (h + 1) * n)
            kk_h = kk[:, sl]
            kk_h = kk_h * lax.rsqrt(jnp.maximum(jnp.sum(kk_h * kk_h, axis=-1, keepdims=True), 1e-24))
            b_h = kk_h * a[:, sl]
            ch = dict(
                idx=bi * B_HEADS + h,
                v=v[:, sl],
                lhs=jnp.concatenate([-kk_h * g_exc[:, sl], r[:, sl] * g_inc[:, sl]], axis=0).astype(BF16),
                bt=(b_h * g_inv[:, sl]).astype(BF16),
                kt=(k_mod[:, sl] * g_inv[:, sl]).astype(BF16),
                bk=jnp.concatenate([b_h * g_end[:, sl], k_mod[:, sl] * g_end[:, sl]], axis=0).astype(BF16),
                g_all=g_all[:, sl],
                bonus=jnp.sum(rkr[:, sl], axis=-1, keepdims=True) * v[:, sl],
            )
            chains.append(ch)

    for ch in chains:
        xb = _dot_nt(ch["lhs"], ch["bt"])
        xk = _dot_nt(ch["lhs"], ch["kt"])
        ch["a_ab"] = jnp.where(strict, xb[:csz], 0.0)
        ch["a_ak"] = jnp.where(strict, xk[:csz], 0.0).astype(BF16)
        ch["m_rb"] = jnp.where(lower, xb[csz:], 0.0).astype(BF16)
        ch["m_rk"] = jnp.where(lower, xk[csz:], 0.0).astype(BF16)
    for ch in chains:
        vb = ch["v"].astype(BF16)
        ch["akv"] = _dot(ch["a_ak"], vb)
        ch["mrkv"] = _dot(ch["m_rk"], vb)
        ch["tinv"] = eye + ch["a_ab"]
        ch["pw"] = ch["a_ab"].astype(BF16)
    for _ in range(int(math.log2(csz)) - 1):
        for ch in chains:
            pw = _dot(ch["pw"], ch["pw"]).astype(BF16)
            ch["pw"] = pw
        for ch in chains:
            ch["tinv"] = ch["tinv"] + _dot(ch["tinv"], ch["pw"])
    for ch in chains:
        ch["s0"] = state_ref[ch["idx"]]
        ch["as0"] = _dot_nt(ch["lhs"], ch["s0"])
    for ch in chains:
        ch["u"] = _dot(ch["tinv"], ch["as0"][:csz] + ch["akv"])
    outs = []
    for ch in chains:
        u = ch["u"]
        y = ch["as0"][csz:] + _dot(ch["m_rb"], u) + ch["mrkv"]
        uv = jnp.concatenate([u, ch["v"]], axis=0)
        state_ref[ch["idx"]] = ch["s0"] * ch["g_all"] + _dot_tn(uv, ch["bk"])
        mean = jnp.mean(y, axis=-1, keepdims=True)
        yc = y - mean
        var = jnp.mean(yc * yc, axis=-1, keepdims=True)
        outs.append(yc * lax.rsqrt(var + LNX_EPS))
    for bi in range(nb):
        yn = jnp.concatenate(outs[bi * B_HEADS:(bi + 1) * B_HEADS], axis=-1)
        bonus = jnp.concatenate([ch["bonus"] for ch in chains[bi * B_HEADS:(bi + 1) * B_HEADS]], axis=-1)
        o_ref[bi] = yn * lw_ref[...] + lb_ref[...] + bonus


RWKV_NB = 4


def _rwkv(cols_rwkv, mu, w0, wl, a0, al, k_k, k_a, r_k, ln_w, ln_b):
    bsz, s, _ = cols_rwkv.shape
    nb = RWKV_NB if bsz % RWKV_NB == 0 else 1
    const = lambda b, c: (0, 0)
    vec = pl.BlockSpec((1, B_WIDTH), const)
    return pl.pallas_call(
        _rwkv_kernel,
        grid=(bsz // nb, s // CHUNK),
        in_specs=[pl.BlockSpec((nb, CHUNK, RWKV_COLS), lambda b, c: (b, c, 0)),
                  pl.BlockSpec((1, RWKV_COLS), const),
                  vec, pl.BlockSpec((DECAY_LORA, B_WIDTH), const),
                  vec, pl.BlockSpec((ICLR_LORA, B_WIDTH), const),
                  vec, vec, vec, vec, vec],
        out_specs=pl.BlockSpec((nb, CHUNK, B_WIDTH), lambda b, c: (b, c, 0)),
        out_shape=jax.ShapeDtypeStruct((bsz, s, B_WIDTH), F32),
        scratch_shapes=[pltpu.VMEM((nb * B_HEADS, B_HEAD_DIM, B_HEAD_DIM), F32),
                        pltpu.VMEM((nb, 8, RWKV_COLS), F32)],
        compiler_params=pltpu.CompilerParams(dimension_semantics=("parallel", "arbitrary")),
        name="rwkv",
    )(cols_rwkv, mu, w0, wl, a0, al, k_k, k_a, r_k, ln_w, ln_b)


def _final_kernel(x_ref, ya_ref, yb_ref, cf_ref, gate_ref, wa_ref, wb_ref, wo_ref, o_ref):
    a_silu = cf_ref[0, :, 0:A_WIDTH]
    b_silu = cf_ref[0, :, A_WIDTH:A_WIDTH + B_WIDTH]
    merge_a = cf_ref[0, :, A_WIDTH + B_WIDTH:A_WIDTH + B_WIDTH + D_MODEL]
    merge_b = cf_ref[0, :, A_WIDTH + B_WIDTH + D_MODEL:A_WIDTH + B_WIDTH + 2 * D_MODEL]
    ya = ya_ref[0] * (a_silu * _sigmoid(a_silu))
    yb = yb_ref[0] * (b_silu * _sigmoid(b_silu))
    merged = _sigmoid(merge_a) * _dot(ya, wa_ref[...]) + _sigmoid(merge_b) * _dot(yb, wb_ref[...])
    o_ref[0] = x_ref[0] + gate_ref[0] * _dot(merged, wo_ref[...])


def _final(x, y_a, y_b, cols_fin, gate, w_out_a, w_out_b, w_o, tm=256):
    bsz, s, _ = x.shape
    const = lambda b, i: (0, 0)
    row = lambda w: pl.BlockSpec((1, tm, w), lambda b, i: (b, i, 0))
    return pl.pallas_call(
        _final_kernel,
        grid=(bsz, s // tm),
        in_specs=[row(D_MODEL), row(A_WIDTH), row(B_WIDTH), row(FIN_COLS),
                  pl.BlockSpec((1, 1, D_MODEL), lambda b, i: (b, 0, 0)),
                  pl.BlockSpec((A_WIDTH, D_MODEL), const),
                  pl.BlockSpec((B_WIDTH, D_MODEL), const),
                  pl.BlockSpec((D_MODEL, D_MODEL), const)],
        out_specs=row(D_MODEL),
        out_shape=jax.ShapeDtypeStruct((bsz, s, D_MODEL), F32),
        compiler_params=pltpu.CompilerParams(dimension_semantics=("parallel", "parallel"),
                                             vmem_limit_bytes=VMEM_LIMIT),
        name="final",
    )(x, y_a, y_b, cols_fin, gate, w_out_a, w_out_b, w_o)


def _split_w_in(w_in):
    nsa_in = 2 * A_WIDTH + 6 * A_KV_WIDTH + 3 * A_HEADS
    o_gate = A_WIDTH + 6 * A_KV_WIDTH
    o_asilu = o_gate + 3 * A_HEADS
    o_shift = nsa_in
    o_rest = nsa_in + RWKV_COLS
    gate_w = w_in[:, o_gate:o_asilu].reshape(D_MODEL, 3, A_KV_GROUPS, A_HPG)
    gate_w = gate_w.transpose(0, 2, 1, 3).reshape(D_MODEL, A_KV_GROUPS, 3 * A_HPG)
    gate_w = jnp.pad(gate_w, ((0, 0), (0, 0), (0, LANES - 3 * A_HPG))).reshape(D_MODEL, GATE_PAD)
    w_nsa = jnp.concatenate([w_in[:, :o_gate], gate_w], axis=1)
    w_fin = jnp.concatenate([w_in[:, o_asilu:o_shift], w_in[:, o_rest:]], axis=1)
    w_rwkv = w_in[:, o_shift:o_rest]
    return w_nsa.astype(BF16), w_fin.astype(BF16), w_rwkv.astype(BF16)


def _layer(x, c, rel_bias, w_ada, b_ada, norm_gain, w_in, q_norm_gain, k_norm_gain,
           cmp_pos_k, cmp_pos_v, cmp_k_w1, cmp_k_w2, cmp_v_w1, cmp_v_w2,
           shift_mu, w0, w_lora_up, a0, a_lora_up, k_k, k_a, r_k, ln_x_w, ln_x_b,
           w_out_a, w_out_b, w_o):
    bsz, s, _ = x.shape
    assert s % (2 * TQ) == 0 and s // CMP_STRIDE == LANES
    n16 = s // CMP_STRIDE
    mod = _ada(c, w_ada, b_ada)
    w_nsa, w_fin, w_rwkv = _split_w_in(w_in)
    cols_nsa, cols_fin, cols_rwkv = _proj(x, mod, norm_gain, w_nsa, w_fin, w_rwkv)

    scale = A_HEAD_DIM ** -0.5
    qg = (jnp.tile(q_norm_gain, A_HEADS) * scale).reshape(1, A_WIDTH)
    ksg = jnp.tile(k_norm_gain[1], A_KV_GROUPS).reshape(1, A_KV_WIDTH)
    kwg = jnp.tile(k_norm_gain[2], A_KV_GROUPS).reshape(1, A_KV_WIDTH)
    seg = np.arange(A_WIDTH) // A_HEAD_DIM
    bd = jnp.asarray((seg[:, None] == seg[None, :]).astype(np.float32) / A_HEAD_DIM, BF16)
    q_t, ks, vs_t, kw, vw_t, gates_t = _nsaprep(cols_nsa, qg, ksg, kwg, bd)

    def blocks16(t):
        t = t.reshape(bsz, n16, CMP_STRIDE, A_KV_GROUPS, A_HEAD_DIM).transpose(0, 3, 1, 2, 4)
        return t.reshape(bsz, A_KV_GROUPS, n16, CMP_STRIDE * A_HEAD_DIM)

    zk = blocks16(cols_nsa[:, :, A_WIDTH:A_WIDTH + A_KV_WIDTH])
    zv = blocks16(cols_nsa[:, :, A_WIDTH + A_KV_WIDTH:A_WIDTH + 2 * A_KV_WIDTH])
    kc, vc_t = _compress(zk, zv,
                         cmp_pos_k.reshape(2, CMP_STRIDE * A_HEAD_DIM), cmp_pos_v.reshape(2, CMP_STRIDE * A_HEAD_DIM),
                         cmp_k_w1.astype(BF16), cmp_k_w2.astype(BF16), cmp_v_w1.astype(BF16), cmp_v_w2.T.astype(BF16),
                         k_norm_gain[0].reshape(1, A_HEAD_DIM))
    bias_c, bias_d = _bias_tables(rel_bias, s, n16)
    y_a = _attention(q_t, kc, vc_t, ks, vs_t, kw, vw_t, bias_c, bias_d, gates_t)

    vec = lambda t: t.reshape(1, -1)
    y_b = _rwkv(cols_rwkv, vec(shift_mu), vec(w0), w_lora_up.astype(BF16), vec(a0), a_lora_up.astype(BF16),
                vec(k_k), vec(k_a), vec(r_k), vec(ln_x_w), vec(ln_x_b))

    gate = mod[:, 2 * D_MODEL:].reshape(bsz, 1, D_MODEL)
    return _final(x, y_a, y_b, cols_fin, gate, w_out_a.astype(BF16), w_out_b.astype(BF16), w_o.astype(BF16))


def kernel(x, c, w_ada, b_ada, norm_gain, w_in, q_norm_gain, k_norm_gain, cmp_pos_k, cmp_pos_v, cmp_k_w1, cmp_k_w2, cmp_v_w1, cmp_v_w2, rel_bias, shift_mu, w0, w_lora_up, a0, a_lora_up, k_k, k_a, r_k, ln_x_w, ln_x_b, w_out_a, w_out_b, w_o):
    for l in range(w_in.shape[0]):
        x = _layer(x, c, rel_bias, w_ada[l], b_ada[l], norm_gain[l], w_in[l], q_norm_gain[l], k_norm_gain[l],
                   cmp_pos_k[l], cmp_pos_v[l], cmp_k_w1[l], cmp_k_w2[l], cmp_v_w1[l], cmp_v_w2[l],
                   shift_mu[l], w0[l], w_lora_up[l], a0[l], a_lora_up[l], k_k[l], k_a[l], r_k[l],
                   ln_x_w[l], ln_x_b[l], w_out_a[l], w_out_b[l], w_o[l])
    return x
```

```python
import functools
import math

import numpy as np
import jax
import jax.numpy as jnp
from jax import lax
from jax.experimental import pallas as pl
from jax.experimental.pallas import tpu as pltpu

F32 = jnp.float32
BF16 = jnp.bfloat16

D_MODEL = 1024
A_HEADS = 8
A_HEAD_DIM = 64
A_KV_GROUPS = 2
A_HPG = A_HEADS // A_KV_GROUPS
A_WIDTH = A_HEADS * A_HEAD_DIM
A_KV_WIDTH = A_KV_GROUPS * A_HEAD_DIM
CMP_BLOCK = 32
CMP_STRIDE = 16
CMP_HIDDEN = 256
SLC_BLOCK = 64
SLC_TOPN = 16
WINDOW = 512
B_HEADS = 8
B_HEAD_DIM = 64
B_WIDTH = B_HEADS * B_HEAD_DIM
DECAY_LORA = 64
ICLR_LORA = 64
LNX_EPS = 64e-5
REL_BUCKETS = 32
REL_MAX_EXACT = 16
REL_MAX_DIST = 128
NORM_EPS = 1e-6
NEG_INF = -1e30
FORCE_SCORE = 1e30

LANES = 128
TQ = 128
CHUNK = 64
GATE_PAD = 2 * LANES
GATE_ROWS = 16
NSA_COLS = A_WIDTH + 6 * A_KV_WIDTH + GATE_PAD
FIN_COLS = A_WIDTH + B_WIDTH + 2 * D_MODEL
RWKV_COLS = 3 * B_WIDTH + DECAY_LORA + ICLR_LORA
VMEM_LIMIT = 56 * 1024 * 1024


def _dot(a, b):
    return jnp.dot(a.astype(BF16), b.astype(BF16), preferred_element_type=F32)


def _dot_nt(a, b):
    return lax.dot_general(a.astype(BF16), b.astype(BF16), (((1,), (1,)), ((), ())),
                           preferred_element_type=F32)


def _dot_tn(a, b):
    return lax.dot_general(a.astype(BF16), b.astype(BF16), (((0,), (0,)), ((), ())),
                           preferred_element_type=F32)


def _split2(x):
    hi = x.astype(BF16)
    lo = (x - hi.astype(F32)).astype(BF16)
    return hi, lo


def _split3(x):
    h1 = x.astype(BF16)
    r1 = x - h1.astype(F32)
    h2 = r1.astype(BF16)
    h3 = (r1 - h2.astype(F32)).astype(BF16)
    return h1, h2, h3


def _sigmoid(x):
    return 1.0 / (1.0 + jnp.exp(-x))


def _bucket_thresholds():
    n = np.arange(0, 4096)
    nf = np.maximum(n, REL_MAX_EXACT).astype(np.float64)
    val = np.log(nf / REL_MAX_EXACT) / math.log(REL_MAX_DIST / REL_MAX_EXACT) * (REL_BUCKETS - REL_MAX_EXACT)
    frac = np.abs(val - np.round(val))
    assert np.all((frac > 1e-4) | (n <= REL_MAX_EXACT) | (n >= REL_MAX_DIST))
    large = REL_MAX_EXACT + np.floor(val + 1e-9).astype(np.int64)
    bucket = np.where(n < REL_MAX_EXACT, n, np.minimum(large, REL_BUCKETS - 1))
    return [int(np.argmax(bucket >= j)) for j in range(REL_BUCKETS)]


_BUCKET_TH = _bucket_thresholds()


def _bias_from_dist(dist, tbl_ref, head):
    val = jnp.full(dist.shape, tbl_ref[0, head], F32)
    for j in range(1, REL_BUCKETS):
        val = jnp.where(dist >= _BUCKET_TH[j], tbl_ref[j, head], val)
    return val


def _ada_kernel(c_ref, w_ref, b_ref, o_ref):
    c = c_ref[...]
    o_ref[...] = _dot(c * _sigmoid(c), w_ref[...]) + b_ref[...]


def _ada(c, w_ada, b_ada):
    bsz = c.shape[0]
    return pl.pallas_call(
        _ada_kernel,
        grid=(3,),
        in_specs=[pl.BlockSpec((bsz, D_MODEL), lambda j: (0, 0)),
                  pl.BlockSpec((D_MODEL, D_MODEL), lambda j: (0, j)),
                  pl.BlockSpec((1, D_MODEL), lambda j: (0, j))],
        out_specs=pl.BlockSpec((bsz, D_MODEL), lambda j: (0, j)),
        out_shape=jax.ShapeDtypeStruct((bsz, 3 * D_MODEL), F32),
        name="ada",
    )(c, w_ada, b_ada.reshape(1, 3 * D_MODEL))


def _proj_kernel(x_ref, mod_ref, g_ref, wn_ref, wf_ref, wr_ref, on_ref, of_ref, or_ref):
    x = x_ref[0]
    ms = jnp.mean(x * x, axis=-1, keepdims=True)
    y = x * lax.rsqrt(ms + NORM_EPS) * g_ref[...]
    mod = mod_ref[0]
    h = (y * (1.0 + mod[:, D_MODEL:2 * D_MODEL]) + mod[:, :D_MODEL]).astype(BF16)
    on_ref[0] = jnp.dot(h, wn_ref[...], preferred_element_type=F32)
    of_ref[0] = jnp.dot(h, wf_ref[...], preferred_element_type=F32).astype(BF16)
    or_ref[0] = jnp.dot(h, wr_ref[...], preferred_element_type=F32)


def _proj(x, mod, norm_gain, w_nsa, w_fin, w_rwkv, tm=256):
    bsz, s, _ = x.shape
    const = lambda b, i: (0, 0)
    return pl.pallas_call(
        _proj_kernel,
        grid=(bsz, s // tm),
        in_specs=[pl.BlockSpec((1, tm, D_MODEL), lambda b, i: (b, i, 0)),
                  pl.BlockSpec((1, 1, 3 * D_MODEL), lambda b, i: (b, 0, 0)),
                  pl.BlockSpec((1, D_MODEL), const),
                  pl.BlockSpec((D_MODEL, NSA_COLS), const),
                  pl.BlockSpec((D_MODEL, FIN_COLS), const),
                  pl.BlockSpec((D_MODEL, RWKV_COLS), const)],
        out_specs=[pl.BlockSpec((1, tm, NSA_COLS), lambda b, i: (b, i, 0)),
                   pl.BlockSpec((1, tm, FIN_COLS), lambda b, i: (b, i, 0)),
                   pl.BlockSpec((1, tm, RWKV_COLS), lambda b, i: (b, i, 0))],
        out_shape=[jax.ShapeDtypeStruct((bsz, s, NSA_COLS), F32),
                   jax.ShapeDtypeStruct((bsz, s, FIN_COLS), BF16),
                   jax.ShapeDtypeStruct((bsz, s, RWKV_COLS), F32)],
        compiler_params=pltpu.CompilerParams(dimension_semantics=("parallel", "parallel"),
                                             vmem_limit_bytes=VMEM_LIMIT),
        name="proj",
    )(x, mod.reshape(bsz, 1, 3 * D_MODEL), norm_gain.reshape(1, D_MODEL), w_nsa, w_fin, w_rwkv)


def _seg_mean(x2, bd):
    hi, lo = _split2(x2)
    return (jnp.dot(hi, bd, preferred_element_type=F32) + jnp.dot(lo, bd, preferred_element_type=F32))


def _nsaprep_kernel(c_ref, qg_ref, ksg_ref, kwg_ref, bd_ref, q_ref, ks_ref, vs_ref, kw_ref, vw_ref, gt_ref):
    bd = bd_ref[...]
    q = c_ref[0, :, 0:A_WIDTH]
    qn_t = (q * lax.rsqrt(_seg_mean(q * q, bd) + NORM_EPS) * qg_ref[...]).T
    off = A_WIDTH + 2 * A_KV_WIDTH
    k_slc = c_ref[0, :, off:off + A_KV_WIDTH]
    v_slc = c_ref[0, :, off + A_KV_WIDTH:off + 2 * A_KV_WIDTH]
    k_win = c_ref[0, :, off + 2 * A_KV_WIDTH:off + 3 * A_KV_WIDTH]
    v_win = c_ref[0, :, off + 3 * A_KV_WIDTH:off + 4 * A_KV_WIDTH]
    bd_kv = bd[:A_KV_WIDTH, :A_KV_WIDTH]
    ksn = k_slc * lax.rsqrt(_seg_mean(k_slc * k_slc, bd_kv) + NORM_EPS) * ksg_ref[...]
    kwn = (k_win * lax.rsqrt(_seg_mean(k_win * k_win, bd_kv) + NORM_EPS) * kwg_ref[...]).astype(BF16)
    vs_t = v_slc.T.astype(BF16)
    vw_t = v_win.T.astype(BF16)
    tq = c_ref.shape[1]
    lane = lax.broadcasted_iota(jnp.int32, (tq, LANES), 1)
    blk = (pl.program_id(1) * tq + lax.broadcasted_iota(jnp.int32, (tq, LANES), 0)) // SLC_BLOCK
    onehot = jnp.where(lane - A_HEAD_DIM == blk, 1.0, 0.0)
    gate0 = A_WIDTH + 6 * A_KV_WIDTH
    for g in range(A_KV_GROUPS):
        heads = [qn_t[(g * A_HPG + h) * A_HEAD_DIM:(g * A_HPG + h + 1) * A_HEAD_DIM, :] for h in range(A_HPG)]
        q_ref[0, g, 0] = jnp.concatenate(heads, axis=1).astype(BF16)
        sl = slice(g * A_HEAD_DIM, (g + 1) * A_HEAD_DIM)
        k_g = ksn if g == 0 else pltpu.roll(ksn, A_HEAD_DIM, axis=1)
        ks_ref[0, g] = jnp.where(lane < A_HEAD_DIM, k_g, onehot).astype(BF16)
        kw_ref[0, g] = kwn[:, sl]
        vs_ref[0, g, 0] = vs_t[sl, :]
        vw_ref[0, g, 0] = vw_t[sl, :]
        gates_t = _sigmoid(c_ref[0, :, gate0 + g * LANES:gate0 + (g + 1) * LANES]).T
        gt_ref[0, g, 0] = gates_t[0:GATE_ROWS, :]


def _nsaprep(cols_nsa, qg, ksg, kwg, bd):
    bsz, s, _ = cols_nsa.shape
    nt = s // TQ
    const = lambda b, i: (0, 0)
    assert A_HEAD_DIM + s // SLC_BLOCK <= LANES
    k_spec = pl.BlockSpec((1, A_KV_GROUPS, TQ, A_HEAD_DIM), lambda b, i: (b, 0, i, 0))
    k_shape = jax.ShapeDtypeStruct((bsz, A_KV_GROUPS, s, A_HEAD_DIM), BF16)
    ka_spec = pl.BlockSpec((1, A_KV_GROUPS, TQ, LANES), lambda b, i: (b, 0, i, 0))
    ka_shape = jax.ShapeDtypeStruct((bsz, A_KV_GROUPS, s, LANES), BF16)
    vt_spec = pl.BlockSpec((1, A_KV_GROUPS, 1, A_HEAD_DIM, TQ), lambda b, i: (b, 0, i, 0, 0))
    vt_shape = jax.ShapeDtypeStruct((bsz, A_KV_GROUPS, nt, A_HEAD_DIM, TQ), BF16)
    return pl.pallas_call(
        _nsaprep_kernel,
        grid=(bsz, nt),
        in_specs=[pl.BlockSpec((1, TQ, NSA_COLS), lambda b, i: (b, i, 0)),
                  pl.BlockSpec((1, A_WIDTH), const),
                  pl.BlockSpec((1, A_KV_WIDTH), const),
                  pl.BlockSpec((1, A_KV_WIDTH), const),
                  pl.BlockSpec((A_WIDTH, A_WIDTH), const)],
        out_specs=[pl.BlockSpec((1, A_KV_GROUPS, 1, A_HEAD_DIM, A_HPG * TQ), lambda b, i: (b, 0, i, 0, 0)),
                   ka_spec, vt_spec, k_spec, vt_spec,
                   pl.BlockSpec((1, A_KV_GROUPS, 1, GATE_ROWS, TQ), lambda b, i: (b, 0, i, 0, 0))],
        out_shape=[jax.ShapeDtypeStruct((bsz, A_KV_GROUPS, nt, A_HEAD_DIM, A_HPG * TQ), BF16),
                   ka_shape, vt_shape, k_shape, vt_shape,
                   jax.ShapeDtypeStruct((bsz, A_KV_GROUPS, nt, GATE_ROWS, TQ), F32)],
        compiler_params=pltpu.CompilerParams(dimension_semantics=("parallel", "parallel")),
        name="nsaprep",
    )(cols_nsa, qg, ksg, kwg, bd)


def _compress_kernel(ck_ref, cv_ref, pk_ref, pv_ref, w1k_ref, w2k_ref, w1v_ref, w2v_ref, kg_ref, kc_ref, vc_ref):
    n16 = ck_ref.shape[1] // CMP_STRIDE

    def rows16(ref):
        return jnp.concatenate([ref[0, pl.ds(p, n16, stride=CMP_STRIDE), :] for p in range(CMP_STRIDE)], axis=1)

    def hidden(z, pos_ref, w1_ref, g):
        top = _dot(z + pos_ref[0:1, :], w1_ref[g, 0])
        bot = _dot(z + pos_ref[1:2, :], w1_ref[g, 1])
        return jax.nn.gelu(top + pltpu.roll(bot, n16 - 1, axis=0), approximate=True)

    zk = rows16(ck_ref)
    zv = rows16(cv_ref)
    for g in range(A_KV_GROUPS):
        kc = _dot(hidden(zk, pk_ref, w1k_ref, g), w2k_ref[...])
        ms = jnp.mean(kc * kc, axis=-1, keepdims=True)
        kc_ref[0, g] = (kc * lax.rsqrt(ms + NORM_EPS) * kg_ref[...]).astype(BF16)
        vc_ref[0, g] = _dot_nt(w2v_ref[...], hidden(zv, pv_ref, w1v_ref, g)).astype(BF16)


def _expand_cmp_w1(w1):
    w = w1.reshape(2, CMP_STRIDE, 1, A_HEAD_DIM, CMP_HIDDEN)
    per_group = []
    for g in range(A_KV_GROUPS):
        pad = [(0, 0), (0, 0), (g, A_KV_GROUPS - 1 - g), (0, 0), (0, 0)]
        per_group.append(jnp.pad(w, pad).reshape(2, CMP_STRIDE * A_KV_WIDTH, CMP_HIDDEN))
    return jnp.stack(per_group).astype(BF16)


def _expand_cmp_pos(pos):
    p = jnp.broadcast_to(pos.reshape(2, CMP_STRIDE, 1, A_HEAD_DIM), (2, CMP_STRIDE, A_KV_GROUPS, A_HEAD_DIM))
    return p.reshape(2, CMP_STRIDE * A_KV_WIDTH)


def _compress(cols_nsa, pk, pv, w1k, w2k, w1v, w2v_t, kg):
    bsz, s, _ = cols_nsa.shape
    n16 = s // CMP_STRIDE
    zw = CMP_STRIDE * A_KV_WIDTH
    const = lambda b: (0, 0)
    const4 = lambda b: (0, 0, 0, 0)
    kblk = A_WIDTH // A_KV_WIDTH
    return pl.pallas_call(
        _compress_kernel,
        grid=(bsz,),
        in_specs=[pl.BlockSpec((1, s, A_KV_WIDTH), lambda b: (b, 0, kblk)),
                  pl.BlockSpec((1, s, A_KV_WIDTH), lambda b: (b, 0, kblk + 1)),
                  pl.BlockSpec((2, zw), const), pl.BlockSpec((2, zw), const),
                  pl.BlockSpec((A_KV_GROUPS, 2, zw, CMP_HIDDEN), const4), pl.BlockSpec((CMP_HIDDEN, A_HEAD_DIM), const),
                  pl.BlockSpec((A_KV_GROUPS, 2, zw, CMP_HIDDEN), const4), pl.BlockSpec((A_HEAD_DIM, CMP_HIDDEN), const),
                  pl.BlockSpec((1, A_HEAD_DIM), const)],
        out_specs=[pl.BlockSpec((1, A_KV_GROUPS, n16, A_HEAD_DIM), lambda b: (b, 0, 0, 0)),
                   pl.BlockSpec((1, A_KV_GROUPS, A_HEAD_DIM, n16), lambda b: (b, 0, 0, 0))],
        out_shape=[jax.ShapeDtypeStruct((bsz, A_KV_GROUPS, n16, A_HEAD_DIM), BF16),
                   jax.ShapeDtypeStruct((bsz, A_KV_GROUPS, A_HEAD_DIM, n16), BF16)],
        compiler_params=pltpu.CompilerParams(dimension_semantics=("parallel",)),
        name="compress",
    )(cols_nsa, cols_nsa, pk, pv, w1k, w2k, w1v, w2v_t, kg)


TILE_FAR, TILE_EDGE, TILE_MASKED, N_BIAS_TILES = 2, 3, 4, 5
SUB = 4


def _bias_cmp_kernel(tbl_ref, o_ref):
    i = pl.program_id(0)
    g = pl.program_id(1)
    n_cmp = o_ref.shape[2]
    n = lax.broadcasted_iota(jnp.int32, (n_cmp, TQ), 0)
    q = lax.broadcasted_iota(jnp.int32, (n_cmp, TQ), 1)
    dist = i * TQ + q - (n * CMP_STRIDE + CMP_BLOCK - 1)
    for h in range(A_HPG):
        bias = _bias_from_dist(dist, tbl_ref, g * A_HPG + h)
        o_ref[0, 0, :, h * TQ:(h + 1) * TQ] = jnp.where(dist >= 0, bias, NEG_INF)


def _bias_toeplitz_kernel(tbl_ref, o_ref):
    g = pl.program_id(0)
    r = pl.program_id(1)
    off = jnp.where(r == TILE_EDGE, WINDOW // TQ, jnp.where(r == TILE_MASKED, -2, r))
    k = lax.broadcasted_iota(jnp.int32, (TQ, TQ), 0)
    q = lax.broadcasted_iota(jnp.int32, (TQ, TQ), 1)
    dist = off * TQ + q - k
    valid = (dist >= 0) & (dist < WINDOW)
    for h in range(A_HPG):
        bias = _bias_from_dist(dist, tbl_ref, g * A_HPG + h)
        o_ref[0, 0, :, h * TQ:(h + 1) * TQ] = jnp.where(valid, bias, NEG_INF)


def _bias_tables(rel_bias, s, n_cmp):
    smem = pl.BlockSpec(memory_space=pltpu.SMEM)
    nt = s // TQ
    bias_c = pl.pallas_call(
        _bias_cmp_kernel,
        grid=(nt, A_KV_GROUPS),
        in_specs=[smem],
        out_specs=pl.BlockSpec((1, 1, n_cmp, A_HPG * TQ), lambda i, g: (i, g, 0, 0)),
        out_shape=jax.ShapeDtypeStruct((nt, A_KV_GROUPS, n_cmp, A_HPG * TQ), F32),
        name="bias_cmp",
    )(rel_bias)
    assert _BUCKET_TH[REL_BUCKETS - 1] <= TQ + 1 and WINDOW // TQ >= 3
    bias_d = pl.pallas_call(
        _bias_toeplitz_kernel,
        grid=(A_KV_GROUPS, N_BIAS_TILES),
        in_specs=[smem],
        out_specs=pl.BlockSpec((1, 1, TQ, A_HPG * TQ), lambda g, r: (g, r, 0, 0)),
        out_shape=jax.ShapeDtypeStruct((A_KV_GROUPS, N_BIAS_TILES, TQ, A_HPG * TQ), F32),
        name="bias_toeplitz",
    )(rel_bias)
    return bias_c, bias_d


def _attn_kernel(q_ref, kc_ref, vc_ref, ks_ref, vs_ref, kw_ref, vw_ref, bc_ref, bd_ref, gt_ref, o_ref,
                 m_ref, l_ref, acc_ref, part_ref):
    i = pl.program_id(2)
    tq = TQ
    n_cmp = kc_ref.shape[2]
    n_slc = ks_ref.shape[2] // SLC_BLOCK
    t0 = i * tq
    q_t = q_ref[0, 0, 0]
    gates = gt_ref[0, 0, 0]

    bias = bc_ref[0, 0]
    valid = bias > 0.5 * NEG_INF
    s = jnp.dot(kc_ref[0, 0], q_t, preferred_element_type=F32) + bias
    m = jnp.max(s, axis=0, keepdims=True)
    e = jnp.where(valid, jnp.exp(s - m), 0.0)
    l = jnp.sum(e, axis=0, keepdims=True)
    p = e * (1.0 / jnp.where(l > 0.0, l, 1.0))
    out_c = jnp.dot(vc_ref[0, 0], p.astype(BF16), preferred_element_type=F32)
    p_grp = sum(p[:, h * tq:(h + 1) * tq] for h in range(A_HPG))

    r1, r2 = SLC_BLOCK // CMP_STRIDE, CMP_BLOCK // CMP_STRIDE
    jj = lax.broadcasted_iota(jnp.int32, (n_slc, n_cmp), 0)
    nn = lax.broadcasted_iota(jnp.int32, (n_slc, n_cmp), 1)
    d = nn - r1 * jj
    cnt = jnp.zeros((n_slc, n_cmp), F32)
    for a in range(r1):
        for c in range(r2):
            cnt = cnt + jnp.where(d == a - c, 1.0, 0.0)
    cnt = cnt.astype(BF16)
    imp = sum(jnp.dot(cnt, part, preferred_element_type=F32) for part in _split3(p_grp))
    blk = lax.broadcasted_iota(jnp.int32, (n_slc, tq), 0)
    tpos = t0 + lax.broadcasted_iota(jnp.int32, (n_slc, tq), 1)
    cur = tpos // SLC_BLOCK
    forced = (blk == 0) | (blk == cur) | (blk == cur - 1)
    causal = blk * SLC_BLOCK <= tpos
    imp = jnp.where(forced, FORCE_SCORE, jnp.where(causal, imp, NEG_INF))
    rank = jnp.zeros((n_slc, tq), F32)
    for c in range(n_slc):
        row = imp[c:c + 1, :]
        ahead = (row > imp) | ((row == imp) & (blk > c))
        rank = rank + jnp.where(ahead, 1.0, 0.0)
    pen = jnp.where(rank < float(min(SLC_TOPN, n_slc)), 0.0, -FORCE_SCORE).astype(BF16)

    def scores(k_slab, q_mat, first_tile, n_sub, tile_index):
        s = jnp.dot(k_slab, q_mat, preferred_element_type=F32)
        parts = [s[t * tq:(t + 1) * tq] + bd_ref[0, tile_index(i - (first_tile + t))] for t in range(n_sub)]
        return jnp.concatenate(parts, axis=0)

    def values_t(v_ref, first_tile, n_sub):
        return jnp.concatenate([v_ref[0, 0, first_tile + t] for t in range(n_sub)], axis=1)

    q_aug = jnp.concatenate([q_t, jnp.concatenate([pen] * A_HPG, axis=1),
                             jnp.zeros((LANES - A_HEAD_DIM - n_slc, A_HPG * tq), BF16)], axis=0)
    sel_tile = lambda r: jnp.where(r < 0, TILE_MASKED, jnp.minimum(r, TILE_FAR))
    last = i // SUB

    def chunk(c, m_old, l_old, acc_old):
        first_tile = c * SUB
        k0 = pl.multiple_of(first_tile * tq, SUB * tq)
        s = scores(ks_ref[0, 0, pl.ds(k0, SUB * tq), :], q_aug, first_tile, SUB, sel_tile)
        m_new = jnp.max(s, axis=0, keepdims=True)
        if m_old is not None:
            m_new = jnp.maximum(m_old, m_new)
        p = jnp.exp(s - m_new)
        l_new = jnp.sum(p, axis=0, keepdims=True)
        acc_new = jnp.dot(values_t(vs_ref, first_tile, SUB), p.astype(BF16), preferred_element_type=F32)
        if m_old is not None:
            alpha = jnp.exp(m_old - m_new)
            l_new = alpha * l_old + l_new
            acc_new = alpha * acc_old + acc_new
        return m_new, l_new, acc_new

    m_ref[...], l_ref[...], acc_ref[...] = chunk(last, None, None, None)

    wt = WINDOW // tq
    first_w = jnp.maximum(i - wt, 0)
    win_tile = lambda r: jnp.where(r < 0, TILE_MASKED, jnp.where(r == wt, TILE_EDGE, jnp.minimum(r, TILE_FAR)))
    k0 = pl.multiple_of(first_w * tq, tq)
    s = scores(kw_ref[0, 0, pl.ds(k0, (wt + 1) * tq), :], q_t, first_w, wt + 1, win_tile)
    p = jnp.exp(s - jnp.max(s, axis=0, keepdims=True))
    out_w = (jnp.dot(values_t(vw_ref, first_w, wt + 1), p.astype(BF16), preferred_element_type=F32)
             * (1.0 / jnp.sum(p, axis=0, keepdims=True)))
    for h in range(A_HPG):
        cols = slice(h * tq, (h + 1) * tq)
        part_ref[:, cols] = (gates[h:h + 1, :] * out_c[:, cols]
                             + gates[2 * A_HPG + h:2 * A_HPG + h + 1, :] * out_w[:, cols])

    def body(j, carry):
        m_ref[...], l_ref[...], acc_ref[...] = chunk(last - 1 - j, m_ref[...], l_ref[...], acc_ref[...])
        return carry

    lax.fori_loop(0, last, body, 0)
    out_s = acc_ref[...] * (1.0 / l_ref[...])
    blocks = []
    for h in range(A_HPG):
        cols = slice(h * tq, (h + 1) * tq)
        blocks.append(part_ref[:, cols] + gates[A_HPG + h:A_HPG + h + 1, :] * out_s[:, cols])
    o_ref[0] = jnp.concatenate(blocks, axis=0).T.astype(BF16)


def _attention(q_t, kc, vc_t, ks, vs_t, kw, vw_t, bias_c, bias_d, gates_t):
    bsz, _, nt, _, _ = q_t.shape
    s = ks.shape[2]
    n_cmp = kc.shape[2]
    assert nt % SUB == 0 and WINDOW // TQ + 1 <= nt
    k_spec = lambda width: pl.BlockSpec((1, 1, s, width), lambda b, g, i: (b, g, 0, 0))
    vt_spec = pl.BlockSpec((1, 1, nt, A_HEAD_DIM, TQ), lambda b, g, i: (b, g, 0, 0, 0))
    return pl.pallas_call(
        _attn_kernel,
        grid=(bsz, A_KV_GROUPS, nt),
        in_specs=[pl.BlockSpec((1, 1, 1, A_HEAD_DIM, A_HPG * TQ), lambda b, g, i: (b, g, i, 0, 0)),
                  pl.BlockSpec((1, 1, n_cmp, A_HEAD_DIM), lambda b, g, i: (b, g, 0, 0)),
                  pl.BlockSpec((1, 1, A_HEAD_DIM, n_cmp), lambda b, g, i: (b, g, 0, 0)),
                  k_spec(LANES), vt_spec, k_spec(A_HEAD_DIM), vt_spec,
                  pl.BlockSpec((1, 1, n_cmp, A_HPG * TQ), lambda b, g, i: (i, g, 0, 0)),
                  pl.BlockSpec((1, N_BIAS_TILES, TQ, A_HPG * TQ), lambda b, g, i: (g, 0, 0, 0)),
                  pl.BlockSpec((1, 1, 1, GATE_ROWS, TQ), lambda b, g, i: (b, g, i, 0, 0))],
        out_specs=pl.BlockSpec((1, TQ, A_HPG * A_HEAD_DIM), lambda b, g, i: (b, i, g)),
        out_shape=jax.ShapeDtypeStruct((bsz, s, A_WIDTH), BF16),
        scratch_shapes=[pltpu.VMEM((1, A_HPG * TQ), F32),
                        pltpu.VMEM((1, A_HPG * TQ), F32),
                        pltpu.VMEM((A_HEAD_DIM, A_HPG * TQ), F32),
                        pltpu.VMEM((A_HEAD_DIM, A_HPG * TQ), F32)],
        compiler_params=pltpu.CompilerParams(dimension_semantics=("parallel", "parallel", "parallel")),
        name="attn",
    )(q_t, kc, vc_t, ks, vs_t, kw, vw_t, bias_c, bias_d, gates_t)


def _rwkv_kernel(c_ref, mu_ref, w0_ref, wl_ref, a0_ref, al_ref, kk_ref, ka_ref, rk_ref, lw_ref, lb_ref,
                 o_ref, state_ref, prev_ref):
    cc = pl.program_id(1)
    n = B_HEAD_DIM
    nb, csz = c_ref.shape[0], c_ref.shape[1]

    @pl.when(cc == 0)
    def _():
        state_ref[...] = jnp.zeros(state_ref.shape, F32)
        prev_ref[...] = jnp.zeros(prev_ref.shape, F32)

    ti = lax.broadcasted_iota(jnp.int32, (csz, LANES), 0)
    si = lax.broadcasted_iota(jnp.int32, (csz, LANES), 1) % n
    lower = si <= ti
    strict = si < ti
    eye = jnp.where(si == ti, 1.0, 0.0)
    tri = jnp.where(lax.broadcasted_iota(jnp.int32, (csz, csz), 1) <= lax.broadcasted_iota(jnp.int32, (csz, csz), 0),
                    1.0, 0.0).astype(BF16)
    n_pairs = B_WIDTH // LANES
    left =lax.broadcasted_iota(jnp.int32, (csz, LANES), 1) < n
    row_left = lax.broadcasted_iota(jnp.int32, (LANES, LANES), 0) < n
    same_head = row_left == (lax.broadcasted_iota(jnp.int32, (LANES, LANES), 1) < n)

    def blockdiag(y):
        zero = jnp.zeros_like(y)
        return jnp.concatenate([jnp.where(left, y, zero), jnp.where(left, zero, y)], axis=0)

    def head_sum(x):
        lo = jnp.sum(jnp.where(left, x, 0.0), axis=-1, keepdims=True)
        hi = jnp.sum(jnp.where(left, 0.0, x), axis=-1, keepdims=True)
        return jnp.where(left, lo, hi)

    chains = []
    for bi in range(nb):
        p = c_ref[bi]
        row = lax.broadcasted_iota(jnp.int32, p.shape, 0)
        prev = jnp.where(row == 0, prev_ref[bi, 0:1, :], pltpu.roll(p, 1, axis=0))
        prev_ref[bi, 0:1, :] = p[csz - 1:csz, :]
        x = p + (prev - p) * mu_ref[...]
        r = x[:, 0:B_WIDTH]
        k = x[:, B_WIDTH:2 * B_WIDTH]
        v = x[:, 2 * B_WIDTH:3 * B_WIDTH]
        wd = x[:, 3 * B_WIDTH:3 * B_WIDTH + DECAY_LORA]
        ad = x[:, 3 * B_WIDTH + DECAY_LORA:3 * B_WIDTH + DECAY_LORA + ICLR_LORA]

        z = -(w0_ref[...] + _dot(jnp.tanh(wd), wl_ref[...]))
        softplus = jnp.maximum(z, 0.0) + jnp.log(1.0 + jnp.exp(-jnp.abs(z)))
        ld = -jnp.exp(-softplus - 0.5)
        a = _sigmoid(a0_ref[...] + _dot(ad, al_ref[...]))
        kk = k * kk_ref[...]
        k_mod = k * (1.0 + (a - 1.0) * ka_ref[...])
        rkr = r * k_mod * rk_ref[...]

        ld_hi, ld_lo = _split2(ld)
        cum = jnp.dot(tri, ld_hi, preferred_element_type=F32) + jnp.dot(tri, ld_lo, preferred_element_type=F32)
        g_inc = jnp.exp(cum)
        g_exc = jnp.exp(cum - ld)
        g_inv = jnp.exp(-cum)
        g_end = jnp.exp(cum[csz - 1:csz, :] - cum)
        g_all = g_inc[csz - 1:csz, :]

        for pr in range(n_pairs):
            sl = slice(pr * LANES, (pr + 1) * LANES)
            kk_p = kk[:, sl]
            kk_p = kk_p * lax.rsqrt(jnp.maximum(head_sum(kk_p * kk_p), 1e-24))
            b_p = kk_p * a[:, sl]
            bt = (b_p * g_inv[:, sl]).astype(BF16)
            kt = (k_mod[:, sl] * g_inv[:, sl]).astype(BF16)
            ch = dict(
                idx=bi * n_pairs + pr,
                v=v[:, sl],
                lhs=jnp.concatenate([-kk_p * g_exc[:, sl], r[:, sl] * g_inc[:, sl]], axis=0).astype(BF16),
                rhs=jnp.concatenate([blockdiag(bt), blockdiag(kt)], axis=0),
                bk=jnp.concatenate([b_p * g_end[:, sl], k_mod[:, sl] * g_end[:, sl]], axis=0).astype(BF16),
                g_all=g_all[:, sl],
                bonus=head_sum(rkr[:, sl]) * v[:, sl],
            )
            chains.append(ch)

    for ch in chains:
        x = _dot_nt(ch["lhs"], ch["rhs"])
        xb, xk = x[:, :LANES], x[:, LANES:]
        ch["a_ab"] = jnp.where(strict, xb[:csz], 0.0)
        a_ak = jnp.where(strict, xk[:csz], 0.0)
        m_rk = jnp.where(lower, xk[csz:], 0.0)
        ch["ak_rk"] = jnp.concatenate([a_ak, m_rk], axis=0).astype(BF16)
        ch["m_rb"] = jnp.where(lower, xb[csz:], 0.0).astype(BF16)
    for ch in chains:
        akv = _dot(ch["ak_rk"], blockdiag(ch["v"].astype(BF16)))
        ch["akv"], ch["mrkv"] = akv[:csz], akv[csz:]
        ch["tinv"] = eye + ch["a_ab"]
        ch["pw"] = ch["a_ab"].astype(BF16)
    n_sq = int(math.log2(csz)) - 1
    for ch in chains:
        ch["pw"] = _dot(ch["pw"], blockdiag(ch["pw"])).astype(BF16)
    for step in range(n_sq):
        for ch in chains:
            if step + 1 < n_sq:
                both = _dot(jnp.concatenate([ch["pw"], ch["tinv"].astype(BF16)], axis=0), blockdiag(ch["pw"]))
                ch["tinv"] = ch["tinv"] + both[csz:]
                ch["pw"] = both[:csz].astype(BF16)
            else:
                ch["tinv"] = ch["tinv"] + _dot(ch["tinv"], blockdiag(ch["pw"]))
    for ch in chains:
        ch["s0"] = state_ref[ch["idx"]]
        ch["as0"] = _dot_nt(ch["lhs"], ch["s0"])
    for ch in chains:
        w = (ch["as0"][:csz] + ch["akv"]).astype(BF16)
        ch["u"] = _dot(ch["tinv"], blockdiag(w))
    outs = []
    for ch in chains:
        u = ch["u"]
        y = ch["as0"][csz:] + _dot(ch["m_rb"], blockdiag(u.astype(BF16))) + ch["mrkv"]
        uv = jnp.concatenate([u, ch["v"]], axis=0)
        state_ref[ch["idx"]] = ch["s0"] * ch["g_all"] + jnp.where(same_head, _dot_tn(uv, ch["bk"]), 0.0)
        yc = y - head_sum(y) * (1.0 / n)
        var = head_sum(yc * yc) * (1.0 / n)
        outs.append(yc * lax.rsqrt(var + LNX_EPS))
    for bi in range(nb):
        yn = jnp.concatenate(outs[bi * n_pairs:(bi + 1) * n_pairs], axis=-1)
        bonus = jnp.concatenate([ch["bonus"] for ch in chains[bi * n_pairs:(bi + 1) * n_pairs]], axis=-1)
        o_ref[bi] = (yn * lw_ref[...] + lb_ref[...] + bonus).astype(BF16)


RWKV_NB = 4


def _rwkv(cols_rwkv, mu, w0, wl, a0, al, k_k, k_a, r_k, ln_w, ln_b):
    bsz, s, _ = cols_rwkv.shape
    nb = RWKV_NB if bsz % RWKV_NB == 0 else 1
    const = lambda b, c: (0, 0)
    vec = pl.BlockSpec((1, B_WIDTH), const)
    return pl.pallas_call(
        _rwkv_kernel,
        grid=(bsz // nb, s // CHUNK),
        in_specs=[pl.BlockSpec((nb, CHUNK, RWKV_COLS), lambda b, c: (b, c, 0)),
                  pl.BlockSpec((1, RWKV_COLS), const),
                  vec, pl.BlockSpec((DECAY_LORA, B_WIDTH), const),
                  vec, pl.BlockSpec((ICLR_LORA, B_WIDTH), const),
                  vec, vec, vec, vec, vec],
        out_specs=pl.BlockSpec((nb, CHUNK, B_WIDTH), lambda b, c: (b, c, 0)),
        out_shape=jax.ShapeDtypeStruct((bsz, s, B_WIDTH), BF16),
        scratch_shapes=[pltpu.VMEM((nb * B_WIDTH // LANES, LANES, LANES), F32),
                        pltpu.VMEM((nb, 8, RWKV_COLS), F32)],
        compiler_params=pltpu.CompilerParams(dimension_semantics=("parallel", "arbitrary")),
        name="rwkv",
    )(cols_rwkv, mu, w0, wl, a0, al, k_k, k_a, r_k, ln_w, ln_b)


def _final_kernel(x_ref, ya_ref, yb_ref, cf_ref, gate_ref, wa_ref, wb_ref, wo_ref, o_ref):
    a_silu = cf_ref[0, :, 0:A_WIDTH].astype(F32)
    b_silu = cf_ref[0, :, A_WIDTH:A_WIDTH + B_WIDTH].astype(F32)
    merge_a = cf_ref[0, :, A_WIDTH + B_WIDTH:A_WIDTH + B_WIDTH + D_MODEL].astype(F32)
    merge_b = cf_ref[0, :, A_WIDTH + B_WIDTH + D_MODEL:A_WIDTH + B_WIDTH + 2 * D_MODEL].astype(F32)
    ya = ya_ref[0].astype(F32) * (a_silu * _sigmoid(a_silu))
    yb = yb_ref[0].astype(F32) * (b_silu * _sigmoid(b_silu))
    merged = _sigmoid(merge_a) * _dot(ya, wa_ref[...]) + _sigmoid(merge_b) * _dot(yb, wb_ref[...])
    o_ref[0] = x_ref[0] + gate_ref[0] * _dot(merged, wo_ref[...])


def _final(x, y_a, y_b, cols_fin, gate, w_out_a, w_out_b, w_o, tm=256):
    bsz, s, _ = x.shape
    const = lambda b, i: (0, 0)
    row = lambda w: pl.BlockSpec((1, tm, w), lambda b, i: (b, i, 0))
    return pl.pallas_call(
        _final_kernel,
        grid=(bsz, s // tm),
        in_specs=[row(D_MODEL), row(A_WIDTH), row(B_WIDTH), row(FIN_COLS),
                  pl.BlockSpec((1, 1, D_MODEL), lambda b, i: (b, 0, 0)),
                  pl.BlockSpec((A_WIDTH, D_MODEL), const),
                  pl.BlockSpec((B_WIDTH, D_MODEL), const),
                  pl.BlockSpec((D_MODEL, D_MODEL), const)],
        out_specs=row(D_MODEL),
        out_shape=jax.ShapeDtypeStruct((bsz, s, D_MODEL), F32),
        compiler_params=pltpu.CompilerParams(dimension_semantics=("parallel", "parallel"),
                                             vmem_limit_bytes=VMEM_LIMIT),
        name="final",
    )(x, y_a, y_b, cols_fin, gate, w_out_a, w_out_b, w_o)


def _split_w_in(w_in):
    nsa_in = 2 * A_WIDTH + 6 * A_KV_WIDTH + 3 * A_HEADS
    o_gate = A_WIDTH + 6 * A_KV_WIDTH
    o_asilu = o_gate + 3 * A_HEADS
    o_shift = nsa_in
    o_rest = nsa_in + RWKV_COLS
    gate_w = w_in[:, o_gate:o_asilu].reshape(D_MODEL, 3, A_KV_GROUPS, A_HPG)
    gate_w = gate_w.transpose(0, 2, 1, 3).reshape(D_MODEL, A_KV_GROUPS, 3 * A_HPG)
    gate_w = jnp.pad(gate_w, ((0, 0), (0, 0), (0, LANES - 3 * A_HPG))).reshape(D_MODEL, GATE_PAD)
    w_nsa = jnp.concatenate([w_in[:, :o_gate], gate_w], axis=1)
    w_fin = jnp.concatenate([w_in[:, o_asilu:o_shift], w_in[:, o_rest:]], axis=1)
    w_rwkv = w_in[:, o_shift:o_rest]
    return w_nsa.astype(BF16), w_fin.astype(BF16), w_rwkv.astype(BF16)


def _layer(x, c, rel_bias, w_ada, b_ada, norm_gain, w_in, q_norm_gain, k_norm_gain,
           cmp_pos_k, cmp_pos_v, cmp_k_w1, cmp_k_w2, cmp_v_w1, cmp_v_w2,
           shift_mu, w0, w_lora_up, a0, a_lora_up, k_k, k_a, r_k, ln_x_w, ln_x_b,
           w_out_a, w_out_b, w_o):
    bsz, s, _ = x.shape
    assert s % (2 * TQ) == 0 and s // CMP_STRIDE == LANES
    n16 = s // CMP_STRIDE
    mod = _ada(c, w_ada, b_ada)
    w_nsa, w_fin, w_rwkv = _split_w_in(w_in)
    cols_nsa, cols_fin, cols_rwkv = _proj(x, mod, norm_gain, w_nsa, w_fin, w_rwkv)

    scale = A_HEAD_DIM ** -0.5
    qg = (jnp.tile(q_norm_gain, A_HEADS) * scale).reshape(1, A_WIDTH)
    ksg = jnp.tile(k_norm_gain[1], A_KV_GROUPS).reshape(1, A_KV_WIDTH)
    kwg = jnp.tile(k_norm_gain[2], A_KV_GROUPS).reshape(1, A_KV_WIDTH)
    seg = np.arange(A_WIDTH) // A_HEAD_DIM
    bd = jnp.asarray((seg[:, None] == seg[None, :]).astype(np.float32) / A_HEAD_DIM, BF16)
    q_t, ks, vs_t, kw, vw_t, gates_t = _nsaprep(cols_nsa, qg, ksg, kwg, bd)

    kc, vc_t = _compress(cols_nsa, _expand_cmp_pos(cmp_pos_k), _expand_cmp_pos(cmp_pos_v),
                         _expand_cmp_w1(cmp_k_w1), cmp_k_w2.astype(BF16),
                         _expand_cmp_w1(cmp_v_w1), cmp_v_w2.T.astype(BF16),
                         k_norm_gain[0].reshape(1, A_HEAD_DIM))
    bias_c, bias_d = _bias_tables(rel_bias, s, n16)
    y_a = _attention(q_t, kc, vc_t, ks, vs_t, kw, vw_t, bias_c, bias_d, gates_t)

    vec = lambda t: t.reshape(1, -1)
    y_b = _rwkv(cols_rwkv, vec(shift_mu), vec(w0), w_lora_up.astype(BF16), vec(a0), a_lora_up.astype(BF16),
                vec(k_k), vec(k_a), vec(r_k), vec(ln_x_w), vec(ln_x_b))

    gate = mod[:, 2 * D_MODEL:].reshape(bsz, 1, D_MODEL)
    return _final(x, y_a, y_b, cols_fin, gate, w_out_a.astype(BF16), w_out_b.astype(BF16), w_o.astype(BF16))


def kernel(x, c, w_ada, b_ada, norm_gain, w_in, q_norm_gain, k_norm_gain, cmp_pos_k, cmp_pos_v, cmp_k_w1, cmp_k_w2, cmp_v_w1, cmp_v_w2, rel_bias, shift_mu, w0, w_lora_up, a0, a_lora_up, k_k, k_a, r_k, ln_x_w, ln_x_b, w_out_a, w_out_b, w_o):
    for l in range(w_in.shape[0]):
        x = _layer(x, c, rel_bias, w_ada[l], b_ada[l], norm_gain[l], w_in[l], q_norm_gain[l], k_norm_gain[l],
                   cmp_pos_k[l], cmp_pos_v[l], cmp_k_w1[l], cmp_k_w2[l], cmp_v_w1[l], cmp_v_w2[l],
                   shift_mu[l], w0[l], w_lora_up[l], a0[l], a_lora_up[l], k_k[l], k_a[l], r_k[l],
                   ln_x_w[l], ln_x_b[l], w_out_a[l], w_out_b[l], w_o[l])
    return x
```

```python
import functools
import math

import numpy as np
import jax
import jax.numpy as jnp
from jax import lax
from jax.experimental import pallas as pl
from jax.experimental.pallas import tpu as pltpu

F32 = jnp.float32
BF16 = jnp.bfloat16

D_MODEL = 1024
A_HEADS = 8
A_HEAD_DIM = 64
A_KV_GROUPS = 2
A_HPG = A_HEADS // A_KV_GROUPS
A_WIDTH = A_HEADS * A_HEAD_DIM
A_KV_WIDTH = A_KV_GROUPS * A_HEAD_DIM
CMP_BLOCK = 32
CMP_STRIDE = 16
CMP_HIDDEN = 256
SLC_BLOCK = 64
SLC_TOPN = 16
WINDOW = 512
B_HEADS = 8
B_HEAD_DIM = 64
B_WIDTH = B_HEADS * B_HEAD_DIM
DECAY_LORA = 64
ICLR_LORA = 64
LNX_EPS = 64e-5
REL_BUCKETS = 32
REL_MAX_EXACT = 16
REL_MAX_DIST = 128
NORM_EPS = 1e-6
NEG_INF = -1e30
FORCE_SCORE = 1e30

LANES = 128
TQ = 128
CHUNK = 64
GATE_PAD = 2 * LANES
LOG2E = math.log2(math.e)
V_ROWS = A_HEAD_DIM + 16
GATE_ROWS = 16
NSA_COLS = A_WIDTH + 6 * A_KV_WIDTH + GATE_PAD
FIN_COLS = A_WIDTH + B_WIDTH + 2 * D_MODEL
RWKV_COLS = 3 * B_WIDTH + DECAY_LORA + ICLR_LORA
VMEM_LIMIT = 56 * 1024 * 1024


def _dot(a, b):
    return jnp.dot(a.astype(BF16), b.astype(BF16), preferred_element_type=F32)


def _dot_nt(a, b):
    return lax.dot_general(a.astype(BF16), b.astype(BF16), (((1,), (1,)), ((), ())),
                           preferred_element_type=F32)


def _dot_tn(a, b):
    return lax.dot_general(a.astype(BF16), b.astype(BF16), (((0,), (0,)), ((), ())),
                           preferred_element_type=F32)


def _split2(x):
    hi = x.astype(BF16)
    lo = (x - hi.astype(F32)).astype(BF16)
    return hi, lo


def _split3(x):
    h1 = x.astype(BF16)
    r1 = x - h1.astype(F32)
    h2 = r1.astype(BF16)
    h3 = (r1 - h2.astype(F32)).astype(BF16)
    return h1, h2, h3


def _sigmoid(x):
    return 1.0 / (1.0 + jnp.exp(-x))


def _bucket_thresholds():
    n = np.arange(0, 4096)
    nf = np.maximum(n, REL_MAX_EXACT).astype(np.float64)
    val = np.log(nf / REL_MAX_EXACT) / math.log(REL_MAX_DIST / REL_MAX_EXACT) * (REL_BUCKETS - REL_MAX_EXACT)
    frac = np.abs(val - np.round(val))
    assert np.all((frac > 1e-4) | (n <= REL_MAX_EXACT) | (n >= REL_MAX_DIST))
    large = REL_MAX_EXACT + np.floor(val + 1e-9).astype(np.int64)
    bucket = np.where(n < REL_MAX_EXACT, n, np.minimum(large, REL_BUCKETS - 1))
    return [int(np.argmax(bucket >= j)) for j in range(REL_BUCKETS)]


_BUCKET_TH = _bucket_thresholds()


def _bias_from_dist(dist, tbl_ref, head):
    val = jnp.full(dist.shape, tbl_ref[0, head], F32)
    for j in range(1, REL_BUCKETS):
        val = jnp.where(dist >= _BUCKET_TH[j], tbl_ref[j, head], val)
    return val


def _ada_kernel(c_ref, w_ref, b_ref, o_ref):
    c = c_ref[...]
    o_ref[...] = _dot(c * _sigmoid(c), w_ref[...]) + b_ref[...]


def _ada(c, w_ada, b_ada):
    bsz = c.shape[0]
    return pl.pallas_call(
        _ada_kernel,
        grid=(3,),
        in_specs=[pl.BlockSpec((bsz, D_MODEL), lambda j: (0, 0)),
                  pl.BlockSpec((D_MODEL, D_MODEL), lambda j: (0, j)),
                  pl.BlockSpec((1, D_MODEL), lambda j: (0, j))],
        out_specs=pl.BlockSpec((bsz, D_MODEL), lambda j: (0, j)),
        out_shape=jax.ShapeDtypeStruct((bsz, 3 * D_MODEL), F32),
        name="ada",
    )(c, w_ada, b_ada.reshape(1, 3 * D_MODEL))


def _proj_kernel(x_ref, mod_ref, g_ref, wn_ref, wf_ref, wr_ref, on_ref, of_ref, or_ref):
    x = x_ref[0]
    ms = jnp.mean(x * x, axis=-1, keepdims=True)
    y = x * lax.rsqrt(ms + NORM_EPS) * g_ref[...]
    mod = mod_ref[0]
    h = (y * (1.0 + mod[:, D_MODEL:2 * D_MODEL]) + mod[:, :D_MODEL]).astype(BF16)
    on_ref[0] = jnp.dot(h, wn_ref[...], preferred_element_type=F32)
    of_ref[0] = jnp.dot(h, wf_ref[...], preferred_element_type=F32).astype(BF16)
    or_ref[0] = jnp.dot(h, wr_ref[...], preferred_element_type=F32)


def _proj(x, mod, norm_gain, w_nsa, w_fin, w_rwkv, tm=256):
    bsz, s, _ = x.shape
    const = lambda b, i: (0, 0)
    return pl.pallas_call(
        _proj_kernel,
        grid=(bsz, s // tm),
        in_specs=[pl.BlockSpec((1, tm, D_MODEL), lambda b, i: (b, i, 0)),
                  pl.BlockSpec((1, 1, 3 * D_MODEL), lambda b, i: (b, 0, 0)),
                  pl.BlockSpec((1, D_MODEL), const),
                  pl.BlockSpec((D_MODEL, NSA_COLS), const),
                  pl.BlockSpec((D_MODEL, FIN_COLS), const),
                  pl.BlockSpec((D_MODEL, RWKV_COLS), const)],
        out_specs=[pl.BlockSpec((1, tm, NSA_COLS), lambda b, i: (b, i, 0)),
                   pl.BlockSpec((1, tm, FIN_COLS), lambda b, i: (b, i, 0)),
                   pl.BlockSpec((1, tm, RWKV_COLS), lambda b, i: (b, i, 0))],
        out_shape=[jax.ShapeDtypeStruct((bsz, s, NSA_COLS), F32),
                   jax.ShapeDtypeStruct((bsz, s, FIN_COLS), BF16),
                   jax.ShapeDtypeStruct((bsz, s, RWKV_COLS), F32)],
        compiler_params=pltpu.CompilerParams(dimension_semantics=("parallel", "parallel"),
                                             vmem_limit_bytes=VMEM_LIMIT),
        name="proj",
    )(x, mod.reshape(bsz, 1, 3 * D_MODEL), norm_gain.reshape(1, D_MODEL), w_nsa, w_fin, w_rwkv)


def _seg_mean(x2, bd):
    hi, lo = _split2(x2)
    return (jnp.dot(hi, bd, preferred_element_type=F32) + jnp.dot(lo, bd, preferred_element_type=F32))


def _nsaprep_kernel(c_ref, qg_ref, ksg_ref, kwg_ref, bd_ref, q_ref, ks_ref, vs_ref, kw_ref, vw_ref, gt_ref):
    bd = bd_ref[...]
    q = c_ref[0, :, 0:A_WIDTH]
    qn_t = (q * lax.rsqrt(_seg_mean(q * q, bd) + NORM_EPS) * qg_ref[...]).T
    off = A_WIDTH + 2 * A_KV_WIDTH
    k_slc = c_ref[0, :, off:off + A_KV_WIDTH]
    v_slc = c_ref[0, :, off + A_KV_WIDTH:off + 2 * A_KV_WIDTH]
    k_win = c_ref[0, :, off + 2 * A_KV_WIDTH:off + 3 * A_KV_WIDTH]
    v_win = c_ref[0, :, off + 3 * A_KV_WIDTH:off + 4 * A_KV_WIDTH]
    bd_kv = bd[:A_KV_WIDTH, :A_KV_WIDTH]
    ksn = k_slc * lax.rsqrt(_seg_mean(k_slc * k_slc, bd_kv) + NORM_EPS) * ksg_ref[...]
    kwn = (k_win * lax.rsqrt(_seg_mean(k_win * k_win, bd_kv) + NORM_EPS) * kwg_ref[...]).astype(BF16)
    vs_t = v_slc.T.astype(BF16)
    vw_t = v_win.T.astype(BF16)
    tq = c_ref.shape[1]
    lane = lax.broadcasted_iota(jnp.int32, (tq, LANES), 1)
    blk = (pl.program_id(1) * tq + lax.broadcasted_iota(jnp.int32, (tq, LANES), 0)) // SLC_BLOCK
    onehot = jnp.where(lane - A_HEAD_DIM == blk, 1.0, 0.0)
    gate0 = A_WIDTH + 6 * A_KV_WIDTH
    for g in range(A_KV_GROUPS):
        heads = [qn_t[(g * A_HPG + h) * A_HEAD_DIM:(g * A_HPG + h + 1) * A_HEAD_DIM, :] for h in range(A_HPG)]
        q_ref[0, g, 0] = jnp.concatenate(heads, axis=1).astype(BF16)
        sl = slice(g * A_HEAD_DIM, (g + 1) * A_HEAD_DIM)
        k_g = ksn if g == 0 else pltpu.roll(ksn, A_HEAD_DIM, axis=1)
        ks_ref[0, g] = jnp.where(lane < A_HEAD_DIM, k_g, onehot).astype(BF16)
        kw_ref[0, g] = kwn[:, sl]
        ones_rows = (lax.broadcasted_iota(jnp.int32, (V_ROWS - A_HEAD_DIM, tq), 0) == 0).astype(BF16)
        vs_ref[0, g, 0] = jnp.concatenate([vs_t[sl, :], ones_rows], axis=0)
        vw_ref[0, g, 0] = jnp.concatenate([vw_t[sl, :], ones_rows], axis=0)
        gates_t = _sigmoid(c_ref[0, :, gate0 + g * LANES:gate0 + (g + 1) * LANES]).T
        gt_ref[0, g, 0] = gates_t[0:GATE_ROWS, :]


def _nsaprep(cols_nsa, qg, ksg, kwg, bd):
    bsz, s, _ = cols_nsa.shape
    nt = s // TQ
    const = lambda b, i: (0, 0)
    assert A_HEAD_DIM + s // SLC_BLOCK <= LANES
    k_spec = pl.BlockSpec((1, A_KV_GROUPS, TQ, A_HEAD_DIM), lambda b, i: (b, 0, i, 0))
    k_shape = jax.ShapeDtypeStruct((bsz, A_KV_GROUPS, s, A_HEAD_DIM), BF16)
    ka_spec = pl.BlockSpec((1, A_KV_GROUPS, TQ, LANES), lambda b, i: (b, 0, i, 0))
    ka_shape = jax.ShapeDtypeStruct((bsz, A_KV_GROUPS, s, LANES), BF16)
    vt_spec = pl.BlockSpec((1, A_KV_GROUPS, 1, V_ROWS, TQ), lambda b, i: (b, 0, i, 0, 0))
    vt_shape = jax.ShapeDtypeStruct((bsz, A_KV_GROUPS, nt, V_ROWS, TQ), BF16)
    return pl.pallas_call(
        _nsaprep_kernel,
        grid=(bsz, nt),
        in_specs=[pl.BlockSpec((1, TQ, NSA_COLS), lambda b, i: (b, i, 0)),
                  pl.BlockSpec((1, A_WIDTH), const),
                  pl.BlockSpec((1, A_KV_WIDTH), const),
                  pl.BlockSpec((1, A_KV_WIDTH), const),
                  pl.BlockSpec((A_WIDTH, A_WIDTH), const)],
        out_specs=[pl.BlockSpec((1, A_KV_GROUPS, 1, A_HEAD_DIM, A_HPG * TQ), lambda b, i: (b, 0, i, 0, 0)),
                   ka_spec, vt_spec, k_spec, vt_spec,
                   pl.BlockSpec((1, A_KV_GROUPS, 1, GATE_ROWS, TQ), lambda b, i: (b, 0, i, 0, 0))],
        out_shape=[jax.ShapeDtypeStruct((bsz, A_KV_GROUPS, nt, A_HEAD_DIM, A_HPG * TQ), BF16),
                   ka_shape, vt_shape, k_shape, vt_shape,
                   jax.ShapeDtypeStruct((bsz, A_KV_GROUPS, nt, GATE_ROWS, TQ), F32)],
        compiler_params=pltpu.CompilerParams(dimension_semantics=("parallel", "parallel")),
        name="nsaprep",
    )(cols_nsa, qg, ksg, kwg, bd)


def _compress_kernel(ck_ref, cv_ref, pk_ref, pv_ref, w1k_ref, w2k_ref, w1v_ref, w2v_ref, kg_ref, kc_ref, vc_ref):
    n16 = ck_ref.shape[1] // CMP_STRIDE

    def rows16(ref):
        return jnp.concatenate([ref[0, pl.ds(p, n16, stride=CMP_STRIDE), :] for p in range(CMP_STRIDE)], axis=1)

    def hidden(z, pos_ref, w1_ref, g):
        top = _dot(z + pos_ref[0:1, :], w1_ref[g, 0])
        bot = _dot(z + pos_ref[1:2, :], w1_ref[g, 1])
        return jax.nn.gelu(top + pltpu.roll(bot, n16 - 1, axis=0), approximate=True)

    zk = rows16(ck_ref)
    zv = rows16(cv_ref)
    for g in range(A_KV_GROUPS):
        kc = _dot(hidden(zk, pk_ref, w1k_ref, g), w2k_ref[...])
        ms = jnp.mean(kc * kc, axis=-1, keepdims=True)
        kc_ref[0, g] = (kc * lax.rsqrt(ms + NORM_EPS) * kg_ref[...]).astype(BF16)
        vc_ref[0, g] = _dot_nt(w2v_ref[...], hidden(zv, pv_ref, w1v_ref, g)).astype(BF16)


def _expand_cmp_w1(w1):
    w = w1.reshape(2, CMP_STRIDE, 1, A_HEAD_DIM, CMP_HIDDEN)
    per_group = []
    for g in range(A_KV_GROUPS):
        pad = [(0, 0), (0, 0), (g, A_KV_GROUPS - 1 - g), (0, 0), (0, 0)]
        per_group.append(jnp.pad(w, pad).reshape(2, CMP_STRIDE * A_KV_WIDTH, CMP_HIDDEN))
    return jnp.stack(per_group).astype(BF16)


def _expand_cmp_pos(pos):
    p = jnp.broadcast_to(pos.reshape(2, CMP_STRIDE, 1, A_HEAD_DIM), (2, CMP_STRIDE, A_KV_GROUPS, A_HEAD_DIM))
    return p.reshape(2, CMP_STRIDE * A_KV_WIDTH)


def _compress(cols_nsa, pk, pv, w1k, w2k, w1v, w2v_t, kg):
    bsz, s, _ = cols_nsa.shape
    n16 = s // CMP_STRIDE
    zw = CMP_STRIDE * A_KV_WIDTH
    const = lambda b: (0, 0)
    const4 = lambda b: (0, 0, 0, 0)
    kblk = A_WIDTH // A_KV_WIDTH
    return pl.pallas_call(
        _compress_kernel,
        grid=(bsz,),
        in_specs=[pl.BlockSpec((1, s, A_KV_WIDTH), lambda b: (b, 0, kblk)),
                  pl.BlockSpec((1, s, A_KV_WIDTH), lambda b: (b, 0, kblk + 1)),
                  pl.BlockSpec((2, zw), const), pl.BlockSpec((2, zw), const),
                  pl.BlockSpec((A_KV_GROUPS, 2, zw, CMP_HIDDEN), const4), pl.BlockSpec((CMP_HIDDEN, A_HEAD_DIM), const),
                  pl.BlockSpec((A_KV_GROUPS, 2, zw, CMP_HIDDEN), const4), pl.BlockSpec((A_HEAD_DIM, CMP_HIDDEN), const),
                  pl.BlockSpec((1, A_HEAD_DIM), const)],
        out_specs=[pl.BlockSpec((1, A_KV_GROUPS, n16, A_HEAD_DIM), lambda b: (b, 0, 0, 0)),
                   pl.BlockSpec((1, A_KV_GROUPS, A_HEAD_DIM, n16), lambda b: (b, 0, 0, 0))],
        out_shape=[jax.ShapeDtypeStruct((bsz, A_KV_GROUPS, n16, A_HEAD_DIM), BF16),
                   jax.ShapeDtypeStruct((bsz, A_KV_GROUPS, A_HEAD_DIM, n16), BF16)],
        compiler_params=pltpu.CompilerParams(dimension_semantics=("parallel",)),
        name="compress",
    )(cols_nsa, cols_nsa, pk, pv, w1k, w2k, w1v, w2v_t, kg)


TILE_FAR, TILE_EDGE, TILE_MASKED, N_BIAS_TILES = 2, 3, 4, 5
SUB = 4


def _bias_cmp_kernel(tbl_ref, o_ref):
    i = pl.program_id(0)
    g = pl.program_id(1)
    n_cmp = o_ref.shape[2]
    n = lax.broadcasted_iota(jnp.int32, (n_cmp, TQ), 0)
    q = lax.broadcasted_iota(jnp.int32, (n_cmp, TQ), 1)
    dist = i * TQ + q - (n * CMP_STRIDE + CMP_BLOCK - 1)
    for h in range(A_HPG):
        bias = _bias_from_dist(dist, tbl_ref, g * A_HPG + h)
        o_ref[0, 0, :, h * TQ:(h + 1) * TQ] = jnp.where(dist >= 0, bias * LOG2E, NEG_INF)


def _bias_toeplitz_kernel(tbl_ref, o_ref):
    g = pl.program_id(0)
    r = pl.program_id(1)
    off = jnp.where(r == TILE_EDGE, WINDOW // TQ, jnp.where(r == TILE_MASKED, -2, r))
    k = lax.broadcasted_iota(jnp.int32, (TQ, TQ), 0)
    q = lax.broadcasted_iota(jnp.int32, (TQ, TQ), 1)
    dist = off * TQ + q - k
    valid = (dist >= 0) & (dist < WINDOW)
    for h in range(A_HPG):
        bias = _bias_from_dist(dist, tbl_ref, g * A_HPG + h)
        o_ref[0, 0, :, h * TQ:(h + 1) * TQ] = jnp.where(valid, bias * LOG2E, NEG_INF)


def _bias_tables(rel_bias, s, n_cmp):
    smem = pl.BlockSpec(memory_space=pltpu.SMEM)
    nt = s // TQ
    bias_c = pl.pallas_call(
        _bias_cmp_kernel,
        grid=(nt, A_KV_GROUPS),
        in_specs=[smem],
        out_specs=pl.BlockSpec((1, 1, n_cmp, A_HPG * TQ), lambda i, g: (i, g, 0, 0)),
        out_shape=jax.ShapeDtypeStruct((nt, A_KV_GROUPS, n_cmp, A_HPG * TQ), F32),
        name="bias_cmp",
    )(rel_bias)
    assert _BUCKET_TH[REL_BUCKETS - 1] <= TQ + 1 and WINDOW // TQ >= 3
    bias_d = pl.pallas_call(
        _bias_toeplitz_kernel,
        grid=(A_KV_GROUPS, N_BIAS_TILES),
        in_specs=[smem],
        out_specs=pl.BlockSpec((1, 1, TQ, A_HPG * TQ), lambda g, r: (g, r, 0, 0)),
        out_shape=jax.ShapeDtypeStruct((A_KV_GROUPS, N_BIAS_TILES, TQ, A_HPG * TQ), F32),
        name="bias_toeplitz",
    )(rel_bias)
    return bias_c, bias_d


def _attn_kernel(qa_ref, qb_ref, kc_ref, vc_ref, ks_ref, vs_ref, kw_ref, vw_ref, bca_ref, bcb_ref, bd_ref,
                 gta_ref, gtb_ref, o_ref, m_ref, acc_ref, part_ref):
    j = pl.program_id(2)
    tq = TQ
    n_cmp = kc_ref.shape[2]
    n_slc = ks_ref.shape[2] // SLC_BLOCK
    wt = WINDOW // tq
    dh = A_HEAD_DIM
    n_tiles = 2
    tiles = [dict(i=j, q=qa_ref[0, 0, 0], bias_c=bca_ref, gates=gta_ref[0, 0, 0]),
             dict(i=j + pl.num_programs(2), q=qb_ref[0, 0, 0], bias_c=bcb_ref, gates=gtb_ref[0, 0, 0])]

    def scores(k_slab, q_mat, i, first_tile, n_sub, tile_index):
        s = jnp.dot(k_slab, q_mat, preferred_element_type=F32)
        parts = [s[t * tq:(t + 1) * tq] + bd_ref[0, tile_index(i - (first_tile + t))] for t in range(n_sub)]
        return jnp.concatenate(parts, axis=0)

    def values_t(v_ref, first_tile, n_sub):
        return jnp.concatenate([v_ref[0, 0, first_tile + t] for t in range(n_sub)], axis=1)

    win_tile = lambda r: jnp.where(r < 0, TILE_MASKED, jnp.where(r == wt, TILE_EDGE, jnp.minimum(r, TILE_FAR)))
    for t in tiles:
        t["first_w"] = jnp.maximum(t["i"] - wt, 0)
        k0 = pl.multiple_of(t["first_w"] * tq, tq)
        t["s_w"] = scores(kw_ref[0, 0, pl.ds(k0, (wt + 1) * tq), :], t["q"], t["i"], t["first_w"], wt + 1, win_tile)
    for t in tiles:
        bias = t["bias_c"][0, 0]
        t["valid_c"] = bias > 0.5 * NEG_INF
        t["s_c"] = jnp.dot(kc_ref[0, 0], t["q"], preferred_element_type=F32) + bias

    r1, r2 = SLC_BLOCK // CMP_STRIDE, CMP_BLOCK // CMP_STRIDE
    jj = lax.broadcasted_iota(jnp.int32, (n_slc, n_cmp), 0)
    nn = lax.broadcasted_iota(jnp.int32, (n_slc, n_cmp), 1)
    d = nn - r1 * jj
    cnt = jnp.zeros((n_slc, n_cmp), F32)
    for a in range(r1):
        for c in range(r2):
            cnt = cnt + jnp.where(d == a - c, 1.0, 0.0)
    cnt = cnt.astype(BF16)
    for t in tiles:
        s = t["s_c"]
        e = jnp.where(t["valid_c"], jnp.exp2(s - jnp.max(s, axis=0, keepdims=True)), 0.0)
        l = jnp.sum(e, axis=0, keepdims=True)
        p = e * (1.0 / jnp.where(l > 0.0, l, 1.0))
        t["out_c"] = jnp.dot(vc_ref[0, 0], p.astype(BF16), preferred_element_type=F32)
        p_grp = sum(p[:, h * tq:(h + 1) * tq] for h in range(A_HPG))
        t["imp"] = sum(jnp.dot(cnt, part, preferred_element_type=F32) for part in _split3(p_grp))

    for t in tiles:
        s = t["s_w"]
        t["p_w"] = jnp.exp2(s - jnp.max(s, axis=0, keepdims=True)).astype(BF16)
    for k, t in enumerate(tiles):
        acc = jnp.dot(values_t(vw_ref, t["first_w"], wt + 1), t["p_w"], preferred_element_type=F32)
        out_w = acc[:dh] * (1.0 / acc[dh:dh + 1])
        gates = t["gates"]
        for h in range(A_HPG):
            cols = slice(h * tq, (h + 1) * tq)
            part_ref[k, :, cols] = (gates[h:h + 1, :] * t["out_c"][:, cols]
                                    + gates[2 * A_HPG + h:2 * A_HPG + h + 1, :] * out_w[:, cols])

    blk = lax.broadcasted_iota(jnp.int32, (n_slc, tq), 0)
    for t in tiles:
        tpos = t["i"] * tq + lax.broadcasted_iota(jnp.int32, (n_slc, tq), 1)
        cur = tpos // SLC_BLOCK
        forced = (blk == 0) | (blk == cur) | (blk == cur - 1)
        causal = blk * SLC_BLOCK <= tpos
        imp = jnp.where(forced, FORCE_SCORE, jnp.where(causal, t["imp"], NEG_INF))
        rank = jnp.zeros((n_slc, tq), F32)
        for c in range(n_slc):
            row = imp[c:c + 1, :]
            ahead = (row > imp) | ((row == imp) & (blk > c))
            rank = rank + jnp.where(ahead, 1.0, 0.0)
        pen = jnp.where(rank < float(min(SLC_TOPN, n_slc)), 0.0, -FORCE_SCORE).astype(BF16)
        t["q_aug"] = jnp.concatenate([t["q"], jnp.concatenate([pen] * A_HPG, axis=1),
                                      jnp.zeros((LANES - A_HEAD_DIM - n_slc, A_HPG * tq), BF16)], axis=0)
        t["last"] = t["i"] // SUB

    sel_tile = lambda r: jnp.where(r < 0, TILE_MASKED, jnp.minimum(r, TILE_FAR))

    def partials(jobs):
        ss = []
        for t, c in jobs:
            k0 = pl.multiple_of(c * (SUB * tq), SUB * tq)
            ss.append(scores(ks_ref[0, 0, pl.ds(k0, SUB * tq), :], t["q_aug"], t["i"], c * SUB, SUB, sel_tile))
        ms = [jnp.max(s, axis=0, keepdims=True) for s in ss]
        ps = [jnp.exp2(s - m).astype(BF16) for s, m in zip(ss, ms)]
        accs = [jnp.dot(values_t(vs_ref, c * SUB, SUB), p, preferred_element_type=F32)
                for (t, c), p in zip(jobs, ps)]
        return list(zip(ms, accs))

    def merge(parts):
        m = functools.reduce(jnp.maximum, [p[0] for p in parts])
        return m, sum(jnp.exp2(p[0] - m) * p[1] for p in parts)

    ta, tb = tiles
    main = partials([(ta, ta["last"]), (tb, tb["last"]), (tb, tb["last"] - 1), (tb, tb["last"] - 2)])
    for k, parts in enumerate([main[:1], main[1:]]):
        m_ref[k], acc_ref[k] = merge(parts)

    @pl.when(ta["last"] > 0)
    def _():
        extra = partials([(ta, 0), (tb, 0)])
        for k in range(n_tiles):
            m_ref[k], acc_ref[k] = merge([(m_ref[k], acc_ref[k]), extra[k]])

    for k, t in enumerate(tiles):
        out_s = acc_ref[k, :dh] * (1.0 / acc_ref[k, dh:dh + 1])
        gates = t["gates"]
        blocks = []
        for h in range(A_HPG):
            cols = slice(h * tq, (h + 1) * tq)
            blocks.append(part_ref[k, :, cols] + gates[A_HPG + h:A_HPG + h + 1, :] * out_s[:, cols])
        o_ref[0, k] = jnp.concatenate(blocks, axis=0).T.astype(BF16)


def _attention(q_t, kc, vc_t, ks, vs_t, kw, vw_t, bias_c, bias_d, gates_t):
    bsz, _, nt, _, _ = q_t.shape
    s = ks.shape[2]
    n_cmp = kc.shape[2]
    assert nt == 4 * SUB and WINDOW // TQ + 1 <= nt
    half = nt // 2
    k_spec = lambda width: pl.BlockSpec((1, 1, s, width), lambda b, g, j: (b, g, 0, 0))
    vt_spec = pl.BlockSpec((1, 1, nt, V_ROWS, TQ), lambda b, g, j: (b, g, 0, 0, 0))
    q_spec = lambda off: pl.BlockSpec((1, 1, 1, A_HEAD_DIM, A_HPG * TQ), lambda b, g, j: (b, g, j + off, 0, 0))
    bc_spec = lambda off: pl.BlockSpec((1, 1, n_cmp, A_HPG * TQ), lambda b, g, j: (j + off, g, 0, 0))
    gt_spec = lambda off: pl.BlockSpec((1, 1, 1, GATE_ROWS, TQ), lambda b, g, j: (b, g, j + off, 0, 0))
    out = pl.pallas_call(
        _attn_kernel,
        grid=(bsz, A_KV_GROUPS, half),
        in_specs=[q_spec(0), q_spec(half),
                  pl.BlockSpec((1, 1, n_cmp, A_HEAD_DIM), lambda b, g, j: (b, g, 0, 0)),
                  pl.BlockSpec((1, 1, A_HEAD_DIM, n_cmp), lambda b, g, j: (b, g, 0, 0)),
                  k_spec(LANES), vt_spec, k_spec(A_HEAD_DIM), vt_spec,
                  bc_spec(0), bc_spec(half),
                  pl.BlockSpec((1, N_BIAS_TILES, TQ, A_HPG * TQ), lambda b, g, j: (g, 0, 0, 0)),
                  gt_spec(0), gt_spec(half)],
        out_specs=pl.BlockSpec((1, 2, TQ, A_HPG * A_HEAD_DIM), lambda b, g, j: (b, 0, j, g)),
        out_shape=jax.ShapeDtypeStruct((bsz, 2, s // 2, A_WIDTH), BF16),
        scratch_shapes=[pltpu.VMEM((2, 1, A_HPG * TQ), F32),
                        pltpu.VMEM((2, V_ROWS, A_HPG * TQ), F32),
                        pltpu.VMEM((2, A_HEAD_DIM, A_HPG * TQ), F32)],
        compiler_params=pltpu.CompilerParams(dimension_semantics=("parallel", "parallel", "parallel")),
        name="attn",
    )(q_t, q_t, kc, vc_t, ks, vs_t, kw, vw_t, bias_c, bias_c, bias_d, gates_t, gates_t)
    return out.reshape(bsz, s, A_WIDTH)


def _rwkv_kernel(c_ref, mu_ref, w0_ref, wl_ref, a0_ref, al_ref, kk_ref, ka_ref, rk_ref, lw_ref, lb_ref,
                 o_ref, state_ref, prev_ref):
    cc = pl.program_id(1)
    n = B_HEAD_DIM
    nb, csz = c_ref.shape[0], c_ref.shape[1]

    @pl.when(cc == 0)
    def _():
        state_ref[...] = jnp.zeros(state_ref.shape, F32)
        prev_ref[...] = jnp.zeros(prev_ref.shape, F32)

    ti = lax.broadcasted_iota(jnp.int32, (csz, LANES), 0)
    si = lax.broadcasted_iota(jnp.int32, (csz, LANES), 1) % n
    lower = si <= ti
    strict = si < ti
    eye = jnp.where(si == ti, 1.0, 0.0)
    tri = jnp.where(lax.broadcasted_iota(jnp.int32, (csz, csz), 1) <= lax.broadcasted_iota(jnp.int32, (csz, csz), 0),
                    1.0, 0.0).astype(BF16)
    n_pairs = B_WIDTH // LANES
    left =lax.broadcasted_iota(jnp.int32, (csz, LANES), 1) < n
    row_left = lax.broadcasted_iota(jnp.int32, (LANES, LANES), 0) < n
    same_head = row_left == (lax.broadcasted_iota(jnp.int32, (LANES, LANES), 1) < n)

    def blockdiag(y):
        zero = jnp.zeros_like(y)
        return jnp.concatenate([jnp.where(left, y, zero), jnp.where(left, zero, y)], axis=0)

    def head_sum(x):
        lo = jnp.sum(jnp.where(left, x, 0.0), axis=-1, keepdims=True)
        hi = jnp.sum(jnp.where(left, 0.0, x), axis=-1, keepdims=True)
        return jnp.where(left, lo, hi)

    chains = []
    for bi in range(nb):
        p = c_ref[bi]
        row = lax.broadcasted_iota(jnp.int32, p.shape, 0)
        prev = jnp.where(row == 0, prev_ref[bi, 0:1, :], pltpu.roll(p, 1, axis=0))
        prev_ref[bi, 0:1, :] = p[csz - 1:csz, :]
        x = p + (prev - p) * mu_ref[...]
        r = x[:, 0:B_WIDTH]
        k = x[:, B_WIDTH:2 * B_WIDTH]
        v = x[:, 2 * B_WIDTH:3 * B_WIDTH]
        wd = x[:, 3 * B_WIDTH:3 * B_WIDTH + DECAY_LORA]
        ad = x[:, 3 * B_WIDTH + DECAY_LORA:3 * B_WIDTH + DECAY_LORA + ICLR_LORA]

        z = -(w0_ref[...] + _dot(jnp.tanh(wd), wl_ref[...]))
        softplus = jnp.maximum(z, 0.0) + jnp.log(1.0 + jnp.exp(-jnp.abs(z)))
        ld = -jnp.exp(-softplus - 0.5)
        a = _sigmoid(a0_ref[...] + _dot(ad, al_ref[...]))
        kk = k * kk_ref[...]
        k_mod = k * (1.0 + (a - 1.0) * ka_ref[...])
        rkr = r * k_mod * rk_ref[...]

        ld_hi, ld_lo = _split2(ld)
        cum = jnp.dot(tri, ld_hi, preferred_element_type=F32) + jnp.dot(tri, ld_lo, preferred_element_type=F32)
        g_inc = jnp.exp(cum)
        g_exc = jnp.exp(cum - ld)
        g_inv = jnp.exp(-cum)
        g_end = jnp.exp(cum[csz - 1:csz, :] - cum)
        g_all = g_inc[csz - 1:csz, :]

        for pr in range(n_pairs):
            sl = slice(pr * LANES, (pr + 1) * LANES)
            kk_p = kk[:, sl]
            kk_p = kk_p * lax.rsqrt(jnp.maximum(head_sum(kk_p * kk_p), 1e-24))
            b_p = kk_p * a[:, sl]
            bt = (b_p * g_inv[:, sl]).astype(BF16)
            kt = (k_mod[:, sl] * g_inv[:, sl]).astype(BF16)
            ch = dict(
                idx=bi * n_pairs + pr,
                v=v[:, sl],
                lhs=jnp.concatenate([-kk_p * g_exc[:, sl], r[:, sl] * g_inc[:, sl]], axis=0).astype(BF16),
                rhs=jnp.concatenate([blockdiag(bt), blockdiag(kt)], axis=0),
                bk=jnp.concatenate([b_p * g_end[:, sl], k_mod[:, sl] * g_end[:, sl]], axis=0).astype(BF16),
                g_all=g_all[:, sl],
                bonus=head_sum(rkr[:, sl]) * v[:, sl],
            )
            chains.append(ch)

    for ch in chains:
        x = _dot_nt(ch["lhs"], ch["rhs"])
        xb, xk = x[:, :LANES], x[:, LANES:]
        ch["a_ab"] = jnp.where(strict, xb[:csz], 0.0)
        a_ak = jnp.where(strict, xk[:csz], 0.0)
        m_rk = jnp.where(lower, xk[csz:], 0.0)
        ch["ak_rk"] = jnp.concatenate([a_ak, m_rk], axis=0).astype(BF16)
        ch["m_rb"] = jnp.where(lower, xb[csz:], 0.0).astype(BF16)
    for ch in chains:
        akv = _dot(ch["ak_rk"], blockdiag(ch["v"].astype(BF16)))
        ch["akv"], ch["mrkv"] = akv[:csz], akv[csz:]
        ch["tinv"] = eye + ch["a_ab"]
        ch["pw"] = ch["a_ab"].astype(BF16)
    n_sq = int(math.log2(csz)) - 1
    for ch in chains:
        ch["pw"] = _dot(ch["pw"], blockdiag(ch["pw"])).astype(BF16)
    for step in range(n_sq):
        for ch in chains:
            if step + 1 < n_sq:
                both = _dot(jnp.concatenate([ch["pw"], ch["tinv"].astype(BF16)], axis=0), blockdiag(ch["pw"]))
                ch["tinv"] = ch["tinv"] + both[csz:]
                ch["pw"] = both[:csz].astype(BF16)
            else:
                ch["tinv"] = ch["tinv"] + _dot(ch["tinv"], blockdiag(ch["pw"]))
    for ch in chains:
        ch["s0"] = state_ref[ch["idx"]]
        ch["as0"] = _dot_nt(ch["lhs"], ch["s0"])
    for ch in chains:
        w = (ch["as0"][:csz] + ch["akv"]).astype(BF16)
        ch["u"] = _dot(ch["tinv"], blockdiag(w))
    outs = []
    for ch in chains:
        u = ch["u"]
        y = ch["as0"][csz:] + _dot(ch["m_rb"], blockdiag(u.astype(BF16))) + ch["mrkv"]
        uv = jnp.concatenate([u, ch["v"]], axis=0)
        state_ref[ch["idx"]] = ch["s0"] * ch["g_all"] + jnp.where(same_head, _dot_tn(uv, ch["bk"]), 0.0)
        yc = y - head_sum(y) * (1.0 / n)
        var = head_sum(yc * yc) * (1.0 / n)
        outs.append(yc * lax.rsqrt(var + LNX_EPS))
    for bi in range(nb):
        yn = jnp.concatenate(outs[bi * n_pairs:(bi + 1) * n_pairs], axis=-1)
        bonus = jnp.concatenate([ch["bonus"] for ch in chains[bi * n_pairs:(bi + 1) * n_pairs]], axis=-1)
        o_ref[bi] = (yn * lw_ref[...] + lb_ref[...] + bonus).astype(BF16)


RWKV_NB = 4


def _rwkv(cols_rwkv, mu, w0, wl, a0, al, k_k, k_a, r_k, ln_w, ln_b):
    bsz, s, _ = cols_rwkv.shape
    nb = RWKV_NB if bsz % RWKV_NB == 0 else 1
    const = lambda b, c: (0, 0)
    vec = pl.BlockSpec((1, B_WIDTH), const)
    return pl.pallas_call(
        _rwkv_kernel,
        grid=(bsz // nb, s // CHUNK),
        in_specs=[pl.BlockSpec((nb, CHUNK, RWKV_COLS), lambda b, c: (b, c, 0)),
                  pl.BlockSpec((1, RWKV_COLS), const),
                  vec, pl.BlockSpec((DECAY_LORA, B_WIDTH), const),
                  vec, pl.BlockSpec((ICLR_LORA, B_WIDTH), const),
                  vec, vec, vec, vec, vec],
        out_specs=pl.BlockSpec((nb, CHUNK, B_WIDTH), lambda b, c: (b, c, 0)),
        out_shape=jax.ShapeDtypeStruct((bsz, s, B_WIDTH), BF16),
        scratch_shapes=[pltpu.VMEM((nb * B_WIDTH // LANES, LANES, LANES), F32),
                        pltpu.VMEM((nb, 8, RWKV_COLS), F32)],
        compiler_params=pltpu.CompilerParams(dimension_semantics=("parallel", "arbitrary")),
        name="rwkv",
    )(cols_rwkv, mu, w0, wl, a0, al, k_k, k_a, r_k, ln_w, ln_b)


def _final_kernel(x_ref, ya_ref, yb_ref, cf_ref, gate_ref, wa_ref, wb_ref, wo_ref, o_ref):
    a_silu = cf_ref[0, :, 0:A_WIDTH].astype(F32)
    b_silu = cf_ref[0, :, A_WIDTH:A_WIDTH + B_WIDTH].astype(F32)
    merge_a = cf_ref[0, :, A_WIDTH + B_WIDTH:A_WIDTH + B_WIDTH + D_MODEL].astype(F32)
    merge_b = cf_ref[0, :, A_WIDTH + B_WIDTH + D_MODEL:A_WIDTH + B_WIDTH + 2 * D_MODEL].astype(F32)
    ya = ya_ref[0].astype(F32) * (a_silu * _sigmoid(a_silu))
    yb = yb_ref[0].astype(F32) * (b_silu * _sigmoid(b_silu))
    merged = _sigmoid(merge_a) * _dot(ya, wa_ref[...]) + _sigmoid(merge_b) * _dot(yb, wb_ref[...])
    o_ref[0] = x_ref[0] + gate_ref[0] * _dot(merged, wo_ref[...])


def _final(x, y_a, y_b, cols_fin, gate, w_out_a, w_out_b, w_o, tm=256):
    bsz, s, _ = x.shape
    const = lambda b, i: (0, 0)
    row = lambda w: pl.BlockSpec((1, tm, w), lambda b, i: (b, i, 0))
    return pl.pallas_call(
        _final_kernel,
        grid=(bsz, s // tm),
        in_specs=[row(D_MODEL), row(A_WIDTH), row(B_WIDTH), row(FIN_COLS),
                  pl.BlockSpec((1, 1, D_MODEL), lambda b, i: (b, 0, 0)),
                  pl.BlockSpec((A_WIDTH, D_MODEL), const),
                  pl.BlockSpec((B_WIDTH, D_MODEL), const),
                  pl.BlockSpec((D_MODEL, D_MODEL), const)],
        out_specs=row(D_MODEL),
        out_shape=jax.ShapeDtypeStruct((bsz, s, D_MODEL), F32),
        compiler_params=pltpu.CompilerParams(dimension_semantics=("parallel", "parallel"),
                                             vmem_limit_bytes=VMEM_LIMIT),
        name="final",
    )(x, y_a, y_b, cols_fin, gate, w_out_a, w_out_b, w_o)


def _split_w_in(w_in):
    nsa_in = 2 * A_WIDTH + 6 * A_KV_WIDTH + 3 * A_HEADS
    o_gate = A_WIDTH + 6 * A_KV_WIDTH
    o_asilu = o_gate + 3 * A_HEADS
    o_shift = nsa_in
    o_rest = nsa_in + RWKV_COLS
    gate_w = w_in[:, o_gate:o_asilu].reshape(D_MODEL, 3, A_KV_GROUPS, A_HPG)
    gate_w = gate_w.transpose(0, 2, 1, 3).reshape(D_MODEL, A_KV_GROUPS, 3 * A_HPG)
    gate_w = jnp.pad(gate_w, ((0, 0), (0, 0), (0, LANES - 3 * A_HPG))).reshape(D_MODEL, GATE_PAD)
    w_nsa = jnp.concatenate([w_in[:, :o_gate], gate_w], axis=1)
    w_fin = jnp.concatenate([w_in[:, o_asilu:o_shift], w_in[:, o_rest:]], axis=1)
    w_rwkv = w_in[:, o_shift:o_rest]
    return w_nsa.astype(BF16), w_fin.astype(BF16), w_rwkv.astype(BF16)


def _layer(x, c, rel_bias, w_ada, b_ada, norm_gain, w_in, q_norm_gain, k_norm_gain,
           cmp_pos_k, cmp_pos_v, cmp_k_w1, cmp_k_w2, cmp_v_w1, cmp_v_w2,
           shift_mu, w0, w_lora_up, a0, a_lora_up, k_k, k_a, r_k, ln_x_w, ln_x_b,
           w_out_a, w_out_b, w_o):
    bsz, s, _ = x.shape
    assert s % (2 * TQ) == 0 and s // CMP_STRIDE == LANES
    n16 = s // CMP_STRIDE
    mod = _ada(c, w_ada, b_ada)
    w_nsa, w_fin, w_rwkv = _split_w_in(w_in)
    cols_nsa, cols_fin, cols_rwkv = _proj(x, mod, norm_gain, w_nsa, w_fin, w_rwkv)

    scale = A_HEAD_DIM ** -0.5 * LOG2E
    qg = (jnp.tile(q_norm_gain, A_HEADS) * scale).reshape(1, A_WIDTH)
    ksg = jnp.tile(k_norm_gain[1], A_KV_GROUPS).reshape(1, A_KV_WIDTH)
    kwg = jnp.tile(k_norm_gain[2], A_KV_GROUPS).reshape(1, A_KV_WIDTH)
    seg = np.arange(A_WIDTH) // A_HEAD_DIM
    bd = jnp.asarray((seg[:, None] == seg[None, :]).astype(np.float32) / A_HEAD_DIM, BF16)
    q_t, ks, vs_t, kw, vw_t, gates_t = _nsaprep(cols_nsa, qg, ksg, kwg, bd)

    kc, vc_t = _compress(cols_nsa, _expand_cmp_pos(cmp_pos_k), _expand_cmp_pos(cmp_pos_v),
                         _expand_cmp_w1(cmp_k_w1), cmp_k_w2.astype(BF16),
                         _expand_cmp_w1(cmp_v_w1), cmp_v_w2.T.astype(BF16),
                         k_norm_gain[0].reshape(1, A_HEAD_DIM))
    bias_c, bias_d = _bias_tables(rel_bias, s, n16)
    y_a = _attention(q_t, kc, vc_t, ks, vs_t, kw, vw_t, bias_c, bias_d, gates_t)

    vec = lambda t: t.reshape(1, -1)
    y_b = _rwkv(cols_rwkv, vec(shift_mu), vec(w0), w_lora_up.astype(BF16), vec(a0), a_lora_up.astype(BF16),
                vec(k_k), vec(k_a), vec(r_k), vec(ln_x_w), vec(ln_x_b))

    gate = mod[:, 2 * D_MODEL:].reshape(bsz, 1, D_MODEL)
    return _final(x, y_a, y_b, cols_fin, gate, w_out_a.astype(BF16), w_out_b.astype(BF16), w_o.astype(BF16))


def kernel(x, c, w_ada, b_ada, norm_gain, w_in, q_norm_gain, k_norm_gain, cmp_pos_k, cmp_pos_v, cmp_k_w1, cmp_k_w2, cmp_v_w1, cmp_v_w2, rel_bias, shift_mu, w0, w_lora_up, a0, a_lora_up, k_k, k_a, r_k, ln_x_w, ln_x_b, w_out_a, w_out_b, w_o):
    for l in range(w_in.shape[0]):
        x = _layer(x, c, rel_bias, w_ada[l], b_ada[l], norm_gain[l], w_in[l], q_norm_gain[l], k_norm_gain[l],
                   cmp_pos_k[l], cmp_pos_v[l], cmp_k_w1[l], cmp_k_w2[l], cmp_v_w1[l], cmp_v_w2[l],
                   shift_mu[l], w0[l], w_lora_up[l], a0[l], a_lora_up[l], k_k[l], k_a[l], r_k[l],
                   ln_x_w[l], ln_x_b[l], w_out_a[l], w_out_b[l], w_o[l])
    return x
```

```python
import functools
import math

import numpy as np
import jax
import jax.numpy as jnp
from jax import lax
from jax.experimental import pallas as pl
from jax.experimental.pallas import tpu as pltpu

F32 = jnp.float32
BF16 = jnp.bfloat16

D_MODEL = 1024
A_HEADS = 8
A_HEAD_DIM = 64
A_KV_GROUPS = 2
A_HPG = A_HEADS // A_KV_GROUPS
A_WIDTH = A_HEADS * A_HEAD_DIM
A_KV_WIDTH = A_KV_GROUPS * A_HEAD_DIM
CMP_BLOCK = 32
CMP_STRIDE = 16
CMP_HIDDEN = 256
SLC_BLOCK = 64
SLC_TOPN = 16
WINDOW = 512
B_HEADS = 8
B_HEAD_DIM = 64
B_WIDTH = B_HEADS * B_HEAD_DIM
DECAY_LORA = 64
ICLR_LORA = 64
LNX_EPS = 64e-5
REL_BUCKETS = 32
REL_MAX_EXACT = 16
REL_MAX_DIST = 128
NORM_EPS = 1e-6
NEG_INF = -1e30
FORCE_SCORE = 1e30

LANES = 128
TQ = 128
CHUNK = 64
GATE_PAD = LANES
LOG2E = math.log2(math.e)
V_ROWS = A_HEAD_DIM + 16
GATE_ROWS = 16
NSA_COLS = A_WIDTH + 6 * A_KV_WIDTH + GATE_PAD
FIN_COLS = A_WIDTH + B_WIDTH + 2 * D_MODEL
RWKV_COLS = 3 * B_WIDTH + DECAY_LORA + ICLR_LORA
VMEM_LIMIT = 56 * 1024 * 1024


def _dot(a, b):
    return jnp.dot(a.astype(BF16), b.astype(BF16), preferred_element_type=F32)


def _dot_nt(a, b):
    return lax.dot_general(a.astype(BF16), b.astype(BF16), (((1,), (1,)), ((), ())),
                           preferred_element_type=F32)


def _dot_tn(a, b):
    return lax.dot_general(a.astype(BF16), b.astype(BF16), (((0,), (0,)), ((), ())),
                           preferred_element_type=F32)


def _split2(x):
    hi = x.astype(BF16)
    lo = (x - hi.astype(F32)).astype(BF16)
    return hi, lo


def _split3(x):
    h1 = x.astype(BF16)
    r1 = x - h1.astype(F32)
    h2 = r1.astype(BF16)
    h3 = (r1 - h2.astype(F32)).astype(BF16)
    return h1, h2, h3


def _sigmoid(x):
    return 1.0 / (1.0 + jnp.exp(-x))


def _bucket_thresholds():
    n = np.arange(0, 4096)
    nf = np.maximum(n, REL_MAX_EXACT).astype(np.float64)
    val = np.log(nf / REL_MAX_EXACT) / math.log(REL_MAX_DIST / REL_MAX_EXACT) * (REL_BUCKETS - REL_MAX_EXACT)
    frac = np.abs(val - np.round(val))
    assert np.all((frac > 1e-4) | (n <= REL_MAX_EXACT) | (n >= REL_MAX_DIST))
    large = REL_MAX_EXACT + np.floor(val + 1e-9).astype(np.int64)
    bucket = np.where(n < REL_MAX_EXACT, n, np.minimum(large, REL_BUCKETS - 1))
    return [int(np.argmax(bucket >= j)) for j in range(REL_BUCKETS)]


_BUCKET_TH = _bucket_thresholds()


def _bias_from_dist(dist, tbl_ref, head):
    val = jnp.full(dist.shape, tbl_ref[0, head], F32)
    for j in range(1, REL_BUCKETS):
        val = jnp.where(dist >= _BUCKET_TH[j], tbl_ref[j, head], val)
    return val


def _ada_kernel(c_ref, w_ref, b_ref, o_ref):
    c = c_ref[...]
    o_ref[...] = _dot(c * _sigmoid(c), w_ref[...]) + b_ref[...]


def _ada(c, w_ada, b_ada):
    bsz = c.shape[0]
    return pl.pallas_call(
        _ada_kernel,
        grid=(3,),
        in_specs=[pl.BlockSpec((bsz, D_MODEL), lambda j: (0, 0)),
                  pl.BlockSpec((D_MODEL, D_MODEL), lambda j: (0, j)),
                  pl.BlockSpec((1, D_MODEL), lambda j: (0, j))],
        out_specs=pl.BlockSpec((bsz, D_MODEL), lambda j: (0, j)),
        out_shape=jax.ShapeDtypeStruct((bsz, 3 * D_MODEL), F32),
        name="ada",
    )(c, w_ada, b_ada.reshape(1, 3 * D_MODEL))


def _norm_rows(x_t, gain_col, n_seg):
    out = []
    for seg in range(n_seg):
        blk = x_t[seg * A_HEAD_DIM:(seg + 1) * A_HEAD_DIM, :]
        ms = jnp.mean(blk * blk, axis=0, keepdims=True)
        out.append(blk * lax.rsqrt(ms + NORM_EPS) * gain_col[seg * A_HEAD_DIM:(seg + 1) * A_HEAD_DIM, :])
    return out


def _proj_kernel(x_ref, mod_ref, g_ref, wn_ref, wf_ref, wr_ref, qg_ref, ksg_ref, kwg_ref,
                 q_ref, ks_ref, vs_ref, kw_ref, vw_ref, gt_ref, ck_ref, of_ref, or_ref):
    tm = x_ref.shape[1]
    x = x_ref[0]
    ms = jnp.mean(x * x, axis=-1, keepdims=True)
    y = x * lax.rsqrt(ms + NORM_EPS) * g_ref[...]
    mod = mod_ref[0]
    h = (y * (1.0 + mod[:, D_MODEL:2 * D_MODEL]) + mod[:, :D_MODEL]).astype(BF16)
    of_ref[0] = jnp.dot(h, wf_ref[...], preferred_element_type=F32).astype(BF16)
    or_ref[0] = jnp.dot(h, wr_ref[...], preferred_element_type=F32)
    cn = jnp.dot(h, wn_ref[...], preferred_element_type=F32)
    ck_ref[0] = cn[:, A_WIDTH:A_WIDTH + 2 * A_KV_WIDTH]

    lane = lax.broadcasted_iota(jnp.int32, (TQ, LANES), 1)
    row = lax.broadcasted_iota(jnp.int32, (TQ, LANES), 0)
    ones_rows = (lax.broadcasted_iota(jnp.int32, (V_ROWS - A_HEAD_DIM, TQ), 0) == 0).astype(BF16)
    off = A_WIDTH + 2 * A_KV_WIDTH
    for sub in range(tm // TQ):
        c = cn[sub * TQ:(sub + 1) * TQ]
        q_heads = _norm_rows(c[:, 0:A_WIDTH].T, qg_ref[...], A_HEADS)
        ks_t = jnp.concatenate(_norm_rows(c[:, off:off + A_KV_WIDTH].T, ksg_ref[...], A_KV_GROUPS), axis=0)
        kw_t = jnp.concatenate(_norm_rows(c[:, off + 2 * A_KV_WIDTH:off + 3 * A_KV_WIDTH].T, kwg_ref[...],
                                          A_KV_GROUPS), axis=0)
        ksn = ks_t.T
        kwn = kw_t.T.astype(BF16)
        vs_t = c[:, off + A_KV_WIDTH:off + 2 * A_KV_WIDTH].T.astype(BF16)
        vw_t = c[:, off + 3 * A_KV_WIDTH:off + 4 * A_KV_WIDTH].T.astype(BF16)
        gates_t = _sigmoid(c[:, off + 4 * A_KV_WIDTH:off + 5 * A_KV_WIDTH]).T
        blk = (pl.program_id(1) * tm + sub * TQ + row) // SLC_BLOCK
        onehot = jnp.where(lane - A_HEAD_DIM == blk, 1.0, 0.0)
        for g in range(A_KV_GROUPS):
            q_ref[0, g, sub] = jnp.concatenate(q_heads[g * A_HPG:(g + 1) * A_HPG], axis=1).astype(BF16)
            sl = slice(g * A_HEAD_DIM, (g + 1) * A_HEAD_DIM)
            k_g = ksn if g == 0 else pltpu.roll(ksn, A_HEAD_DIM, axis=1)
            ks_ref[0, g, sub * TQ:(sub + 1) * TQ, :] = jnp.where(lane < A_HEAD_DIM, k_g, onehot).astype(BF16)
            kw_ref[0, g, sub * TQ:(sub + 1) * TQ, :] = kwn[:, sl]
            vs_ref[0, g, sub] = jnp.concatenate([vs_t[sl, :], ones_rows], axis=0)
            vw_ref[0, g, sub] = jnp.concatenate([vw_t[sl, :], ones_rows], axis=0)
            gt_ref[0, g, sub] = gates_t[g * A_HEAD_DIM:g * A_HEAD_DIM + GATE_ROWS, :]


def _proj(x, mod, norm_gain, w_nsa, w_fin, w_rwkv, qg, ksg, kwg, tm=512):
    bsz, s, _ = x.shape
    nt, nsub = s // TQ, tm // TQ
    assert A_HEAD_DIM + s // SLC_BLOCK <= LANES and A_KV_WIDTH == LANES
    const = lambda b, i: (0, 0)
    weight = lambda cols: pl.BlockSpec((D_MODEL, cols), const, pipeline_mode=pl.Buffered(1))
    col = lambda t: jnp.broadcast_to(t.reshape(-1, 1), (t.size, LANES))
    k_spec = lambda width: pl.BlockSpec((1, A_KV_GROUPS, tm, width), lambda b, i: (b, 0, i, 0))
    k_shape = lambda width: jax.ShapeDtypeStruct((bsz, A_KV_GROUPS, s, width), BF16)
    tile_spec = lambda r, c: pl.BlockSpec((1, A_KV_GROUPS, nsub, r, c), lambda b, i: (b, 0, i, 0, 0))
    tile_shape = lambda r, c, dt: jax.ShapeDtypeStruct((bsz, A_KV_GROUPS, nt, r, c), dt)
    return pl.pallas_call(
        _proj_kernel,
        grid=(bsz, s // tm),
        in_specs=[pl.BlockSpec((1, tm, D_MODEL), lambda b, i: (b, i, 0)),
                  pl.BlockSpec((1, 1, 3 * D_MODEL), lambda b, i: (b, 0, 0)),
                  pl.BlockSpec((1, D_MODEL), const),
                  weight(NSA_COLS), weight(FIN_COLS), weight(RWKV_COLS),
                  pl.BlockSpec((A_WIDTH, LANES), const),
                  pl.BlockSpec((A_KV_WIDTH, LANES), const),
                  pl.BlockSpec((A_KV_WIDTH, LANES), const)],
        out_specs=[tile_spec(A_HEAD_DIM, A_HPG * TQ),
                   k_spec(LANES), tile_spec(V_ROWS, TQ), k_spec(A_HEAD_DIM), tile_spec(V_ROWS, TQ),
                   tile_spec(GATE_ROWS, TQ),
                   pl.BlockSpec((1, tm, 2 * A_KV_WIDTH), lambda b, i: (b, i, 0)),
                   pl.BlockSpec((1, tm, FIN_COLS), lambda b, i: (b, i, 0)),
                   pl.BlockSpec((1, tm, RWKV_COLS), lambda b, i: (b, i, 0))],
        out_shape=[tile_shape(A_HEAD_DIM, A_HPG * TQ, BF16),
                   k_shape(LANES), tile_shape(V_ROWS, TQ, BF16), k_shape(A_HEAD_DIM), tile_shape(V_ROWS, TQ, BF16),
                   tile_shape(GATE_ROWS, TQ, F32),
                   jax.ShapeDtypeStruct((bsz, s, 2 * A_KV_WIDTH), F32),
                   jax.ShapeDtypeStruct((bsz, s, FIN_COLS), BF16),
                   jax.ShapeDtypeStruct((bsz, s, RWKV_COLS), F32)],
        compiler_params=pltpu.CompilerParams(dimension_semantics=("parallel", "parallel"),
                                             vmem_limit_bytes=VMEM_LIMIT),
        name="proj",
    )(x, mod.reshape(bsz, 1, 3 * D_MODEL), norm_gain.reshape(1, D_MODEL), w_nsa, w_fin, w_rwkv,
      col(qg), col(ksg), col(kwg))


def _compress_kernel(ck_ref, cv_ref, pk_ref, pv_ref, w1k_ref, w2k_ref, w1v_ref, w2v_ref, kg_ref, kc_ref, vc_ref):
    n16 = ck_ref.shape[1] // CMP_STRIDE

    def rows16(ref):
        return jnp.concatenate([ref[0, pl.ds(p, n16, stride=CMP_STRIDE), :] for p in range(CMP_STRIDE)], axis=1)

    def hidden(z, pos_ref, w1_ref, g):
        top = _dot(z + pos_ref[0:1, :], w1_ref[g, 0])
        bot = _dot(z + pos_ref[1:2, :], w1_ref[g, 1])
        return jax.nn.gelu(top + pltpu.roll(bot, n16 - 1, axis=0), approximate=True)

    zk = rows16(ck_ref)
    zv = rows16(cv_ref)
    for g in range(A_KV_GROUPS):
        kc = _dot(hidden(zk, pk_ref, w1k_ref, g), w2k_ref[...])
        ms = jnp.mean(kc * kc, axis=-1, keepdims=True)
        kc_ref[0, g] = (kc * lax.rsqrt(ms + NORM_EPS) * kg_ref[...]).astype(BF16)
        vc_ref[0, g] = _dot_nt(w2v_ref[...], hidden(zv, pv_ref, w1v_ref, g)).astype(BF16)


def _expand_cmp_w1(w1):
    w = w1.reshape(2, CMP_STRIDE, 1, A_HEAD_DIM, CMP_HIDDEN)
    per_group = []
    for g in range(A_KV_GROUPS):
        pad = [(0, 0), (0, 0), (g, A_KV_GROUPS - 1 - g), (0, 0), (0, 0)]
        per_group.append(jnp.pad(w, pad).reshape(2, CMP_STRIDE * A_KV_WIDTH, CMP_HIDDEN))
    return jnp.stack(per_group).astype(BF16)


def _expand_cmp_pos(pos):
    p = jnp.broadcast_to(pos.reshape(2, CMP_STRIDE, 1, A_HEAD_DIM), (2, CMP_STRIDE, A_KV_GROUPS, A_HEAD_DIM))
    return p.reshape(2, CMP_STRIDE * A_KV_WIDTH)


def _compress(ck, pk, pv, w1k, w2k, w1v, w2v_t, kg):
    bsz, s, _ = ck.shape
    n16 = s // CMP_STRIDE
    zw = CMP_STRIDE * A_KV_WIDTH
    const = lambda b: (0, 0)
    const4 = lambda b: (0, 0, 0, 0)
    return pl.pallas_call(
        _compress_kernel,
        grid=(bsz,),
        in_specs=[pl.BlockSpec((1, s, A_KV_WIDTH), lambda b: (b, 0, 0)),
                  pl.BlockSpec((1, s, A_KV_WIDTH), lambda b: (b, 0, 1)),
                  pl.BlockSpec((2, zw), const), pl.BlockSpec((2, zw), const),
                  pl.BlockSpec((A_KV_GROUPS, 2, zw, CMP_HIDDEN), const4), pl.BlockSpec((CMP_HIDDEN, A_HEAD_DIM), const),
                  pl.BlockSpec((A_KV_GROUPS, 2, zw, CMP_HIDDEN), const4), pl.BlockSpec((A_HEAD_DIM, CMP_HIDDEN), const),
                  pl.BlockSpec((1, A_HEAD_DIM), const)],
        out_specs=[pl.BlockSpec((1, A_KV_GROUPS, n16, A_HEAD_DIM), lambda b: (b, 0, 0, 0)),
                   pl.BlockSpec((1, A_KV_GROUPS, A_HEAD_DIM, n16), lambda b: (b, 0, 0, 0))],
        out_shape=[jax.ShapeDtypeStruct((bsz, A_KV_GROUPS, n16, A_HEAD_DIM), BF16),
                   jax.ShapeDtypeStruct((bsz, A_KV_GROUPS, A_HEAD_DIM, n16), BF16)],
        compiler_params=pltpu.CompilerParams(dimension_semantics=("parallel",)),
        name="compress",
    )(ck, ck, pk, pv, w1k, w2k, w1v, w2v_t, kg)


TILE_FAR, TILE_EDGE, TILE_MASKED, N_BIAS_TILES = 2, 3, 4, 5
SUB = 4


def _bias_cmp_kernel(tbl_ref, o_ref):
    i = pl.program_id(0)
    g = pl.program_id(1)
    n_cmp = o_ref.shape[2]
    n = lax.broadcasted_iota(jnp.int32, (n_cmp, TQ), 0)
    q = lax.broadcasted_iota(jnp.int32, (n_cmp, TQ), 1)
    dist = i * TQ + q - (n * CMP_STRIDE + CMP_BLOCK - 1)
    for h in range(A_HPG):
        bias = _bias_from_dist(dist, tbl_ref, g * A_HPG + h)
        o_ref[0, 0, :, h * TQ:(h + 1) * TQ] = jnp.where(dist >= 0, bias * LOG2E, NEG_INF)


def _bias_toeplitz_kernel(tbl_ref, o_ref):
    g = pl.program_id(0)
    r = pl.program_id(1)
    off = jnp.where(r == TILE_EDGE, WINDOW // TQ, jnp.where(r == TILE_MASKED, -2, r))
    k = lax.broadcasted_iota(jnp.int32, (TQ, TQ), 0)
    q = lax.broadcasted_iota(jnp.int32, (TQ, TQ), 1)
    dist = off * TQ + q - k
    valid = (dist >= 0) & (dist < WINDOW)
    for h in range(A_HPG):
        bias = _bias_from_dist(dist, tbl_ref, g * A_HPG + h)
        o_ref[0, 0, :, h * TQ:(h + 1) * TQ] = jnp.where(valid, bias * LOG2E, NEG_INF)


def _bias_tables(rel_bias, s, n_cmp):
    smem = pl.BlockSpec(memory_space=pltpu.SMEM)
    nt = s // TQ
    bias_c = pl.pallas_call(
        _bias_cmp_kernel,
        grid=(nt, A_KV_GROUPS),
        in_specs=[smem],
        out_specs=pl.BlockSpec((1, 1, n_cmp, A_HPG * TQ), lambda i, g: (i, g, 0, 0)),
        out_shape=jax.ShapeDtypeStruct((nt, A_KV_GROUPS, n_cmp, A_HPG * TQ), F32),
        name="bias_cmp",
    )(rel_bias)
    assert _BUCKET_TH[REL_BUCKETS - 1] <= TQ + 1 and WINDOW // TQ >= 3
    bias_d = pl.pallas_call(
        _bias_toeplitz_kernel,
        grid=(A_KV_GROUPS, N_BIAS_TILES),
        in_specs=[smem],
        out_specs=pl.BlockSpec((1, 1, TQ, A_HPG * TQ), lambda g, r: (g, r, 0, 0)),
        out_shape=jax.ShapeDtypeStruct((A_KV_GROUPS, N_BIAS_TILES, TQ, A_HPG * TQ), F32),
        name="bias_toeplitz",
    )(rel_bias)
    return bias_c, bias_d


def _attn_kernel(qa_ref, qb_ref, kc_ref, vc_ref, ks_ref, vs_ref, kw_ref, vw_ref, bca_ref, bcb_ref, bd_ref,
                 gta_ref, gtb_ref, o_ref, m_ref, acc_ref, part_ref):
    j = pl.program_id(2)
    tq = TQ
    n_cmp = kc_ref.shape[2]
    n_slc = ks_ref.shape[2] // SLC_BLOCK
    wt = WINDOW // tq
    dh = A_HEAD_DIM
    n_tiles = 2
    tiles = [dict(i=j, q=qa_ref[0, 0, 0], bias_c=bca_ref, gates=gta_ref[0, 0, 0]),
             dict(i=j + pl.num_programs(2), q=qb_ref[0, 0, 0], bias_c=bcb_ref, gates=gtb_ref[0, 0, 0])]

    def scores(k_slab, q_mat, i, first_tile, n_sub, tile_index):
        s = jnp.dot(k_slab, q_mat, preferred_element_type=F32)
        parts = [s[t * tq:(t + 1) * tq] + bd_ref[0, tile_index(i - (first_tile + t))] for t in range(n_sub)]
        return jnp.concatenate(parts, axis=0)

    def values_t(v_ref, first_tile, n_sub):
        return jnp.concatenate([v_ref[0, 0, first_tile + t] for t in range(n_sub)], axis=1)

    win_tile = lambda r: jnp.where(r < 0, TILE_MASKED, jnp.where(r == wt, TILE_EDGE, jnp.minimum(r, TILE_FAR)))
    for t in tiles:
        t["first_w"] = jnp.maximum(t["i"] - wt, 0)
        k0 = pl.multiple_of(t["first_w"] * tq, tq)
        t["s_w"] = scores(kw_ref[0, 0, pl.ds(k0, (wt + 1) * tq), :], t["q"], t["i"], t["first_w"], wt + 1, win_tile)
    for t in tiles:
        bias = t["bias_c"][0, 0]
        t["valid_c"] = bias > 0.5 * NEG_INF
        t["s_c"] = jnp.dot(kc_ref[0, 0], t["q"], preferred_element_type=F32) + bias

    r1, r2 = SLC_BLOCK // CMP_STRIDE, CMP_BLOCK // CMP_STRIDE
    jj = lax.broadcasted_iota(jnp.int32, (n_slc, n_cmp), 0)
    nn = lax.broadcasted_iota(jnp.int32, (n_slc, n_cmp), 1)
    d = nn - r1 * jj
    cnt = jnp.zeros((n_slc, n_cmp), F32)
    for a in range(r1):
        for c in range(r2):
            cnt = cnt + jnp.where(d == a - c, 1.0, 0.0)
    cnt = cnt.astype(BF16)
    for t in tiles:
        s = t["s_c"]
        e = jnp.where(t["valid_c"], jnp.exp2(s - jnp.max(s, axis=0, keepdims=True)), 0.0)
        l = jnp.sum(e, axis=0, keepdims=True)
        p = e * (1.0 / jnp.where(l > 0.0, l, 1.0))
        t["out_c"] = jnp.dot(vc_ref[0, 0], p.astype(BF16), preferred_element_type=F32)
        p_grp = sum(p[:, h * tq:(h + 1) * tq] for h in range(A_HPG))
        t["imp"] = sum(jnp.dot(cnt, part, preferred_element_type=F32) for part in _split3(p_grp))

    for t in tiles:
        s = t["s_w"]
        t["p_w"] = jnp.exp2(s - jnp.max(s, axis=0, keepdims=True)).astype(BF16)
    for k, t in enumerate(tiles):
        acc = jnp.dot(values_t(vw_ref, t["first_w"], wt + 1), t["p_w"], preferred_element_type=F32)
        out_w = acc[:dh] * (1.0 / acc[dh:dh + 1])
        gates = t["gates"]
        for h in range(A_HPG):
            cols = slice(h * tq, (h + 1) * tq)
            part_ref[k, :, cols] = (gates[h:h + 1, :] * t["out_c"][:, cols]
                                    + gates[2 * A_HPG + h:2 * A_HPG + h + 1, :] * out_w[:, cols])

    blk = lax.broadcasted_iota(jnp.int32, (n_slc, tq), 0)
    for t in tiles:
        tpos = t["i"] * tq + lax.broadcasted_iota(jnp.int32, (n_slc, tq), 1)
        cur = tpos // SLC_BLOCK
        forced = (blk == 0) | (blk == cur) | (blk == cur - 1)
        causal = blk * SLC_BLOCK <= tpos
        imp = jnp.where(forced, FORCE_SCORE, jnp.where(causal, t["imp"], NEG_INF))
        rank = jnp.zeros((n_slc, tq), F32)
        for c in range(n_slc):
            row = imp[c:c + 1, :]
            ahead = (row > imp) | ((row == imp) & (blk > c))
            rank = rank + jnp.where(ahead, 1.0, 0.0)
        pen = jnp.where(rank < float(min(SLC_TOPN, n_slc)), 0.0, -FORCE_SCORE).astype(BF16)
        t["q_aug"] = jnp.concatenate([t["q"], jnp.concatenate([pen] * A_HPG, axis=1),
                                      jnp.zeros((LANES - A_HEAD_DIM - n_slc, A_HPG * tq), BF16)], axis=0)
        t["last"] = t["i"] // SUB

    sel_tile = lambda r: jnp.where(r < 0, TILE_MASKED, jnp.minimum(r, TILE_FAR))

    def partials(jobs):
        ss = []
        for t, c in jobs:
            k0 = pl.multiple_of(c * (SUB * tq), SUB * tq)
            ss.append(scores(ks_ref[0, 0, pl.ds(k0, SUB * tq), :], t["q_aug"], t["i"], c * SUB, SUB, sel_tile))
        ms = [jnp.max(s, axis=0, keepdims=True) for s in ss]
        ps = [jnp.exp2(s - m).astype(BF16) for s, m in zip(ss, ms)]
        accs = [jnp.dot(values_t(vs_ref, c * SUB, SUB), p, preferred_element_type=F32)
                for (t, c), p in zip(jobs, ps)]
        return list(zip(ms, accs))

    def merge(parts):
        m = functools.reduce(jnp.maximum, [p[0] for p in parts])
        return m, sum(jnp.exp2(p[0] - m) * p[1] for p in parts)

    ta, tb = tiles
    main = partials([(ta, ta["last"]), (tb, tb["last"]), (tb, tb["last"] - 1), (tb, tb["last"] - 2)])
    for k, parts in enumerate([main[:1], main[1:]]):
        m_ref[k], acc_ref[k] = merge(parts)

    @pl.when(ta["last"] > 0)
    def _():
        extra = partials([(ta, 0), (tb, 0)])
        for k in range(n_tiles):
            m_ref[k], acc_ref[k] = merge([(m_ref[k], acc_ref[k]), extra[k]])

    for k, t in enumerate(tiles):
        out_s = acc_ref[k, :dh] * (1.0 / acc_ref[k, dh:dh + 1])
        gates = t["gates"]
        blocks = []
        for h in range(A_HPG):
            cols = slice(h * tq, (h + 1) * tq)
            blocks.append(part_ref[k, :, cols] + gates[A_HPG + h:A_HPG + h + 1, :] * out_s[:, cols])
        o_ref[0, k] = jnp.concatenate(blocks, axis=0).T.astype(BF16)


def _attention(q_t, kc, vc_t, ks, vs_t, kw, vw_t, bias_c, bias_d, gates_t):
    bsz, _, nt, _, _ = q_t.shape
    s = ks.shape[2]
    n_cmp = kc.shape[2]
    assert nt == 4 * SUB and WINDOW // TQ + 1 <= nt
    half = nt // 2
    k_spec = lambda width: pl.BlockSpec((1, 1, s, width), lambda b, g, j: (b, g, 0, 0))
    vt_spec = pl.BlockSpec((1, 1, nt, V_ROWS, TQ), lambda b, g, j: (b, g, 0, 0, 0))
    q_spec = lambda off: pl.BlockSpec((1, 1, 1, A_HEAD_DIM, A_HPG * TQ), lambda b, g, j: (b, g, j + off, 0, 0))
    bc_spec = lambda off: pl.BlockSpec((1, 1, n_cmp, A_HPG * TQ), lambda b, g, j: (j + off, g, 0, 0))
    gt_spec = lambda off: pl.BlockSpec((1, 1, 1, GATE_ROWS, TQ), lambda b, g, j: (b, g, j + off, 0, 0))
    out = pl.pallas_call(
        _attn_kernel,
        grid=(bsz, A_KV_GROUPS, half),
        in_specs=[q_spec(0), q_spec(half),
                  pl.BlockSpec((1, 1, n_cmp, A_HEAD_DIM), lambda b, g, j: (b, g, 0, 0)),
                  pl.BlockSpec((1, 1, A_HEAD_DIM, n_cmp), lambda b, g, j: (b, g, 0, 0)),
                  k_spec(LANES), vt_spec, k_spec(A_HEAD_DIM), vt_spec,
                  bc_spec(0), bc_spec(half),
                  pl.BlockSpec((1, N_BIAS_TILES, TQ, A_HPG * TQ), lambda b, g, j: (g, 0, 0, 0)),
                  gt_spec(0), gt_spec(half)],
        out_specs=pl.BlockSpec((1, 2, TQ, A_HPG * A_HEAD_DIM), lambda b, g, j: (b, 0, j, g)),
        out_shape=jax.ShapeDtypeStruct((bsz, 2, s // 2, A_WIDTH), BF16),
        scratch_shapes=[pltpu.VMEM((2, 1, A_HPG * TQ), F32),
                        pltpu.VMEM((2, V_ROWS, A_HPG * TQ), F32),
                        pltpu.VMEM((2, A_HEAD_DIM, A_HPG * TQ), F32)],
        compiler_params=pltpu.CompilerParams(dimension_semantics=("parallel", "parallel", "parallel")),
        name="attn",
    )(q_t, q_t, kc, vc_t, ks, vs_t, kw, vw_t, bias_c, bias_c, bias_d, gates_t, gates_t)
    return out.reshape(bsz, s, A_WIDTH)


def _rwkv_kernel(c_ref, mu_ref, w0_ref, wl_ref, a0_ref, al_ref, kk_ref, ka_ref, rk_ref, lw_ref, lb_ref,
                 o_ref, state_ref, prev_ref):
    cc = pl.program_id(1)
    n = B_HEAD_DIM
    nb, csz = c_ref.shape[0], c_ref.shape[1]

    @pl.when(cc == 0)
    def _():
        state_ref[...] = jnp.zeros(state_ref.shape, F32)
        prev_ref[...] = jnp.zeros(prev_ref.shape, F32)

    ti = lax.broadcasted_iota(jnp.int32, (csz, LANES), 0)
    si = lax.broadcasted_iota(jnp.int32, (csz, LANES), 1) % n
    lower = si <= ti
    strict = si < ti
    eye = jnp.where(si == ti, 1.0, 0.0)
    tri = jnp.where(lax.broadcasted_iota(jnp.int32, (csz, csz), 1) <= lax.broadcasted_iota(jnp.int32, (csz, csz), 0),
                    1.0, 0.0).astype(BF16)
    n_pairs = B_WIDTH // LANES
    left =lax.broadcasted_iota(jnp.int32, (csz, LANES), 1) < n
    row_left = lax.broadcasted_iota(jnp.int32, (LANES, LANES), 0) < n
    same_head = row_left == (lax.broadcasted_iota(jnp.int32, (LANES, LANES), 1) < n)

    def blockdiag(y):
        zero = jnp.zeros_like(y)
        return jnp.concatenate([jnp.where(left, y, zero), jnp.where(left, zero, y)], axis=0)

    def head_sum(x):
        lo = jnp.sum(jnp.where(left, x, 0.0), axis=-1, keepdims=True)
        hi = jnp.sum(jnp.where(left, 0.0, x), axis=-1, keepdims=True)
        return jnp.where(left, lo, hi)

    chains = []
    for bi in range(nb):
        p = c_ref[bi]
        row = lax.broadcasted_iota(jnp.int32, p.shape, 0)
        prev = jnp.where(row == 0, prev_ref[bi, 0:1, :], pltpu.roll(p, 1, axis=0))
        prev_ref[bi, 0:1, :] = p[csz - 1:csz, :]
        x = p + (prev - p) * mu_ref[...]
        r = x[:, 0:B_WIDTH]
        k = x[:, B_WIDTH:2 * B_WIDTH]
        v = x[:, 2 * B_WIDTH:3 * B_WIDTH]
        wd = x[:, 3 * B_WIDTH:3 * B_WIDTH + DECAY_LORA]
        ad = x[:, 3 * B_WIDTH + DECAY_LORA:3 * B_WIDTH + DECAY_LORA + ICLR_LORA]

        z = -(w0_ref[...] + _dot(jnp.tanh(wd), wl_ref[...]))
        softplus = jnp.maximum(z, 0.0) + jnp.log(1.0 + jnp.exp(-jnp.abs(z)))
        ld = -jnp.exp(-softplus - 0.5)
        a = _sigmoid(a0_ref[...] + _dot(ad, al_ref[...]))
        kk = k * kk_ref[...]
        k_mod = k * (1.0 + (a - 1.0) * ka_ref[...])
        rkr = r * k_mod * rk_ref[...]

        ld_hi, ld_lo = _split2(ld)
        cum = jnp.dot(tri, ld_hi, preferred_element_type=F32) + jnp.dot(tri, ld_lo, preferred_element_type=F32)
        g_inc = jnp.exp(cum)
        g_exc = jnp.exp(cum - ld)
        g_inv = jnp.exp(-cum)
        g_end = jnp.exp(cum[csz - 1:csz, :] - cum)
        g_all = g_inc[csz - 1:csz, :]

        for pr in range(n_pairs):
            sl = slice(pr * LANES, (pr + 1) * LANES)
            kk_p = kk[:, sl]
            kk_p = kk_p * lax.rsqrt(jnp.maximum(head_sum(kk_p * kk_p), 1e-24))
            b_p = kk_p * a[:, sl]
            bt = (b_p * g_inv[:, sl]).astype(BF16)
            kt = (k_mod[:, sl] * g_inv[:, sl]).astype(BF16)
            ch = dict(
                idx=bi * n_pairs + pr,
                v=v[:, sl],
                lhs=jnp.concatenate([-kk_p * g_exc[:, sl], r[:, sl] * g_inc[:, sl]], axis=0).astype(BF16),
                rhs=jnp.concatenate([blockdiag(bt), blockdiag(kt)], axis=0),
                bk=jnp.concatenate([b_p * g_end[:, sl], k_mod[:, sl] * g_end[:, sl]], axis=0).astype(BF16),
                g_all=g_all[:, sl],
                bonus=head_sum(rkr[:, sl]) * v[:, sl],
            )
            chains.append(ch)

    for ch in chains:
        x = _dot_nt(ch["lhs"], ch["rhs"])
        xb, xk = x[:, :LANES], x[:, LANES:]
        ch["a_ab"] = jnp.where(strict, xb[:csz], 0.0)
        a_ak = jnp.where(strict, xk[:csz], 0.0)
        m_rk = jnp.where(lower, xk[csz:], 0.0)
        ch["ak_rk"] = jnp.concatenate([a_ak, m_rk], axis=0).astype(BF16)
        ch["m_rb"] = jnp.where(lower, xb[csz:], 0.0).astype(BF16)
    for ch in chains:
        akv = _dot(ch["ak_rk"], blockdiag(ch["v"].astype(BF16)))
        ch["akv"], ch["mrkv"] = akv[:csz], akv[csz:]
        ch["tinv"] = eye + ch["a_ab"]
        ch["pw"] = ch["a_ab"].astype(BF16)
    n_sq = int(math.log2(csz)) - 1
    for ch in chains:
        ch["pw"] = _dot(ch["pw"], blockdiag(ch["pw"])).astype(BF16)
    for step in range(n_sq):
        for ch in chains:
            if step + 1 < n_sq:
                both = _dot(jnp.concatenate([ch["pw"], ch["tinv"].astype(BF16)], axis=0), blockdiag(ch["pw"]))
                ch["tinv"] = ch["tinv"] + both[csz:]
                ch["pw"] = both[:csz].astype(BF16)
            else:
                ch["tinv"] = ch["tinv"] + _dot(ch["tinv"], blockdiag(ch["pw"]))
    for ch in chains:
        ch["s0"] = state_ref[ch["idx"]]
        ch["as0"] = _dot_nt(ch["lhs"], ch["s0"])
    for ch in chains:
        w = (ch["as0"][:csz] + ch["akv"]).astype(BF16)
        ch["u"] = _dot(ch["tinv"], blockdiag(w))
    outs = []
    for ch in chains:
        u = ch["u"]
        y = ch["as0"][csz:] + _dot(ch["m_rb"], blockdiag(u.astype(BF16))) + ch["mrkv"]
        uv = jnp.concatenate([u, ch["v"]], axis=0)
        state_ref[ch["idx"]] = ch["s0"] * ch["g_all"] + jnp.where(same_head, _dot_tn(uv, ch["bk"]), 0.0)
        yc = y - head_sum(y) * (1.0 / n)
        var = head_sum(yc * yc) * (1.0 / n)
        outs.append(yc * lax.rsqrt(var + LNX_EPS))
    for bi in range(nb):
        yn = jnp.concatenate(outs[bi * n_pairs:(bi + 1) * n_pairs], axis=-1)
        bonus = jnp.concatenate([ch["bonus"] for ch in chains[bi * n_pairs:(bi + 1) * n_pairs]], axis=-1)
        o_ref[bi] = (yn * lw_ref[...] + lb_ref[...] + bonus).astype(BF16)


RWKV_NB = 4


def _rwkv(cols_rwkv, mu, w0, wl, a0, al, k_k, k_a, r_k, ln_w, ln_b):
    bsz, s, _ = cols_rwkv.shape
    nb = RWKV_NB if bsz % RWKV_NB == 0 else 1
    const = lambda b, c: (0, 0)
    vec = pl.BlockSpec((1, B_WIDTH), const)
    return pl.pallas_call(
        _rwkv_kernel,
        grid=(bsz // nb, s // CHUNK),
        in_specs=[pl.BlockSpec((nb, CHUNK, RWKV_COLS), lambda b, c: (b, c, 0)),
                  pl.BlockSpec((1, RWKV_COLS), const),
                  vec, pl.BlockSpec((DECAY_LORA, B_WIDTH), const),
                  vec, pl.BlockSpec((ICLR_LORA, B_WIDTH), const),
                  vec, vec, vec, vec, vec],
        out_specs=pl.BlockSpec((nb, CHUNK, B_WIDTH), lambda b, c: (b, c, 0)),
        out_shape=jax.ShapeDtypeStruct((bsz, s, B_WIDTH), BF16),
        scratch_shapes=[pltpu.VMEM((nb * B_WIDTH // LANES, LANES, LANES), F32),
                        pltpu.VMEM((nb, 8, RWKV_COLS), F32)],
        compiler_params=pltpu.CompilerParams(dimension_semantics=("parallel", "arbitrary")),
        name="rwkv",
    )(cols_rwkv, mu, w0, wl, a0, al, k_k, k_a, r_k, ln_w, ln_b)


def _final_kernel(x_ref, ya_ref, yb_ref, cf_ref, gate_ref, wa_ref, wb_ref, wo_ref, o_ref):
    a_silu = cf_ref[0, :, 0:A_WIDTH].astype(F32)
    b_silu = cf_ref[0, :, A_WIDTH:A_WIDTH + B_WIDTH].astype(F32)
    merge_a = cf_ref[0, :, A_WIDTH + B_WIDTH:A_WIDTH + B_WIDTH + D_MODEL].astype(F32)
    merge_b = cf_ref[0, :, A_WIDTH + B_WIDTH + D_MODEL:A_WIDTH + B_WIDTH + 2 * D_MODEL].astype(F32)
    ya = ya_ref[0].astype(F32) * (a_silu * _sigmoid(a_silu))
    yb = yb_ref[0].astype(F32) * (b_silu * _sigmoid(b_silu))
    merged = _sigmoid(merge_a) * _dot(ya, wa_ref[...]) + _sigmoid(merge_b) * _dot(yb, wb_ref[...])
    o_ref[0] = x_ref[0] + gate_ref[0] * _dot(merged, wo_ref[...])


def _final(x, y_a, y_b, cols_fin, gate, w_out_a, w_out_b, w_o, tm=512):
    bsz, s, _ = x.shape
    const = lambda b, i: (0, 0)
    row = lambda w: pl.BlockSpec((1, tm, w), lambda b, i: (b, i, 0))
    return pl.pallas_call(
        _final_kernel,
        grid=(bsz, s // tm),
        in_specs=[row(D_MODEL), row(A_WIDTH), row(B_WIDTH), row(FIN_COLS),
                  pl.BlockSpec((1, 1, D_MODEL), lambda b, i: (b, 0, 0)),
                  pl.BlockSpec((A_WIDTH, D_MODEL), const),
                  pl.BlockSpec((B_WIDTH, D_MODEL), const),
                  pl.BlockSpec((D_MODEL, D_MODEL), const)],
        out_specs=row(D_MODEL),
        out_shape=jax.ShapeDtypeStruct((bsz, s, D_MODEL), F32),
        compiler_params=pltpu.CompilerParams(dimension_semantics=("parallel", "parallel"),
                                             vmem_limit_bytes=VMEM_LIMIT),
        name="final",
    )(x, y_a, y_b, cols_fin, gate, w_out_a, w_out_b, w_o)


def _split_w_in(w_in):
    nsa_in = 2 * A_WIDTH + 6 * A_KV_WIDTH + 3 * A_HEADS
    o_gate = A_WIDTH + 6 * A_KV_WIDTH
    o_asilu = o_gate + 3 * A_HEADS
    o_shift = nsa_in
    o_rest = nsa_in + RWKV_COLS
    gate_w = w_in[:, o_gate:o_asilu].reshape(D_MODEL, 3, A_KV_GROUPS, A_HPG)
    gate_w = gate_w.transpose(0, 2, 1, 3).reshape(D_MODEL, A_KV_GROUPS, 3 * A_HPG)
    gate_w = jnp.pad(gate_w, ((0, 0), (0, 0), (0, A_HEAD_DIM - 3 * A_HPG))).reshape(D_MODEL, GATE_PAD)
    w_nsa = jnp.concatenate([w_in[:, :o_gate], gate_w], axis=1)
    w_fin = jnp.concatenate([w_in[:, o_asilu:o_shift], w_in[:, o_rest:]], axis=1)
    w_rwkv = w_in[:, o_shift:o_rest]
    return w_nsa.astype(BF16), w_fin.astype(BF16), w_rwkv.astype(BF16)


def _layer(x, c, rel_bias, w_ada, b_ada, norm_gain, w_in, q_norm_gain, k_norm_gain,
           cmp_pos_k, cmp_pos_v, cmp_k_w1, cmp_k_w2, cmp_v_w1, cmp_v_w2,
           shift_mu, w0, w_lora_up, a0, a_lora_up, k_k, k_a, r_k, ln_x_w, ln_x_b,
           w_out_a, w_out_b, w_o):
    bsz, s, _ = x.shape
    assert s % (2 * TQ) == 0 and s // CMP_STRIDE == LANES
    n16 = s // CMP_STRIDE
    mod = _ada(c, w_ada, b_ada)
    w_nsa, w_fin, w_rwkv = _split_w_in(w_in)
    scale = A_HEAD_DIM ** -0.5 * LOG2E
    qg = jnp.tile(q_norm_gain, A_HEADS) * scale
    ksg = jnp.tile(k_norm_gain[1], A_KV_GROUPS)
    kwg = jnp.tile(k_norm_gain[2], A_KV_GROUPS)
    q_t, ks, vs_t, kw, vw_t, gates_t, ck, cols_fin, cols_rwkv = _proj(
        x, mod, norm_gain, w_nsa, w_fin, w_rwkv, qg, ksg, kwg)

    kc, vc_t = _compress(ck, _expand_cmp_pos(cmp_pos_k), _expand_cmp_pos(cmp_pos_v),
                         _expand_cmp_w1(cmp_k_w1), cmp_k_w2.astype(BF16),
                         _expand_cmp_w1(cmp_v_w1), cmp_v_w2.T.astype(BF16),
                         k_norm_gain[0].reshape(1, A_HEAD_DIM))
    bias_c, bias_d = _bias_tables(rel_bias, s, n16)
    y_a = _attention(q_t, kc, vc_t, ks, vs_t, kw, vw_t, bias_c, bias_d, gates_t)

    vec = lambda t: t.reshape(1, -1)
    y_b = _rwkv(cols_rwkv, vec(shift_mu), vec(w0), w_lora_up.astype(BF16), vec(a0), a_lora_up.astype(BF16),
                vec(k_k), vec(k_a), vec(r_k), vec(ln_x_w), vec(ln_x_b))

    gate = mod[:, 2 * D_MODEL:].reshape(bsz, 1, D_MODEL)
    return _final(x, y_a, y_b, cols_fin, gate, w_out_a.astype(BF16), w_out_b.astype(BF16), w_o.astype(BF16))


def kernel(x, c, w_ada, b_ada, norm_gain, w_in, q_norm_gain, k_norm_gain, cmp_pos_k, cmp_pos_v, cmp_k_w1, cmp_k_w2, cmp_v_w1, cmp_v_w2, rel_bias, shift_mu, w0, w_lora_up, a0, a_lora_up, k_k, k_a, r_k, ln_x_w, ln_x_b, w_out_a, w_out_b, w_o):
    for l in range(w_in.shape[0]):
        x = _layer(x, c, rel_bias, w_ada[l], b_ada[l], norm_gain[l], w_in[l], q_norm_gain[l], k_norm_gain[l],
                   cmp_pos_k[l], cmp_pos_v[l], cmp_k_w1[l], cmp_k_w2[l], cmp_v_w1[l], cmp_v_w2[l],
                   shift_mu[l], w0[l], w_lora_up[l], a0[l], a_lora_up[l], k_k[l], k_a[l], r_k[l],
                   ln_x_w[l], ln_x_b[l], w_out_a[l], w_out_b[l], w_o[l])
    return x
```

```python
import functools
import math

import numpy as np
import jax
import jax.numpy as jnp
from jax import lax
from jax.experimental import pallas as pl
from jax.experimental.pallas import tpu as pltpu

F32 = jnp.float32
BF16 = jnp.bfloat16

D_MODEL = 1024
A_HEADS = 8
A_HEAD_DIM = 64
A_KV_GROUPS = 2
A_HPG = A_HEADS // A_KV_GROUPS
A_WIDTH = A_HEADS * A_HEAD_DIM
A_KV_WIDTH = A_KV_GROUPS * A_HEAD_DIM
CMP_BLOCK = 32
CMP_STRIDE = 16
CMP_HIDDEN = 256
SLC_BLOCK = 64
SLC_TOPN = 16
WINDOW = 512
B_HEADS = 8
B_HEAD_DIM = 64
B_WIDTH = B_HEADS * B_HEAD_DIM
DECAY_LORA = 64
ICLR_LORA = 64
LNX_EPS = 64e-5
REL_BUCKETS = 32
REL_MAX_EXACT = 16
REL_MAX_DIST = 128
NORM_EPS = 1e-6
NEG_INF = -1e30
FORCE_SCORE = 1e30

LANES = 128
TQ = 128
CHUNK = 64
GATE_PAD = LANES
LOG2E = math.log2(math.e)
V_ROWS = A_HEAD_DIM + 16
GATE_ROWS = 16
NSA_COLS = A_WIDTH + 6 * A_KV_WIDTH + GATE_PAD
FIN_COLS = A_WIDTH + B_WIDTH + 2 * D_MODEL
RWKV_COLS = 3 * B_WIDTH + DECAY_LORA + ICLR_LORA
VMEM_LIMIT = 56 * 1024 * 1024


def _dot(a, b):
    return jnp.dot(a.astype(BF16), b.astype(BF16), preferred_element_type=F32)


def _dot_nt(a, b):
    return lax.dot_general(a.astype(BF16), b.astype(BF16), (((1,), (1,)), ((), ())),
                           preferred_element_type=F32)


def _dot_tn(a, b):
    return lax.dot_general(a.astype(BF16), b.astype(BF16), (((0,), (0,)), ((), ())),
                           preferred_element_type=F32)


def _split2(x):
    hi = x.astype(BF16)
    lo = (x - hi.astype(F32)).astype(BF16)
    return hi, lo


def _split3(x):
    h1 = x.astype(BF16)
    r1 = x - h1.astype(F32)
    h2 = r1.astype(BF16)
    h3 = (r1 - h2.astype(F32)).astype(BF16)
    return h1, h2, h3


def _sigmoid(x):
    return 1.0 / (1.0 + jnp.exp(-x))


def _bucket_thresholds():
    n = np.arange(0, 4096)
    nf = np.maximum(n, REL_MAX_EXACT).astype(np.float64)
    val = np.log(nf / REL_MAX_EXACT) / math.log(REL_MAX_DIST / REL_MAX_EXACT) * (REL_BUCKETS - REL_MAX_EXACT)
    frac = np.abs(val - np.round(val))
    assert np.all((frac > 1e-4) | (n <= REL_MAX_EXACT) | (n >= REL_MAX_DIST))
    large = REL_MAX_EXACT + np.floor(val + 1e-9).astype(np.int64)
    bucket = np.where(n < REL_MAX_EXACT, n, np.minimum(large, REL_BUCKETS - 1))
    return [int(np.argmax(bucket >= j)) for j in range(REL_BUCKETS)]


_BUCKET_TH = _bucket_thresholds()


def _bias_from_dist(dist, tbl_ref, head):
    val = jnp.full(dist.shape, tbl_ref[0, head], F32)
    for j in range(1, REL_BUCKETS):
        val = jnp.where(dist >= _BUCKET_TH[j], tbl_ref[j, head], val)
    return val


def _ada_kernel(c_ref, w_ref, b_ref, o_ref):
    c = c_ref[...]
    o_ref[...] = _dot(c * _sigmoid(c), w_ref[...]) + b_ref[...]


def _ada(c, w_ada, b_ada):
    bsz = c.shape[0]
    return pl.pallas_call(
        _ada_kernel,
        grid=(3,),
        in_specs=[pl.BlockSpec((bsz, D_MODEL), lambda j: (0, 0)),
                  pl.BlockSpec((D_MODEL, D_MODEL), lambda j: (0, j)),
                  pl.BlockSpec((1, D_MODEL), lambda j: (0, j))],
        out_specs=pl.BlockSpec((bsz, D_MODEL), lambda j: (0, j)),
        out_shape=jax.ShapeDtypeStruct((bsz, 3 * D_MODEL), F32),
        name="ada",
    )(c, w_ada, b_ada.reshape(1, 3 * D_MODEL))


def _norm_rows(x_t, gain_col, n_seg):
    out = []
    for seg in range(n_seg):
        blk = x_t[seg * A_HEAD_DIM:(seg + 1) * A_HEAD_DIM, :]
        ms = jnp.mean(blk * blk, axis=0, keepdims=True)
        out.append(blk * lax.rsqrt(ms + NORM_EPS) * gain_col[seg * A_HEAD_DIM:(seg + 1) * A_HEAD_DIM, :])
    return out


def _proj_kernel(x_ref, mod_ref, g_ref, wn_ref, wf_ref, wr_ref, qg_ref, ksg_ref, kwg_ref,
                 q_ref, ks_ref, vs_ref, kw_ref, vw_ref, gt_ref, ck_ref, of_ref, or_ref):
    tm = x_ref.shape[1]
    x = x_ref[0]
    ms = jnp.mean(x * x, axis=-1, keepdims=True)
    y = x * lax.rsqrt(ms + NORM_EPS) * g_ref[...]
    mod = mod_ref[0]
    h = (y * (1.0 + mod[:, D_MODEL:2 * D_MODEL]) + mod[:, :D_MODEL]).astype(BF16)
    of_ref[0] = jnp.dot(h, wf_ref[...], preferred_element_type=F32).astype(BF16)
    or_ref[0] = jnp.dot(h, wr_ref[...], preferred_element_type=F32)
    cn = jnp.dot(h, wn_ref[...], preferred_element_type=F32)
    ck_ref[0] = cn[:, A_WIDTH:A_WIDTH + 2 * A_KV_WIDTH]

    lane = lax.broadcasted_iota(jnp.int32, (TQ, LANES), 1)
    row = lax.broadcasted_iota(jnp.int32, (TQ, LANES), 0)
    ones_rows = (lax.broadcasted_iota(jnp.int32, (V_ROWS - A_HEAD_DIM, TQ), 0) == 0).astype(BF16)
    off = A_WIDTH + 2 * A_KV_WIDTH
    for sub in range(tm // TQ):
        c = cn[sub * TQ:(sub + 1) * TQ]
        q_heads = _norm_rows(c[:, 0:A_WIDTH].T, qg_ref[...], A_HEADS)
        ks_t = jnp.concatenate(_norm_rows(c[:, off:off + A_KV_WIDTH].T, ksg_ref[...], A_KV_GROUPS), axis=0)
        kw_t = jnp.concatenate(_norm_rows(c[:, off + 2 * A_KV_WIDTH:off + 3 * A_KV_WIDTH].T, kwg_ref[...],
                                          A_KV_GROUPS), axis=0)
        ksn = ks_t.T
        kwn = kw_t.T
        vs_t = c[:, off + A_KV_WIDTH:off + 2 * A_KV_WIDTH].T.astype(BF16)
        vw_t = c[:, off + 3 * A_KV_WIDTH:off + 4 * A_KV_WIDTH].T.astype(BF16)
        gates_t = _sigmoid(c[:, off + 4 * A_KV_WIDTH:off + 5 * A_KV_WIDTH]).T
        blk = (pl.program_id(1) * tm + sub * TQ + row) // SLC_BLOCK
        onehot = jnp.where(lane - A_HEAD_DIM == blk, 1.0, 0.0)
        for g in range(A_KV_GROUPS):
            q_ref[0, g, sub] = jnp.concatenate(q_heads[g * A_HPG:(g + 1) * A_HPG], axis=1).astype(BF16)
            sl = slice(g * A_HEAD_DIM, (g + 1) * A_HEAD_DIM)
            k_g = ksn if g == 0 else pltpu.roll(ksn, A_HEAD_DIM, axis=1)
            ks_ref[0, g, sub * TQ:(sub + 1) * TQ, :] = jnp.where(lane < A_HEAD_DIM, k_g, onehot).astype(BF16)
            kw_g = kwn if g == 0 else pltpu.roll(kwn, A_HEAD_DIM, axis=1)
            kw_ref[0, g, sub * TQ:(sub + 1) * TQ, :] = jnp.where(lane < A_HEAD_DIM, kw_g, 0.0).astype(BF16)
            vs_ref[0, g, sub] = jnp.concatenate([vs_t[sl, :], ones_rows], axis=0)
            vw_ref[0, g, sub] = jnp.concatenate([vw_t[sl, :], ones_rows], axis=0)
            gt_ref[0, g, sub] = gates_t[g * A_HEAD_DIM:g * A_HEAD_DIM + GATE_ROWS, :]


def _proj(x, mod, norm_gain, w_nsa, w_fin, w_rwkv, qg, ksg, kwg, tm=512):
    bsz, s, _ = x.shape
    nt, nsub = s // TQ, tm // TQ
    assert A_HEAD_DIM + s // SLC_BLOCK <= LANES and A_KV_WIDTH == LANES
    const = lambda b, i: (0, 0)
    weight = lambda cols: pl.BlockSpec((D_MODEL, cols), const, pipeline_mode=pl.Buffered(1))
    col = lambda t: jnp.broadcast_to(t.reshape(-1, 1), (t.size, LANES))
    k_spec = lambda width: pl.BlockSpec((1, A_KV_GROUPS, tm, width), lambda b, i: (b, 0, i, 0))
    k_shape = lambda width: jax.ShapeDtypeStruct((bsz, A_KV_GROUPS, s, width), BF16)
    tile_spec = lambda r, c: pl.BlockSpec((1, A_KV_GROUPS, nsub, r, c), lambda b, i: (b, 0, i, 0, 0))
    tile_shape = lambda r, c, dt: jax.ShapeDtypeStruct((bsz, A_KV_GROUPS, nt, r, c), dt)
    return pl.pallas_call(
        _proj_kernel,
        grid=(bsz, s // tm),
        in_specs=[pl.BlockSpec((1, tm, D_MODEL), lambda b, i: (b, i, 0)),
                  pl.BlockSpec((1, 1, 3 * D_MODEL), lambda b, i: (b, 0, 0)),
                  pl.BlockSpec((1, D_MODEL), const),
                  weight(NSA_COLS), weight(FIN_COLS), weight(RWKV_COLS),
                  pl.BlockSpec((A_WIDTH, LANES), const),
                  pl.BlockSpec((A_KV_WIDTH, LANES), const),
                  pl.BlockSpec((A_KV_WIDTH, LANES), const)],
        out_specs=[tile_spec(A_HEAD_DIM, A_HPG * TQ),
                   k_spec(LANES), tile_spec(V_ROWS, TQ), k_spec(LANES), tile_spec(V_ROWS, TQ),
                   tile_spec(GATE_ROWS, TQ),
                   pl.BlockSpec((1, tm, 2 * A_KV_WIDTH), lambda b, i: (b, i, 0)),
                   pl.BlockSpec((1, tm, FIN_COLS), lambda b, i: (b, i, 0)),
                   pl.BlockSpec((1, tm, RWKV_COLS), lambda b, i: (b, i, 0))],
        out_shape=[tile_shape(A_HEAD_DIM, A_HPG * TQ, BF16),
                   k_shape(LANES), tile_shape(V_ROWS, TQ, BF16), k_shape(LANES), tile_shape(V_ROWS, TQ, BF16),
                   tile_shape(GATE_ROWS, TQ, F32),
                   jax.ShapeDtypeStruct((bsz, s, 2 * A_KV_WIDTH), F32),
                   jax.ShapeDtypeStruct((bsz, s, FIN_COLS), BF16),
                   jax.ShapeDtypeStruct((bsz, s, RWKV_COLS), F32)],
        compiler_params=pltpu.CompilerParams(dimension_semantics=("parallel", "parallel"),
                                             vmem_limit_bytes=VMEM_LIMIT),
        name="proj",
    )(x, mod.reshape(bsz, 1, 3 * D_MODEL), norm_gain.reshape(1, D_MODEL), w_nsa, w_fin, w_rwkv,
      col(qg), col(ksg), col(kwg))


def _compress_kernel(ck_ref, cv_ref, pk_ref, pv_ref, w1k_ref, w2k_ref, w1v_ref, w2v_ref, kg_ref, kc_ref, vc_ref):
    n16 = ck_ref.shape[1] // CMP_STRIDE

    def rows16(ref):
        return jnp.concatenate([ref[0, pl.ds(p, n16, stride=CMP_STRIDE), :] for p in range(CMP_STRIDE)], axis=1)

    def hidden(z, pos_ref, w1_ref, g):
        top = _dot(z + pos_ref[0:1, :], w1_ref[g, 0])
        bot = _dot(z + pos_ref[1:2, :], w1_ref[g, 1])
        return jax.nn.gelu(top + pltpu.roll(bot, n16 - 1, axis=0), approximate=True)

    zk = rows16(ck_ref)
    zv = rows16(cv_ref)
    for g in range(A_KV_GROUPS):
        kc = _dot(hidden(zk, pk_ref, w1k_ref, g), w2k_ref[...])
        ms = jnp.mean(kc * kc, axis=-1, keepdims=True)
        kc_ref[0, g] = (kc * lax.rsqrt(ms + NORM_EPS) * kg_ref[...]).astype(BF16)
        vc_ref[0, g] = _dot_nt(w2v_ref[...], hidden(zv, pv_ref, w1v_ref, g)).astype(BF16)


def _expand_cmp_w1(w1):
    w = w1.reshape(2, CMP_STRIDE, 1, A_HEAD_DIM, CMP_HIDDEN)
    per_group = []
    for g in range(A_KV_GROUPS):
        pad = [(0, 0), (0, 0), (g, A_KV_GROUPS - 1 - g), (0, 0), (0, 0)]
        per_group.append(jnp.pad(w, pad).reshape(2, CMP_STRIDE * A_KV_WIDTH, CMP_HIDDEN))
    return jnp.stack(per_group).astype(BF16)


def _expand_cmp_pos(pos):
    p = jnp.broadcast_to(pos.reshape(2, CMP_STRIDE, 1, A_HEAD_DIM), (2, CMP_STRIDE, A_KV_GROUPS, A_HEAD_DIM))
    return p.reshape(2, CMP_STRIDE * A_KV_WIDTH)


def _compress(ck, pk, pv, w1k, w2k, w1v, w2v_t, kg):
    bsz, s, _ = ck.shape
    n16 = s // CMP_STRIDE
    zw = CMP_STRIDE * A_KV_WIDTH
    const = lambda b: (0, 0)
    const4 = lambda b: (0, 0, 0, 0)
    return pl.pallas_call(
        _compress_kernel,
        grid=(bsz,),
        in_specs=[pl.BlockSpec((1, s, A_KV_WIDTH), lambda b: (b, 0, 0)),
                  pl.BlockSpec((1, s, A_KV_WIDTH), lambda b: (b, 0, 1)),
                  pl.BlockSpec((2, zw), const), pl.BlockSpec((2, zw), const),
                  pl.BlockSpec((A_KV_GROUPS, 2, zw, CMP_HIDDEN), const4), pl.BlockSpec((CMP_HIDDEN, A_HEAD_DIM), const),
                  pl.BlockSpec((A_KV_GROUPS, 2, zw, CMP_HIDDEN), const4), pl.BlockSpec((A_HEAD_DIM, CMP_HIDDEN), const),
                  pl.BlockSpec((1, A_HEAD_DIM), const)],
        out_specs=[pl.BlockSpec((1, A_KV_GROUPS, n16, A_HEAD_DIM), lambda b: (b, 0, 0, 0)),
                   pl.BlockSpec((1, A_KV_GROUPS, A_HEAD_DIM, n16), lambda b: (b, 0, 0, 0))],
        out_shape=[jax.ShapeDtypeStruct((bsz, A_KV_GROUPS, n16, A_HEAD_DIM), BF16),
                   jax.ShapeDtypeStruct((bsz, A_KV_GROUPS, A_HEAD_DIM, n16), BF16)],
        compiler_params=pltpu.CompilerParams(dimension_semantics=("parallel",)),
        name="compress",
    )(ck, ck, pk, pv, w1k, w2k, w1v, w2v_t, kg)


TILE_FAR, TILE_EDGE, TILE_MASKED, N_BIAS_TILES = 2, 3, 4, 5
SUB = 4
ATT_TILES = 4


def _bias_cmp_kernel(tbl_ref, o_ref):
    i = pl.program_id(0)
    g = pl.program_id(1)
    n_cmp = o_ref.shape[2]
    n = lax.broadcasted_iota(jnp.int32, (n_cmp, TQ), 0)
    q = lax.broadcasted_iota(jnp.int32, (n_cmp, TQ), 1)
    dist = i * TQ + q - (n * CMP_STRIDE + CMP_BLOCK - 1)
    for h in range(A_HPG):
        bias = _bias_from_dist(dist, tbl_ref, g * A_HPG + h)
        o_ref[0, 0, :, h * TQ:(h + 1) * TQ] = jnp.where(dist >= 0, bias * LOG2E, NEG_INF)


def _bias_toeplitz_kernel(tbl_ref, o_ref):
    g = pl.program_id(0)
    r = pl.program_id(1)
    off = jnp.where(r == TILE_EDGE, WINDOW // TQ, jnp.where(r == TILE_MASKED, -2, r))
    k = lax.broadcasted_iota(jnp.int32, (TQ, TQ), 0)
    q = lax.broadcasted_iota(jnp.int32, (TQ, TQ), 1)
    dist = off * TQ + q - k
    valid = (dist >= 0) & (dist < WINDOW)
    for h in range(A_HPG):
        bias = _bias_from_dist(dist, tbl_ref, g * A_HPG + h)
        o_ref[0, 0, :, h * TQ:(h + 1) * TQ] = jnp.where(valid, bias * LOG2E, NEG_INF)


def _bias_tables(rel_bias, s, n_cmp):
    smem = pl.BlockSpec(memory_space=pltpu.SMEM)
    nt = s // TQ
    bias_c = pl.pallas_call(
        _bias_cmp_kernel,
        grid=(nt, A_KV_GROUPS),
        in_specs=[smem],
        out_specs=pl.BlockSpec((1, 1, n_cmp, A_HPG * TQ), lambda i, g: (i, g, 0, 0)),
        out_shape=jax.ShapeDtypeStruct((nt, A_KV_GROUPS, n_cmp, A_HPG * TQ), F32),
        name="bias_cmp",
    )(rel_bias)
    assert _BUCKET_TH[REL_BUCKETS - 1] <= TQ + 1 and WINDOW // TQ >= 3
    bias_d = pl.pallas_call(
        _bias_toeplitz_kernel,
        grid=(A_KV_GROUPS, N_BIAS_TILES),
        in_specs=[smem],
        out_specs=pl.BlockSpec((1, 1, TQ, A_HPG * TQ), lambda g, r: (g, r, 0, 0)),
        out_shape=jax.ShapeDtypeStruct((A_KV_GROUPS, N_BIAS_TILES, TQ, A_HPG * TQ), F32),
        name="bias_toeplitz",
    )(rel_bias)
    return bias_c, bias_d


def _attn_kernel(*refs):
    q_refs, refs = refs[:ATT_TILES], refs[ATT_TILES:]
    kc_ref, vc_ref, ks_ref, vs_ref, kw_ref, vw_ref = refs[:6]
    bc_refs, bd_ref, gt_refs, o_ref = refs[6:6 + ATT_TILES], refs[6 + ATT_TILES], refs[7 + ATT_TILES:-1], refs[-1]
    j = pl.program_id(2)
    tq = TQ
    n_cmp = kc_ref.shape[2]
    n_slc = ks_ref.shape[2] // SLC_BLOCK
    wt = WINDOW // tq
    dh = A_HEAD_DIM
    tiles = [dict(i=j + t * SUB, last=t, q=q_refs[t][0, 0, 0], bias_c=bc_refs[t], gates=gt_refs[t][0, 0, 0])
             for t in range(ATT_TILES)]
    zero_rows = jnp.zeros((LANES - dh, A_HPG * tq), BF16)

    def scores(k_slab, q_mat, i, first_tile, n_sub, tile_index):
        s = jnp.dot(k_slab, q_mat, preferred_element_type=F32)
        parts = [s[t * tq:(t + 1) * tq] + bd_ref[0, tile_index(i - (first_tile + t))] for t in range(n_sub)]
        return jnp.concatenate(parts, axis=0)

    def values_t(v_ref, first_tile, n_sub):
        return jnp.concatenate([v_ref[0, 0, first_tile + t] for t in range(n_sub)], axis=1)

    win_tile = lambda r: jnp.where(r < 0, TILE_MASKED, jnp.where(r == wt, TILE_EDGE, jnp.minimum(r, TILE_FAR)))
    for t in tiles:
        t["first_w"] = jnp.maximum(t["i"] - wt, 0)
        k0 = pl.multiple_of(t["first_w"] * tq, tq)
        q_pad = jnp.concatenate([t["q"], zero_rows], axis=0)
        t["s_w"] = scores(kw_ref[0, 0, pl.ds(k0, (wt + 1) * tq), :], q_pad, t["i"], t["first_w"], wt + 1, win_tile)
    for t in tiles:
        bias = t["bias_c"][0, 0]
        t["valid_c"] = bias > 0.5 * NEG_INF
        t["s_c"] = jnp.dot(kc_ref[0, 0], t["q"], preferred_element_type=F32) + bias

    r1, r2 = SLC_BLOCK // CMP_STRIDE, CMP_BLOCK // CMP_STRIDE
    jj = lax.broadcasted_iota(jnp.int32, (n_slc, n_cmp), 0)
    nn = lax.broadcasted_iota(jnp.int32, (n_slc, n_cmp), 1)
    d = nn - r1 * jj
    cnt = jnp.zeros((n_slc, n_cmp), F32)
    for a in range(r1):
        for c in range(r2):
            cnt = cnt + jnp.where(d == a - c, 1.0, 0.0)
    cnt = cnt.astype(BF16)
    for t in tiles:
        s = t["s_c"]
        e = jnp.where(t["valid_c"], jnp.exp2(s - jnp.max(s, axis=0, keepdims=True)), 0.0)
        l = jnp.sum(e, axis=0, keepdims=True)
        p = e * (1.0 / jnp.where(l > 0.0, l, 1.0))
        t["out_c"] = jnp.dot(vc_ref[0, 0], p.astype(BF16), preferred_element_type=F32)
        p_grp = sum(p[:, h * tq:(h + 1) * tq] for h in range(A_HPG))
        t["imp"] = sum(jnp.dot(cnt, part, preferred_element_type=F32) for part in _split3(p_grp))

    for t in tiles:
        s = t["s_w"]
        t["p_w"] = jnp.exp2(s - jnp.max(s, axis=0, keepdims=True)).astype(BF16)
    for t in tiles:
        acc = jnp.dot(values_t(vw_ref, t["first_w"], wt + 1), t["p_w"], preferred_element_type=F32)
        out_w = acc[:dh] * (1.0 / acc[dh:dh + 1])
        gates = t["gates"]
        t["part"] = [gates[h:h + 1, :] * t["out_c"][:, h * tq:(h + 1) * tq]
                     + gates[2 * A_HPG + h:2 * A_HPG + h + 1, :] * out_w[:, h * tq:(h + 1) * tq]
                     for h in range(A_HPG)]

    blk = lax.broadcasted_iota(jnp.int32, (n_slc, tq), 0)
    for t in tiles:
        tpos = t["i"] * tq + lax.broadcasted_iota(jnp.int32, (n_slc, tq), 1)
        cur = tpos // SLC_BLOCK
        forced = (blk == 0) | (blk == cur) | (blk == cur - 1)
        causal = blk * SLC_BLOCK <= tpos
        imp = jnp.where(forced, FORCE_SCORE, jnp.where(causal, t["imp"], NEG_INF))
        rank = jnp.zeros((n_slc, tq), F32)
        for c in range(n_slc):
            row = imp[c:c + 1, :]
            ahead = (row > imp) | ((row == imp) & (blk > c))
            rank = rank + jnp.where(ahead, 1.0, 0.0)
        pen = jnp.where(rank < float(min(SLC_TOPN, n_slc)), 0.0, -FORCE_SCORE).astype(BF16)
        t["q_aug"] = jnp.concatenate([t["q"], jnp.concatenate([pen] * A_HPG, axis=1),
                                      zero_rows[:LANES - dh - n_slc]], axis=0)

    sel_tile = lambda r: jnp.where(r < 0, TILE_MASKED, jnp.minimum(r, TILE_FAR))

    jobs = [(t, c) for t in tiles for c in range(t["last"], -1, -1)]
    ss = [scores(ks_ref[0, 0, c * SUB * tq:(c + 1) * SUB * tq, :], t["q_aug"], t["i"], c * SUB, SUB, sel_tile)
          for t, c in jobs]
    ms = [jnp.max(s, axis=0, keepdims=True) for s in ss]
    ps = [jnp.exp2(s - m).astype(BF16) for s, m in zip(ss, ms)]
    accs = [jnp.dot(values_t(vs_ref, c * SUB, SUB), p, preferred_element_type=F32) for (t, c), p in zip(jobs, ps)]

    for k, t in enumerate(tiles):
        mine = [n for n, (tt, c) in enumerate(jobs) if tt is t]
        m = functools.reduce(jnp.maximum, [ms[n] for n in mine])
        acc = sum(jnp.exp2(ms[n] - m) * accs[n] for n in mine)
        out_s = acc[:dh] * (1.0 / acc[dh:dh + 1])
        gates = t["gates"]
        blocks = [t["part"][h] + gates[A_HPG + h:A_HPG + h + 1, :] * out_s[:, h * tq:(h + 1) * tq]
                  for h in range(A_HPG)]
        o_ref[0, k] = jnp.concatenate(blocks, axis=0).T.astype(BF16)


def _attention(q_t, kc, vc_t, ks, vs_t, kw, vw_t, bias_c, bias_d, gates_t):
    bsz, _, nt, _, _ = q_t.shape
    s = ks.shape[2]
    n_cmp = kc.shape[2]
    assert nt == ATT_TILES * SUB and WINDOW // TQ + 1 <= nt
    k_spec = pl.BlockSpec((1, 1, s, LANES), lambda b, g, j: (b, g, 0, 0))
    vt_spec = pl.BlockSpec((1, 1, nt, V_ROWS, TQ), lambda b, g, j: (b, g, 0, 0, 0))
    per_tile = lambda spec: [spec(t * SUB) for t in range(ATT_TILES)]
    q_spec = lambda off: pl.BlockSpec((1, 1, 1, A_HEAD_DIM, A_HPG * TQ), lambda b, g, j: (b, g, j + off, 0, 0))
    bc_spec = lambda off: pl.BlockSpec((1, 1, n_cmp, A_HPG * TQ), lambda b, g, j: (j + off, g, 0, 0))
    gt_spec = lambda off: pl.BlockSpec((1, 1, 1, GATE_ROWS, TQ), lambda b, g, j: (b, g, j + off, 0, 0))
    out = pl.pallas_call(
        _attn_kernel,
        grid=(bsz, A_KV_GROUPS, SUB),
        in_specs=(per_tile(q_spec)
                  + [pl.BlockSpec((1, 1, n_cmp, A_HEAD_DIM), lambda b, g, j: (b, g, 0, 0)),
                     pl.BlockSpec((1, 1, A_HEAD_DIM, n_cmp), lambda b, g, j: (b, g, 0, 0)),
                     k_spec, vt_spec, k_spec, vt_spec]
                  + per_tile(bc_spec)
                  + [pl.BlockSpec((1, N_BIAS_TILES, TQ, A_HPG * TQ), lambda b, g, j: (g, 0, 0, 0))]
                  + per_tile(gt_spec)),
        out_specs=pl.BlockSpec((1, ATT_TILES, TQ, A_HPG * A_HEAD_DIM), lambda b, g, j: (b, 0, j, g)),
        out_shape=jax.ShapeDtypeStruct((bsz, ATT_TILES, s // ATT_TILES, A_WIDTH), BF16),
        compiler_params=pltpu.CompilerParams(dimension_semantics=("parallel", "parallel", "parallel"),
                                             vmem_limit_bytes=VMEM_LIMIT),
        name="attn",
    )(*([q_t] * ATT_TILES), kc, vc_t, ks, vs_t, kw, vw_t, *([bias_c] * ATT_TILES), bias_d,
      *([gates_t] * ATT_TILES))
    return out.reshape(bsz, s, A_WIDTH)


def _rwkv_kernel(c_ref, mu_ref, w0_ref, wl_ref, a0_ref, al_ref, kk_ref, ka_ref, rk_ref, lw_ref, lb_ref,
                 o_ref, state_ref, prev_ref):
    cc = pl.program_id(1)
    n = B_HEAD_DIM
    nb, csz = c_ref.shape[0], c_ref.shape[1]

    @pl.when(cc == 0)
    def _():
        state_ref[...] = jnp.zeros(state_ref.shape, F32)
        prev_ref[...] = jnp.zeros(prev_ref.shape, F32)

    ti = lax.broadcasted_iota(jnp.int32, (csz, LANES), 0)
    si = lax.broadcasted_iota(jnp.int32, (csz, LANES), 1) % n
    lower = si <= ti
    strict = si < ti
    eye = jnp.where(si == ti, 1.0, 0.0)
    tri = jnp.where(lax.broadcasted_iota(jnp.int32, (csz, csz), 1) <= lax.broadcasted_iota(jnp.int32, (csz, csz), 0),
                    1.0, 0.0).astype(BF16)
    n_pairs = B_WIDTH // LANES
    left =lax.broadcasted_iota(jnp.int32, (csz, LANES), 1) < n
    row_left = lax.broadcasted_iota(jnp.int32, (LANES, LANES), 0) < n
    same_head = row_left == (lax.broadcasted_iota(jnp.int32, (LANES, LANES), 1) < n)

    def blockdiag(y):
        zero = jnp.zeros_like(y)
        return jnp.concatenate([jnp.where(left, y, zero), jnp.where(left, zero, y)], axis=0)

    def head_sum(x):
        lo = jnp.sum(jnp.where(left, x, 0.0), axis=-1, keepdims=True)
        hi = jnp.sum(jnp.where(left, 0.0, x), axis=-1, keepdims=True)
        return jnp.where(left, lo, hi)

    chains = []
    for bi in range(nb):
        p = c_ref[bi]
        row = lax.broadcasted_iota(jnp.int32, p.shape, 0)
        prev = jnp.where(row == 0, prev_ref[bi, 0:1, :], pltpu.roll(p, 1, axis=0))
        prev_ref[bi, 0:1, :] = p[csz - 1:csz, :]
        x = p + (prev - p) * mu_ref[...]
        r = x[:, 0:B_WIDTH]
        k = x[:, B_WIDTH:2 * B_WIDTH]
        v = x[:, 2 * B_WIDTH:3 * B_WIDTH]
        wd = x[:, 3 * B_WIDTH:3 * B_WIDTH + DECAY_LORA]
        ad = x[:, 3 * B_WIDTH + DECAY_LORA:3 * B_WIDTH + DECAY_LORA + ICLR_LORA]

        z = -(w0_ref[...] + _dot(jnp.tanh(wd), wl_ref[...]))
        softplus = jnp.maximum(z, 0.0) + jnp.log(1.0 + jnp.exp(-jnp.abs(z)))
        ld = -jnp.exp(-softplus - 0.5)
        a = _sigmoid(a0_ref[...] + _dot(ad, al_ref[...]))
        kk = k * kk_ref[...]
        k_mod = k * (1.0 + (a - 1.0) * ka_ref[...])
        rkr = r * k_mod * rk_ref[...]

        ld_hi, ld_lo = _split2(ld)
        cum = jnp.dot(tri, ld_hi, preferred_element_type=F32) + jnp.dot(tri, ld_lo, preferred_element_type=F32)
        g_inc = jnp.exp(cum)
        g_exc = jnp.exp(cum - ld)
        g_inv = jnp.exp(-cum)
        g_end = jnp.exp(cum[csz - 1:csz, :] - cum)
        g_all = g_inc[csz - 1:csz, :]

        for pr in range(n_pairs):
            sl = slice(pr * LANES, (pr + 1) * LANES)
            kk_p = kk[:, sl]
            kk_p = kk_p * lax.rsqrt(jnp.maximum(head_sum(kk_p * kk_p), 1e-24))
            b_p = kk_p * a[:, sl]
            bt = (b_p * g_inv[:, sl]).astype(BF16)
            kt = (k_mod[:, sl] * g_inv[:, sl]).astype(BF16)
            ch = dict(
                idx=bi * n_pairs + pr,
                v=v[:, sl],
                lhs=jnp.concatenate([-kk_p * g_exc[:, sl], r[:, sl] * g_inc[:, sl]], axis=0).astype(BF16),
                rhs=jnp.concatenate([blockdiag(bt), blockdiag(kt)], axis=0),
                bk=jnp.concatenate([b_p * g_end[:, sl], k_mod[:, sl] * g_end[:, sl]], axis=0).astype(BF16),
                g_all=g_all[:, sl],
                bonus=head_sum(rkr[:, sl]) * v[:, sl],
            )
            chains.append(ch)

    for ch in chains:
        x = _dot_nt(ch["lhs"], ch["rhs"])
        xb, xk = x[:, :LANES], x[:, LANES:]
        ch["a_ab"] = jnp.where(strict, xb[:csz], 0.0)
        a_ak = jnp.where(strict, xk[:csz], 0.0)
        m_rk = jnp.where(lower, xk[csz:], 0.0)
        ch["ak_rk"] = jnp.concatenate([a_ak, m_rk], axis=0).astype(BF16)
        ch["m_rb"] = jnp.where(lower, xb[csz:], 0.0).astype(BF16)
    for ch in chains:
        akv = _dot(ch["ak_rk"], blockdiag(ch["v"].astype(BF16)))
        ch["akv"], ch["mrkv"] = akv[:csz], akv[csz:]
        ch["tinv"] = eye + ch["a_ab"]
        ch["pw"] = ch["a_ab"].astype(BF16)
    n_sq = int(math.log2(csz)) - 1
    for ch in chains:
        ch["pw"] = _dot(ch["pw"], blockdiag(ch["pw"])).astype(BF16)
    for step in range(n_sq):
        for ch in chains:
            if step + 1 < n_sq:
                both = _dot(jnp.concatenate([ch["pw"], ch["tinv"].astype(BF16)], axis=0), blockdiag(ch["pw"]))
                ch["tinv"] = ch["tinv"] + both[csz:]
                ch["pw"] = both[:csz].astype(BF16)
            else:
                ch["tinv"] = ch["tinv"] + _dot(ch["tinv"], blockdiag(ch["pw"]))
    for ch in chains:
        ch["s0"] = state_ref[ch["idx"]]
        ch["as0"] = _dot_nt(ch["lhs"], ch["s0"])
    for ch in chains:
        w = (ch["as0"][:csz] + ch["akv"]).astype(BF16)
        ch["u"] = _dot(ch["tinv"], blockdiag(w))
    outs = []
    for ch in chains:
        u = ch["u"]
        y = ch["as0"][csz:] + _dot(ch["m_rb"], blockdiag(u.astype(BF16))) + ch["mrkv"]
        uv = jnp.concatenate([u, ch["v"]], axis=0)
        state_ref[ch["idx"]] = ch["s0"] * ch["g_all"] + jnp.where(same_head, _dot_tn(uv, ch["bk"]), 0.0)
        yc = y - head_sum(y) * (1.0 / n)
        var = head_sum(yc * yc) * (1.0 / n)
        outs.append(yc * lax.rsqrt(var + LNX_EPS))
    for bi in range(nb):
        yn = jnp.concatenate(outs[bi * n_pairs:(bi + 1) * n_pairs], axis=-1)
        bonus = jnp.concatenate([ch["bonus"] for ch in chains[bi * n_pairs:(bi + 1) * n_pairs]], axis=-1)
        o_ref[bi] = (yn * lw_ref[...] + lb_ref[...] + bonus).astype(BF16)


RWKV_NB = 8


def _rwkv(cols_rwkv, mu, w0, wl, a0, al, k_k, k_a, r_k, ln_w, ln_b):
    bsz, s, _ = cols_rwkv.shape
    nb = RWKV_NB if bsz % RWKV_NB == 0 else 1
    const = lambda b, c: (0, 0)
    vec = pl.BlockSpec((1, B_WIDTH), const)
    return pl.pallas_call(
        _rwkv_kernel,
        grid=(bsz // nb, s // CHUNK),
        in_specs=[pl.BlockSpec((nb, CHUNK, RWKV_COLS), lambda b, c: (b, c, 0)),
                  pl.BlockSpec((1, RWKV_COLS), const),
                  vec, pl.BlockSpec((DECAY_LORA, B_WIDTH), const),
                  vec, pl.BlockSpec((ICLR_LORA, B_WIDTH), const),
                  vec, vec, vec, vec, vec],
        out_specs=pl.BlockSpec((nb, CHUNK, B_WIDTH), lambda b, c: (b, c, 0)),
        out_shape=jax.ShapeDtypeStruct((bsz, s, B_WIDTH), BF16),
        scratch_shapes=[pltpu.VMEM((nb * B_WIDTH // LANES, LANES, LANES), F32),
                        pltpu.VMEM((nb, 8, RWKV_COLS), F32)],
        compiler_params=pltpu.CompilerParams(dimension_semantics=("parallel", "arbitrary")),
        name="rwkv",
    )(cols_rwkv, mu, w0, wl, a0, al, k_k, k_a, r_k, ln_w, ln_b)


def _final_kernel(x_ref, ya_ref, yb_ref, cf_ref, gate_ref, wa_ref, wb_ref, wo_ref, o_ref):
    a_silu = cf_ref[0, :, 0:A_WIDTH].astype(F32)
    b_silu = cf_ref[0, :, A_WIDTH:A_WIDTH + B_WIDTH].astype(F32)
    merge_a = cf_ref[0, :, A_WIDTH + B_WIDTH:A_WIDTH + B_WIDTH + D_MODEL].astype(F32)
    merge_b = cf_ref[0, :, A_WIDTH + B_WIDTH + D_MODEL:A_WIDTH + B_WIDTH + 2 * D_MODEL].astype(F32)
    ya = ya_ref[0].astype(F32) * (a_silu * _sigmoid(a_silu))
    yb = yb_ref[0].astype(F32) * (b_silu * _sigmoid(b_silu))
    merged = _sigmoid(merge_a) * _dot(ya, wa_ref[...]) + _sigmoid(merge_b) * _dot(yb, wb_ref[...])
    o_ref[0] = x_ref[0] + gate_ref[0] * _dot(merged, wo_ref[...])


def _final(x, y_a, y_b, cols_fin, gate, w_out_a, w_out_b, w_o, tm=512):
    bsz, s, _ = x.shape
    const = lambda b, i: (0, 0)
    row = lambda w: pl.BlockSpec((1, tm, w), lambda b, i: (b, i, 0))
    return pl.pallas_call(
        _final_kernel,
        grid=(bsz, s // tm),
        in_specs=[row(D_MODEL), row(A_WIDTH), row(B_WIDTH), row(FIN_COLS),
                  pl.BlockSpec((1, 1, D_MODEL), lambda b, i: (b, 0, 0)),
                  pl.BlockSpec((A_WIDTH, D_MODEL), const),
                  pl.BlockSpec((B_WIDTH, D_MODEL), const),
                  pl.BlockSpec((D_MODEL, D_MODEL), const)],
        out_specs=row(D_MODEL),
        out_shape=jax.ShapeDtypeStruct((bsz, s, D_MODEL), F32),
        compiler_params=pltpu.CompilerParams(dimension_semantics=("parallel", "parallel"),
                                             vmem_limit_bytes=VMEM_LIMIT),
        name="final",
    )(x, y_a, y_b, cols_fin, gate, w_out_a, w_out_b, w_o)


def _split_w_in(w_in):
    nsa_in = 2 * A_WIDTH + 6 * A_KV_WIDTH + 3 * A_HEADS
    o_gate = A_WIDTH + 6 * A_KV_WIDTH
    o_asilu = o_gate + 3 * A_HEADS
    o_shift = nsa_in
    o_rest = nsa_in + RWKV_COLS
    gate_w = w_in[:, o_gate:o_asilu].reshape(D_MODEL, 3, A_KV_GROUPS, A_HPG)
    gate_w = gate_w.transpose(0, 2, 1, 3).reshape(D_MODEL, A_KV_GROUPS, 3 * A_HPG)
    gate_w = jnp.pad(gate_w, ((0, 0), (0, 0), (0, A_HEAD_DIM - 3 * A_HPG))).reshape(D_MODEL, GATE_PAD)
    w_nsa = jnp.concatenate([w_in[:, :o_gate], gate_w], axis=1)
    w_fin = jnp.concatenate([w_in[:, o_asilu:o_shift], w_in[:, o_rest:]], axis=1)
    w_rwkv = w_in[:, o_shift:o_rest]
    return w_nsa.astype(BF16), w_fin.astype(BF16), w_rwkv.astype(BF16)


def _layer(x, c, rel_bias, w_ada, b_ada, norm_gain, w_in, q_norm_gain, k_norm_gain,
           cmp_pos_k, cmp_pos_v, cmp_k_w1, cmp_k_w2, cmp_v_w1, cmp_v_w2,
           shift_mu, w0, w_lora_up, a0, a_lora_up, k_k, k_a, r_k, ln_x_w, ln_x_b,
           w_out_a, w_out_b, w_o):
    bsz, s, _ = x.shape
    assert s % (2 * TQ) == 0 and s // CMP_STRIDE == LANES
    n16 = s // CMP_STRIDE
    mod = _ada(c, w_ada, b_ada)
    w_nsa, w_fin, w_rwkv = _split_w_in(w_in)
    scale = A_HEAD_DIM ** -0.5 * LOG2E
    qg = jnp.tile(q_norm_gain, A_HEADS) * scale
    ksg = jnp.tile(k_norm_gain[1], A_KV_GROUPS)
    kwg = jnp.tile(k_norm_gain[2], A_KV_GROUPS)
    q_t, ks, vs_t, kw, vw_t, gates_t, ck, cols_fin, cols_rwkv = _proj(
        x, mod, norm_gain, w_nsa, w_fin, w_rwkv, qg, ksg, kwg)

    kc, vc_t = _compress(ck, _expand_cmp_pos(cmp_pos_k), _expand_cmp_pos(cmp_pos_v),
                         _expand_cmp_w1(cmp_k_w1), cmp_k_w2.astype(BF16),
                         _expand_cmp_w1(cmp_v_w1), cmp_v_w2.T.astype(BF16),
                         k_norm_gain[0].reshape(1, A_HEAD_DIM))
    bias_c, bias_d = _bias_tables(rel_bias, s, n16)
    y_a = _attention(q_t, kc, vc_t, ks, vs_t, kw, vw_t, bias_c, bias_d, gates_t)

    vec = lambda t: t.reshape(1, -1)
    y_b = _rwkv(cols_rwkv, vec(shift_mu), vec(w0), w_lora_up.astype(BF16), vec(a0), a_lora_up.astype(BF16),
                vec(k_k), vec(k_a), vec(r_k), vec(ln_x_w), vec(ln_x_b))

    gate = mod[:, 2 * D_MODEL:].reshape(bsz, 1, D_MODEL)
    return _final(x, y_a, y_b, cols_fin, gate, w_out_a.astype(BF16), w_out_b.astype(BF16), w_o.astype(BF16))


def kernel(x, c, w_ada, b_ada, norm_gain, w_in, q_norm_gain, k_norm_gain, cmp_pos_k, cmp_pos_v, cmp_k_w1, cmp_k_w2, cmp_v_w1, cmp_v_w2, rel_bias, shift_mu, w0, w_lora_up, a0, a_lora_up, k_k, k_a, r_k, ln_x_w, ln_x_b, w_out_a, w_out_b, w_o):
    for l in range(w_in.shape[0]):
        x = _layer(x, c, rel_bias, w_ada[l], b_ada[l], norm_gain[l], w_in[l], q_norm_gain[l], k_norm_gain[l],
                   cmp_pos_k[l], cmp_pos_v[l], cmp_k_w1[l], cmp_k_w2[l], cmp_v_w1[l], cmp_v_w2[l],
                   shift_mu[l], w0[l], w_lora_up[l], a0[l], a_lora_up[l], k_k[l], k_a[l], r_k[l],
                   ln_x_w[l], ln_x_b[l], w_out_a[l], w_out_b[l], w_o[l])
    return x
```

```python
import functools
import math

import numpy as np
import jax
import jax.numpy as jnp
from jax import lax
from jax.experimental import pallas as pl
from jax.experimental.pallas import tpu as pltpu

F32 = jnp.float32
BF16 = jnp.bfloat16

D_MODEL = 1024
A_HEADS = 8
A_HEAD_DIM = 64
A_KV_GROUPS = 2
A_HPG = A_HEADS // A_KV_GROUPS
A_WIDTH = A_HEADS * A_HEAD_DIM
A_KV_WIDTH = A_KV_GROUPS * A_HEAD_DIM
CMP_BLOCK = 32
CMP_STRIDE = 16
CMP_HIDDEN = 256
SLC_BLOCK = 64
SLC_TOPN = 16
WINDOW = 512
B_HEADS = 8
B_HEAD_DIM = 64
B_WIDTH = B_HEADS * B_HEAD_DIM
DECAY_LORA = 64
ICLR_LORA = 64
LNX_EPS = 64e-5
REL_BUCKETS = 32
REL_MAX_EXACT = 16
REL_MAX_DIST = 128
NORM_EPS = 1e-6
NEG_INF = -1e30
FORCE_SCORE = 1e30

LANES = 128
TQ = 128
CHUNK = 64
GATE_PAD = LANES
LOG2E = math.log2(math.e)
V_ROWS = A_HEAD_DIM + 16
GATE_ROWS = 16
NSA_COLS = A_WIDTH + 6 * A_KV_WIDTH + GATE_PAD
FIN_COLS = A_WIDTH + B_WIDTH + 2 * D_MODEL
RWKV_COLS = 3 * B_WIDTH + DECAY_LORA + ICLR_LORA
VMEM_LIMIT = 56 * 1024 * 1024


def _dot(a, b):
    return jnp.dot(a.astype(BF16), b.astype(BF16), preferred_element_type=F32)


def _dot_nt(a, b):
    return lax.dot_general(a.astype(BF16), b.astype(BF16), (((1,), (1,)), ((), ())),
                           preferred_element_type=F32)


def _dot_tn(a, b):
    return lax.dot_general(a.astype(BF16), b.astype(BF16), (((0,), (0,)), ((), ())),
                           preferred_element_type=F32)


def _split2(x):
    hi = x.astype(BF16)
    lo = (x - hi.astype(F32)).astype(BF16)
    return hi, lo


def _split3(x):
    h1 = x.astype(BF16)
    r1 = x - h1.astype(F32)
    h2 = r1.astype(BF16)
    h3 = (r1 - h2.astype(F32)).astype(BF16)
    return h1, h2, h3


def _sigmoid(x):
    return 1.0 / (1.0 + jnp.exp(-x))


def _bucket_thresholds():
    n = np.arange(0, 4096)
    nf = np.maximum(n, REL_MAX_EXACT).astype(np.float64)
    val = np.log(nf / REL_MAX_EXACT) / math.log(REL_MAX_DIST / REL_MAX_EXACT) * (REL_BUCKETS - REL_MAX_EXACT)
    frac = np.abs(val - np.round(val))
    assert np.all((frac > 1e-4) | (n <= REL_MAX_EXACT) | (n >= REL_MAX_DIST))
    large = REL_MAX_EXACT + np.floor(val + 1e-9).astype(np.int64)
    bucket = np.where(n < REL_MAX_EXACT, n, np.minimum(large, REL_BUCKETS - 1))
    return [int(np.argmax(bucket >= j)) for j in range(REL_BUCKETS)]


_BUCKET_TH = _bucket_thresholds()


def _bias_from_dist(dist, tbl_ref, head):
    val = jnp.full(dist.shape, tbl_ref[0, head], F32)
    for j in range(1, REL_BUCKETS):
        val = jnp.where(dist >= _BUCKET_TH[j], tbl_ref[j, head], val)
    return val


def _ada_kernel(c_ref, w_ref, b_ref, o_ref):
    c = c_ref[...]
    o_ref[...] = _dot(c * _sigmoid(c), w_ref[...]) + b_ref[...]


def _ada(c, w_ada, b_ada):
    bsz = c.shape[0]
    return pl.pallas_call(
        _ada_kernel,
        grid=(3,),
        in_specs=[pl.BlockSpec((bsz, D_MODEL), lambda j: (0, 0)),
                  pl.BlockSpec((D_MODEL, D_MODEL), lambda j: (0, j)),
                  pl.BlockSpec((1, D_MODEL), lambda j: (0, j))],
        out_specs=pl.BlockSpec((bsz, D_MODEL), lambda j: (0, j)),
        out_shape=jax.ShapeDtypeStruct((bsz, 3 * D_MODEL), F32),
        name="ada",
    )(c, w_ada, b_ada.reshape(1, 3 * D_MODEL))


def _norm_rows(x_t, gain_col, n_seg):
    out = []
    for seg in range(n_seg):
        blk = x_t[seg * A_HEAD_DIM:(seg + 1) * A_HEAD_DIM, :]
        ms = jnp.mean(blk * blk, axis=0, keepdims=True)
        out.append(blk * lax.rsqrt(ms + NORM_EPS) * gain_col[seg * A_HEAD_DIM:(seg + 1) * A_HEAD_DIM, :])
    return out


def _proj_kernel(x_ref, mod_ref, g_ref, wn_ref, wf_ref, wr_ref, qg_ref, ksg_ref, kwg_ref,
                 q_ref, ks_ref, vs_ref, kw_ref, vw_ref, gt_ref, ck_ref, of_ref, or_ref):
    tm = x_ref.shape[1]
    x = x_ref[0]
    ms = jnp.mean(x * x, axis=-1, keepdims=True)
    y = x * lax.rsqrt(ms + NORM_EPS) * g_ref[...]
    mod = mod_ref[0]
    h = (y * (1.0 + mod[:, D_MODEL:2 * D_MODEL]) + mod[:, :D_MODEL]).astype(BF16)
    cn = jnp.dot(h, wn_ref[...], preferred_element_type=F32)
    or_ref[0] = jnp.dot(h, wr_ref[...], preferred_element_type=F32)
    ck_ref[0] = cn[:, A_WIDTH:A_WIDTH + 2 * A_KV_WIDTH]

    lane = lax.broadcasted_iota(jnp.int32, (TQ, LANES), 1)
    row = lax.broadcasted_iota(jnp.int32, (TQ, LANES), 0)
    ones_rows = (lax.broadcasted_iota(jnp.int32, (V_ROWS - A_HEAD_DIM, TQ), 0) == 0).astype(BF16)
    off = A_WIDTH + 2 * A_KV_WIDTH
    for sub in range(tm // TQ):
        c = cn[sub * TQ:(sub + 1) * TQ]
        q_heads = _norm_rows(c[:, 0:A_WIDTH].T, qg_ref[...], A_HEADS)
        ks_t = jnp.concatenate(_norm_rows(c[:, off:off + A_KV_WIDTH].T, ksg_ref[...], A_KV_GROUPS), axis=0)
        kw_t = jnp.concatenate(_norm_rows(c[:, off + 2 * A_KV_WIDTH:off + 3 * A_KV_WIDTH].T, kwg_ref[...],
                                          A_KV_GROUPS), axis=0)
        ksn = ks_t.T
        kwn = kw_t.T
        vs_t = c[:, off + A_KV_WIDTH:off + 2 * A_KV_WIDTH].T.astype(BF16)
        vw_t = c[:, off + 3 * A_KV_WIDTH:off + 4 * A_KV_WIDTH].T.astype(BF16)
        gates_t = _sigmoid(c[:, off + 4 * A_KV_WIDTH:off + 5 * A_KV_WIDTH]).T
        blk = (pl.program_id(1) * tm + sub * TQ + row) // SLC_BLOCK
        onehot = jnp.where(lane - A_HEAD_DIM == blk, 1.0, 0.0)
        for g in range(A_KV_GROUPS):
            q_ref[0, g, sub] = jnp.concatenate(q_heads[g * A_HPG:(g + 1) * A_HPG], axis=1).astype(BF16)
            sl = slice(g * A_HEAD_DIM, (g + 1) * A_HEAD_DIM)
            k_g = ksn if g == 0 else pltpu.roll(ksn, A_HEAD_DIM, axis=1)
            ks_ref[0, g, sub * TQ:(sub + 1) * TQ, :] = jnp.where(lane < A_HEAD_DIM, k_g, onehot).astype(BF16)
            kw_g = kwn if g == 0 else pltpu.roll(kwn, A_HEAD_DIM, axis=1)
            kw_ref[0, g, sub * TQ:(sub + 1) * TQ, :] = jnp.where(lane < A_HEAD_DIM, kw_g, 0.0).astype(BF16)
            vs_ref[0, g, sub] = jnp.concatenate([vs_t[sl, :], ones_rows], axis=0)
            vw_ref[0, g, sub] = jnp.concatenate([vw_t[sl, :], ones_rows], axis=0)
            gt_ref[0, g, sub] = gates_t[g * A_HEAD_DIM:g * A_HEAD_DIM + GATE_ROWS, :]
    of_ref[0] = jnp.dot(h, wf_ref[...], preferred_element_type=F32).astype(BF16)


def _proj(x, mod, norm_gain, w_nsa, w_fin, w_rwkv, qg, ksg, kwg, tm=512):
    bsz, s, _ = x.shape
    nt, nsub = s // TQ, tm // TQ
    assert A_HEAD_DIM + s // SLC_BLOCK <= LANES and A_KV_WIDTH == LANES
    const = lambda b, i: (0, 0)
    weight = lambda cols: pl.BlockSpec((D_MODEL, cols), const, pipeline_mode=pl.Buffered(1))
    col = lambda t: jnp.broadcast_to(t.reshape(-1, 1), (t.size, LANES))
    k_spec = lambda width: pl.BlockSpec((1, A_KV_GROUPS, tm, width), lambda b, i: (b, 0, i, 0))
    k_shape = lambda width: jax.ShapeDtypeStruct((bsz, A_KV_GROUPS, s, width), BF16)
    tile_spec = lambda r, c: pl.BlockSpec((1, A_KV_GROUPS, nsub, r, c), lambda b, i: (b, 0, i, 0, 0))
    tile_shape = lambda r, c, dt: jax.ShapeDtypeStruct((bsz, A_KV_GROUPS, nt, r, c), dt)
    return pl.pallas_call(
        _proj_kernel,
        grid=(bsz, s // tm),
        in_specs=[pl.BlockSpec((1, tm, D_MODEL), lambda b, i: (b, i, 0)),
                  pl.BlockSpec((1, 1, 3 * D_MODEL), lambda b, i: (b, 0, 0)),
                  pl.BlockSpec((1, D_MODEL), const),
                  weight(NSA_COLS), weight(FIN_COLS), weight(RWKV_COLS),
                  pl.BlockSpec((A_WIDTH, LANES), const),
                  pl.BlockSpec((A_KV_WIDTH, LANES), const),
                  pl.BlockSpec((A_KV_WIDTH, LANES), const)],
        out_specs=[tile_spec(A_HEAD_DIM, A_HPG * TQ),
                   k_spec(LANES), tile_spec(V_ROWS, TQ), k_spec(LANES), tile_spec(V_ROWS, TQ),
                   tile_spec(GATE_ROWS, TQ),
                   pl.BlockSpec((1, tm, 2 * A_KV_WIDTH), lambda b, i: (b, i, 0)),
                   pl.BlockSpec((1, tm, FIN_COLS), lambda b, i: (b, i, 0)),
                   pl.BlockSpec((1, tm, RWKV_COLS), lambda b, i: (b, i, 0))],
        out_shape=[tile_shape(A_HEAD_DIM, A_HPG * TQ, BF16),
                   k_shape(LANES), tile_shape(V_ROWS, TQ, BF16), k_shape(LANES), tile_shape(V_ROWS, TQ, BF16),
                   tile_shape(GATE_ROWS, TQ, F32),
                   jax.ShapeDtypeStruct((bsz, s, 2 * A_KV_WIDTH), F32),
                   jax.ShapeDtypeStruct((bsz, s, FIN_COLS), BF16),
                   jax.ShapeDtypeStruct((bsz, s, RWKV_COLS), F32)],
        compiler_params=pltpu.CompilerParams(dimension_semantics=("parallel", "parallel"),
                                             vmem_limit_bytes=VMEM_LIMIT),
        name="proj",
    )(x, mod.reshape(bsz, 1, 3 * D_MODEL), norm_gain.reshape(1, D_MODEL), w_nsa, w_fin, w_rwkv,
      col(qg), col(ksg), col(kwg))


def _compress_kernel(ck_ref, cv_ref, pk_ref, pv_ref, w1k_ref, w2k_ref, w1v_ref, w2v_ref, kg_ref, kc_ref, vc_ref):
    n16 = ck_ref.shape[1] // CMP_STRIDE

    def rows16(ref):
        return jnp.concatenate([ref[0, pl.ds(p, n16, stride=CMP_STRIDE), :] for p in range(CMP_STRIDE)], axis=1)

    def hidden(z, pos_ref, w1_ref, g):
        top = _dot(z + pos_ref[0:1, :], w1_ref[g, 0])
        bot = _dot(z + pos_ref[1:2, :], w1_ref[g, 1])
        return jax.nn.gelu(top + pltpu.roll(bot, n16 - 1, axis=0), approximate=True)

    zk = rows16(ck_ref)
    zv = rows16(cv_ref)
    for g in range(A_KV_GROUPS):
        kc = _dot(hidden(zk, pk_ref, w1k_ref, g), w2k_ref[...])
        ms = jnp.mean(kc * kc, axis=-1, keepdims=True)
        kc_ref[0, g] = (kc * lax.rsqrt(ms + NORM_EPS) * kg_ref[...]).astype(BF16)
        vc_ref[0, g] = _dot_nt(w2v_ref[...], hidden(zv, pv_ref, w1v_ref, g)).astype(BF16)


def _expand_cmp_w1(w1):
    w = w1.reshape(2, CMP_STRIDE, 1, A_HEAD_DIM, CMP_HIDDEN)
    per_group = []
    for g in range(A_KV_GROUPS):
        pad = [(0, 0), (0, 0), (g, A_KV_GROUPS - 1 - g), (0, 0), (0, 0)]
        per_group.append(jnp.pad(w, pad).reshape(2, CMP_STRIDE * A_KV_WIDTH, CMP_HIDDEN))
    return jnp.stack(per_group).astype(BF16)


def _expand_cmp_pos(pos):
    p = jnp.broadcast_to(pos.reshape(2, CMP_STRIDE, 1, A_HEAD_DIM), (2, CMP_STRIDE, A_KV_GROUPS, A_HEAD_DIM))
    return p.reshape(2, CMP_STRIDE * A_KV_WIDTH)


def _compress(ck, pk, pv, w1k, w2k, w1v, w2v_t, kg):
    bsz, s, _ = ck.shape
    n16 = s // CMP_STRIDE
    zw = CMP_STRIDE * A_KV_WIDTH
    const = lambda b: (0, 0)
    const4 = lambda b: (0, 0, 0, 0)
    return pl.pallas_call(
        _compress_kernel,
        grid=(bsz,),
        in_specs=[pl.BlockSpec((1, s, A_KV_WIDTH), lambda b: (b, 0, 0)),
                  pl.BlockSpec((1, s, A_KV_WIDTH), lambda b: (b, 0, 1)),
                  pl.BlockSpec((2, zw), const), pl.BlockSpec((2, zw), const),
                  pl.BlockSpec((A_KV_GROUPS, 2, zw, CMP_HIDDEN), const4), pl.BlockSpec((CMP_HIDDEN, A_HEAD_DIM), const),
                  pl.BlockSpec((A_KV_GROUPS, 2, zw, CMP_HIDDEN), const4), pl.BlockSpec((A_HEAD_DIM, CMP_HIDDEN), const),
                  pl.BlockSpec((1, A_HEAD_DIM), const)],
        out_specs=[pl.BlockSpec((1, A_KV_GROUPS, n16, A_HEAD_DIM), lambda b: (b, 0, 0, 0)),
                   pl.BlockSpec((1, A_KV_GROUPS, A_HEAD_DIM, n16), lambda b: (b, 0, 0, 0))],
        out_shape=[jax.ShapeDtypeStruct((bsz, A_KV_GROUPS, n16, A_HEAD_DIM), BF16),
                   jax.ShapeDtypeStruct((bsz, A_KV_GROUPS, A_HEAD_DIM, n16), BF16)],
        compiler_params=pltpu.CompilerParams(dimension_semantics=("parallel",)),
        name="compress",
    )(ck, ck, pk, pv, w1k, w2k, w1v, w2v_t, kg)


TILE_FAR, TILE_EDGE, TILE_MASKED, N_BIAS_TILES = 2, 3, 4, 5
SUB = 4
ATT_TILES = 4


def _bias_cmp_kernel(tbl_ref, o_ref):
    i = pl.program_id(0)
    g = pl.program_id(1)
    n_cmp = o_ref.shape[2]
    n = lax.broadcasted_iota(jnp.int32, (n_cmp, TQ), 0)
    q = lax.broadcasted_iota(jnp.int32, (n_cmp, TQ), 1)
    dist = i * TQ + q - (n * CMP_STRIDE + CMP_BLOCK - 1)
    for h in range(A_HPG):
        bias = _bias_from_dist(dist, tbl_ref, g * A_HPG + h)
        o_ref[0, 0, :, h * TQ:(h + 1) * TQ] = jnp.where(dist >= 0, bias * LOG2E, NEG_INF)


def _bias_toeplitz_kernel(tbl_ref, o_ref):
    g = pl.program_id(0)
    r = pl.program_id(1)
    off = jnp.where(r == TILE_EDGE, WINDOW // TQ, jnp.where(r == TILE_MASKED, -2, r))
    k = lax.broadcasted_iota(jnp.int32, (TQ, TQ), 0)
    q = lax.broadcasted_iota(jnp.int32, (TQ, TQ), 1)
    dist = off * TQ + q - k
    valid = (dist >= 0) & (dist < WINDOW)
    for h in range(A_HPG):
        head = g * A_HPG + h
        bias = _bias_from_dist(dist, tbl_ref, head) - tbl_ref[REL_BUCKETS - 1, head]
        o_ref[0, 0, :, h * TQ:(h + 1) * TQ] = jnp.where(valid, bias * LOG2E, NEG_INF)


def _bias_tables(rel_bias, s, n_cmp):
    smem = pl.BlockSpec(memory_space=pltpu.SMEM)
    nt = s // TQ
    bias_c = pl.pallas_call(
        _bias_cmp_kernel,
        grid=(nt, A_KV_GROUPS),
        in_specs=[smem],
        out_specs=pl.BlockSpec((1, 1, n_cmp, A_HPG * TQ), lambda i, g: (i, g, 0, 0)),
        out_shape=jax.ShapeDtypeStruct((nt, A_KV_GROUPS, n_cmp, A_HPG * TQ), F32),
        name="bias_cmp",
    )(rel_bias)
    assert _BUCKET_TH[REL_BUCKETS - 1] <= TQ + 1 and WINDOW // TQ >= 3
    bias_d = pl.pallas_call(
        _bias_toeplitz_kernel,
        grid=(A_KV_GROUPS, N_BIAS_TILES),
        in_specs=[smem],
        out_specs=pl.BlockSpec((1, 1, TQ, A_HPG * TQ), lambda g, r: (g, r, 0, 0)),
        out_shape=jax.ShapeDtypeStruct((A_KV_GROUPS, N_BIAS_TILES, TQ, A_HPG * TQ), F32),
        name="bias_toeplitz",
    )(rel_bias)
    return bias_c, bias_d


def _attn_kernel(*refs):
    q_refs, refs = refs[:ATT_TILES], refs[ATT_TILES:]
    kc_ref, vc_ref, ks_ref, vs_ref, kw_ref, vw_ref = refs[:6]
    bc_refs, bd_ref, gt_refs, o_ref = refs[6:6 + ATT_TILES], refs[6 + ATT_TILES], refs[7 + ATT_TILES:-1], refs[-1]
    j = pl.program_id(2)
    tq = TQ
    n_cmp = kc_ref.shape[2]
    n_slc = ks_ref.shape[2] // SLC_BLOCK
    wt = WINDOW // tq
    dh = A_HEAD_DIM
    tiles = [dict(i=j + t * SUB, last=t, q=q_refs[t][0, 0, 0], bias_c=bc_refs[t], gates=gt_refs[t][0, 0, 0])
             for t in range(ATT_TILES)]
    zero_rows = jnp.zeros((LANES - dh, A_HPG * tq), BF16)

    def scores(k_slab, q_mat, i, first_tile, n_sub, tile_index, far=()):
        s = jnp.dot(k_slab, q_mat, preferred_element_type=F32)
        parts = [s[t * tq:(t + 1) * tq] if t in far
                 else s[t * tq:(t + 1) * tq] + bd_ref[0, tile_index(i - (first_tile + t))] for t in range(n_sub)]
        return jnp.concatenate(parts, axis=0)

    def values_t(v_ref, first_tile, n_sub):
        return jnp.concatenate([v_ref[0, 0, first_tile + t] for t in range(n_sub)], axis=1)

    win_tile = lambda r: jnp.where(r < 0, TILE_MASKED, jnp.where(r == wt, TILE_EDGE, jnp.minimum(r, TILE_FAR)))
    for t in tiles:
        t["first_w"] = jnp.maximum(t["i"] - wt, 0)
        k0 = pl.multiple_of(t["first_w"] * tq, tq)
        q_pad = jnp.concatenate([t["q"], zero_rows], axis=0)
        far = range(1, wt - 1) if t["last"] * SUB >= wt else ()
        t["s_w"] = scores(kw_ref[0, 0, pl.ds(k0, (wt + 1) * tq), :], q_pad, t["i"], t["first_w"], wt + 1, win_tile,
                          far)
    for t in tiles:
        bias = t["bias_c"][0, 0]
        t["valid_c"] = bias > 0.5 * NEG_INF
        t["s_c"] = jnp.dot(kc_ref[0, 0], t["q"], preferred_element_type=F32) + bias

    r1, r2 = SLC_BLOCK // CMP_STRIDE, CMP_BLOCK // CMP_STRIDE
    jj = lax.broadcasted_iota(jnp.int32, (n_slc, n_cmp), 0)
    nn = lax.broadcasted_iota(jnp.int32, (n_slc, n_cmp), 1)
    d = nn - r1 * jj
    cnt = jnp.zeros((n_slc, n_cmp), F32)
    for a in range(r1):
        for c in range(r2):
            cnt = cnt + jnp.where(d == a - c, 1.0, 0.0)
    cnt = cnt.astype(BF16)
    for t in tiles:
        s = t["s_c"]
        e = jnp.where(t["valid_c"], jnp.exp2(s - jnp.max(s, axis=0, keepdims=True)), 0.0)
        l = jnp.sum(e, axis=0, keepdims=True)
        p = e * (1.0 / jnp.where(l > 0.0, l, 1.0))
        t["out_c"] = jnp.dot(vc_ref[0, 0], p.astype(BF16), preferred_element_type=F32)
        p_grp = sum(p[:, h * tq:(h + 1) * tq] for h in range(A_HPG))
        t["imp"] = sum(jnp.dot(cnt, part, preferred_element_type=F32) for part in _split3(p_grp))

    for t in tiles:
        s = t["s_w"]
        t["p_w"] = jnp.exp2(s - jnp.max(s, axis=0, keepdims=True)).astype(BF16)
    for t in tiles:
        acc = jnp.dot(values_t(vw_ref, t["first_w"], wt + 1), t["p_w"], preferred_element_type=F32)
        out_w = acc[:dh] * (1.0 / acc[dh:dh + 1])
        gates = t["gates"]
        t["part"] = [gates[h:h + 1, :] * t["out_c"][:, h * tq:(h + 1) * tq]
                     + gates[2 * A_HPG + h:2 * A_HPG + h + 1, :] * out_w[:, h * tq:(h + 1) * tq]
                     for h in range(A_HPG)]

    blk = lax.broadcasted_iota(jnp.int32, (n_slc, tq), 0)
    for t in tiles:
        tpos = t["i"] * tq + lax.broadcasted_iota(jnp.int32, (n_slc, tq), 1)
        cur = tpos // SLC_BLOCK
        forced = (blk == 0) | (blk == cur) | (blk == cur - 1)
        causal = blk * SLC_BLOCK <= tpos
        imp = jnp.where(forced, FORCE_SCORE, jnp.where(causal, t["imp"], NEG_INF))
        rank = jnp.zeros((n_slc, tq), F32)
        for c in range(n_slc):
            row = imp[c:c + 1, :]
            ahead = (row > imp) | ((row == imp) & (blk > c))
            rank = rank + jnp.where(ahead, 1.0, 0.0)
        pen = jnp.where(rank < float(min(SLC_TOPN, n_slc)), 0.0, -FORCE_SCORE).astype(BF16)
        t["q_aug"] = jnp.concatenate([t["q"], jnp.concatenate([pen] * A_HPG, axis=1),
                                      zero_rows[:LANES - dh - n_slc]], axis=0)

    sel_tile = lambda r: jnp.where(r < 0, TILE_MASKED, jnp.minimum(r, TILE_FAR))

    jobs = [(t, c) for t in tiles for c in range(t["last"], -1, -1)]
    ss = [scores(ks_ref[0, 0, c * SUB * tq:(c + 1) * SUB * tq, :], t["q_aug"], t["i"], c * SUB, SUB, sel_tile,
                 [n for n in range(SUB) if SUB * (t["last"] - c) - n >= TILE_FAR])
          for t, c in jobs]
    ms = [jnp.max(s, axis=0, keepdims=True) for s in ss]
    ps = [jnp.exp2(s - m).astype(BF16) for s, m in zip(ss, ms)]
    accs = [jnp.dot(values_t(vs_ref, c * SUB, SUB), p, preferred_element_type=F32) for (t, c), p in zip(jobs, ps)]

    for k, t in enumerate(tiles):
        mine = [n for n, (tt, c) in enumerate(jobs) if tt is t]
        m = functools.reduce(jnp.maximum, [ms[n] for n in mine])
        acc = sum(jnp.exp2(ms[n] - m) * accs[n] for n in mine)
        out_s = acc[:dh] * (1.0 / acc[dh:dh + 1])
        gates = t["gates"]
        blocks = [t["part"][h] + gates[A_HPG + h:A_HPG + h + 1, :] * out_s[:, h * tq:(h + 1) * tq]
                  for h in range(A_HPG)]
        o_ref[0, k] = jnp.concatenate(blocks, axis=0).T.astype(BF16)


def _attention(q_t, kc, vc_t, ks, vs_t, kw, vw_t, bias_c, bias_d, gates_t):
    bsz, _, nt, _, _ = q_t.shape
    s = ks.shape[2]
    n_cmp = kc.shape[2]
    assert nt == ATT_TILES * SUB and WINDOW // TQ + 1 <= nt
    k_spec = pl.BlockSpec((1, 1, s, LANES), lambda b, g, j: (b, g, 0, 0))
    vt_spec = pl.BlockSpec((1, 1, nt, V_ROWS, TQ), lambda b, g, j: (b, g, 0, 0, 0))
    per_tile = lambda spec: [spec(t * SUB) for t in range(ATT_TILES)]
    q_spec = lambda off: pl.BlockSpec((1, 1, 1, A_HEAD_DIM, A_HPG * TQ), lambda b, g, j: (b, g, j + off, 0, 0))
    bc_spec = lambda off: pl.BlockSpec((1, 1, n_cmp, A_HPG * TQ), lambda b, g, j: (j + off, g, 0, 0))
    gt_spec = lambda off: pl.BlockSpec((1, 1, 1, GATE_ROWS, TQ), lambda b, g, j: (b, g, j + off, 0, 0))
    out = pl.pallas_call(
        _attn_kernel,
        grid=(bsz, A_KV_GROUPS, SUB),
        in_specs=(per_tile(q_spec)
                  + [pl.BlockSpec((1, 1, n_cmp, A_HEAD_DIM), lambda b, g, j: (b, g, 0, 0)),
                     pl.BlockSpec((1, 1, A_HEAD_DIM, n_cmp), lambda b, g, j: (b, g, 0, 0)),
                     k_spec, vt_spec, k_spec, vt_spec]
                  + per_tile(bc_spec)
                  + [pl.BlockSpec((1, N_BIAS_TILES, TQ, A_HPG * TQ), lambda b, g, j: (g, 0, 0, 0))]
                  + per_tile(gt_spec)),
        out_specs=pl.BlockSpec((1, ATT_TILES, TQ, A_HPG * A_HEAD_DIM), lambda b, g, j: (b, 0, j, g)),
        out_shape=jax.ShapeDtypeStruct((bsz, ATT_TILES, s // ATT_TILES, A_WIDTH), BF16),
        compiler_params=pltpu.CompilerParams(dimension_semantics=("parallel", "parallel", "parallel"),
                                             vmem_limit_bytes=VMEM_LIMIT),
        name="attn",
    )(*([q_t] * ATT_TILES), kc, vc_t, ks, vs_t, kw, vw_t, *([bias_c] * ATT_TILES), bias_d,
      *([gates_t] * ATT_TILES))
    return out.reshape(bsz, s, A_WIDTH)


def _rwkv_kernel(c_ref, mu_ref, w0_ref, wl_ref, a0_ref, al_ref, kk_ref, ka_ref, rk_ref, lw_ref, lb_ref,
                 o_ref, state_ref, prev_ref):
    cc = pl.program_id(1)
    n = B_HEAD_DIM
    nb, csz = c_ref.shape[0], c_ref.shape[1]

    @pl.when(cc == 0)
    def _():
        state_ref[...] = jnp.zeros(state_ref.shape, F32)
        prev_ref[...] = jnp.zeros(prev_ref.shape, F32)

    ti = lax.broadcasted_iota(jnp.int32, (csz, LANES), 0)
    si = lax.broadcasted_iota(jnp.int32, (csz, LANES), 1) % n
    lower = si <= ti
    strict = si < ti
    eye = jnp.where(si == ti, 1.0, 0.0)
    tri = jnp.where(lax.broadcasted_iota(jnp.int32, (csz, csz), 1) <= lax.broadcasted_iota(jnp.int32, (csz, csz), 0),
                    1.0, 0.0).astype(BF16)
    n_pairs = B_WIDTH // LANES
    left =lax.broadcasted_iota(jnp.int32, (csz, LANES), 1) < n
    row_left = lax.broadcasted_iota(jnp.int32, (LANES, LANES), 0) < n
    same_head = row_left == (lax.broadcasted_iota(jnp.int32, (LANES, LANES), 1) < n)

    def blockdiag(y):
        zero = jnp.zeros_like(y)
        return jnp.concatenate([jnp.where(left, y, zero), jnp.where(left, zero, y)], axis=0)

    def head_sum(x):
        lo = jnp.sum(jnp.where(left, x, 0.0), axis=-1, keepdims=True)
        hi = jnp.sum(jnp.where(left, 0.0, x), axis=-1, keepdims=True)
        return jnp.where(left, lo, hi)

    chains = []
    for bi in range(nb):
        p = c_ref[bi]
        rolled = pltpu.roll(p, 1, axis=0)
        row = lax.broadcasted_iota(jnp.int32, (8, p.shape[1]), 0)
        prev = jnp.concatenate([jnp.where(row == 0, prev_ref[bi, 0:1, :], rolled[0:8]), rolled[8:]], axis=0)
        prev_ref[bi, 0:1, :] = p[csz - 1:csz, :]
        x = p + (prev - p) * mu_ref[...]
        r = x[:, 0:B_WIDTH]
        k = x[:, B_WIDTH:2 * B_WIDTH]
        v = x[:, 2 * B_WIDTH:3 * B_WIDTH]
        wd = x[:, 3 * B_WIDTH:3 * B_WIDTH + DECAY_LORA]
        ad = x[:, 3 * B_WIDTH + DECAY_LORA:3 * B_WIDTH + DECAY_LORA + ICLR_LORA]

        ld = -math.exp(-0.5) * _sigmoid(w0_ref[...] + _dot(jnp.tanh(wd), wl_ref[...]))
        a = _sigmoid(a0_ref[...] + _dot(ad, al_ref[...]))
        kk = k * kk_ref[...]
        k_mod = k * (1.0 + (a - 1.0) * ka_ref[...])
        rkr = r * k_mod * rk_ref[...]

        ld_hi, ld_lo = _split2(ld)
        cum = jnp.dot(tri, ld_hi, preferred_element_type=F32) + jnp.dot(tri, ld_lo, preferred_element_type=F32)
        g_inc = jnp.exp(cum)
        g_exc = jnp.exp(cum - ld)
        g_inv = jnp.exp(-cum)
        g_end = jnp.exp(cum[csz - 1:csz, :] - cum)
        g_all = g_inc[csz - 1:csz, :]

        for pr in range(n_pairs):
            sl = slice(pr * LANES, (pr + 1) * LANES)
            kk_p = kk[:, sl]
            kk_p = kk_p * lax.rsqrt(jnp.maximum(head_sum(kk_p * kk_p), 1e-24))
            b_p = kk_p * a[:, sl]
            bt = (b_p * g_inv[:, sl]).astype(BF16)
            kt = (k_mod[:, sl] * g_inv[:, sl]).astype(BF16)
            ch = dict(
                idx=bi * n_pairs + pr,
                v=v[:, sl],
                lhs=jnp.concatenate([-kk_p * g_exc[:, sl], r[:, sl] * g_inc[:, sl]], axis=0).astype(BF16),
                rhs=jnp.concatenate([blockdiag(bt), blockdiag(kt)], axis=0),
                bk=jnp.concatenate([b_p * g_end[:, sl], k_mod[:, sl] * g_end[:, sl]], axis=0).astype(BF16),
                g_all=g_all[:, sl],
                bonus=head_sum(rkr[:, sl]) * v[:, sl],
            )
            chains.append(ch)

    for ch in chains:
        x = _dot_nt(ch["lhs"], ch["rhs"])
        xb, xk = x[:, :LANES], x[:, LANES:]
        ch["a_ab"] = jnp.where(strict, xb[:csz], 0.0)
        a_ak = jnp.where(strict, xk[:csz], 0.0)
        m_rk = jnp.where(lower, xk[csz:], 0.0)
        ch["ak_rk"] = jnp.concatenate([a_ak, m_rk], axis=0).astype(BF16)
        ch["m_rb"] = jnp.where(lower, xb[csz:], 0.0).astype(BF16)
    for ch in chains:
        akv = _dot(ch["ak_rk"], blockdiag(ch["v"].astype(BF16)))
        ch["akv"], ch["mrkv"] = akv[:csz], akv[csz:]
        ch["tinv"] = eye + ch["a_ab"]
        ch["pw"] = ch["a_ab"].astype(BF16)
    n_sq = int(math.log2(csz)) - 1
    for ch in chains:
        ch["pw"] = _dot(ch["pw"], blockdiag(ch["pw"])).astype(BF16)
    for step in range(n_sq):
        for ch in chains:
            if step + 1 < n_sq:
                both = _dot(jnp.concatenate([ch["pw"], ch["tinv"].astype(BF16)], axis=0), blockdiag(ch["pw"]))
                ch["tinv"] = ch["tinv"] + both[csz:]
                ch["pw"] = both[:csz].astype(BF16)
            else:
                ch["tinv"] = ch["tinv"] + _dot(ch["tinv"], blockdiag(ch["pw"]))
    for ch in chains:
        ch["s0"] = state_ref[ch["idx"]]
        ch["as0"] = _dot_nt(ch["lhs"], ch["s0"])
    for ch in chains:
        w = (ch["as0"][:csz] + ch["akv"]).astype(BF16)
        ch["u"] = _dot(ch["tinv"], blockdiag(w))
    outs = []
    for ch in chains:
        u = ch["u"]
        y = ch["as0"][csz:] + _dot(ch["m_rb"], blockdiag(u.astype(BF16))) + ch["mrkv"]
        uv = jnp.concatenate([u, ch["v"]], axis=0)
        state_ref[ch["idx"]] = ch["s0"] * ch["g_all"] + jnp.where(same_head, _dot_tn(uv, ch["bk"]), 0.0)
        yc = y - head_sum(y) * (1.0 / n)
        var = head_sum(yc * yc) * (1.0 / n)
        outs.append(yc * lax.rsqrt(var + LNX_EPS))
    for bi in range(nb):
        yn = jnp.concatenate(outs[bi * n_pairs:(bi + 1) * n_pairs], axis=-1)
        bonus = jnp.concatenate([ch["bonus"] for ch in chains[bi * n_pairs:(bi + 1) * n_pairs]], axis=-1)
        o_ref[bi] = (yn * lw_ref[...] + lb_ref[...] + bonus).astype(BF16)


RWKV_NB = 8


def _rwkv(cols_rwkv, mu, w0, wl, a0, al, k_k, k_a, r_k, ln_w, ln_b):
    bsz, s, _ = cols_rwkv.shape
    nb = RWKV_NB if bsz % RWKV_NB == 0 else 1
    const = lambda b, c: (0, 0)
    vec = pl.BlockSpec((1, B_WIDTH), const)
    return pl.pallas_call(
        _rwkv_kernel,
        grid=(bsz // nb, s // CHUNK),
        in_specs=[pl.BlockSpec((nb, CHUNK, RWKV_COLS), lambda b, c: (b, c, 0)),
                  pl.BlockSpec((1, RWKV_COLS), const),
                  vec, pl.BlockSpec((DECAY_LORA, B_WIDTH), const),
                  vec, pl.BlockSpec((ICLR_LORA, B_WIDTH), const),
                  vec, vec, vec, vec, vec],
        out_specs=pl.BlockSpec((nb, CHUNK, B_WIDTH), lambda b, c: (b, c, 0)),
        out_shape=jax.ShapeDtypeStruct((bsz, s, B_WIDTH), BF16),
        scratch_shapes=[pltpu.VMEM((nb * B_WIDTH // LANES, LANES, LANES), F32),
                        pltpu.VMEM((nb, 8, RWKV_COLS), F32)],
        compiler_params=pltpu.CompilerParams(dimension_semantics=("parallel", "arbitrary")),
        name="rwkv",
    )(cols_rwkv, mu, w0, wl, a0, al, k_k, k_a, r_k, ln_w, ln_b)


def _final_kernel(x_ref, ya_ref, yb_ref, cf_ref, gate_ref, wa_ref, wb_ref, wo_ref, o_ref):
    a_silu = cf_ref[0, :, 0:A_WIDTH].astype(F32)
    b_silu = cf_ref[0, :, A_WIDTH:A_WIDTH + B_WIDTH].astype(F32)
    merge_a = cf_ref[0, :, A_WIDTH + B_WIDTH:A_WIDTH + B_WIDTH + D_MODEL].astype(F32)
    merge_b = cf_ref[0, :, A_WIDTH + B_WIDTH + D_MODEL:A_WIDTH + B_WIDTH + 2 * D_MODEL].astype(F32)
    ya = ya_ref[0].astype(F32) * (a_silu * _sigmoid(a_silu))
    yb = yb_ref[0].astype(F32) * (b_silu * _sigmoid(b_silu))
    merged = _sigmoid(merge_a) * _dot(ya, wa_ref[...]) + _sigmoid(merge_b) * _dot(yb, wb_ref[...])
    o_ref[0] = x_ref[0] + gate_ref[0] * _dot(merged, wo_ref[...])


def _final(x, y_a, y_b, cols_fin, gate, w_out_a, w_out_b, w_o, tm=512):
    bsz, s, _ = x.shape
    const = lambda b, i: (0, 0)
    row = lambda w: pl.BlockSpec((1, tm, w), lambda b, i: (b, i, 0))
    return pl.pallas_call(
        _final_kernel,
        grid=(bsz, s // tm),
        in_specs=[row(D_MODEL), row(A_WIDTH), row(B_WIDTH), row(FIN_COLS),
                  pl.BlockSpec((1, 1, D_MODEL), lambda b, i: (b, 0, 0)),
                  pl.BlockSpec((A_WIDTH, D_MODEL), const),
                  pl.BlockSpec((B_WIDTH, D_MODEL), const),
                  pl.BlockSpec((D_MODEL, D_MODEL), const)],
        out_specs=row(D_MODEL),
        out_shape=jax.ShapeDtypeStruct((bsz, s, D_MODEL), F32),
        compiler_params=pltpu.CompilerParams(dimension_semantics=("parallel", "parallel"),
                                             vmem_limit_bytes=VMEM_LIMIT),
        name="final",
    )(x, y_a, y_b, cols_fin, gate, w_out_a, w_out_b, w_o)


def _split_w_in(w_in):
    nsa_in = 2 * A_WIDTH + 6 * A_KV_WIDTH + 3 * A_HEADS
    o_gate = A_WIDTH + 6 * A_KV_WIDTH
    o_asilu = o_gate + 3 * A_HEADS
    o_shift = nsa_in
    o_rest = nsa_in + RWKV_COLS
    gate_w = w_in[:, o_gate:o_asilu].reshape(D_MODEL, 3, A_KV_GROUPS, A_HPG)
    gate_w = gate_w.transpose(0, 2, 1, 3).reshape(D_MODEL, A_KV_GROUPS, 3 * A_HPG)
    gate_w = jnp.pad(gate_w, ((0, 0), (0, 0), (0, A_HEAD_DIM - 3 * A_HPG))).reshape(D_MODEL, GATE_PAD)
    w_nsa = jnp.concatenate([w_in[:, :o_gate], gate_w], axis=1)
    w_fin = jnp.concatenate([w_in[:, o_asilu:o_shift], w_in[:, o_rest:]], axis=1)
    w_rwkv = w_in[:, o_shift:o_rest]
    return w_nsa.astype(BF16), w_fin.astype(BF16), w_rwkv.astype(BF16)


def _layer(x, c, rel_bias, w_ada, b_ada, norm_gain, w_in, q_norm_gain, k_norm_gain,
           cmp_pos_k, cmp_pos_v, cmp_k_w1, cmp_k_w2, cmp_v_w1, cmp_v_w2,
           shift_mu, w0, w_lora_up, a0, a_lora_up, k_k, k_a, r_k, ln_x_w, ln_x_b,
           w_out_a, w_out_b, w_o):
    bsz, s, _ = x.shape
    assert s % (2 * TQ) == 0 and s // CMP_STRIDE == LANES
    n16 = s // CMP_STRIDE
    mod = _ada(c, w_ada, b_ada)
    w_nsa, w_fin, w_rwkv = _split_w_in(w_in)
    scale = A_HEAD_DIM ** -0.5 * LOG2E
    qg = jnp.tile(q_norm_gain, A_HEADS) * scale
    ksg = jnp.tile(k_norm_gain[1], A_KV_GROUPS)
    kwg = jnp.tile(k_norm_gain[2], A_KV_GROUPS)
    q_t, ks, vs_t, kw, vw_t, gates_t, ck, cols_fin, cols_rwkv = _proj(
        x, mod, norm_gain, w_nsa, w_fin, w_rwkv, qg, ksg, kwg)

    kc, vc_t = _compress(ck, _expand_cmp_pos(cmp_pos_k), _expand_cmp_pos(cmp_pos_v),
                         _expand_cmp_w1(cmp_k_w1), cmp_k_w2.astype(BF16),
                         _expand_cmp_w1(cmp_v_w1), cmp_v_w2.T.astype(BF16),
                         k_norm_gain[0].reshape(1, A_HEAD_DIM))
    bias_c, bias_d = _bias_tables(rel_bias, s, n16)
    y_a = _attention(q_t, kc, vc_t, ks, vs_t, kw, vw_t, bias_c, bias_d, gates_t)

    vec = lambda t: t.reshape(1, -1)
    y_b = _rwkv(cols_rwkv, vec(shift_mu), vec(w0), w_lora_up.astype(BF16), vec(a0), a_lora_up.astype(BF16),
                vec(k_k), vec(k_a), vec(r_k), vec(ln_x_w), vec(ln_x_b))

    gate = mod[:, 2 * D_MODEL:].reshape(bsz, 1, D_MODEL)
    return _final(x, y_a, y_b, cols_fin, gate, w_out_a.astype(BF16), w_out_b.astype(BF16), w_o.astype(BF16))


def kernel(x, c, w_ada, b_ada, norm_gain, w_in, q_norm_gain, k_norm_gain, cmp_pos_k, cmp_pos_v, cmp_k_w1, cmp_k_w2, cmp_v_w1, cmp_v_w2, rel_bias, shift_mu, w0, w_lora_up, a0, a_lora_up, k_k, k_a, r_k, ln_x_w, ln_x_b, w_out_a, w_out_b, w_o):
    for l in range(w_in.shape[0]):
        x = _layer(x, c, rel_bias, w_ada[l], b_ada[l], norm_gain[l], w_in[l], q_norm_gain[l], k_norm_gain[l],
                   cmp_pos_k[l], cmp_pos_v[l], cmp_k_w1[l], cmp_k_w2[l], cmp_v_w1[l], cmp_v_w2[l],
                   shift_mu[l], w0[l], w_lora_up[l], a0[l], a_lora_up[l], k_k[l], k_a[l], r_k[l],
                   ln_x_w[l], ln_x_b[l], w_out_a[l], w_out_b[l], w_o[l])
    return x
```

```python
import functools
import math

import numpy as np
import jax
import jax.numpy as jnp
from jax import lax
from jax.experimental import pallas as pl
from jax.experimental.pallas import tpu as pltpu

F32 = jnp.float32
BF16 = jnp.bfloat16

D_MODEL = 1024
A_HEADS = 8
A_HEAD_DIM = 64
A_KV_GROUPS = 2
A_HPG = A_HEADS // A_KV_GROUPS
A_WIDTH = A_HEADS * A_HEAD_DIM
A_KV_WIDTH = A_KV_GROUPS * A_HEAD_DIM
CMP_BLOCK = 32
CMP_STRIDE = 16
CMP_HIDDEN = 256
SLC_BLOCK = 64
SLC_TOPN = 16
WINDOW = 512
B_HEADS = 8
B_HEAD_DIM = 64
B_WIDTH = B_HEADS * B_HEAD_DIM
DECAY_LORA = 64
ICLR_LORA = 64
LNX_EPS = 64e-5
REL_BUCKETS = 32
REL_MAX_EXACT = 16
REL_MAX_DIST = 128
NORM_EPS = 1e-6
NEG_INF = -1e30
FORCE_SCORE = 1e30

LANES = 128
TQ = 128
CHUNK = 64
GATE_PAD = LANES
LOG2E = math.log2(math.e)
V_ROWS = A_HEAD_DIM + 16
GATE_ROWS = 16
NSA_COLS = A_WIDTH + 6 * A_KV_WIDTH + GATE_PAD
FIN_COLS = A_WIDTH + B_WIDTH + 2 * D_MODEL
RWKV_COLS = 3 * B_WIDTH + DECAY_LORA + ICLR_LORA
VMEM_LIMIT = 56 * 1024 * 1024


def _dot(a, b):
    return jnp.dot(a.astype(BF16), b.astype(BF16), preferred_element_type=F32)


def _dot_nt(a, b):
    return lax.dot_general(a.astype(BF16), b.astype(BF16), (((1,), (1,)), ((), ())),
                           preferred_element_type=F32)


def _dot_tn(a, b):
    return lax.dot_general(a.astype(BF16), b.astype(BF16), (((0,), (0,)), ((), ())),
                           preferred_element_type=F32)


def _split2(x):
    hi = x.astype(BF16)
    lo = (x - hi.astype(F32)).astype(BF16)
    return hi, lo


def _split3(x):
    h1 = x.astype(BF16)
    r1 = x - h1.astype(F32)
    h2 = r1.astype(BF16)
    h3 = (r1 - h2.astype(F32)).astype(BF16)
    return h1, h2, h3


def _sigmoid(x):
    return 1.0 / (1.0 + jnp.exp(-x))


def _bucket_thresholds():
    n = np.arange(0, 4096)
    nf = np.maximum(n, REL_MAX_EXACT).astype(np.float64)
    val = np.log(nf / REL_MAX_EXACT) / math.log(REL_MAX_DIST / REL_MAX_EXACT) * (REL_BUCKETS - REL_MAX_EXACT)
    frac = np.abs(val - np.round(val))
    assert np.all((frac > 1e-4) | (n <= REL_MAX_EXACT) | (n >= REL_MAX_DIST))
    large = REL_MAX_EXACT + np.floor(val + 1e-9).astype(np.int64)
    bucket = np.where(n < REL_MAX_EXACT, n, np.minimum(large, REL_BUCKETS - 1))
    return [int(np.argmax(bucket >= j)) for j in range(REL_BUCKETS)]


_BUCKET_TH = _bucket_thresholds()


def _bias_from_dist(dist, tbl_ref, head):
    val = jnp.full(dist.shape, tbl_ref[0, head], F32)
    for j in range(1, REL_BUCKETS):
        val = jnp.where(dist >= _BUCKET_TH[j], tbl_ref[j, head], val)
    return val


def _ada_kernel(c_ref, w_ref, b_ref, o_ref):
    c = c_ref[...]
    o_ref[...] = _dot(c * _sigmoid(c), w_ref[...]) + b_ref[...]


def _ada(c, w_ada, b_ada):
    bsz = c.shape[0]
    return pl.pallas_call(
        _ada_kernel,
        grid=(3,),
        in_specs=[pl.BlockSpec((bsz, D_MODEL), lambda j: (0, 0)),
                  pl.BlockSpec((D_MODEL, D_MODEL), lambda j: (0, j)),
                  pl.BlockSpec((1, D_MODEL), lambda j: (0, j))],
        out_specs=pl.BlockSpec((bsz, D_MODEL), lambda j: (0, j)),
        out_shape=jax.ShapeDtypeStruct((bsz, 3 * D_MODEL), F32),
        name="ada",
    )(c, w_ada, b_ada.reshape(1, 3 * D_MODEL))


def _norm_rows(x_t, gain_col, n_seg):
    out = []
    for seg in range(n_seg):
        blk = x_t[seg * A_HEAD_DIM:(seg + 1) * A_HEAD_DIM, :]
        ms = jnp.mean(blk * blk, axis=0, keepdims=True)
        out.append(blk * lax.rsqrt(ms + NORM_EPS) * gain_col[seg * A_HEAD_DIM:(seg + 1) * A_HEAD_DIM, :])
    return out


def _proj_kernel(x_ref, mod_ref, g_ref, wn_ref, wf_ref, wr_ref, qg_ref, ksg_ref, kwg_ref,
                 q_ref, ks_ref, vs_ref, kw_ref, vw_ref, gt_ref, ck_ref, of_ref, or_ref):
    tm = x_ref.shape[1]
    x = x_ref[0]
    ms = jnp.mean(x * x, axis=-1, keepdims=True)
    y = x * lax.rsqrt(ms + NORM_EPS) * g_ref[...]
    mod = mod_ref[0]
    h = (y * (1.0 + mod[:, D_MODEL:2 * D_MODEL]) + mod[:, :D_MODEL]).astype(BF16)
    cn = jnp.dot(h, wn_ref[...], preferred_element_type=F32)
    or_ref[0] = jnp.dot(h, wr_ref[...], preferred_element_type=F32)
    ck_ref[0] = cn[:, A_WIDTH:A_WIDTH + 2 * A_KV_WIDTH]

    lane = lax.broadcasted_iota(jnp.int32, (TQ, LANES), 1)
    row = lax.broadcasted_iota(jnp.int32, (TQ, LANES), 0)
    ones_rows = (lax.broadcasted_iota(jnp.int32, (V_ROWS - A_HEAD_DIM, TQ), 0) == 0).astype(BF16)
    off = A_WIDTH + 2 * A_KV_WIDTH
    for sub in range(tm // TQ):
        c = cn[sub * TQ:(sub + 1) * TQ]
        q_heads = _norm_rows(c[:, 0:A_WIDTH].T, qg_ref[...], A_HEADS)
        ks_t = jnp.concatenate(_norm_rows(c[:, off:off + A_KV_WIDTH].T, ksg_ref[...], A_KV_GROUPS), axis=0)
        kw_t = jnp.concatenate(_norm_rows(c[:, off + 2 * A_KV_WIDTH:off + 3 * A_KV_WIDTH].T, kwg_ref[...],
                                          A_KV_GROUPS), axis=0)
        ksn = ks_t.T
        kwn = kw_t.T
        vs_t = c[:, off + A_KV_WIDTH:off + 2 * A_KV_WIDTH].T.astype(BF16)
        vw_t = c[:, off + 3 * A_KV_WIDTH:off + 4 * A_KV_WIDTH].T.astype(BF16)
        gates_t = _sigmoid(c[:, off + 4 * A_KV_WIDTH:off + 5 * A_KV_WIDTH]).T
        blk = (pl.program_id(1) * tm + sub * TQ + row) // SLC_BLOCK
        onehot = jnp.where(lane - A_HEAD_DIM == blk, 1.0, 0.0)
        for g in range(A_KV_GROUPS):
            q_ref[0, g, sub] = jnp.concatenate(q_heads[g * A_HPG:(g + 1) * A_HPG], axis=1).astype(BF16)
            sl = slice(g * A_HEAD_DIM, (g + 1) * A_HEAD_DIM)
            k_g = ksn if g == 0 else pltpu.roll(ksn, A_HEAD_DIM, axis=1)
            ks_ref[0, g, sub * TQ:(sub + 1) * TQ, :] = jnp.where(lane < A_HEAD_DIM, k_g, onehot).astype(BF16)
            kw_g = kwn if g == 0 else pltpu.roll(kwn, A_HEAD_DIM, axis=1)
            kw_ref[0, g, sub * TQ:(sub + 1) * TQ, :] = jnp.where(lane < A_HEAD_DIM, kw_g, 0.0).astype(BF16)
            vs_ref[0, g, sub] = jnp.concatenate([vs_t[sl, :], ones_rows], axis=0)
            vw_ref[0, g, sub] = jnp.concatenate([vw_t[sl, :], ones_rows], axis=0)
            gt_ref[0, g, sub] = gates_t[g * A_HEAD_DIM:g * A_HEAD_DIM + GATE_ROWS, :]
    of_ref[0] = jnp.dot(h, wf_ref[...], preferred_element_type=F32).astype(BF16)


def _proj(x, mod, norm_gain, w_nsa, w_fin, w_rwkv, qg, ksg, kwg, tm=512):
    bsz, s, _ = x.shape
    nt, nsub = s // TQ, tm // TQ
    assert A_HEAD_DIM + s // SLC_BLOCK <= LANES and A_KV_WIDTH == LANES
    const = lambda b, i: (0, 0)
    weight = lambda cols: pl.BlockSpec((D_MODEL, cols), const, pipeline_mode=pl.Buffered(1))
    col = lambda t: jnp.broadcast_to(t.reshape(-1, 1), (t.size, LANES))
    k_spec = lambda width: pl.BlockSpec((1, A_KV_GROUPS, tm, width), lambda b, i: (b, 0, i, 0))
    k_shape = lambda width: jax.ShapeDtypeStruct((bsz, A_KV_GROUPS, s, width), BF16)
    tile_spec = lambda r, c: pl.BlockSpec((1, A_KV_GROUPS, nsub, r, c), lambda b, i: (b, 0, i, 0, 0))
    tile_shape = lambda r, c, dt: jax.ShapeDtypeStruct((bsz, A_KV_GROUPS, nt, r, c), dt)
    return pl.pallas_call(
        _proj_kernel,
        grid=(bsz, s // tm),
        in_specs=[pl.BlockSpec((1, tm, D_MODEL), lambda b, i: (b, i, 0)),
                  pl.BlockSpec((1, 1, 3 * D_MODEL), lambda b, i: (b, 0, 0)),
                  pl.BlockSpec((1, D_MODEL), const),
                  weight(NSA_COLS), weight(FIN_COLS), weight(RWKV_COLS),
                  pl.BlockSpec((A_WIDTH, LANES), const),
                  pl.BlockSpec((A_KV_WIDTH, LANES), const),
                  pl.BlockSpec((A_KV_WIDTH, LANES), const)],
        out_specs=[tile_spec(A_HEAD_DIM, A_HPG * TQ),
                   k_spec(LANES), tile_spec(V_ROWS, TQ), k_spec(LANES), tile_spec(V_ROWS, TQ),
                   tile_spec(GATE_ROWS, TQ),
                   pl.BlockSpec((1, tm, 2 * A_KV_WIDTH), lambda b, i: (b, i, 0)),
                   pl.BlockSpec((1, tm, FIN_COLS), lambda b, i: (b, i, 0)),
                   pl.BlockSpec((1, tm, RWKV_COLS), lambda b, i: (b, i, 0))],
        out_shape=[tile_shape(A_HEAD_DIM, A_HPG * TQ, BF16),
                   k_shape(LANES), tile_shape(V_ROWS, TQ, BF16), k_shape(LANES), tile_shape(V_ROWS, TQ, BF16),
                   tile_shape(GATE_ROWS, TQ, F32),
                   jax.ShapeDtypeStruct((bsz, s, 2 * A_KV_WIDTH), F32),
                   jax.ShapeDtypeStruct((bsz, s, FIN_COLS), BF16),
                   jax.ShapeDtypeStruct((bsz, s, RWKV_COLS), F32)],
        compiler_params=pltpu.CompilerParams(dimension_semantics=("parallel", "parallel"),
                                             vmem_limit_bytes=VMEM_LIMIT),
        name="proj",
    )(x, mod.reshape(bsz, 1, 3 * D_MODEL), norm_gain.reshape(1, D_MODEL), w_nsa, w_fin, w_rwkv,
      col(qg), col(ksg), col(kwg))


def _compress_kernel(ck_ref, cv_ref, pk_ref, pv_ref, w1k_ref, w2k_ref, w1v_ref, w2v_ref, kg_ref, kc_ref, vc_ref):
    n16 = ck_ref.shape[1] // CMP_STRIDE

    def rows16(ref):
        return jnp.concatenate([ref[0, pl.ds(p, n16, stride=CMP_STRIDE), :] for p in range(CMP_STRIDE)], axis=1)

    def hidden(z, pos_ref, w1_ref, g):
        top = _dot(z + pos_ref[0:1, :], w1_ref[g, 0])
        bot = _dot(z + pos_ref[1:2, :], w1_ref[g, 1])
        return jax.nn.gelu(top + pltpu.roll(bot, n16 - 1, axis=0), approximate=True)

    zk = rows16(ck_ref)
    zv = rows16(cv_ref)
    for g in range(A_KV_GROUPS):
        kc = _dot(hidden(zk, pk_ref, w1k_ref, g), w2k_ref[...])
        ms = jnp.mean(kc * kc, axis=-1, keepdims=True)
        kc_ref[0, g] = (kc * lax.rsqrt(ms + NORM_EPS) * kg_ref[...]).astype(BF16)
        vc_ref[0, g] = _dot_nt(w2v_ref[...], hidden(zv, pv_ref, w1v_ref, g)).astype(BF16)


def _expand_cmp_w1(w1):
    w = w1.reshape(2, CMP_STRIDE, 1, A_HEAD_DIM, CMP_HIDDEN)
    per_group = []
    for g in range(A_KV_GROUPS):
        pad = [(0, 0), (0, 0), (g, A_KV_GROUPS - 1 - g), (0, 0), (0, 0)]
        per_group.append(jnp.pad(w, pad).reshape(2, CMP_STRIDE * A_KV_WIDTH, CMP_HIDDEN))
    return jnp.stack(per_group).astype(BF16)


def _expand_cmp_pos(pos):
    p = jnp.broadcast_to(pos.reshape(2, CMP_STRIDE, 1, A_HEAD_DIM), (2, CMP_STRIDE, A_KV_GROUPS, A_HEAD_DIM))
    return p.reshape(2, CMP_STRIDE * A_KV_WIDTH)


def _compress(ck, pk, pv, w1k, w2k, w1v, w2v_t, kg):
    bsz, s, _ = ck.shape
    n16 = s // CMP_STRIDE
    zw = CMP_STRIDE * A_KV_WIDTH
    const = lambda b: (0, 0)
    const4 = lambda b: (0, 0, 0, 0)
    return pl.pallas_call(
        _compress_kernel,
        grid=(bsz,),
        in_specs=[pl.BlockSpec((1, s, A_KV_WIDTH), lambda b: (b, 0, 0)),
                  pl.BlockSpec((1, s, A_KV_WIDTH), lambda b: (b, 0, 1)),
                  pl.BlockSpec((2, zw), const), pl.BlockSpec((2, zw), const),
                  pl.BlockSpec((A_KV_GROUPS, 2, zw, CMP_HIDDEN), const4), pl.BlockSpec((CMP_HIDDEN, A_HEAD_DIM), const),
                  pl.BlockSpec((A_KV_GROUPS, 2, zw, CMP_HIDDEN), const4), pl.BlockSpec((A_HEAD_DIM, CMP_HIDDEN), const),
                  pl.BlockSpec((1, A_HEAD_DIM), const)],
        out_specs=[pl.BlockSpec((1, A_KV_GROUPS, n16, A_HEAD_DIM), lambda b: (b, 0, 0, 0)),
                   pl.BlockSpec((1, A_KV_GROUPS, A_HEAD_DIM, n16), lambda b: (b, 0, 0, 0))],
        out_shape=[jax.ShapeDtypeStruct((bsz, A_KV_GROUPS, n16, A_HEAD_DIM), BF16),
                   jax.ShapeDtypeStruct((bsz, A_KV_GROUPS, A_HEAD_DIM, n16), BF16)],
        compiler_params=pltpu.CompilerParams(dimension_semantics=("parallel",)),
        name="compress",
    )(ck, ck, pk, pv, w1k, w2k, w1v, w2v_t, kg)


TILE_FAR, TILE_EDGE, TILE_MASKED, N_BIAS_TILES = 2, 3, 4, 5
SUB = 4
ATT_TILES = 4


def _bias_cmp_kernel(tbl_ref, o_ref):
    i = pl.program_id(0)
    g = pl.program_id(1)
    n_cmp = o_ref.shape[2]
    n = lax.broadcasted_iota(jnp.int32, (n_cmp, TQ), 0)
    q = lax.broadcasted_iota(jnp.int32, (n_cmp, TQ), 1)
    dist = i * TQ + q - (n * CMP_STRIDE + CMP_BLOCK - 1)
    for h in range(A_HPG):
        bias = _bias_from_dist(dist, tbl_ref, g * A_HPG + h)
        o_ref[0, 0, :, h * TQ:(h + 1) * TQ] = jnp.where(dist >= 0, bias * LOG2E, NEG_INF)


def _bias_toeplitz_kernel(tbl_ref, o_ref):
    g = pl.program_id(0)
    r = pl.program_id(1)
    off = jnp.where(r == TILE_EDGE, WINDOW // TQ, jnp.where(r == TILE_MASKED, -2, r))
    k = lax.broadcasted_iota(jnp.int32, (TQ, TQ), 0)
    q = lax.broadcasted_iota(jnp.int32, (TQ, TQ), 1)
    dist = off * TQ + q - k
    valid = (dist >= 0) & (dist < WINDOW)
    for h in range(A_HPG):
        head = g * A_HPG + h
        bias = _bias_from_dist(dist, tbl_ref, head) - tbl_ref[REL_BUCKETS - 1, head]
        o_ref[0, 0, :, h * TQ:(h + 1) * TQ] = jnp.where(valid, bias * LOG2E, NEG_INF)


def _bias_tables(rel_bias, s, n_cmp):
    smem = pl.BlockSpec(memory_space=pltpu.SMEM)
    nt = s // TQ
    bias_c = pl.pallas_call(
        _bias_cmp_kernel,
        grid=(nt, A_KV_GROUPS),
        in_specs=[smem],
        out_specs=pl.BlockSpec((1, 1, n_cmp, A_HPG * TQ), lambda i, g: (i, g, 0, 0)),
        out_shape=jax.ShapeDtypeStruct((nt, A_KV_GROUPS, n_cmp, A_HPG * TQ), F32),
        name="bias_cmp",
    )(rel_bias)
    assert _BUCKET_TH[REL_BUCKETS - 1] <= TQ + 1 and WINDOW // TQ >= 3
    bias_d = pl.pallas_call(
        _bias_toeplitz_kernel,
        grid=(A_KV_GROUPS, N_BIAS_TILES),
        in_specs=[smem],
        out_specs=pl.BlockSpec((1, 1, TQ, A_HPG * TQ), lambda g, r: (g, r, 0, 0)),
        out_shape=jax.ShapeDtypeStruct((A_KV_GROUPS, N_BIAS_TILES, TQ, A_HPG * TQ), F32),
        name="bias_toeplitz",
    )(rel_bias)
    return bias_c, bias_d


def _attn_kernel(*refs):
    q_refs, refs = refs[:ATT_TILES], refs[ATT_TILES:]
    kc_ref, vc_ref, ks_ref, vs_ref, kw_ref, vw_ref = refs[:6]
    bc_refs, bd_ref, gt_refs, o_ref = refs[6:6 + ATT_TILES], refs[6 + ATT_TILES], refs[7 + ATT_TILES:-1], refs[-1]
    j = pl.program_id(2)
    tq = TQ
    n_cmp = kc_ref.shape[2]
    n_slc = ks_ref.shape[2] // SLC_BLOCK
    wt = WINDOW // tq
    dh = A_HEAD_DIM
    tiles = [dict(i=j + t * SUB, last=t, n_tok=(t + 1) * SUB * tq, q=q_refs[t][0, 0, 0], bias_c=bc_refs[t],
                  gates=gt_refs[t][0, 0, 0]) for t in range(ATT_TILES)]
    zero_rows = jnp.zeros((LANES - dh, A_HPG * tq), BF16)

    def scores(k_slab, q_mat, i, first_tile, n_sub, tile_index, far=()):
        s = jnp.dot(k_slab, q_mat, preferred_element_type=F32)
        parts = [s[t * tq:(t + 1) * tq] if t in far
                 else s[t * tq:(t + 1) * tq] + bd_ref[0, tile_index(i - (first_tile + t))] for t in range(n_sub)]
        return jnp.concatenate(parts, axis=0)

    def values_t(v_ref, first_tile, n_sub):
        return jnp.concatenate([v_ref[0, 0, first_tile + t] for t in range(n_sub)], axis=1)

    win_tile = lambda r: jnp.where(r < 0, TILE_MASKED, jnp.where(r == wt, TILE_EDGE, jnp.minimum(r, TILE_FAR)))
    for t in tiles:
        t["first_w"] = jnp.maximum(t["i"] - wt, 0)
        k0 = pl.multiple_of(t["first_w"] * tq, tq)
        q_pad = jnp.concatenate([t["q"], zero_rows], axis=0)
        far = range(1, wt - 1) if t["last"] * SUB >= wt else ()
        t["n_w"] = min(wt + 1, t["n_tok"] // tq)
        t["s_w"] = scores(kw_ref[0, 0, pl.ds(k0, t["n_w"] * tq), :], q_pad, t["i"], t["first_w"], t["n_w"],
                          win_tile, far)
    for t in tiles:
        t["n_cmp"] = min(n_cmp, t["n_tok"] // CMP_STRIDE)
        t["n_slc"] = min(n_slc, t["n_tok"] // SLC_BLOCK)
        bias = t["bias_c"][0, 0, 0:t["n_cmp"], :]
        t["valid_c"] = bias > 0.5 * NEG_INF
        t["s_c"] = jnp.dot(kc_ref[0, 0, 0:t["n_cmp"], :], t["q"], preferred_element_type=F32) + bias

    r1, r2 = SLC_BLOCK // CMP_STRIDE, CMP_BLOCK // CMP_STRIDE
    jj = lax.broadcasted_iota(jnp.int32, (n_slc, n_cmp), 0)
    nn = lax.broadcasted_iota(jnp.int32, (n_slc, n_cmp), 1)
    d = nn - r1 * jj
    cnt = jnp.zeros((n_slc, n_cmp), F32)
    for a in range(r1):
        for c in range(r2):
            cnt = cnt + jnp.where(d == a - c, 1.0, 0.0)
    cnt = cnt.astype(BF16)
    for t in tiles:
        s = t["s_c"]
        e = jnp.where(t["valid_c"], jnp.exp2(s - jnp.max(s, axis=0, keepdims=True)), 0.0)
        l = jnp.sum(e, axis=0, keepdims=True)
        p = e * (1.0 / jnp.where(l > 0.0, l, 1.0))
        rest = n_cmp - t["n_cmp"]
        pad = (lambda a: jnp.concatenate([a, jnp.zeros((rest, a.shape[1]), a.dtype)], axis=0)) if rest else (lambda a: a)
        t["out_c"] = jnp.dot(vc_ref[0, 0], pad(p).astype(BF16), preferred_element_type=F32)
        p_grp = pad(sum(p[:, h * tq:(h + 1) * tq] for h in range(A_HPG)))
        t["imp"] = sum(jnp.dot(cnt[0:t["n_slc"]], part, preferred_element_type=F32)
                       for part in _split3(p_grp))

    for t in tiles:
        s = t["s_w"]
        t["p_w"] = jnp.exp2(s - jnp.max(s, axis=0, keepdims=True)).astype(BF16)
    for t in tiles:
        acc = jnp.dot(values_t(vw_ref, t["first_w"], t["n_w"]), t["p_w"], preferred_element_type=F32)
        out_w = acc[:dh] * (1.0 / acc[dh:dh + 1])
        gates = t["gates"]
        t["part"] = [gates[h:h + 1, :] * t["out_c"][:, h * tq:(h + 1) * tq]
                     + gates[2 * A_HPG + h:2 * A_HPG + h + 1, :] * out_w[:, h * tq:(h + 1) * tq]
                     for h in range(A_HPG)]

    for t in tiles:
        nb = t["n_slc"]
        blk = lax.broadcasted_iota(jnp.int32, (nb, tq), 0)
        tpos = t["i"] * tq + lax.broadcasted_iota(jnp.int32, (nb, tq), 1)
        cur = tpos // SLC_BLOCK
        forced = (blk == 0) | (blk == cur) | (blk == cur - 1)
        causal = blk * SLC_BLOCK <= tpos
        imp = jnp.where(forced, FORCE_SCORE, jnp.where(causal, t["imp"], NEG_INF))
        rank = jnp.zeros((nb, tq), F32)
        for c in range(nb):
            row = imp[c:c + 1, :]
            ahead = (row > imp) | ((row == imp) & (blk > c))
            rank = rank + jnp.where(ahead, 1.0, 0.0)
        pen = jnp.where(rank < float(min(SLC_TOPN, n_slc)), 0.0, -FORCE_SCORE)
        pen = jnp.concatenate([pen, jnp.zeros((LANES - dh - nb, tq), F32)], axis=0)
        t["q_aug"] = jnp.concatenate([t["q"], jnp.concatenate([pen] * A_HPG, axis=1).astype(BF16)], axis=0)

    sel_tile = lambda r: jnp.where(r < 0, TILE_MASKED, jnp.minimum(r, TILE_FAR))

    jobs = [(t, c) for t in tiles for c in range(t["last"], -1, -1)]
    ss = [scores(ks_ref[0, 0, c * SUB * tq:(c + 1) * SUB * tq, :], t["q_aug"], t["i"], c * SUB, SUB, sel_tile,
                 [n for n in range(SUB) if SUB * (t["last"] - c) - n >= TILE_FAR])
          for t, c in jobs]
    ms = [jnp.max(s, axis=0, keepdims=True) for s in ss]
    ps = [jnp.exp2(s - m).astype(BF16) for s, m in zip(ss, ms)]
    accs = [jnp.dot(values_t(vs_ref, c * SUB, SUB), p, preferred_element_type=F32) for (t, c), p in zip(jobs, ps)]

    for k, t in enumerate(tiles):
        mine = [n for n, (tt, c) in enumerate(jobs) if tt is t]
        m = functools.reduce(jnp.maximum, [ms[n] for n in mine])
        acc = sum(jnp.exp2(ms[n] - m) * accs[n] for n in mine)
        out_s = acc[:dh] * (1.0 / acc[dh:dh + 1])
        gates = t["gates"]
        blocks = [t["part"][h] + gates[A_HPG + h:A_HPG + h + 1, :] * out_s[:, h * tq:(h + 1) * tq]
                  for h in range(A_HPG)]
        o_ref[0, k] = jnp.concatenate(blocks, axis=0).T.astype(BF16)


def _attention(q_t, kc, vc_t, ks, vs_t, kw, vw_t, bias_c, bias_d, gates_t):
    bsz, _, nt, _, _ = q_t.shape
    s = ks.shape[2]
    n_cmp = kc.shape[2]
    assert nt == ATT_TILES * SUB and WINDOW // TQ + 1 <= nt
    k_spec = pl.BlockSpec((1, 1, s, LANES), lambda b, g, j: (b, g, 0, 0))
    vt_spec = pl.BlockSpec((1, 1, nt, V_ROWS, TQ), lambda b, g, j: (b, g, 0, 0, 0))
    per_tile = lambda spec: [spec(t * SUB) for t in range(ATT_TILES)]
    q_spec = lambda off: pl.BlockSpec((1, 1, 1, A_HEAD_DIM, A_HPG * TQ), lambda b, g, j: (b, g, j + off, 0, 0))
    bc_spec = lambda off: pl.BlockSpec((1, 1, n_cmp, A_HPG * TQ), lambda b, g, j: (j + off, g, 0, 0))
    gt_spec = lambda off: pl.BlockSpec((1, 1, 1, GATE_ROWS, TQ), lambda b, g, j: (b, g, j + off, 0, 0))
    out = pl.pallas_call(
        _attn_kernel,
        grid=(bsz, A_KV_GROUPS, SUB),
        in_specs=(per_tile(q_spec)
                  + [pl.BlockSpec((1, 1, n_cmp, A_HEAD_DIM), lambda b, g, j: (b, g, 0, 0)),
                     pl.BlockSpec((1, 1, A_HEAD_DIM, n_cmp), lambda b, g, j: (b, g, 0, 0)),
                     k_spec, vt_spec, k_spec, vt_spec]
                  + per_tile(bc_spec)
                  + [pl.BlockSpec((1, N_BIAS_TILES, TQ, A_HPG * TQ), lambda b, g, j: (g, 0, 0, 0))]
                  + per_tile(gt_spec)),
        out_specs=pl.BlockSpec((1, ATT_TILES, TQ, A_HPG * A_HEAD_DIM), lambda b, g, j: (b, 0, j, g)),
        out_shape=jax.ShapeDtypeStruct((bsz, ATT_TILES, s // ATT_TILES, A_WIDTH), BF16),
        compiler_params=pltpu.CompilerParams(dimension_semantics=("parallel", "parallel", "parallel"),
                                             vmem_limit_bytes=VMEM_LIMIT),
        name="attn",
    )(*([q_t] * ATT_TILES), kc, vc_t, ks, vs_t, kw, vw_t, *([bias_c] * ATT_TILES), bias_d,
      *([gates_t] * ATT_TILES))
    return out.reshape(bsz, s, A_WIDTH)


def _rwkv_kernel(c_ref, mu_ref, w0_ref, wl_ref, a0_ref, al_ref, kk_ref, ka_ref, rk_ref, lw_ref, lb_ref,
                 o_ref, state_ref, prev_ref):
    cc = pl.program_id(1)
    n = B_HEAD_DIM
    nb, csz = c_ref.shape[0], c_ref.shape[1]

    @pl.when(cc == 0)
    def _():
        state_ref[...] = jnp.zeros(state_ref.shape, F32)
        prev_ref[...] = jnp.zeros(prev_ref.shape, F32)

    ti = lax.broadcasted_iota(jnp.int32, (csz, LANES), 0)
    si = lax.broadcasted_iota(jnp.int32, (csz, LANES), 1) % n
    lower = si <= ti
    strict = si < ti
    eye = jnp.where(si == ti, 1.0, 0.0)
    tri = jnp.where(lax.broadcasted_iota(jnp.int32, (csz, csz), 1) <= lax.broadcasted_iota(jnp.int32, (csz, csz), 0),
                    1.0, 0.0).astype(BF16)
    n_pairs = B_WIDTH // LANES
    left =lax.broadcasted_iota(jnp.int32, (csz, LANES), 1) < n
    row_left = lax.broadcasted_iota(jnp.int32, (LANES, LANES), 0) < n
    same_head = row_left == (lax.broadcasted_iota(jnp.int32, (LANES, LANES), 1) < n)

    def blockdiag(y):
        zero = jnp.zeros_like(y)
        return jnp.concatenate([jnp.where(left, y, zero), jnp.where(left, zero, y)], axis=0)

    def head_sum(x):
        lo = jnp.sum(jnp.where(left, x, 0.0), axis=-1, keepdims=True)
        hi = jnp.sum(jnp.where(left, 0.0, x), axis=-1, keepdims=True)
        return jnp.where(left, lo, hi)

    chains = []
    for bi in range(nb):
        p = c_ref[bi]
        rolled = pltpu.roll(p, 1, axis=0)
        row = lax.broadcasted_iota(jnp.int32, (8, p.shape[1]), 0)
        prev = jnp.concatenate([jnp.where(row == 0, prev_ref[bi, 0:1, :], rolled[0:8]), rolled[8:]], axis=0)
        prev_ref[bi, 0:1, :] = p[csz - 1:csz, :]
        x = p + (prev - p) * mu_ref[...]
        r = x[:, 0:B_WIDTH]
        k = x[:, B_WIDTH:2 * B_WIDTH]
        v = x[:, 2 * B_WIDTH:3 * B_WIDTH]
        wd = x[:, 3 * B_WIDTH:3 * B_WIDTH + DECAY_LORA]
        ad = x[:, 3 * B_WIDTH + DECAY_LORA:3 * B_WIDTH + DECAY_LORA + ICLR_LORA]

        ld = -math.exp(-0.5) * _sigmoid(w0_ref[...] + _dot(jnp.tanh(wd), wl_ref[...]))
        a = _sigmoid(a0_ref[...] + _dot(ad, al_ref[...]))
        kk = k * kk_ref[...]
        k_mod = k * (1.0 + (a - 1.0) * ka_ref[...])
        rkr = r * k_mod * rk_ref[...]

        ld_hi, ld_lo = _split2(ld)
        cum = jnp.dot(tri, ld_hi, preferred_element_type=F32) + jnp.dot(tri, ld_lo, preferred_element_type=F32)
        g_inc = jnp.exp(cum)
        g_exc = jnp.exp(cum - ld)
        g_inv = jnp.exp(-cum)
        g_end = jnp.exp(cum[csz - 1:csz, :] - cum)
        g_all = g_inc[csz - 1:csz, :]

        for pr in range(n_pairs):
            sl = slice(pr * LANES, (pr + 1) * LANES)
            kk_p = kk[:, sl]
            kk_p = kk_p * lax.rsqrt(jnp.maximum(head_sum(kk_p * kk_p), 1e-24))
            b_p = kk_p * a[:, sl]
            bt = (b_p * g_inv[:, sl]).astype(BF16)
            kt = (k_mod[:, sl] * g_inv[:, sl]).astype(BF16)
            ch = dict(
                idx=bi * n_pairs + pr,
                v=v[:, sl],
                lhs=jnp.concatenate([-kk_p * g_exc[:, sl], r[:, sl] * g_inc[:, sl]], axis=0).astype(BF16),
                rhs=jnp.concatenate([blockdiag(bt), blockdiag(kt)], axis=0),
                bk=jnp.concatenate([b_p * g_end[:, sl], k_mod[:, sl] * g_end[:, sl]], axis=0).astype(BF16),
                g_all=g_all[:, sl],
                bonus=head_sum(rkr[:, sl]) * v[:, sl],
            )
            chains.append(ch)

    for ch in chains:
        x = _dot_nt(ch["lhs"], ch["rhs"])
        xb, xk = x[:, :LANES], x[:, LANES:]
        ch["a_ab"] = jnp.where(strict, xb[:csz], 0.0)
        a_ak = jnp.where(strict, xk[:csz], 0.0)
        m_rk = jnp.where(lower, xk[csz:], 0.0)
        ch["ak_rk"] = jnp.concatenate([a_ak, m_rk], axis=0).astype(BF16)
        ch["m_rb"] = jnp.where(lower, xb[csz:], 0.0).astype(BF16)
    for ch in chains:
        akv = _dot(ch["ak_rk"], blockdiag(ch["v"].astype(BF16)))
        ch["akv"], ch["mrkv"] = akv[:csz], akv[csz:]
        ch["tinv"] = eye + ch["a_ab"]
        ch["pw"] = ch["a_ab"].astype(BF16)
    n_sq = int(math.log2(csz)) - 1
    for ch in chains:
        ch["pw"] = _dot(ch["pw"], blockdiag(ch["pw"])).astype(BF16)
    for step in range(n_sq):
        for ch in chains:
            if step + 1 < n_sq:
                both = _dot(jnp.concatenate([ch["pw"], ch["tinv"].astype(BF16)], axis=0), blockdiag(ch["pw"]))
                ch["tinv"] = ch["tinv"] + both[csz:]
                ch["pw"] = both[:csz].astype(BF16)
            else:
                ch["tinv"] = ch["tinv"] + _dot(ch["tinv"], blockdiag(ch["pw"]))
    for ch in chains:
        ch["s0"] = state_ref[ch["idx"]]
        ch["as0"] = _dot_nt(ch["lhs"], ch["s0"])
    for ch in chains:
        w = (ch["as0"][:csz] + ch["akv"]).astype(BF16)
        ch["u"] = _dot(ch["tinv"], blockdiag(w))
    outs = []
    for ch in chains:
        u = ch["u"]
        y = ch["as0"][csz:] + _dot(ch["m_rb"], blockdiag(u.astype(BF16))) + ch["mrkv"]
        uv = jnp.concatenate([u, ch["v"]], axis=0)
        state_ref[ch["idx"]] = ch["s0"] * ch["g_all"] + jnp.where(same_head, _dot_tn(uv, ch["bk"]), 0.0)
        yc = y - head_sum(y) * (1.0 / n)
        var = head_sum(yc * yc) * (1.0 / n)
        outs.append(yc * lax.rsqrt(var + LNX_EPS))
    for bi in range(nb):
        yn = jnp.concatenate(outs[bi * n_pairs:(bi + 1) * n_pairs], axis=-1)
        bonus = jnp.concatenate([ch["bonus"] for ch in chains[bi * n_pairs:(bi + 1) * n_pairs]], axis=-1)
        o_ref[bi] = (yn * lw_ref[...] + lb_ref[...] + bonus).astype(BF16)


RWKV_NB = 8


def _rwkv(cols_rwkv, mu, w0, wl, a0, al, k_k, k_a, r_k, ln_w, ln_b):
    bsz, s, _ = cols_rwkv.shape
    nb = RWKV_NB if bsz % RWKV_NB == 0 else 1
    const = lambda b, c: (0, 0)
    vec = pl.BlockSpec((1, B_WIDTH), const)
    return pl.pallas_call(
        _rwkv_kernel,
        grid=(bsz // nb, s // CHUNK),
        in_specs=[pl.BlockSpec((nb, CHUNK, RWKV_COLS), lambda b, c: (b, c, 0)),
                  pl.BlockSpec((1, RWKV_COLS), const),
                  vec, pl.BlockSpec((DECAY_LORA, B_WIDTH), const),
                  vec, pl.BlockSpec((ICLR_LORA, B_WIDTH), const),
                  vec, vec, vec, vec, vec],
        out_specs=pl.BlockSpec((nb, CHUNK, B_WIDTH), lambda b, c: (b, c, 0)),
        out_shape=jax.ShapeDtypeStruct((bsz, s, B_WIDTH), BF16),
        scratch_shapes=[pltpu.VMEM((nb * B_WIDTH // LANES, LANES, LANES), F32),
                        pltpu.VMEM((nb, 8, RWKV_COLS), F32)],
        compiler_params=pltpu.CompilerParams(dimension_semantics=("parallel", "arbitrary")),
        name="rwkv",
    )(cols_rwkv, mu, w0, wl, a0, al, k_k, k_a, r_k, ln_w, ln_b)


def _final_kernel(x_ref, ya_ref, yb_ref, cf_ref, gate_ref, wa_ref, wb_ref, wo_ref, o_ref):
    a_silu = cf_ref[0, :, 0:A_WIDTH].astype(F32)
    b_silu = cf_ref[0, :, A_WIDTH:A_WIDTH + B_WIDTH].astype(F32)
    merge_a = cf_ref[0, :, A_WIDTH + B_WIDTH:A_WIDTH + B_WIDTH + D_MODEL].astype(F32)
    merge_b = cf_ref[0, :, A_WIDTH + B_WIDTH + D_MODEL:A_WIDTH + B_WIDTH + 2 * D_MODEL].astype(F32)
    ya = ya_ref[0].astype(F32) * (a_silu * _sigmoid(a_silu))
    yb = yb_ref[0].astype(F32) * (b_silu * _sigmoid(b_silu))
    merged = _sigmoid(merge_a) * _dot(ya, wa_ref[...]) + _sigmoid(merge_b) * _dot(yb, wb_ref[...])
    o_ref[0] = x_ref[0] + gate_ref[0] * _dot(merged, wo_ref[...])


def _final(x, y_a, y_b, cols_fin, gate, w_out_a, w_out_b, w_o, tm=512):
    bsz, s, _ = x.shape
    const = lambda b, i: (0, 0)
    row = lambda w: pl.BlockSpec((1, tm, w), lambda b, i: (b, i, 0))
    return pl.pallas_call(
        _final_kernel,
        grid=(bsz, s // tm),
        in_specs=[row(D_MODEL), row(A_WIDTH), row(B_WIDTH), row(FIN_COLS),
                  pl.BlockSpec((1, 1, D_MODEL), lambda b, i: (b, 0, 0)),
                  pl.BlockSpec((A_WIDTH, D_MODEL), const),
                  pl.BlockSpec((B_WIDTH, D_MODEL), const),
                  pl.BlockSpec((D_MODEL, D_MODEL), const)],
        out_specs=row(D_MODEL),
        out_shape=jax.ShapeDtypeStruct((bsz, s, D_MODEL), F32),
        compiler_params=pltpu.CompilerParams(dimension_semantics=("parallel", "parallel"),
                                             vmem_limit_bytes=VMEM_LIMIT),
        name="final",
    )(x, y_a, y_b, cols_fin, gate, w_out_a, w_out_b, w_o)


def _split_w_in(w_in):
    nsa_in = 2 * A_WIDTH + 6 * A_KV_WIDTH + 3 * A_HEADS
    o_gate = A_WIDTH + 6 * A_KV_WIDTH
    o_asilu = o_gate + 3 * A_HEADS
    o_shift = nsa_in
    o_rest = nsa_in + RWKV_COLS
    gate_w = w_in[:, o_gate:o_asilu].reshape(D_MODEL, 3, A_KV_GROUPS, A_HPG)
    gate_w = gate_w.transpose(0, 2, 1, 3).reshape(D_MODEL, A_KV_GROUPS, 3 * A_HPG)
    gate_w = jnp.pad(gate_w, ((0, 0), (0, 0), (0, A_HEAD_DIM - 3 * A_HPG))).reshape(D_MODEL, GATE_PAD)
    w_nsa = jnp.concatenate([w_in[:, :o_gate], gate_w], axis=1)
    w_fin = jnp.concatenate([w_in[:, o_asilu:o_shift], w_in[:, o_rest:]], axis=1)
    w_rwkv = w_in[:, o_shift:o_rest]
    return w_nsa.astype(BF16), w_fin.astype(BF16), w_rwkv.astype(BF16)


def _layer(x, c, rel_bias, w_ada, b_ada, norm_gain, w_in, q_norm_gain, k_norm_gain,
           cmp_pos_k, cmp_pos_v, cmp_k_w1, cmp_k_w2, cmp_v_w1, cmp_v_w2,
           shift_mu, w0, w_lora_up, a0, a_lora_up, k_k, k_a, r_k, ln_x_w, ln_x_b,
           w_out_a, w_out_b, w_o):
    bsz, s, _ = x.shape
    assert s % (2 * TQ) == 0 and s // CMP_STRIDE == LANES
    n16 = s // CMP_STRIDE
    mod = _ada(c, w_ada, b_ada)
    w_nsa, w_fin, w_rwkv = _split_w_in(w_in)
    scale = A_HEAD_DIM ** -0.5 * LOG2E
    qg = jnp.tile(q_norm_gain, A_HEADS) * scale
    ksg = jnp.tile(k_norm_gain[1], A_KV_GROUPS)
    kwg = jnp.tile(k_norm_gain[2], A_KV_GROUPS)
    q_t, ks, vs_t, kw, vw_t, gates_t, ck, cols_fin, cols_rwkv = _proj(
        x, mod, norm_gain, w_nsa, w_fin, w_rwkv, qg, ksg, kwg)

    kc, vc_t = _compress(ck, _expand_cmp_pos(cmp_pos_k), _expand_cmp_pos(cmp_pos_v),
                         _expand_cmp_w1(cmp_k_w1), cmp_k_w2.astype(BF16),
                         _expand_cmp_w1(cmp_v_w1), cmp_v_w2.T.astype(BF16),
                         k_norm_gain[0].reshape(1, A_HEAD_DIM))
    bias_c, bias_d = _bias_tables(rel_bias, s, n16)
    y_a = _attention(q_t, kc, vc_t, ks, vs_t, kw, vw_t, bias_c, bias_d, gates_t)

    vec = lambda t: t.reshape(1, -1)
    y_b = _rwkv(cols_rwkv, vec(shift_mu), vec(w0), w_lora_up.astype(BF16), vec(a0), a_lora_up.astype(BF16),
                vec(k_k), vec(k_a), vec(r_k), vec(ln_x_w), vec(ln_x_b))

    gate = mod[:, 2 * D_MODEL:].reshape(bsz, 1, D_MODEL)
    return _final(x, y_a, y_b, cols_fin, gate, w_out_a.astype(BF16), w_out_b.astype(BF16), w_o.astype(BF16))


def kernel(x, c, w_ada, b_ada, norm_gain, w_in, q_norm_gain, k_norm_gain, cmp_pos_k, cmp_pos_v, cmp_k_w1, cmp_k_w2, cmp_v_w1, cmp_v_w2, rel_bias, shift_mu, w0, w_lora_up, a0, a_lora_up, k_k, k_a, r_k, ln_x_w, ln_x_b, w_out_a, w_out_b, w_o):
    for l in range(w_in.shape[0]):
        x = _layer(x, c, rel_bias, w_ada[l], b_ada[l], norm_gain[l], w_in[l], q_norm_gain[l], k_norm_gain[l],
                   cmp_pos_k[l], cmp_pos_v[l], cmp_k_w1[l], cmp_k_w2[l], cmp_v_w1[l], cmp_v_w2[l],
                   shift_mu[l], w0[l], w_lora_up[l], a0[l], a_lora_up[l], k_k[l], k_a[l], r_k[l],
                   ln_x_w[l], ln_x_b[l], w_out_a[l], w_out_b[l], w_o[l])
    return x
```

```python
import functools
import math

import numpy as np
import jax
import jax.numpy as jnp
from jax import lax
from jax.experimental import pallas as pl
from jax.experimental.pallas import tpu as pltpu

F32 = jnp.float32
BF16 = jnp.bfloat16

D_MODEL = 1024
A_HEADS = 8
A_HEAD_DIM = 64
A_KV_GROUPS = 2
A_HPG = A_HEADS // A_KV_GROUPS
A_WIDTH = A_HEADS * A_HEAD_DIM
A_KV_WIDTH = A_KV_GROUPS * A_HEAD_DIM
CMP_BLOCK = 32
CMP_STRIDE = 16
CMP_HIDDEN = 256
SLC_BLOCK = 64
SLC_TOPN = 16
WINDOW = 512
B_HEADS = 8
B_HEAD_DIM = 64
B_WIDTH = B_HEADS * B_HEAD_DIM
DECAY_LORA = 64
ICLR_LORA = 64
LNX_EPS = 64e-5
REL_BUCKETS = 32
REL_MAX_EXACT = 16
REL_MAX_DIST = 128
NORM_EPS = 1e-6
NEG_INF = -1e30
FORCE_SCORE = 1e30

LANES = 128
TQ = 128
CHUNK = 64
GATE_PAD = LANES
LOG2E = math.log2(math.e)
V_ROWS = A_HEAD_DIM + 16
GATE_ROWS = 16
NSA_COLS = A_WIDTH + 6 * A_KV_WIDTH + GATE_PAD
FIN_COLS = A_WIDTH + B_WIDTH + 2 * D_MODEL
RWKV_COLS = 3 * B_WIDTH + DECAY_LORA + ICLR_LORA
VMEM_LIMIT = 56 * 1024 * 1024


def _dot(a, b):
    return jnp.dot(a.astype(BF16), b.astype(BF16), preferred_element_type=F32)


def _dot_nt(a, b):
    return lax.dot_general(a.astype(BF16), b.astype(BF16), (((1,), (1,)), ((), ())),
                           preferred_element_type=F32)


def _dot_tn(a, b):
    return lax.dot_general(a.astype(BF16), b.astype(BF16), (((0,), (0,)), ((), ())),
                           preferred_element_type=F32)


def _split2(x):
    hi = x.astype(BF16)
    lo = (x - hi.astype(F32)).astype(BF16)
    return hi, lo


def _split3(x):
    h1 = x.astype(BF16)
    r1 = x - h1.astype(F32)
    h2 = r1.astype(BF16)
    h3 = (r1 - h2.astype(F32)).astype(BF16)
    return h1, h2, h3


def _sigmoid(x):
    return 1.0 / (1.0 + jnp.exp(-x))


def _bucket_thresholds():
    n = np.arange(0, 4096)
    nf = np.maximum(n, REL_MAX_EXACT).astype(np.float64)
    val = np.log(nf / REL_MAX_EXACT) / math.log(REL_MAX_DIST / REL_MAX_EXACT) * (REL_BUCKETS - REL_MAX_EXACT)
    frac = np.abs(val - np.round(val))
    assert np.all((frac > 1e-4) | (n <= REL_MAX_EXACT) | (n >= REL_MAX_DIST))
    large = REL_MAX_EXACT + np.floor(val + 1e-9).astype(np.int64)
    bucket = np.where(n < REL_MAX_EXACT, n, np.minimum(large, REL_BUCKETS - 1))
    return [int(np.argmax(bucket >= j)) for j in range(REL_BUCKETS)]


_BUCKET_TH = _bucket_thresholds()


def _bias_from_dist(dist, tbl_ref, head):
    val = jnp.full(dist.shape, tbl_ref[0, head], F32)
    for j in range(1, REL_BUCKETS):
        val = jnp.where(dist >= _BUCKET_TH[j], tbl_ref[j, head], val)
    return val


def _ada_kernel(c_ref, w_ref, b_ref, o_ref):
    c = c_ref[...]
    o_ref[...] = _dot(c * _sigmoid(c), w_ref[...]) + b_ref[...]


def _ada(c, w_ada, b_ada):
    bsz = c.shape[0]
    return pl.pallas_call(
        _ada_kernel,
        grid=(3,),
        in_specs=[pl.BlockSpec((bsz, D_MODEL), lambda j: (0, 0)),
                  pl.BlockSpec((D_MODEL, D_MODEL), lambda j: (0, j)),
                  pl.BlockSpec((1, D_MODEL), lambda j: (0, j))],
        out_specs=pl.BlockSpec((bsz, D_MODEL), lambda j: (0, j)),
        out_shape=jax.ShapeDtypeStruct((bsz, 3 * D_MODEL), F32),
        name="ada",
    )(c, w_ada, b_ada.reshape(1, 3 * D_MODEL))


def _norm_rows(x_t, gain_col, n_seg):
    out = []
    for seg in range(n_seg):
        blk = x_t[seg * A_HEAD_DIM:(seg + 1) * A_HEAD_DIM, :]
        ms = jnp.mean(blk * blk, axis=0, keepdims=True)
        out.append(blk * lax.rsqrt(ms + NORM_EPS) * gain_col[seg * A_HEAD_DIM:(seg + 1) * A_HEAD_DIM, :])
    return out


def _proj_kernel(x_ref, mod_ref, g_ref, wn_ref, wf_ref, wr_ref, qg_ref, ksg_ref, kwg_ref,
                 q_ref, ks_ref, vs_ref, kw_ref, vw_ref, gt_ref, ck_ref, of_ref, or_ref):
    tm = x_ref.shape[1]
    x = x_ref[0]
    ms = jnp.mean(x * x, axis=-1, keepdims=True)
    y = x * lax.rsqrt(ms + NORM_EPS) * g_ref[...]
    mod = mod_ref[0]
    h = (y * (1.0 + mod[:, D_MODEL:2 * D_MODEL]) + mod[:, :D_MODEL]).astype(BF16)
    cn = jnp.dot(h, wn_ref[...], preferred_element_type=F32)
    or_ref[0] = jnp.dot(h, wr_ref[...], preferred_element_type=F32)
    ck_ref[0] = cn[:, A_WIDTH:A_WIDTH + 2 * A_KV_WIDTH]

    lane = lax.broadcasted_iota(jnp.int32, (TQ, LANES), 1)
    row = lax.broadcasted_iota(jnp.int32, (TQ, LANES), 0)
    ones_rows = (lax.broadcasted_iota(jnp.int32, (V_ROWS - A_HEAD_DIM, TQ), 0) == 0).astype(BF16)
    off = A_WIDTH + 2 * A_KV_WIDTH
    for sub in range(tm // TQ):
        c = cn[sub * TQ:(sub + 1) * TQ]
        q_heads = _norm_rows(c[:, 0:A_WIDTH].T, qg_ref[...], A_HEADS)
        ks_t = jnp.concatenate(_norm_rows(c[:, off:off + A_KV_WIDTH].T, ksg_ref[...], A_KV_GROUPS), axis=0)
        kw_t = jnp.concatenate(_norm_rows(c[:, off + 2 * A_KV_WIDTH:off + 3 * A_KV_WIDTH].T, kwg_ref[...],
                                          A_KV_GROUPS), axis=0)
        ksn = ks_t.T
        kwn = kw_t.T
        vs_t = c[:, off + A_KV_WIDTH:off + 2 * A_KV_WIDTH].T.astype(BF16)
        vw_t = c[:, off + 3 * A_KV_WIDTH:off + 4 * A_KV_WIDTH].T.astype(BF16)
        gates_t = _sigmoid(c[:, off + 4 * A_KV_WIDTH:off + 5 * A_KV_WIDTH]).T
        blk = (pl.program_id(1) * tm + sub * TQ + row) // SLC_BLOCK
        onehot = jnp.where(lane - A_HEAD_DIM == blk, 1.0, 0.0)
        for g in range(A_KV_GROUPS):
            q_ref[0, g, sub] = jnp.concatenate(q_heads[g * A_HPG:(g + 1) * A_HPG], axis=1).astype(BF16)
            sl = slice(g * A_HEAD_DIM, (g + 1) * A_HEAD_DIM)
            k_g = ksn if g == 0 else pltpu.roll(ksn, A_HEAD_DIM, axis=1)
            ks_ref[0, g, sub * TQ:(sub + 1) * TQ, :] = jnp.where(lane < A_HEAD_DIM, k_g, onehot).astype(BF16)
            kw_g = kwn if g == 0 else pltpu.roll(kwn, A_HEAD_DIM, axis=1)
            kw_ref[0, g, sub * TQ:(sub + 1) * TQ, :] = jnp.where(lane < A_HEAD_DIM, kw_g, 0.0).astype(BF16)
            vs_ref[0, g, sub] = jnp.concatenate([vs_t[sl, :], ones_rows], axis=0)
            vw_ref[0, g, sub] = jnp.concatenate([vw_t[sl, :], ones_rows], axis=0)
            gt_ref[0, g, sub] = gates_t[g * A_HEAD_DIM:g * A_HEAD_DIM + GATE_ROWS, :]
    of_ref[0] = jnp.dot(h, wf_ref[...], preferred_element_type=F32).astype(BF16)


def _proj(x, mod, norm_gain, w_nsa, w_fin, w_rwkv, qg, ksg, kwg, tm=512):
    bsz, s, _ = x.shape
    nt, nsub = s // TQ, tm // TQ
    assert A_HEAD_DIM + s // SLC_BLOCK <= LANES and A_KV_WIDTH == LANES
    const = lambda b, i: (0, 0)
    weight = lambda cols: pl.BlockSpec((D_MODEL, cols), const, pipeline_mode=pl.Buffered(1))
    col = lambda t: jnp.broadcast_to(t.reshape(-1, 1), (t.size, LANES))
    k_spec = lambda width: pl.BlockSpec((1, A_KV_GROUPS, tm, width), lambda b, i: (b, 0, i, 0))
    k_shape = lambda width: jax.ShapeDtypeStruct((bsz, A_KV_GROUPS, s, width), BF16)
    tile_spec = lambda r, c: pl.BlockSpec((1, A_KV_GROUPS, nsub, r, c), lambda b, i: (b, 0, i, 0, 0))
    tile_shape = lambda r, c, dt: jax.ShapeDtypeStruct((bsz, A_KV_GROUPS, nt, r, c), dt)
    return pl.pallas_call(
        _proj_kernel,
        grid=(bsz, s // tm),
        in_specs=[pl.BlockSpec((1, tm, D_MODEL), lambda b, i: (b, i, 0)),
                  pl.BlockSpec((1, 1, 3 * D_MODEL), lambda b, i: (b, 0, 0)),
                  pl.BlockSpec((1, D_MODEL), const),
                  weight(NSA_COLS), weight(FIN_COLS), weight(RWKV_COLS),
                  pl.BlockSpec((A_WIDTH, LANES), const),
                  pl.BlockSpec((A_KV_WIDTH, LANES), const),
                  pl.BlockSpec((A_KV_WIDTH, LANES), const)],
        out_specs=[tile_spec(A_HEAD_DIM, A_HPG * TQ),
                   k_spec(LANES), tile_spec(V_ROWS, TQ), k_spec(LANES), tile_spec(V_ROWS, TQ),
                   tile_spec(GATE_ROWS, TQ),
                   pl.BlockSpec((1, tm, 2 * A_KV_WIDTH), lambda b, i: (b, i, 0)),
                   pl.BlockSpec((1, tm, FIN_COLS), lambda b, i: (b, i, 0)),
                   pl.BlockSpec((1, tm, RWKV_COLS), lambda b, i: (b, i, 0))],
        out_shape=[tile_shape(A_HEAD_DIM, A_HPG * TQ, BF16),
                   k_shape(LANES), tile_shape(V_ROWS, TQ, BF16), k_shape(LANES), tile_shape(V_ROWS, TQ, BF16),
                   tile_shape(GATE_ROWS, TQ, F32),
                   jax.ShapeDtypeStruct((bsz, s, 2 * A_KV_WIDTH), F32),
                   jax.ShapeDtypeStruct((bsz, s, FIN_COLS), BF16),
                   jax.ShapeDtypeStruct((bsz, s, RWKV_COLS), F32)],
        compiler_params=pltpu.CompilerParams(dimension_semantics=("parallel", "parallel"),
                                             vmem_limit_bytes=VMEM_LIMIT),
        name="proj",
    )(x, mod.reshape(bsz, 1, 3 * D_MODEL), norm_gain.reshape(1, D_MODEL), w_nsa, w_fin, w_rwkv,
      col(qg), col(ksg), col(kwg))


def _compress_kernel(ck_ref, cv_ref, pk_ref, pv_ref, w1k_ref, w2k_ref, w1v_ref, w2v_ref, kg_ref, kc_ref, vc_ref):
    n16 = ck_ref.shape[1] // CMP_STRIDE

    def rows16(ref):
        return jnp.concatenate([ref[0, pl.ds(p, n16, stride=CMP_STRIDE), :] for p in range(CMP_STRIDE)], axis=1)

    def hidden(z, pos_ref, w1_ref, g):
        top = _dot(z + pos_ref[0:1, :], w1_ref[g, 0])
        bot = _dot(z + pos_ref[1:2, :], w1_ref[g, 1])
        return jax.nn.gelu(top + pltpu.roll(bot, n16 - 1, axis=0), approximate=True)

    zk = rows16(ck_ref)
    zv = rows16(cv_ref)
    for g in range(A_KV_GROUPS):
        kc = _dot(hidden(zk, pk_ref, w1k_ref, g), w2k_ref[...])
        ms = jnp.mean(kc * kc, axis=-1, keepdims=True)
        kc_ref[0, g] = (kc * lax.rsqrt(ms + NORM_EPS) * kg_ref[...]).astype(BF16)
        vc_ref[0, g] = _dot_nt(w2v_ref[...], hidden(zv, pv_ref, w1v_ref, g)).astype(BF16)


def _expand_cmp_w1(w1):
    w = w1.reshape(2, CMP_STRIDE, 1, A_HEAD_DIM, CMP_HIDDEN)
    per_group = []
    for g in range(A_KV_GROUPS):
        pad = [(0, 0), (0, 0), (g, A_KV_GROUPS - 1 - g), (0, 0), (0, 0)]
        per_group.append(jnp.pad(w, pad).reshape(2, CMP_STRIDE * A_KV_WIDTH, CMP_HIDDEN))
    return jnp.stack(per_group).astype(BF16)


def _expand_cmp_pos(pos):
    p = jnp.broadcast_to(pos.reshape(2, CMP_STRIDE, 1, A_HEAD_DIM), (2, CMP_STRIDE, A_KV_GROUPS, A_HEAD_DIM))
    return p.reshape(2, CMP_STRIDE * A_KV_WIDTH)


def _compress(ck, pk, pv, w1k, w2k, w1v, w2v_t, kg):
    bsz, s, _ = ck.shape
    n16 = s // CMP_STRIDE
    zw = CMP_STRIDE * A_KV_WIDTH
    const = lambda b: (0, 0)
    const4 = lambda b: (0, 0, 0, 0)
    return pl.pallas_call(
        _compress_kernel,
        grid=(bsz,),
        in_specs=[pl.BlockSpec((1, s, A_KV_WIDTH), lambda b: (b, 0, 0)),
                  pl.BlockSpec((1, s, A_KV_WIDTH), lambda b: (b, 0, 1)),
                  pl.BlockSpec((2, zw), const), pl.BlockSpec((2, zw), const),
                  pl.BlockSpec((A_KV_GROUPS, 2, zw, CMP_HIDDEN), const4), pl.BlockSpec((CMP_HIDDEN, A_HEAD_DIM), const),
                  pl.BlockSpec((A_KV_GROUPS, 2, zw, CMP_HIDDEN), const4), pl.BlockSpec((A_HEAD_DIM, CMP_HIDDEN), const),
                  pl.BlockSpec((1, A_HEAD_DIM), const)],
        out_specs=[pl.BlockSpec((1, A_KV_GROUPS, n16, A_HEAD_DIM), lambda b: (b, 0, 0, 0)),
                   pl.BlockSpec((1, A_KV_GROUPS, A_HEAD_DIM, n16), lambda b: (b, 0, 0, 0))],
        out_shape=[jax.ShapeDtypeStruct((bsz, A_KV_GROUPS, n16, A_HEAD_DIM), BF16),
                   jax.ShapeDtypeStruct((bsz, A_KV_GROUPS, A_HEAD_DIM, n16), BF16)],
        compiler_params=pltpu.CompilerParams(dimension_semantics=("parallel",)),
        name="compress",
    )(ck, ck, pk, pv, w1k, w2k, w1v, w2v_t, kg)


TILE_FAR, TILE_EDGE, TILE_MASKED, N_BIAS_TILES = 2, 3, 4, 5
SUB = 4
ATT_TILES = 4


def _bias_cmp_kernel(tbl_ref, o_ref):
    i = pl.program_id(0)
    g = pl.program_id(1)
    n_cmp = o_ref.shape[2]
    n = lax.broadcasted_iota(jnp.int32, (n_cmp, TQ), 0)
    q = lax.broadcasted_iota(jnp.int32, (n_cmp, TQ), 1)
    dist = i * TQ + q - (n * CMP_STRIDE + CMP_BLOCK - 1)
    for h in range(A_HPG):
        bias = _bias_from_dist(dist, tbl_ref, g * A_HPG + h)
        o_ref[0, 0, :, h * TQ:(h + 1) * TQ] = jnp.where(dist >= 0, bias * LOG2E, NEG_INF)


def _bias_toeplitz_kernel(tbl_ref, o_ref):
    g = pl.program_id(0)
    r = pl.program_id(1)
    off = jnp.where(r == TILE_EDGE, WINDOW // TQ, jnp.where(r == TILE_MASKED, -2, r))
    k = lax.broadcasted_iota(jnp.int32, (TQ, TQ), 0)
    q = lax.broadcasted_iota(jnp.int32, (TQ, TQ), 1)
    dist = off * TQ + q - k
    valid = (dist >= 0) & (dist < WINDOW)
    for h in range(A_HPG):
        head = g * A_HPG + h
        bias = _bias_from_dist(dist, tbl_ref, head) - tbl_ref[REL_BUCKETS - 1, head]
        o_ref[0, 0, :, h * TQ:(h + 1) * TQ] = jnp.where(valid, bias * LOG2E, NEG_INF)


def _bias_tables(rel_bias, s, n_cmp):
    smem = pl.BlockSpec(memory_space=pltpu.SMEM)
    nt = s // TQ
    bias_c = pl.pallas_call(
        _bias_cmp_kernel,
        grid=(nt, A_KV_GROUPS),
        in_specs=[smem],
        out_specs=pl.BlockSpec((1, 1, n_cmp, A_HPG * TQ), lambda i, g: (i, g, 0, 0)),
        out_shape=jax.ShapeDtypeStruct((nt, A_KV_GROUPS, n_cmp, A_HPG * TQ), F32),
        name="bias_cmp",
    )(rel_bias)
    assert _BUCKET_TH[REL_BUCKETS - 1] <= TQ + 1 and WINDOW // TQ >= 3
    bias_d = pl.pallas_call(
        _bias_toeplitz_kernel,
        grid=(A_KV_GROUPS, N_BIAS_TILES),
        in_specs=[smem],
        out_specs=pl.BlockSpec((1, 1, TQ, A_HPG * TQ), lambda g, r: (g, r, 0, 0)),
        out_shape=jax.ShapeDtypeStruct((A_KV_GROUPS, N_BIAS_TILES, TQ, A_HPG * TQ), F32),
        name="bias_toeplitz",
    )(rel_bias)
    return bias_c, bias_d


def _attn_kernel(*refs, j):
    q_refs, refs = refs[:ATT_TILES], refs[ATT_TILES:]
    kc_ref, vc_ref, ks_ref, vs_ref, kw_ref, vw_ref = refs[:6]
    bc_refs, bd_ref, gt_refs, o_ref = refs[6:6 + ATT_TILES], refs[6 + ATT_TILES], refs[7 + ATT_TILES:-1], refs[-1]
    tq = TQ
    n_cmp = kc_ref.shape[2]
    n_slc = ks_ref.shape[2] // SLC_BLOCK
    wt = WINDOW // tq
    dh = A_HEAD_DIM
    round8 = lambda n: -(-n // 8) * 8
    tiles = [dict(i=j + t * SUB, last=(j + t * SUB) // SUB, n_tok=(j + t * SUB + 1) * tq, q=q_refs[t][0, 0, 0],
                  bias_c=bc_refs[t], gates=gt_refs[t][0, 0, 0]) for t in range(ATT_TILES)]
    zero_rows = jnp.zeros((LANES - dh, A_HPG * tq), BF16)

    def scores(k_ref, q_mat, i, first_tile, n_sub, tile_index):
        s = jnp.dot(k_ref[0, 0, first_tile * tq:(first_tile + n_sub) * tq, :], q_mat,
                    preferred_element_type=F32)
        idx = [tile_index(i - (first_tile + t)) for t in range(n_sub)]
        parts = [s[t * tq:(t + 1) * tq] if idx[t] == TILE_FAR else s[t * tq:(t + 1) * tq] + bd_ref[0, idx[t]]
                 for t in range(n_sub)]
        return jnp.concatenate(parts, axis=0)

    def values_t(v_ref, first_tile, n_sub):
        return jnp.concatenate([v_ref[0, 0, first_tile + t] for t in range(n_sub)], axis=1)

    win_tile = lambda r: TILE_EDGE if r == wt else min(r, TILE_FAR)
    for t in tiles:
        t["first_w"] = max(t["i"] - wt, 0)
        t["n_w"] = t["i"] - t["first_w"] + 1
        q_pad = jnp.concatenate([t["q"], zero_rows], axis=0)
        t["s_w"] = scores(kw_ref, q_pad, t["i"], t["first_w"], t["n_w"], win_tile)
    for t in tiles:
        t["n_cmp"] = min(n_cmp, round8(t["n_tok"] // CMP_STRIDE))
        t["n_slc"] = min(n_slc, round8(t["n_tok"] // SLC_BLOCK))
        bias = t["bias_c"][0, 0, 0:t["n_cmp"], :]
        t["valid_c"] = bias > 0.5 * NEG_INF
        t["s_c"] = jnp.dot(kc_ref[0, 0, 0:t["n_cmp"], :], t["q"], preferred_element_type=F32) + bias

    r1, r2 = SLC_BLOCK // CMP_STRIDE, CMP_BLOCK // CMP_STRIDE
    jj = lax.broadcasted_iota(jnp.int32, (n_slc, n_cmp), 0)
    nn = lax.broadcasted_iota(jnp.int32, (n_slc, n_cmp), 1)
    d = nn - r1 * jj
    cnt = jnp.zeros((n_slc, n_cmp), F32)
    for a in range(r1):
        for c in range(r2):
            cnt = cnt + jnp.where(d == a - c, 1.0, 0.0)
    cnt = cnt.astype(BF16)
    for t in tiles:
        s = t["s_c"]
        e = jnp.where(t["valid_c"], jnp.exp2(s - jnp.max(s, axis=0, keepdims=True)), 0.0)
        l = jnp.sum(e, axis=0, keepdims=True)
        p = e * (1.0 / jnp.where(l > 0.0, l, 1.0))
        rest = n_cmp - t["n_cmp"]
        pad = (lambda a: jnp.concatenate([a, jnp.zeros((rest, a.shape[1]), a.dtype)], axis=0)) if rest else (lambda a: a)
        t["out_c"] = jnp.dot(vc_ref[0, 0], pad(p).astype(BF16), preferred_element_type=F32)
        p_grp = pad(sum(p[:, h * tq:(h + 1) * tq] for h in range(A_HPG)))
        t["imp"] = sum(jnp.dot(cnt[0:t["n_slc"]], part, preferred_element_type=F32)
                       for part in _split3(p_grp))

    for t in tiles:
        s = t["s_w"]
        t["p_w"] = jnp.exp2(s - jnp.max(s, axis=0, keepdims=True)).astype(BF16)
    for t in tiles:
        acc = jnp.dot(values_t(vw_ref, t["first_w"], t["n_w"]), t["p_w"], preferred_element_type=F32)
        out_w = acc[:dh] * (1.0 / acc[dh:dh + 1])
        gates = t["gates"]
        t["part"] = [gates[h:h + 1, :] * t["out_c"][:, h * tq:(h + 1) * tq]
                     + gates[2 * A_HPG + h:2 * A_HPG + h + 1, :] * out_w[:, h * tq:(h + 1) * tq]
                     for h in range(A_HPG)]

    for t in tiles:
        nb = t["n_slc"]
        blk = lax.broadcasted_iota(jnp.int32, (nb, tq), 0)
        tpos = t["i"] * tq + lax.broadcasted_iota(jnp.int32, (nb, tq), 1)
        cur = tpos // SLC_BLOCK
        forced = (blk == 0) | (blk == cur) | (blk == cur - 1)
        causal = blk * SLC_BLOCK <= tpos
        imp = jnp.where(forced, FORCE_SCORE, jnp.where(causal, t["imp"], NEG_INF))
        rank = jnp.zeros((nb, tq), F32)
        for c in range(nb):
            row = imp[c:c + 1, :]
            ahead = (row > imp) | ((row == imp) & (blk > c))
            rank = rank + jnp.where(ahead, 1.0, 0.0)
        pen = jnp.where(rank < float(min(SLC_TOPN, n_slc)), 0.0, -FORCE_SCORE)
        pen = jnp.concatenate([pen, jnp.zeros((LANES - dh - nb, tq), F32)], axis=0)
        t["q_aug"] = jnp.concatenate([t["q"], jnp.concatenate([pen] * A_HPG, axis=1).astype(BF16)], axis=0)

    sel_tile = lambda r: min(r, TILE_FAR)

    jobs = [(t, c, min(SUB, t["i"] - c * SUB + 1)) for t in tiles for c in range(t["last"], -1, -1)]
    ss = [scores(ks_ref, t["q_aug"], t["i"], c * SUB, n, sel_tile) for t, c, n in jobs]
    ms = [jnp.max(s, axis=0, keepdims=True) for s in ss]
    ps = [jnp.exp2(s - m).astype(BF16) for s, m in zip(ss, ms)]
    accs = [jnp.dot(values_t(vs_ref, c * SUB, n), p, preferred_element_type=F32) for (t, c, n), p in zip(jobs, ps)]

    for k, t in enumerate(tiles):
        mine = [n for n, job in enumerate(jobs) if job[0] is t]
        m = functools.reduce(jnp.maximum, [ms[n] for n in mine])
        acc = sum(jnp.exp2(ms[n] - m) * accs[n] for n in mine)
        out_s = acc[:dh] * (1.0 / acc[dh:dh + 1])
        gates = t["gates"]
        blocks = [t["part"][h] + gates[A_HPG + h:A_HPG + h + 1, :] * out_s[:, h * tq:(h + 1) * tq]
                  for h in range(A_HPG)]
        o_ref[0, k] = jnp.concatenate(blocks, axis=0).T.astype(BF16)


def _attention(q_t, kc, vc_t, ks, vs_t, kw, vw_t, bias_c, bias_d, gates_t):
    bsz, _, nt, _, _ = q_t.shape
    s = ks.shape[2]
    n_cmp = kc.shape[2]
    assert nt == ATT_TILES * SUB and WINDOW // TQ + 1 <= nt
    k_spec = pl.BlockSpec((1, 1, s, LANES), lambda b, g: (b, g, 0, 0))
    vt_spec = pl.BlockSpec((1, 1, nt, V_ROWS, TQ), lambda b, g: (b, g, 0, 0, 0))
    outs = []
    for j in range(SUB):
        per_tile = lambda spec: [spec(j + t * SUB) for t in range(ATT_TILES)]
        q_spec = lambda i: pl.BlockSpec((1, 1, 1, A_HEAD_DIM, A_HPG * TQ), lambda b, g: (b, g, i, 0, 0))
        bc_spec = lambda i: pl.BlockSpec((1, 1, n_cmp, A_HPG * TQ), lambda b, g: (i, g, 0, 0))
        gt_spec = lambda i: pl.BlockSpec((1, 1, 1, GATE_ROWS, TQ), lambda b, g: (b, g, i, 0, 0))
        outs.append(pl.pallas_call(
            functools.partial(_attn_kernel, j=j),
            grid=(bsz, A_KV_GROUPS),
            in_specs=(per_tile(q_spec)
                      + [pl.BlockSpec((1, 1, n_cmp, A_HEAD_DIM), lambda b, g: (b, g, 0, 0)),
                         pl.BlockSpec((1, 1, A_HEAD_DIM, n_cmp), lambda b, g: (b, g, 0, 0)),
                         k_spec, vt_spec, k_spec, vt_spec]
                      + per_tile(bc_spec)
                      + [pl.BlockSpec((1, N_BIAS_TILES, TQ, A_HPG * TQ), lambda b, g: (g, 0, 0, 0))]
                      + per_tile(gt_spec)),
            out_specs=pl.BlockSpec((1, ATT_TILES, TQ, A_HPG * A_HEAD_DIM), lambda b, g: (b, 0, 0, g)),
            out_shape=jax.ShapeDtypeStruct((bsz, ATT_TILES, TQ, A_WIDTH), BF16),
            compiler_params=pltpu.CompilerParams(dimension_semantics=("parallel", "parallel"),
                                                 vmem_limit_bytes=VMEM_LIMIT),
            name=f"attn{j}",
        )(*([q_t] * ATT_TILES), kc, vc_t, ks, vs_t, kw, vw_t, *([bias_c] * ATT_TILES), bias_d,
          *([gates_t] * ATT_TILES)))
    return outs


def _rwkv_kernel(c_ref, mu_ref, w0_ref, wl_ref, a0_ref, al_ref, kk_ref, ka_ref, rk_ref, lw_ref, lb_ref,
                 o_ref, state_ref, prev_ref):
    cc = pl.program_id(1)
    n = B_HEAD_DIM
    nb, csz = c_ref.shape[0], c_ref.shape[1]

    @pl.when(cc == 0)
    def _():
        state_ref[...] = jnp.zeros(state_ref.shape, F32)
        prev_ref[...] = jnp.zeros(prev_ref.shape, F32)

    ti = lax.broadcasted_iota(jnp.int32, (csz, LANES), 0)
    si = lax.broadcasted_iota(jnp.int32, (csz, LANES), 1) % n
    lower = si <= ti
    strict = si < ti
    eye = jnp.where(si == ti, 1.0, 0.0)
    tri = jnp.where(lax.broadcasted_iota(jnp.int32, (csz, csz), 1) <= lax.broadcasted_iota(jnp.int32, (csz, csz), 0),
                    1.0, 0.0).astype(BF16)
    n_pairs = B_WIDTH // LANES
    left =lax.broadcasted_iota(jnp.int32, (csz, LANES), 1) < n
    row_left = lax.broadcasted_iota(jnp.int32, (LANES, LANES), 0) < n
    same_head = row_left == (lax.broadcasted_iota(jnp.int32, (LANES, LANES), 1) < n)

    def blockdiag(y):
        zero = jnp.zeros_like(y)
        return jnp.concatenate([jnp.where(left, y, zero), jnp.where(left, zero, y)], axis=0)

    def head_sum(x):
        lo = jnp.sum(jnp.where(left, x, 0.0), axis=-1, keepdims=True)
        hi = jnp.sum(jnp.where(left, 0.0, x), axis=-1, keepdims=True)
        return jnp.where(left, lo, hi)

    chains = []
    for bi in range(nb):
        p = c_ref[bi]
        rolled = pltpu.roll(p, 1, axis=0)
        row = lax.broadcasted_iota(jnp.int32, (8, p.shape[1]), 0)
        prev = jnp.concatenate([jnp.where(row == 0, prev_ref[bi, 0:1, :], rolled[0:8]), rolled[8:]], axis=0)
        prev_ref[bi, 0:1, :] = p[csz - 1:csz, :]
        x = p + (prev - p) * mu_ref[...]
        r = x[:, 0:B_WIDTH]
        k = x[:, B_WIDTH:2 * B_WIDTH]
        v = x[:, 2 * B_WIDTH:3 * B_WIDTH]
        wd = x[:, 3 * B_WIDTH:3 * B_WIDTH + DECAY_LORA]
        ad = x[:, 3 * B_WIDTH + DECAY_LORA:3 * B_WIDTH + DECAY_LORA + ICLR_LORA]

        ld = -math.exp(-0.5) * _sigmoid(w0_ref[...] + _dot(jnp.tanh(wd), wl_ref[...]))
        a = _sigmoid(a0_ref[...] + _dot(ad, al_ref[...]))
        kk = k * kk_ref[...]
        k_mod = k * (1.0 + (a - 1.0) * ka_ref[...])
        rkr = r * k_mod * rk_ref[...]

        ld_hi, ld_lo = _split2(ld)
        cum = jnp.dot(tri, ld_hi, preferred_element_type=F32) + jnp.dot(tri, ld_lo, preferred_element_type=F32)
        g_inc = jnp.exp(cum)
        g_exc = jnp.exp(cum - ld)
        g_inv = jnp.exp(-cum)
        g_end = jnp.exp(cum[csz - 1:csz, :] - cum)
        g_all = g_inc[csz - 1:csz, :]

        for pr in range(n_pairs):
            sl = slice(pr * LANES, (pr + 1) * LANES)
            kk_p = kk[:, sl]
            kk_p = kk_p * lax.rsqrt(jnp.maximum(head_sum(kk_p * kk_p), 1e-24))
            b_p = kk_p * a[:, sl]
            bt = (b_p * g_inv[:, sl]).astype(BF16)
            kt = (k_mod[:, sl] * g_inv[:, sl]).astype(BF16)
            ch = dict(
                idx=bi * n_pairs + pr,
                v=v[:, sl],
                lhs=jnp.concatenate([-kk_p * g_exc[:, sl], r[:, sl] * g_inc[:, sl]], axis=0).astype(BF16),
                rhs=jnp.concatenate([blockdiag(bt), blockdiag(kt)], axis=0),
                bk=jnp.concatenate([b_p * g_end[:, sl], k_mod[:, sl] * g_end[:, sl]], axis=0).astype(BF16),
                g_all=g_all[:, sl],
                bonus=head_sum(rkr[:, sl]) * v[:, sl],
            )
            chains.append(ch)

    for ch in chains:
        x = _dot_nt(ch["lhs"], ch["rhs"])
        xb, xk = x[:, :LANES], x[:, LANES:]
        ch["a_ab"] = jnp.where(strict, xb[:csz], 0.0)
        a_ak = jnp.where(strict, xk[:csz], 0.0)
        m_rk = jnp.where(lower, xk[csz:], 0.0)
        ch["ak_rk"] = jnp.concatenate([a_ak, m_rk], axis=0).astype(BF16)
        ch["m_rb"] = jnp.where(lower, xb[csz:], 0.0).astype(BF16)
    for ch in chains:
        akv = _dot(ch["ak_rk"], blockdiag(ch["v"].astype(BF16)))
        ch["akv"], ch["mrkv"] = akv[:csz], akv[csz:]
        ch["tinv"] = eye + ch["a_ab"]
        ch["pw"] = ch["a_ab"].astype(BF16)
    n_sq = int(math.log2(csz)) - 1
    for ch in chains:
        ch["pw"] = _dot(ch["pw"], blockdiag(ch["pw"])).astype(BF16)
    for step in range(n_sq):
        for ch in chains:
            if step + 1 < n_sq:
                both = _dot(jnp.concatenate([ch["pw"], ch["tinv"].astype(BF16)], axis=0), blockdiag(ch["pw"]))
                ch["tinv"] = ch["tinv"] + both[csz:]
                ch["pw"] = both[:csz].astype(BF16)
            else:
                ch["tinv"] = ch["tinv"] + _dot(ch["tinv"], blockdiag(ch["pw"]))
    for ch in chains:
        ch["s0"] = state_ref[ch["idx"]]
        ch["as0"] = _dot_nt(ch["lhs"], ch["s0"])
    for ch in chains:
        w = (ch["as0"][:csz] + ch["akv"]).astype(BF16)
        ch["u"] = _dot(ch["tinv"], blockdiag(w))
    outs = []
    for ch in chains:
        u = ch["u"]
        y = ch["as0"][csz:] + _dot(ch["m_rb"], blockdiag(u.astype(BF16))) + ch["mrkv"]
        uv = jnp.concatenate([u, ch["v"]], axis=0)
        state_ref[ch["idx"]] = ch["s0"] * ch["g_all"] + jnp.where(same_head, _dot_tn(uv, ch["bk"]), 0.0)
        yc = y - head_sum(y) * (1.0 / n)
        var = head_sum(yc * yc) * (1.0 / n)
        outs.append(yc * lax.rsqrt(var + LNX_EPS))
    for bi in range(nb):
        yn = jnp.concatenate(outs[bi * n_pairs:(bi + 1) * n_pairs], axis=-1)
        bonus = jnp.concatenate([ch["bonus"] for ch in chains[bi * n_pairs:(bi + 1) * n_pairs]], axis=-1)
        o_ref[bi] = (yn * lw_ref[...] + lb_ref[...] + bonus).astype(BF16)


RWKV_NB = 8


def _rwkv(cols_rwkv, mu, w0, wl, a0, al, k_k, k_a, r_k, ln_w, ln_b):
    bsz, s, _ = cols_rwkv.shape
    nb = RWKV_NB if bsz % RWKV_NB == 0 else 1
    const = lambda b, c: (0, 0)
    vec = pl.BlockSpec((1, B_WIDTH), const)
    return pl.pallas_call(
        _rwkv_kernel,
        grid=(bsz // nb, s // CHUNK),
        in_specs=[pl.BlockSpec((nb, CHUNK, RWKV_COLS), lambda b, c: (b, c, 0)),
                  pl.BlockSpec((1, RWKV_COLS), const),
                  vec, pl.BlockSpec((DECAY_LORA, B_WIDTH), const),
                  vec, pl.BlockSpec((ICLR_LORA, B_WIDTH), const),
                  vec, vec, vec, vec, vec],
        out_specs=pl.BlockSpec((nb, CHUNK, B_WIDTH), lambda b, c: (b, c, 0)),
        out_shape=jax.ShapeDtypeStruct((bsz, s, B_WIDTH), BF16),
        scratch_shapes=[pltpu.VMEM((nb * B_WIDTH // LANES, LANES, LANES), F32),
                        pltpu.VMEM((nb, 8, RWKV_COLS), F32)],
        compiler_params=pltpu.CompilerParams(dimension_semantics=("parallel", "arbitrary")),
        name="rwkv",
    )(cols_rwkv, mu, w0, wl, a0, al, k_k, k_a, r_k, ln_w, ln_b)


def _final_kernel(x_ref, *refs):
    ya_refs, (yb_ref, cf_ref, gate_ref, wa_ref, wb_ref, wo_ref, o_ref) = refs[:SUB], refs[SUB:]
    a_silu = cf_ref[0, :, 0:A_WIDTH].astype(F32)
    b_silu = cf_ref[0, :, A_WIDTH:A_WIDTH + B_WIDTH].astype(F32)
    merge_a = cf_ref[0, :, A_WIDTH + B_WIDTH:A_WIDTH + B_WIDTH + D_MODEL].astype(F32)
    merge_b = cf_ref[0, :, A_WIDTH + B_WIDTH + D_MODEL:A_WIDTH + B_WIDTH + 2 * D_MODEL].astype(F32)
    y_a = jnp.concatenate([r[0, 0] for r in ya_refs], axis=0)
    ya = y_a.astype(F32) * (a_silu * _sigmoid(a_silu))
    yb = yb_ref[0].astype(F32) * (b_silu * _sigmoid(b_silu))
    merged = _sigmoid(merge_a) * _dot(ya, wa_ref[...]) + _sigmoid(merge_b) * _dot(yb, wb_ref[...])
    o_ref[0] = x_ref[0] + gate_ref[0] * _dot(merged, wo_ref[...])


def _final(x, y_a, y_b, cols_fin, gate, w_out_a, w_out_b, w_o):
    bsz, s, _ = x.shape
    tm = SUB * TQ
    const = lambda b, i: (0, 0)
    row = lambda w: pl.BlockSpec((1, tm, w), lambda b, i: (b, i, 0))
    ya_spec = pl.BlockSpec((1, 1, TQ, A_WIDTH), lambda b, i: (b, i, 0, 0))
    return pl.pallas_call(
        _final_kernel,
        grid=(bsz, s // tm),
        in_specs=[row(D_MODEL)] + [ya_spec] * SUB + [row(B_WIDTH), row(FIN_COLS),
                  pl.BlockSpec((1, 1, D_MODEL), lambda b, i: (b, 0, 0)),
                  pl.BlockSpec((A_WIDTH, D_MODEL), const),
                  pl.BlockSpec((B_WIDTH, D_MODEL), const),
                  pl.BlockSpec((D_MODEL, D_MODEL), const)],
        out_specs=row(D_MODEL),
        out_shape=jax.ShapeDtypeStruct((bsz, s, D_MODEL), F32),
        compiler_params=pltpu.CompilerParams(dimension_semantics=("parallel", "parallel"),
                                             vmem_limit_bytes=VMEM_LIMIT),
        name="final",
    )(x, *y_a, y_b, cols_fin, gate, w_out_a, w_out_b, w_o)


def _split_w_in(w_in):
    nsa_in = 2 * A_WIDTH + 6 * A_KV_WIDTH + 3 * A_HEADS
    o_gate = A_WIDTH + 6 * A_KV_WIDTH
    o_asilu = o_gate + 3 * A_HEADS
    o_shift = nsa_in
    o_rest = nsa_in + RWKV_COLS
    gate_w = w_in[:, o_gate:o_asilu].reshape(D_MODEL, 3, A_KV_GROUPS, A_HPG)
    gate_w = gate_w.transpose(0, 2, 1, 3).reshape(D_MODEL, A_KV_GROUPS, 3 * A_HPG)
    gate_w = jnp.pad(gate_w, ((0, 0), (0, 0), (0, A_HEAD_DIM - 3 * A_HPG))).reshape(D_MODEL, GATE_PAD)
    w_nsa = jnp.concatenate([w_in[:, :o_gate], gate_w], axis=1)
    w_fin = jnp.concatenate([w_in[:, o_asilu:o_shift], w_in[:, o_rest:]], axis=1)
    w_rwkv = w_in[:, o_shift:o_rest]
    return w_nsa.astype(BF16), w_fin.astype(BF16), w_rwkv.astype(BF16)


def _layer(x, c, rel_bias, w_ada, b_ada, norm_gain, w_in, q_norm_gain, k_norm_gain,
           cmp_pos_k, cmp_pos_v, cmp_k_w1, cmp_k_w2, cmp_v_w1, cmp_v_w2,
           shift_mu, w0, w_lora_up, a0, a_lora_up, k_k, k_a, r_k, ln_x_w, ln_x_b,
           w_out_a, w_out_b, w_o):
    bsz, s, _ = x.shape
    assert s % (2 * TQ) == 0 and s // CMP_STRIDE == LANES
    n16 = s // CMP_STRIDE
    mod = _ada(c, w_ada, b_ada)
    w_nsa, w_fin, w_rwkv = _split_w_in(w_in)
    scale = A_HEAD_DIM ** -0.5 * LOG2E
    qg = jnp.tile(q_norm_gain, A_HEADS) * scale
    ksg = jnp.tile(k_norm_gain[1], A_KV_GROUPS)
    kwg = jnp.tile(k_norm_gain[2], A_KV_GROUPS)
    q_t, ks, vs_t, kw, vw_t, gates_t, ck, cols_fin, cols_rwkv = _proj(
        x, mod, norm_gain, w_nsa, w_fin, w_rwkv, qg, ksg, kwg)

    kc, vc_t = _compress(ck, _expand_cmp_pos(cmp_pos_k), _expand_cmp_pos(cmp_pos_v),
                         _expand_cmp_w1(cmp_k_w1), cmp_k_w2.astype(BF16),
                         _expand_cmp_w1(cmp_v_w1), cmp_v_w2.T.astype(BF16),
                         k_norm_gain[0].reshape(1, A_HEAD_DIM))
    bias_c, bias_d = _bias_tables(rel_bias, s, n16)
    y_a = _attention(q_t, kc, vc_t, ks, vs_t, kw, vw_t, bias_c, bias_d, gates_t)

    vec = lambda t: t.reshape(1, -1)
    y_b = _rwkv(cols_rwkv, vec(shift_mu), vec(w0), w_lora_up.astype(BF16), vec(a0), a_lora_up.astype(BF16),
                vec(k_k), vec(k_a), vec(r_k), vec(ln_x_w), vec(ln_x_b))

    gate = mod[:, 2 * D_MODEL:].reshape(bsz, 1, D_MODEL)
    return _final(x, y_a, y_b, cols_fin, gate, w_out_a.astype(BF16), w_out_b.astype(BF16), w_o.astype(BF16))


def kernel(x, c, w_ada, b_ada, norm_gain, w_in, q_norm_gain, k_norm_gain, cmp_pos_k, cmp_pos_v, cmp_k_w1, cmp_k_w2, cmp_v_w1, cmp_v_w2, rel_bias, shift_mu, w0, w_lora_up, a0, a_lora_up, k_k, k_a, r_k, ln_x_w, ln_x_b, w_out_a, w_out_b, w_o):
    for l in range(w_in.shape[0]):
        x = _layer(x, c, rel_bias, w_ada[l], b_ada[l], norm_gain[l], w_in[l], q_norm_gain[l], k_norm_gain[l],
                   cmp_pos_k[l], cmp_pos_v[l], cmp_k_w1[l], cmp_k_w2[l], cmp_v_w1[l], cmp_v_w2[l],
                   shift_mu[l], w0[l], w_lora_up[l], a0[l], a_lora_up[l], k_k[l], k_a[l], r_k[l],
                   ln_x_w[l], ln_x_b[l], w_out_a[l], w_out_b[l], w_o[l])
    return x
```

```python
import functools
import math

import numpy as np
import jax
import jax.numpy as jnp
from jax import lax
from jax.experimental import pallas as pl
from jax.experimental.pallas import tpu as pltpu

F32 = jnp.float32
BF16 = jnp.bfloat16

D_MODEL = 1024
A_HEADS = 8
A_HEAD_DIM = 64
A_KV_GROUPS = 2
A_HPG = A_HEADS // A_KV_GROUPS
A_WIDTH = A_HEADS * A_HEAD_DIM
A_KV_WIDTH = A_KV_GROUPS * A_HEAD_DIM
CMP_BLOCK = 32
CMP_STRIDE = 16
CMP_HIDDEN = 256
SLC_BLOCK = 64
SLC_TOPN = 16
WINDOW = 512
B_HEADS = 8
B_HEAD_DIM = 64
B_WIDTH = B_HEADS * B_HEAD_DIM
DECAY_LORA = 64
ICLR_LORA = 64
LNX_EPS = 64e-5
REL_BUCKETS = 32
REL_MAX_EXACT = 16
REL_MAX_DIST = 128
NORM_EPS = 1e-6
NEG_INF = -1e30
FORCE_SCORE = 1e30

LANES = 128
TQ = 128
CHUNK = 64
GATE_PAD = LANES
LOG2E = math.log2(math.e)
V_ROWS = A_HEAD_DIM + 16
GATE_ROWS = 16
NSA_COLS = A_WIDTH + 6 * A_KV_WIDTH + GATE_PAD
FIN_COLS = A_WIDTH + B_WIDTH + 2 * D_MODEL
RWKV_COLS = 3 * B_WIDTH + DECAY_LORA + ICLR_LORA
SCAN_BF16_COLS = 5 * B_WIDTH
SCAN_F32_COLS = 2 * B_WIDTH
VMEM_LIMIT = 56 * 1024 * 1024


def _dot(a, b):
    return jnp.dot(a.astype(BF16), b.astype(BF16), preferred_element_type=F32)


def _dot_nt(a, b):
    return lax.dot_general(a.astype(BF16), b.astype(BF16), (((1,), (1,)), ((), ())),
                           preferred_element_type=F32)


def _dot_tn(a, b):
    return lax.dot_general(a.astype(BF16), b.astype(BF16), (((0,), (0,)), ((), ())),
                           preferred_element_type=F32)


def _split2(x):
    hi = x.astype(BF16)
    lo = (x - hi.astype(F32)).astype(BF16)
    return hi, lo


def _split3(x):
    h1 = x.astype(BF16)
    r1 = x - h1.astype(F32)
    h2 = r1.astype(BF16)
    h3 = (r1 - h2.astype(F32)).astype(BF16)
    return h1, h2, h3


def _sigmoid(x):
    return 1.0 / (1.0 + jnp.exp(-x))


def _bucket_thresholds():
    n = np.arange(0, 4096)
    nf = np.maximum(n, REL_MAX_EXACT).astype(np.float64)
    val = np.log(nf / REL_MAX_EXACT) / math.log(REL_MAX_DIST / REL_MAX_EXACT) * (REL_BUCKETS - REL_MAX_EXACT)
    frac = np.abs(val - np.round(val))
    assert np.all((frac > 1e-4) | (n <= REL_MAX_EXACT) | (n >= REL_MAX_DIST))
    large = REL_MAX_EXACT + np.floor(val + 1e-9).astype(np.int64)
    bucket = np.where(n < REL_MAX_EXACT, n, np.minimum(large, REL_BUCKETS - 1))
    return [int(np.argmax(bucket >= j)) for j in range(REL_BUCKETS)]


_BUCKET_TH = _bucket_thresholds()


def _bias_from_dist(dist, tbl_ref, head):
    val = jnp.full(dist.shape, tbl_ref[0, head], F32)
    for j in range(1, REL_BUCKETS):
        val = jnp.where(dist >= _BUCKET_TH[j], tbl_ref[j, head], val)
    return val


def _ada_kernel(c_ref, w_ref, b_ref, o_ref):
    c = c_ref[...]
    o_ref[...] = _dot(c * _sigmoid(c), w_ref[...]) + b_ref[...]


def _ada(c, w_ada, b_ada):
    bsz = c.shape[0]
    return pl.pallas_call(
        _ada_kernel,
        grid=(3,),
        in_specs=[pl.BlockSpec((bsz, D_MODEL), lambda j: (0, 0)),
                  pl.BlockSpec((D_MODEL, D_MODEL), lambda j: (0, j)),
                  pl.BlockSpec((1, D_MODEL), lambda j: (0, j))],
        out_specs=pl.BlockSpec((bsz, D_MODEL), lambda j: (0, j)),
        out_shape=jax.ShapeDtypeStruct((bsz, 3 * D_MODEL), F32),
        name="ada",
    )(c, w_ada, b_ada.reshape(1, 3 * D_MODEL))


def _norm_rows(x_t, gain_col, n_seg):
    out = []
    for seg in range(n_seg):
        blk = x_t[seg * A_HEAD_DIM:(seg + 1) * A_HEAD_DIM, :]
        ms = jnp.mean(blk * blk, axis=0, keepdims=True)
        out.append(blk * lax.rsqrt(ms + NORM_EPS) * gain_col[seg * A_HEAD_DIM:(seg + 1) * A_HEAD_DIM, :])
    return out


def _head_sum(x):
    left = lax.broadcasted_iota(jnp.int32, x.shape, x.ndim - 1) < B_HEAD_DIM
    lo = jnp.sum(jnp.where(left, x, 0.0), axis=-1, keepdims=True)
    hi = jnp.sum(jnp.where(left, 0.0, x), axis=-1, keepdims=True)
    return jnp.where(left, lo, hi)


def _proj_kernel(x_ref, mod_ref, g_ref, wn_ref, wf_ref, wr_ref, qg_ref, ksg_ref, kwg_ref,
                 mu_ref, w0_ref, wl_ref, a0_ref, al_ref, kk_ref, ka_ref, rk_ref,
                 q_ref, ks_ref, vs_ref, kw_ref, vw_ref, gt_ref, ck_ref, of_ref, rw_ref, ld_ref, prev_ref):
    tm = x_ref.shape[1]

    @pl.when(pl.program_id(1) == 0)
    def _():
        prev_ref[...] = jnp.zeros(prev_ref.shape, F32)

    x = x_ref[0]
    ms = jnp.mean(x * x, axis=-1, keepdims=True)
    y = x * lax.rsqrt(ms + NORM_EPS) * g_ref[...]
    mod = mod_ref[0]
    h = (y * (1.0 + mod[:, D_MODEL:2 * D_MODEL]) + mod[:, :D_MODEL]).astype(BF16)
    cr = jnp.dot(h, wr_ref[...], preferred_element_type=F32)
    cn = jnp.dot(h, wn_ref[...], preferred_element_type=F32)
    ck_ref[0] = cn[:, A_WIDTH:A_WIDTH + 2 * A_KV_WIDTH]

    rolled = pltpu.roll(cr, 1, axis=0)
    row8 = lax.broadcasted_iota(jnp.int32, (8, RWKV_COLS), 0)
    prev = jnp.concatenate([jnp.where(row8 == 0, prev_ref[0:1, :], rolled[0:8]), rolled[8:]], axis=0)
    prev_ref[0:1, :] = cr[tm - 1:tm, :]
    xs = cr + (prev - cr) * mu_ref[...]
    r = xs[:, 0:B_WIDTH]
    k = xs[:, B_WIDTH:2 * B_WIDTH]
    v = xs[:, 2 * B_WIDTH:3 * B_WIDTH]
    wd = xs[:, 3 * B_WIDTH:3 * B_WIDTH + DECAY_LORA]
    ad = xs[:, 3 * B_WIDTH + DECAY_LORA:3 * B_WIDTH + DECAY_LORA + ICLR_LORA]
    ld = -math.exp(-0.5) * _sigmoid(w0_ref[...] + _dot(jnp.tanh(wd), wl_ref[...]))
    a = _sigmoid(a0_ref[...] + _dot(ad, al_ref[...]))
    kk = k * kk_ref[...]
    k_mod = k * (1.0 + (a - 1.0) * ka_ref[...])
    rkr = r * k_mod * rk_ref[...]
    ld_ref[0, :, 0:B_WIDTH] = ld
    rw_ref[0, :, 0:B_WIDTH] = r.astype(BF16)
    rw_ref[0, :, 3 * B_WIDTH:4 * B_WIDTH] = k_mod.astype(BF16)
    rw_ref[0, :, 4 * B_WIDTH:5 * B_WIDTH] = v.astype(BF16)
    for pr in range(B_WIDTH // LANES):
        sl = slice(pr * LANES, (pr + 1) * LANES)
        kk_p = kk[:, sl]
        kk_p = kk_p * lax.rsqrt(jnp.maximum(_head_sum(kk_p * kk_p), 1e-24))
        rw_ref[0, :, B_WIDTH + pr * LANES:B_WIDTH + (pr + 1) * LANES] = kk_p.astype(BF16)
        rw_ref[0, :, 2 * B_WIDTH + pr * LANES:2 * B_WIDTH + (pr + 1) * LANES] = (kk_p * a[:, sl]).astype(BF16)
        ld_ref[0, :, B_WIDTH + pr * LANES:B_WIDTH + (pr + 1) * LANES] = _head_sum(rkr[:, sl])

    lane = lax.broadcasted_iota(jnp.int32, (TQ, LANES), 1)
    row = lax.broadcasted_iota(jnp.int32, (TQ, LANES), 0)
    ones_rows = (lax.broadcasted_iota(jnp.int32, (V_ROWS - A_HEAD_DIM, TQ), 0) == 0).astype(BF16)
    off = A_WIDTH + 2 * A_KV_WIDTH
    for sub in range(tm // TQ):
        c = cn[sub * TQ:(sub + 1) * TQ]
        q_heads = _norm_rows(c[:, 0:A_WIDTH].T, qg_ref[...], A_HEADS)
        ks_t = jnp.concatenate(_norm_rows(c[:, off:off + A_KV_WIDTH].T, ksg_ref[...], A_KV_GROUPS), axis=0)
        kw_t = jnp.concatenate(_norm_rows(c[:, off + 2 * A_KV_WIDTH:off + 3 * A_KV_WIDTH].T, kwg_ref[...],
                                          A_KV_GROUPS), axis=0)
        ksn = ks_t.T
        kwn = kw_t.T
        vs_t = c[:, off + A_KV_WIDTH:off + 2 * A_KV_WIDTH].T.astype(BF16)
        vw_t = c[:, off + 3 * A_KV_WIDTH:off + 4 * A_KV_WIDTH].T.astype(BF16)
        gates_t = _sigmoid(c[:, off + 4 * A_KV_WIDTH:off + 5 * A_KV_WIDTH]).T
        blk = (pl.program_id(1) * tm + sub * TQ + row) // SLC_BLOCK
        onehot = jnp.where(lane - A_HEAD_DIM == blk, 1.0, 0.0)
        for g in range(A_KV_GROUPS):
            q_ref[0, g, sub] = jnp.concatenate(q_heads[g * A_HPG:(g + 1) * A_HPG], axis=1).astype(BF16)
            sl = slice(g * A_HEAD_DIM, (g + 1) * A_HEAD_DIM)
            k_g = ksn if g == 0 else pltpu.roll(ksn, A_HEAD_DIM, axis=1)
            ks_ref[0, g, sub * TQ:(sub + 1) * TQ, :] = jnp.where(lane < A_HEAD_DIM, k_g, onehot).astype(BF16)
            kw_g = kwn if g == 0 else pltpu.roll(kwn, A_HEAD_DIM, axis=1)
            kw_ref[0, g, sub * TQ:(sub + 1) * TQ, :] = jnp.where(lane < A_HEAD_DIM, kw_g, 0.0).astype(BF16)
            vs_ref[0, g, sub] = jnp.concatenate([vs_t[sl, :], ones_rows], axis=0)
            vw_ref[0, g, sub] = jnp.concatenate([vw_t[sl, :], ones_rows], axis=0)
            gt_ref[0, g, sub] = gates_t[g * A_HEAD_DIM:g * A_HEAD_DIM + GATE_ROWS, :]
    of_ref[0] = jnp.dot(h, wf_ref[...], preferred_element_type=F32).astype(BF16)


def _proj(x, mod, norm_gain, w_nsa, w_fin, w_rwkv, qg, ksg, kwg, rwkv_params, tm=512):
    bsz, s, _ = x.shape
    nt, nsub = s // TQ, tm // TQ
    assert A_HEAD_DIM + s // SLC_BLOCK <= LANES and A_KV_WIDTH == LANES and 2 * B_HEAD_DIM == LANES
    const = lambda b, i: (0, 0)
    weight = lambda cols: pl.BlockSpec((D_MODEL, cols), const, pipeline_mode=pl.Buffered(1))
    whole = lambda t: pl.BlockSpec(t.shape, const)
    col = lambda t: jnp.broadcast_to(t.reshape(-1, 1), (t.size, LANES))
    k_spec = lambda width: pl.BlockSpec((1, A_KV_GROUPS, tm, width), lambda b, i: (b, 0, i, 0))
    k_shape = lambda width: jax.ShapeDtypeStruct((bsz, A_KV_GROUPS, s, width), BF16)
    tile_spec = lambda r, c: pl.BlockSpec((1, A_KV_GROUPS, nsub, r, c), lambda b, i: (b, 0, i, 0, 0))
    tile_shape = lambda r, c, dt: jax.ShapeDtypeStruct((bsz, A_KV_GROUPS, nt, r, c), dt)
    return pl.pallas_call(
        _proj_kernel,
        grid=(bsz, s // tm),
        in_specs=[pl.BlockSpec((1, tm, D_MODEL), lambda b, i: (b, i, 0)),
                  pl.BlockSpec((1, 1, 3 * D_MODEL), lambda b, i: (b, 0, 0)),
                  pl.BlockSpec((1, D_MODEL), const),
                  weight(NSA_COLS), weight(FIN_COLS), weight(RWKV_COLS),
                  pl.BlockSpec((A_WIDTH, LANES), const),
                  pl.BlockSpec((A_KV_WIDTH, LANES), const),
                  pl.BlockSpec((A_KV_WIDTH, LANES), const)] + [whole(t) for t in rwkv_params],
        out_specs=[tile_spec(A_HEAD_DIM, A_HPG * TQ),
                   k_spec(LANES), tile_spec(V_ROWS, TQ), k_spec(LANES), tile_spec(V_ROWS, TQ),
                   tile_spec(GATE_ROWS, TQ),
                   pl.BlockSpec((1, tm, 2 * A_KV_WIDTH), lambda b, i: (b, i, 0)),
                   pl.BlockSpec((1, tm, FIN_COLS), lambda b, i: (b, i, 0)),
                   pl.BlockSpec((1, tm, SCAN_BF16_COLS), lambda b, i: (b, i, 0)),
                   pl.BlockSpec((1, tm, SCAN_F32_COLS), lambda b, i: (b, i, 0))],
        out_shape=[tile_shape(A_HEAD_DIM, A_HPG * TQ, BF16),
                   k_shape(LANES), tile_shape(V_ROWS, TQ, BF16), k_shape(LANES), tile_shape(V_ROWS, TQ, BF16),
                   tile_shape(GATE_ROWS, TQ, F32),
                   jax.ShapeDtypeStruct((bsz, s, 2 * A_KV_WIDTH), F32),
                   jax.ShapeDtypeStruct((bsz, s, FIN_COLS), BF16),
                   jax.ShapeDtypeStruct((bsz, s, SCAN_BF16_COLS), BF16),
                   jax.ShapeDtypeStruct((bsz, s, SCAN_F32_COLS), F32)],
        scratch_shapes=[pltpu.VMEM((8, RWKV_COLS), F32)],
        compiler_params=pltpu.CompilerParams(dimension_semantics=("parallel", "arbitrary"),
                                             vmem_limit_bytes=VMEM_LIMIT),
        name="proj",
    )(x, mod.reshape(bsz, 1, 3 * D_MODEL), norm_gain.reshape(1, D_MODEL), w_nsa, w_fin, w_rwkv,
      col(qg), col(ksg), col(kwg), *rwkv_params)


def _compress_kernel(ck_ref, cv_ref, pk_ref, pv_ref, w1k_ref, w2k_ref, w1v_ref, w2v_ref, kg_ref, kc_ref, vc_ref):
    n16 = ck_ref.shape[1] // CMP_STRIDE

    def rows16(ref):
        return jnp.concatenate([ref[0, pl.ds(p, n16, stride=CMP_STRIDE), :] for p in range(CMP_STRIDE)], axis=1)

    def hidden(z, pos_ref, w1_ref, g):
        top = _dot(z + pos_ref[0:1, :], w1_ref[g, 0])
        bot = _dot(z + pos_ref[1:2, :], w1_ref[g, 1])
        return jax.nn.gelu(top + pltpu.roll(bot, n16 - 1, axis=0), approximate=True)

    zk = rows16(ck_ref)
    zv = rows16(cv_ref)
    for g in range(A_KV_GROUPS):
        kc = _dot(hidden(zk, pk_ref, w1k_ref, g), w2k_ref[...])
        ms = jnp.mean(kc * kc, axis=-1, keepdims=True)
        kc_ref[0, g] = (kc * lax.rsqrt(ms + NORM_EPS) * kg_ref[...]).astype(BF16)
        vc_ref[0, g] = _dot_nt(w2v_ref[...], hidden(zv, pv_ref, w1v_ref, g)).astype(BF16)


def _expand_cmp_w1(w1):
    w = w1.reshape(2, CMP_STRIDE, 1, A_HEAD_DIM, CMP_HIDDEN)
    per_group = []
    for g in range(A_KV_GROUPS):
        pad = [(0, 0), (0, 0), (g, A_KV_GROUPS - 1 - g), (0, 0), (0, 0)]
        per_group.append(jnp.pad(w, pad).reshape(2, CMP_STRIDE * A_KV_WIDTH, CMP_HIDDEN))
    return jnp.stack(per_group).astype(BF16)


def _expand_cmp_pos(pos):
    p = jnp.broadcast_to(pos.reshape(2, CMP_STRIDE, 1, A_HEAD_DIM), (2, CMP_STRIDE, A_KV_GROUPS, A_HEAD_DIM))
    return p.reshape(2, CMP_STRIDE * A_KV_WIDTH)


def _compress(ck, pk, pv, w1k, w2k, w1v, w2v_t, kg):
    bsz, s, _ = ck.shape
    n16 = s // CMP_STRIDE
    zw = CMP_STRIDE * A_KV_WIDTH
    const = lambda b: (0, 0)
    const4 = lambda b: (0, 0, 0, 0)
    return pl.pallas_call(
        _compress_kernel,
        grid=(bsz,),
        in_specs=[pl.BlockSpec((1, s, A_KV_WIDTH), lambda b: (b, 0, 0)),
                  pl.BlockSpec((1, s, A_KV_WIDTH), lambda b: (b, 0, 1)),
                  pl.BlockSpec((2, zw), const), pl.BlockSpec((2, zw), const),
                  pl.BlockSpec((A_KV_GROUPS, 2, zw, CMP_HIDDEN), const4), pl.BlockSpec((CMP_HIDDEN, A_HEAD_DIM), const),
                  pl.BlockSpec((A_KV_GROUPS, 2, zw, CMP_HIDDEN), const4), pl.BlockSpec((A_HEAD_DIM, CMP_HIDDEN), const),
                  pl.BlockSpec((1, A_HEAD_DIM), const)],
        out_specs=[pl.BlockSpec((1, A_KV_GROUPS, n16, A_HEAD_DIM), lambda b: (b, 0, 0, 0)),
                   pl.BlockSpec((1, A_KV_GROUPS, A_HEAD_DIM, n16), lambda b: (b, 0, 0, 0))],
        out_shape=[jax.ShapeDtypeStruct((bsz, A_KV_GROUPS, n16, A_HEAD_DIM), BF16),
                   jax.ShapeDtypeStruct((bsz, A_KV_GROUPS, A_HEAD_DIM, n16), BF16)],
        compiler_params=pltpu.CompilerParams(dimension_semantics=("parallel",)),
        name="compress",
    )(ck, ck, pk, pv, w1k, w2k, w1v, w2v_t, kg)


TILE_FAR, TILE_EDGE, TILE_MASKED, N_BIAS_TILES = 2, 3, 4, 5
SUB = 4
ATT_TILES = 4


def _bias_cmp_kernel(tbl_ref, o_ref):
    i = pl.program_id(0)
    g = pl.program_id(1)
    n_cmp = o_ref.shape[2]
    n = lax.broadcasted_iota(jnp.int32, (n_cmp, TQ), 0)
    q = lax.broadcasted_iota(jnp.int32, (n_cmp, TQ), 1)
    dist = i * TQ + q - (n * CMP_STRIDE + CMP_BLOCK - 1)
    for h in range(A_HPG):
        bias = _bias_from_dist(dist, tbl_ref, g * A_HPG + h)
        o_ref[0, 0, :, h * TQ:(h + 1) * TQ] = jnp.where(dist >= 0, bias * LOG2E, NEG_INF)


def _bias_toeplitz_kernel(tbl_ref, o_ref):
    g = pl.program_id(0)
    r = pl.program_id(1)
    off = jnp.where(r == TILE_EDGE, WINDOW // TQ, jnp.where(r == TILE_MASKED, -2, r))
    k = lax.broadcasted_iota(jnp.int32, (TQ, TQ), 0)
    q = lax.broadcasted_iota(jnp.int32, (TQ, TQ), 1)
    dist = off * TQ + q - k
    valid = (dist >= 0) & (dist < WINDOW)
    for h in range(A_HPG):
        head = g * A_HPG + h
        bias = _bias_from_dist(dist, tbl_ref, head) - tbl_ref[REL_BUCKETS - 1, head]
        o_ref[0, 0, :, h * TQ:(h + 1) * TQ] = jnp.where(valid, bias * LOG2E, NEG_INF)


def _bias_tables(rel_bias, s, n_cmp):
    smem = pl.BlockSpec(memory_space=pltpu.SMEM)
    nt = s // TQ
    bias_c = pl.pallas_call(
        _bias_cmp_kernel,
        grid=(nt, A_KV_GROUPS),
        in_specs=[smem],
        out_specs=pl.BlockSpec((1, 1, n_cmp, A_HPG * TQ), lambda i, g: (i, g, 0, 0)),
        out_shape=jax.ShapeDtypeStruct((nt, A_KV_GROUPS, n_cmp, A_HPG * TQ), F32),
        name="bias_cmp",
    )(rel_bias)
    assert _BUCKET_TH[REL_BUCKETS - 1] <= TQ + 1 and WINDOW // TQ >= 3
    bias_d = pl.pallas_call(
        _bias_toeplitz_kernel,
        grid=(A_KV_GROUPS, N_BIAS_TILES),
        in_specs=[smem],
        out_specs=pl.BlockSpec((1, 1, TQ, A_HPG * TQ), lambda g, r: (g, r, 0, 0)),
        out_shape=jax.ShapeDtypeStruct((A_KV_GROUPS, N_BIAS_TILES, TQ, A_HPG * TQ), F32),
        name="bias_toeplitz",
    )(rel_bias)
    return bias_c, bias_d


def _attn_kernel(*refs, j):
    q_refs, refs = refs[:ATT_TILES], refs[ATT_TILES:]
    kc_ref, vc_ref, ks_ref, vs_ref, kw_ref, vw_ref = refs[:6]
    bc_refs, bd_ref, gt_refs, o_ref = refs[6:6 + ATT_TILES], refs[6 + ATT_TILES], refs[7 + ATT_TILES:-1], refs[-1]
    tq = TQ
    n_cmp = kc_ref.shape[2]
    n_slc = ks_ref.shape[2] // SLC_BLOCK
    wt = WINDOW // tq
    dh = A_HEAD_DIM
    round8 = lambda n: -(-n // 8) * 8
    tiles = [dict(i=j + t * SUB, last=(j + t * SUB) // SUB, n_tok=(j + t * SUB + 1) * tq, q=q_refs[t][0, 0, 0],
                  bias_c=bc_refs[t], gates=gt_refs[t][0, 0, 0]) for t in range(ATT_TILES)]
    zero_rows = jnp.zeros((LANES - dh, A_HPG * tq), BF16)

    def scores(k_ref, q_mat, i, first_tile, n_sub, tile_index):
        s = jnp.dot(k_ref[0, 0, first_tile * tq:(first_tile + n_sub) * tq, :], q_mat,
                    preferred_element_type=F32)
        idx = [tile_index(i - (first_tile + t)) for t in range(n_sub)]
        parts = [s[t * tq:(t + 1) * tq] if idx[t] == TILE_FAR else s[t * tq:(t + 1) * tq] + bd_ref[0, idx[t]]
                 for t in range(n_sub)]
        return jnp.concatenate(parts, axis=0)

    def values_t(v_ref, first_tile, n_sub):
        return jnp.concatenate([v_ref[0, 0, first_tile + t] for t in range(n_sub)], axis=1)

    win_tile = lambda r: TILE_EDGE if r == wt else min(r, TILE_FAR)
    for t in tiles:
        t["first_w"] = max(t["i"] - wt, 0)
        t["n_w"] = t["i"] - t["first_w"] + 1
        q_pad = jnp.concatenate([t["q"], zero_rows], axis=0)
        t["s_w"] = scores(kw_ref, q_pad, t["i"], t["first_w"], t["n_w"], win_tile)
    for t in tiles:
        t["n_cmp"] = min(n_cmp, round8(t["n_tok"] // CMP_STRIDE))
        t["n_slc"] = min(n_slc, round8(t["n_tok"] // SLC_BLOCK))
        bias = t["bias_c"][0, 0, 0:t["n_cmp"], :]
        t["valid_c"] = bias > 0.5 * NEG_INF
        t["s_c"] = jnp.dot(kc_ref[0, 0, 0:t["n_cmp"], :], t["q"], preferred_element_type=F32) + bias

    r1, r2 = SLC_BLOCK // CMP_STRIDE, CMP_BLOCK // CMP_STRIDE
    jj = lax.broadcasted_iota(jnp.int32, (n_slc, n_cmp), 0)
    nn = lax.broadcasted_iota(jnp.int32, (n_slc, n_cmp), 1)
    d = nn - r1 * jj
    cnt = jnp.zeros((n_slc, n_cmp), F32)
    for a in range(r1):
        for c in range(r2):
            cnt = cnt + jnp.where(d == a - c, 1.0, 0.0)
    cnt = cnt.astype(BF16)
    for t in tiles:
        s = t["s_c"]
        e = jnp.where(t["valid_c"], jnp.exp2(s - jnp.max(s, axis=0, keepdims=True)), 0.0)
        l = jnp.sum(e, axis=0, keepdims=True)
        p = e * (1.0 / jnp.where(l > 0.0, l, 1.0))
        rest = n_cmp - t["n_cmp"]
        pad = (lambda a: jnp.concatenate([a, jnp.zeros((rest, a.shape[1]), a.dtype)], axis=0)) if rest else (lambda a: a)
        t["out_c"] = jnp.dot(vc_ref[0, 0], pad(p).astype(BF16), preferred_element_type=F32)
        p_grp = pad(sum(p[:, h * tq:(h + 1) * tq] for h in range(A_HPG)))
        t["imp"] = sum(jnp.dot(cnt[0:t["n_slc"]], part, preferred_element_type=F32)
                       for part in _split3(p_grp))

    for t in tiles:
        s = t["s_w"]
        t["p_w"] = jnp.exp2(s - jnp.max(s, axis=0, keepdims=True)).astype(BF16)
    for t in tiles:
        acc = jnp.dot(values_t(vw_ref, t["first_w"], t["n_w"]), t["p_w"], preferred_element_type=F32)
        out_w = acc[:dh] * (1.0 / acc[dh:dh + 1])
        gates = t["gates"]
        t["part"] = [gates[h:h + 1, :] * t["out_c"][:, h * tq:(h + 1) * tq]
                     + gates[2 * A_HPG + h:2 * A_HPG + h + 1, :] * out_w[:, h * tq:(h + 1) * tq]
                     for h in range(A_HPG)]

    for t in tiles:
        nb = t["n_slc"]
        blk = lax.broadcasted_iota(jnp.int32, (nb, tq), 0)
        tpos = t["i"] * tq + lax.broadcasted_iota(jnp.int32, (nb, tq), 1)
        cur = tpos // SLC_BLOCK
        forced = (blk == 0) | (blk == cur) | (blk == cur - 1)
        causal = blk * SLC_BLOCK <= tpos
        imp = jnp.where(forced, FORCE_SCORE, jnp.where(causal, t["imp"], NEG_INF))
        rank = jnp.zeros((nb, tq), F32)
        for c in range(nb):
            row = imp[c:c + 1, :]
            ahead = (row > imp) | ((row == imp) & (blk > c))
            rank = rank + jnp.where(ahead, 1.0, 0.0)
        pen = jnp.where(rank < float(min(SLC_TOPN, n_slc)), 0.0, -FORCE_SCORE)
        pen = jnp.concatenate([pen, jnp.zeros((LANES - dh - nb, tq), F32)], axis=0)
        t["q_aug"] = jnp.concatenate([t["q"], jnp.concatenate([pen] * A_HPG, axis=1).astype(BF16)], axis=0)

    sel_tile = lambda r: min(r, TILE_FAR)

    jobs = [(t, c, min(SUB, t["i"] - c * SUB + 1)) for t in tiles for c in range(t["last"], -1, -1)]
    ss = [scores(ks_ref, t["q_aug"], t["i"], c * SUB, n, sel_tile) for t, c, n in jobs]
    ms = [jnp.max(s, axis=0, keepdims=True) for s in ss]
    ps = [jnp.exp2(s - m).astype(BF16) for s, m in zip(ss, ms)]
    accs = [jnp.dot(values_t(vs_ref, c * SUB, n), p, preferred_element_type=F32) for (t, c, n), p in zip(jobs, ps)]

    for k, t in enumerate(tiles):
        mine = [n for n, job in enumerate(jobs) if job[0] is t]
        m = functools.reduce(jnp.maximum, [ms[n] for n in mine])
        acc = sum(jnp.exp2(ms[n] - m) * accs[n] for n in mine)
        out_s = acc[:dh] * (1.0 / acc[dh:dh + 1])
        gates = t["gates"]
        blocks = [t["part"][h] + gates[A_HPG + h:A_HPG + h + 1, :] * out_s[:, h * tq:(h + 1) * tq]
                  for h in range(A_HPG)]
        o_ref[0, k] = jnp.concatenate(blocks, axis=0).T.astype(BF16)


def _attention(q_t, kc, vc_t, ks, vs_t, kw, vw_t, bias_c, bias_d, gates_t):
    bsz, _, nt, _, _ = q_t.shape
    s = ks.shape[2]
    n_cmp = kc.shape[2]
    assert nt == ATT_TILES * SUB and WINDOW // TQ + 1 <= nt
    k_spec = pl.BlockSpec((1, 1, s, LANES), lambda b, g: (b, g, 0, 0))
    vt_spec = pl.BlockSpec((1, 1, nt, V_ROWS, TQ), lambda b, g: (b, g, 0, 0, 0))
    outs = []
    for j in range(SUB):
        per_tile = lambda spec: [spec(j + t * SUB) for t in range(ATT_TILES)]
        q_spec = lambda i: pl.BlockSpec((1, 1, 1, A_HEAD_DIM, A_HPG * TQ), lambda b, g: (b, g, i, 0, 0))
        bc_spec = lambda i: pl.BlockSpec((1, 1, n_cmp, A_HPG * TQ), lambda b, g: (i, g, 0, 0))
        gt_spec = lambda i: pl.BlockSpec((1, 1, 1, GATE_ROWS, TQ), lambda b, g: (b, g, i, 0, 0))
        outs.append(pl.pallas_call(
            functools.partial(_attn_kernel, j=j),
            grid=(bsz, A_KV_GROUPS),
            in_specs=(per_tile(q_spec)
                      + [pl.BlockSpec((1, 1, n_cmp, A_HEAD_DIM), lambda b, g: (b, g, 0, 0)),
                         pl.BlockSpec((1, 1, A_HEAD_DIM, n_cmp), lambda b, g: (b, g, 0, 0)),
                         k_spec, vt_spec, k_spec, vt_spec]
                      + per_tile(bc_spec)
                      + [pl.BlockSpec((1, N_BIAS_TILES, TQ, A_HPG * TQ), lambda b, g: (g, 0, 0, 0))]
                      + per_tile(gt_spec)),
            out_specs=pl.BlockSpec((1, ATT_TILES, TQ, A_HPG * A_HEAD_DIM), lambda b, g: (b, 0, 0, g)),
            out_shape=jax.ShapeDtypeStruct((bsz, ATT_TILES, TQ, A_WIDTH), BF16),
            compiler_params=pltpu.CompilerParams(dimension_semantics=("parallel", "parallel"),
                                                 vmem_limit_bytes=VMEM_LIMIT),
            name=f"attn{j}",
        )(*([q_t] * ATT_TILES), kc, vc_t, ks, vs_t, kw, vw_t, *([bias_c] * ATT_TILES), bias_d,
          *([gates_t] * ATT_TILES)))
    return outs


def _rwkv_kernel(rw_ref, ld_ref, lw_ref, lb_ref, o_ref, state_ref):
    cc = pl.program_id(1)
    n = B_HEAD_DIM
    nb, csz = rw_ref.shape[0], rw_ref.shape[1]

    @pl.when(cc == 0)
    def _():
        state_ref[...] = jnp.zeros(state_ref.shape, F32)

    ti = lax.broadcasted_iota(jnp.int32, (csz, LANES), 0)
    si = lax.broadcasted_iota(jnp.int32, (csz, LANES), 1) % n
    lower = si <= ti
    strict = si < ti
    eye = jnp.where(si == ti, 1.0, 0.0)
    tri = jnp.where(lax.broadcasted_iota(jnp.int32, (csz, csz), 1) <= lax.broadcasted_iota(jnp.int32, (csz, csz), 0),
                    1.0, 0.0).astype(BF16)
    n_pairs = B_WIDTH // LANES
    left =lax.broadcasted_iota(jnp.int32, (csz, LANES), 1) < n
    row_left = lax.broadcasted_iota(jnp.int32, (LANES, LANES), 0) < n
    same_head = row_left == (lax.broadcasted_iota(jnp.int32, (LANES, LANES), 1) < n)

    def blockdiag(y):
        zero = jnp.zeros_like(y)
        return jnp.concatenate([jnp.where(left, y, zero), jnp.where(left, zero, y)], axis=0)

    chains = []
    for bi in range(nb):
        r, kk, b, k_mod, v = (rw_ref[bi, :, m * B_WIDTH:(m + 1) * B_WIDTH].astype(F32) for m in range(5))
        ld = ld_ref[bi, :, 0:B_WIDTH]
        rsum = ld_ref[bi, :, B_WIDTH:2 * B_WIDTH]

        ld_hi, ld_lo = _split2(ld)
        cum = jnp.dot(tri, ld_hi, preferred_element_type=F32) + jnp.dot(tri, ld_lo, preferred_element_type=F32)
        g_inc = jnp.exp(cum)
        g_exc = jnp.exp(cum - ld)
        g_inv = jnp.exp(-cum)
        g_end = jnp.exp(cum[csz - 1:csz, :] - cum)
        g_all = g_inc[csz - 1:csz, :]

        for pr in range(n_pairs):
            sl = slice(pr * LANES, (pr + 1) * LANES)
            kk_p = kk[:, sl]
            b_p = b[:, sl]
            bt = (b_p * g_inv[:, sl]).astype(BF16)
            kt = (k_mod[:, sl] * g_inv[:, sl]).astype(BF16)
            ch = dict(
                idx=bi * n_pairs + pr,
                v=v[:, sl],
                lhs=jnp.concatenate([-kk_p * g_exc[:, sl], r[:, sl] * g_inc[:, sl]], axis=0).astype(BF16),
                rhs=jnp.concatenate([blockdiag(bt), blockdiag(kt)], axis=0),
                bk=jnp.concatenate([b_p * g_end[:, sl], k_mod[:, sl] * g_end[:, sl]], axis=0).astype(BF16),
                g_all=g_all[:, sl],
                bonus=rsum[:, sl] * v[:, sl],
            )
            chains.append(ch)

    for ch in chains:
        x = _dot_nt(ch["lhs"], ch["rhs"])
        xb, xk = x[:, :LANES], x[:, LANES:]
        ch["a_ab"] = jnp.where(strict, xb[:csz], 0.0)
        a_ak = jnp.where(strict, xk[:csz], 0.0)
        m_rk = jnp.where(lower, xk[csz:], 0.0)
        ch["ak_rk"] = jnp.concatenate([a_ak, m_rk], axis=0).astype(BF16)
        ch["m_rb"] = jnp.where(lower, xb[csz:], 0.0).astype(BF16)
    for ch in chains:
        akv = _dot(ch["ak_rk"], blockdiag(ch["v"].astype(BF16)))
        ch["akv"], ch["mrkv"] = akv[:csz], akv[csz:]
        ch["tinv"] = eye + ch["a_ab"]
        ch["pw"] = ch["a_ab"].astype(BF16)
    n_sq = int(math.log2(csz)) - 1
    for ch in chains:
        ch["pw"] = _dot(ch["pw"], blockdiag(ch["pw"])).astype(BF16)
    for step in range(n_sq):
        for ch in chains:
            if step + 1 < n_sq:
                both = _dot(jnp.concatenate([ch["pw"], ch["tinv"].astype(BF16)], axis=0), blockdiag(ch["pw"]))
                ch["tinv"] = ch["tinv"] + both[csz:]
                ch["pw"] = both[:csz].astype(BF16)
            else:
                ch["tinv"] = ch["tinv"] + _dot(ch["tinv"], blockdiag(ch["pw"]))
    for ch in chains:
        ch["s0"] = state_ref[ch["idx"]]
        ch["as0"] = _dot_nt(ch["lhs"], ch["s0"])
    for ch in chains:
        w = (ch["as0"][:csz] + ch["akv"]).astype(BF16)
        ch["u"] = _dot(ch["tinv"], blockdiag(w))
    outs = []
    for ch in chains:
        u = ch["u"]
        y = ch["as0"][csz:] + _dot(ch["m_rb"], blockdiag(u.astype(BF16))) + ch["mrkv"]
        uv = jnp.concatenate([u, ch["v"]], axis=0)
        state_ref[ch["idx"]] = ch["s0"] * ch["g_all"] + jnp.where(same_head, _dot_tn(uv, ch["bk"]), 0.0)
        yc = y - _head_sum(y) * (1.0 / n)
        var = _head_sum(yc * yc) * (1.0 / n)
        outs.append(yc * lax.rsqrt(var + LNX_EPS))
    for bi in range(nb):
        yn = jnp.concatenate(outs[bi * n_pairs:(bi + 1) * n_pairs], axis=-1)
        bonus = jnp.concatenate([ch["bonus"] for ch in chains[bi * n_pairs:(bi + 1) * n_pairs]], axis=-1)
        o_ref[bi] = (yn * lw_ref[...] + lb_ref[...] + bonus).astype(BF16)


RWKV_NB = 8


def _rwkv(scan_bf16, scan_f32, ln_w, ln_b):
    bsz, s, _ = scan_bf16.shape
    nb = RWKV_NB if bsz % RWKV_NB == 0 else 1
    const = lambda b, c: (0, 0)
    vec = pl.BlockSpec((1, B_WIDTH), const)
    return pl.pallas_call(
        _rwkv_kernel,
        grid=(bsz // nb, s // CHUNK),
        in_specs=[pl.BlockSpec((nb, CHUNK, SCAN_BF16_COLS), lambda b, c: (b, c, 0)),
                  pl.BlockSpec((nb, CHUNK, SCAN_F32_COLS), lambda b, c: (b, c, 0)),
                  vec, vec],
        out_specs=pl.BlockSpec((nb, CHUNK, B_WIDTH), lambda b, c: (b, c, 0)),
        out_shape=jax.ShapeDtypeStruct((bsz, s, B_WIDTH), BF16),
        scratch_shapes=[pltpu.VMEM((nb * B_WIDTH // LANES, LANES, LANES), F32)],
        compiler_params=pltpu.CompilerParams(dimension_semantics=("parallel", "arbitrary")),
        name="rwkv",
    )(scan_bf16, scan_f32, ln_w, ln_b)


def _final_kernel(x_ref, *refs):
    ya_refs, (yb_ref, cf_ref, gate_ref, wa_ref, wb_ref, wo_ref, o_ref) = refs[:SUB], refs[SUB:]
    a_silu = cf_ref[0, :, 0:A_WIDTH].astype(F32)
    b_silu = cf_ref[0, :, A_WIDTH:A_WIDTH + B_WIDTH].astype(F32)
    merge_a = cf_ref[0, :, A_WIDTH + B_WIDTH:A_WIDTH + B_WIDTH + D_MODEL].astype(F32)
    merge_b = cf_ref[0, :, A_WIDTH + B_WIDTH + D_MODEL:A_WIDTH + B_WIDTH + 2 * D_MODEL].astype(F32)
    y_a = jnp.concatenate([r[0, 0] for r in ya_refs], axis=0)
    ya = y_a.astype(F32) * (a_silu * _sigmoid(a_silu))
    yb = yb_ref[0].astype(F32) * (b_silu * _sigmoid(b_silu))
    merged = _sigmoid(merge_a) * _dot(ya, wa_ref[...]) + _sigmoid(merge_b) * _dot(yb, wb_ref[...])
    o_ref[0] = x_ref[0] + gate_ref[0] * _dot(merged, wo_ref[...])


def _final(x, y_a, y_b, cols_fin, gate, w_out_a, w_out_b, w_o):
    bsz, s, _ = x.shape
    tm = SUB * TQ
    const = lambda b, i: (0, 0)
    row = lambda w: pl.BlockSpec((1, tm, w), lambda b, i: (b, i, 0))
    ya_spec = pl.BlockSpec((1, 1, TQ, A_WIDTH), lambda b, i: (b, i, 0, 0))
    return pl.pallas_call(
        _final_kernel,
        grid=(bsz, s // tm),
        in_specs=[row(D_MODEL)] + [ya_spec] * SUB + [row(B_WIDTH), row(FIN_COLS),
                  pl.BlockSpec((1, 1, D_MODEL), lambda b, i: (b, 0, 0)),
                  pl.BlockSpec((A_WIDTH, D_MODEL), const),
                  pl.BlockSpec((B_WIDTH, D_MODEL), const),
                  pl.BlockSpec((D_MODEL, D_MODEL), const)],
        out_specs=row(D_MODEL),
        out_shape=jax.ShapeDtypeStruct((bsz, s, D_MODEL), F32),
        compiler_params=pltpu.CompilerParams(dimension_semantics=("parallel", "parallel"),
                                             vmem_limit_bytes=VMEM_LIMIT),
        name="final",
    )(x, *y_a, y_b, cols_fin, gate, w_out_a, w_out_b, w_o)


def _split_w_in(w_in):
    nsa_in = 2 * A_WIDTH + 6 * A_KV_WIDTH + 3 * A_HEADS
    o_gate = A_WIDTH + 6 * A_KV_WIDTH
    o_asilu = o_gate + 3 * A_HEADS
    o_shift = nsa_in
    o_rest = nsa_in + RWKV_COLS
    gate_w = w_in[:, o_gate:o_asilu].reshape(D_MODEL, 3, A_KV_GROUPS, A_HPG)
    gate_w = gate_w.transpose(0, 2, 1, 3).reshape(D_MODEL, A_KV_GROUPS, 3 * A_HPG)
    gate_w = jnp.pad(gate_w, ((0, 0), (0, 0), (0, A_HEAD_DIM - 3 * A_HPG))).reshape(D_MODEL, GATE_PAD)
    w_nsa = jnp.concatenate([w_in[:, :o_gate], gate_w], axis=1)
    w_fin = jnp.concatenate([w_in[:, o_asilu:o_shift], w_in[:, o_rest:]], axis=1)
    w_rwkv = w_in[:, o_shift:o_rest]
    return w_nsa.astype(BF16), w_fin.astype(BF16), w_rwkv.astype(BF16)


def _layer(x, c, rel_bias, w_ada, b_ada, norm_gain, w_in, q_norm_gain, k_norm_gain,
           cmp_pos_k, cmp_pos_v, cmp_k_w1, cmp_k_w2, cmp_v_w1, cmp_v_w2,
           shift_mu, w0, w_lora_up, a0, a_lora_up, k_k, k_a, r_k, ln_x_w, ln_x_b,
           w_out_a, w_out_b, w_o):
    bsz, s, _ = x.shape
    assert s % (2 * TQ) == 0 and s // CMP_STRIDE == LANES
    n16 = s // CMP_STRIDE
    mod = _ada(c, w_ada, b_ada)
    w_nsa, w_fin, w_rwkv = _split_w_in(w_in)
    scale = A_HEAD_DIM ** -0.5 * LOG2E
    qg = jnp.tile(q_norm_gain, A_HEADS) * scale
    ksg = jnp.tile(k_norm_gain[1], A_KV_GROUPS)
    kwg = jnp.tile(k_norm_gain[2], A_KV_GROUPS)
    vec = lambda t: t.reshape(1, -1)
    rwkv_params = (vec(shift_mu), vec(w0), w_lora_up.astype(BF16), vec(a0), a_lora_up.astype(BF16),
                   vec(k_k), vec(k_a), vec(r_k))
    q_t, ks, vs_t, kw, vw_t, gates_t, ck, cols_fin, scan_bf16, scan_f32 = _proj(
        x, mod, norm_gain, w_nsa, w_fin, w_rwkv, qg, ksg, kwg, rwkv_params)

    kc, vc_t = _compress(ck, _expand_cmp_pos(cmp_pos_k), _expand_cmp_pos(cmp_pos_v),
                         _expand_cmp_w1(cmp_k_w1), cmp_k_w2.astype(BF16),
                         _expand_cmp_w1(cmp_v_w1), cmp_v_w2.T.astype(BF16),
                         k_norm_gain[0].reshape(1, A_HEAD_DIM))
    bias_c, bias_d = _bias_tables(rel_bias, s, n16)
    y_a = _attention(q_t, kc, vc_t, ks, vs_t, kw, vw_t, bias_c, bias_d, gates_t)

    y_b = _rwkv(scan_bf16, scan_f32, vec(ln_x_w), vec(ln_x_b))

    gate = mod[:, 2 * D_MODEL:].reshape(bsz, 1, D_MODEL)
    return _final(x, y_a, y_b, cols_fin, gate, w_out_a.astype(BF16), w_out_b.astype(BF16), w_o.astype(BF16))


def kernel(x, c, w_ada, b_ada, norm_gain, w_in, q_norm_gain, k_norm_gain, cmp_pos_k, cmp_pos_v, cmp_k_w1, cmp_k_w2, cmp_v_w1, cmp_v_w2, rel_bias, shift_mu, w0, w_lora_up, a0, a_lora_up, k_k, k_a, r_k, ln_x_w, ln_x_b, w_out_a, w_out_b, w_o):
    for l in range(w_in.shape[0]):
        x = _layer(x, c, rel_bias, w_ada[l], b_ada[l], norm_gain[l], w_in[l], q_norm_gain[l], k_norm_gain[l],
                   cmp_pos_k[l], cmp_pos_v[l], cmp_k_w1[l], cmp_k_w2[l], cmp_v_w1[l], cmp_v_w2[l],
                   shift_mu[l], w0[l], w_lora_up[l], a0[l], a_lora_up[l], k_k[l], k_a[l], r_k[l],
                   ln_x_w[l], ln_x_b[l], w_out_a[l], w_out_b[l], w_o[l])
    return x
```

```python
import functools
import math

import numpy as np
import jax
import jax.numpy as jnp
from jax import lax
from jax.experimental import pallas as pl
from jax.experimental.pallas import tpu as pltpu

F32 = jnp.float32
BF16 = jnp.bfloat16

D_MODEL = 1024
A_HEADS = 8
A_HEAD_DIM = 64
A_KV_GROUPS = 2
A_HPG = A_HEADS // A_KV_GROUPS
A_WIDTH = A_HEADS * A_HEAD_DIM
A_KV_WIDTH = A_KV_GROUPS * A_HEAD_DIM
CMP_BLOCK = 32
CMP_STRIDE = 16
CMP_HIDDEN = 256
SLC_BLOCK = 64
SLC_TOPN = 16
WINDOW = 512
B_HEADS = 8
B_HEAD_DIM = 64
B_WIDTH = B_HEADS * B_HEAD_DIM
DECAY_LORA = 64
ICLR_LORA = 64
LNX_EPS = 64e-5
REL_BUCKETS = 32
REL_MAX_EXACT = 16
REL_MAX_DIST = 128
NORM_EPS = 1e-6
NEG_INF = -1e30
FORCE_SCORE = 1e30

LANES = 128
TQ = 128
CHUNK = 64
GATE_PAD = LANES
LOG2E = math.log2(math.e)
V_ROWS = A_HEAD_DIM + 16
GATE_ROWS = 16
NSA_COLS = A_WIDTH + 6 * A_KV_WIDTH + GATE_PAD
FIN_COLS = A_WIDTH + B_WIDTH + 2 * D_MODEL
RWKV_COLS = 3 * B_WIDTH + DECAY_LORA + ICLR_LORA
SCAN_BF16_COLS = 5 * B_WIDTH
SCAN_F32_COLS = 2 * B_WIDTH
VMEM_LIMIT = 56 * 1024 * 1024


def _dot(a, b):
    return jnp.dot(a.astype(BF16), b.astype(BF16), preferred_element_type=F32)


def _dot_nt(a, b):
    return lax.dot_general(a.astype(BF16), b.astype(BF16), (((1,), (1,)), ((), ())),
                           preferred_element_type=F32)


def _dot_tn(a, b):
    return lax.dot_general(a.astype(BF16), b.astype(BF16), (((0,), (0,)), ((), ())),
                           preferred_element_type=F32)


def _split2(x):
    hi = x.astype(BF16)
    lo = (x - hi.astype(F32)).astype(BF16)
    return hi, lo


def _split3(x):
    h1 = x.astype(BF16)
    r1 = x - h1.astype(F32)
    h2 = r1.astype(BF16)
    h3 = (r1 - h2.astype(F32)).astype(BF16)
    return h1, h2, h3


def _sigmoid(x):
    return 1.0 / (1.0 + jnp.exp(-x))


def _bucket_thresholds():
    n = np.arange(0, 4096)
    nf = np.maximum(n, REL_MAX_EXACT).astype(np.float64)
    val = np.log(nf / REL_MAX_EXACT) / math.log(REL_MAX_DIST / REL_MAX_EXACT) * (REL_BUCKETS - REL_MAX_EXACT)
    frac = np.abs(val - np.round(val))
    assert np.all((frac > 1e-4) | (n <= REL_MAX_EXACT) | (n >= REL_MAX_DIST))
    large = REL_MAX_EXACT + np.floor(val + 1e-9).astype(np.int64)
    bucket = np.where(n < REL_MAX_EXACT, n, np.minimum(large, REL_BUCKETS - 1))
    return [int(np.argmax(bucket >= j)) for j in range(REL_BUCKETS)]


_BUCKET_TH = _bucket_thresholds()


def _bias_from_dist(dist, tbl_ref, head):
    val = jnp.full(dist.shape, tbl_ref[0, head], F32)
    for j in range(1, REL_BUCKETS):
        val = jnp.where(dist >= _BUCKET_TH[j], tbl_ref[j, head], val)
    return val


def _ada_kernel(c_ref, w_ref, b_ref, o_ref):
    c = c_ref[...]
    o_ref[...] = _dot(c * _sigmoid(c), w_ref[...]) + b_ref[...]


def _ada(c, w_ada, b_ada):
    bsz = c.shape[0]
    return pl.pallas_call(
        _ada_kernel,
        grid=(3,),
        in_specs=[pl.BlockSpec((bsz, D_MODEL), lambda j: (0, 0)),
                  pl.BlockSpec((D_MODEL, D_MODEL), lambda j: (0, j)),
                  pl.BlockSpec((1, D_MODEL), lambda j: (0, j))],
        out_specs=pl.BlockSpec((bsz, D_MODEL), lambda j: (0, j)),
        out_shape=jax.ShapeDtypeStruct((bsz, 3 * D_MODEL), F32),
        name="ada",
    )(c, w_ada, b_ada.reshape(1, 3 * D_MODEL))


def _norm_rows(x_t, gain_col, n_seg):
    out = []
    for seg in range(n_seg):
        blk = x_t[seg * A_HEAD_DIM:(seg + 1) * A_HEAD_DIM, :]
        ms = jnp.mean(blk * blk, axis=0, keepdims=True)
        out.append(blk * lax.rsqrt(ms + NORM_EPS) * gain_col[seg * A_HEAD_DIM:(seg + 1) * A_HEAD_DIM, :])
    return out


def _head_sum(x):
    left = lax.broadcasted_iota(jnp.int32, x.shape, x.ndim - 1) < B_HEAD_DIM
    lo = jnp.sum(jnp.where(left, x, 0.0), axis=-1, keepdims=True)
    hi = jnp.sum(jnp.where(left, 0.0, x), axis=-1, keepdims=True)
    return jnp.where(left, lo, hi)


def _proj_kernel(x_ref, mod_ref, g_ref, wm_ref, wf_ref, qg_ref, ksg_ref, kwg_ref,
                 mu_ref, w0_ref, wl_ref, a0_ref, al_ref, kk_ref, ka_ref, rk_ref,
                 q_ref, ks_ref, vs_ref, kw_ref, vw_ref, gt_ref, ck_ref, of_ref, rw_ref, ld_ref, prev_ref):
    tm = x_ref.shape[1]

    @pl.when(pl.program_id(1) == 0)
    def _():
        prev_ref[...] = jnp.zeros(prev_ref.shape, F32)

    x = x_ref[0]
    ms = jnp.mean(x * x, axis=-1, keepdims=True)
    y = x * lax.rsqrt(ms + NORM_EPS) * g_ref[...]
    mod = mod_ref[0]
    h = (y * (1.0 + mod[:, D_MODEL:2 * D_MODEL]) + mod[:, :D_MODEL]).astype(BF16)
    crn = jnp.dot(h, wm_ref[...], preferred_element_type=F32)
    cr = crn[:, 0:RWKV_COLS]
    cn = crn[:, RWKV_COLS:RWKV_COLS + NSA_COLS]
    ck_ref[0] = cn[:, A_WIDTH:A_WIDTH + 2 * A_KV_WIDTH]

    rolled = pltpu.roll(cr, 1, axis=0)
    row8 = lax.broadcasted_iota(jnp.int32, (8, RWKV_COLS), 0)
    prev = jnp.concatenate([jnp.where(row8 == 0, prev_ref[0:1, :], rolled[0:8]), rolled[8:]], axis=0)
    prev_ref[0:1, :] = cr[tm - 1:tm, :]
    xs = cr + (prev - cr) * mu_ref[...]
    r = xs[:, 0:B_WIDTH]
    k = xs[:, B_WIDTH:2 * B_WIDTH]
    v = xs[:, 2 * B_WIDTH:3 * B_WIDTH]
    wd = xs[:, 3 * B_WIDTH:3 * B_WIDTH + DECAY_LORA]
    ad = xs[:, 3 * B_WIDTH + DECAY_LORA:3 * B_WIDTH + DECAY_LORA + ICLR_LORA]
    ld = -math.exp(-0.5) * _sigmoid(w0_ref[...] + _dot(jnp.tanh(wd), wl_ref[...]))
    a = _sigmoid(a0_ref[...] + _dot(ad, al_ref[...]))
    kk = k * kk_ref[...]
    k_mod = k * (1.0 + (a - 1.0) * ka_ref[...])
    rkr = r * k_mod * rk_ref[...]
    ld_ref[0, :, 0:B_WIDTH] = ld
    rw_ref[0, :, 0:B_WIDTH] = r.astype(BF16)
    rw_ref[0, :, 3 * B_WIDTH:4 * B_WIDTH] = k_mod.astype(BF16)
    rw_ref[0, :, 4 * B_WIDTH:5 * B_WIDTH] = v.astype(BF16)
    for pr in range(B_WIDTH // LANES):
        sl = slice(pr * LANES, (pr + 1) * LANES)
        kk_p = kk[:, sl]
        kk_p = kk_p * lax.rsqrt(jnp.maximum(_head_sum(kk_p * kk_p), 1e-24))
        rw_ref[0, :, B_WIDTH + pr * LANES:B_WIDTH + (pr + 1) * LANES] = kk_p.astype(BF16)
        rw_ref[0, :, 2 * B_WIDTH + pr * LANES:2 * B_WIDTH + (pr + 1) * LANES] = (kk_p * a[:, sl]).astype(BF16)
        ld_ref[0, :, B_WIDTH + pr * LANES:B_WIDTH + (pr + 1) * LANES] = _head_sum(rkr[:, sl])

    lane = lax.broadcasted_iota(jnp.int32, (TQ, LANES), 1)
    row = lax.broadcasted_iota(jnp.int32, (TQ, LANES), 0)
    ones_rows = (lax.broadcasted_iota(jnp.int32, (V_ROWS - A_HEAD_DIM, TQ), 0) == 0).astype(BF16)
    off = A_WIDTH + 2 * A_KV_WIDTH
    for sub in range(tm // TQ):
        c = cn[sub * TQ:(sub + 1) * TQ]
        q_heads = _norm_rows(c[:, 0:A_WIDTH].T, qg_ref[...], A_HEADS)
        ks_t = jnp.concatenate(_norm_rows(c[:, off:off + A_KV_WIDTH].T, ksg_ref[...], A_KV_GROUPS), axis=0)
        kw_t = jnp.concatenate(_norm_rows(c[:, off + 2 * A_KV_WIDTH:off + 3 * A_KV_WIDTH].T, kwg_ref[...],
                                          A_KV_GROUPS), axis=0)
        ksn = ks_t.T
        kwn = kw_t.T
        vs_t = c[:, off + A_KV_WIDTH:off + 2 * A_KV_WIDTH].T.astype(BF16)
        vw_t = c[:, off + 3 * A_KV_WIDTH:off + 4 * A_KV_WIDTH].T.astype(BF16)
        gates_t = _sigmoid(c[:, off + 4 * A_KV_WIDTH:off + 5 * A_KV_WIDTH]).T
        blk = (pl.program_id(1) * tm + sub * TQ + row) // SLC_BLOCK
        onehot = jnp.where(lane - A_HEAD_DIM == blk, 1.0, 0.0)
        for g in range(A_KV_GROUPS):
            q_ref[0, g, sub] = jnp.concatenate(q_heads[g * A_HPG:(g + 1) * A_HPG], axis=1).astype(BF16)
            sl = slice(g * A_HEAD_DIM, (g + 1) * A_HEAD_DIM)
            k_g = ksn if g == 0 else pltpu.roll(ksn, A_HEAD_DIM, axis=1)
            ks_ref[0, g, sub * TQ:(sub + 1) * TQ, :] = jnp.where(lane < A_HEAD_DIM, k_g, onehot).astype(BF16)
            kw_g = kwn if g == 0 else pltpu.roll(kwn, A_HEAD_DIM, axis=1)
            kw_ref[0, g, sub * TQ:(sub + 1) * TQ, :] = jnp.where(lane < A_HEAD_DIM, kw_g, 0.0).astype(BF16)
            vs_ref[0, g, sub] = jnp.concatenate([vs_t[sl, :], ones_rows], axis=0)
            vw_ref[0, g, sub] = jnp.concatenate([vw_t[sl, :], ones_rows], axis=0)
            gt_ref[0, g, sub] = gates_t[g * A_HEAD_DIM:g * A_HEAD_DIM + GATE_ROWS, :]
    of_ref[0] = jnp.dot(h, wf_ref[...], preferred_element_type=F32).astype(BF16)


def _proj(x, mod, norm_gain, w_mix, w_fin, qg, ksg, kwg, rwkv_params, tm=512):
    bsz, s, _ = x.shape
    nt, nsub = s // TQ, tm // TQ
    assert A_HEAD_DIM + s // SLC_BLOCK <= LANES and A_KV_WIDTH == LANES and 2 * B_HEAD_DIM == LANES
    const = lambda b, i: (0, 0)
    weight = lambda cols: pl.BlockSpec((D_MODEL, cols), const, pipeline_mode=pl.Buffered(1))
    whole = lambda t: pl.BlockSpec(t.shape, const)
    col = lambda t: jnp.broadcast_to(t.reshape(-1, 1), (t.size, LANES))
    k_spec = lambda width: pl.BlockSpec((1, A_KV_GROUPS, tm, width), lambda b, i: (b, 0, i, 0))
    k_shape = lambda width: jax.ShapeDtypeStruct((bsz, A_KV_GROUPS, s, width), BF16)
    tile_spec = lambda r, c: pl.BlockSpec((1, A_KV_GROUPS, nsub, r, c), lambda b, i: (b, 0, i, 0, 0))
    tile_shape = lambda r, c, dt: jax.ShapeDtypeStruct((bsz, A_KV_GROUPS, nt, r, c), dt)
    return pl.pallas_call(
        _proj_kernel,
        grid=(bsz, s // tm),
        in_specs=[pl.BlockSpec((1, tm, D_MODEL), lambda b, i: (b, i, 0)),
                  pl.BlockSpec((1, 1, 3 * D_MODEL), lambda b, i: (b, 0, 0)),
                  pl.BlockSpec((1, D_MODEL), const),
                  weight(RWKV_COLS + NSA_COLS), weight(FIN_COLS),
                  pl.BlockSpec((A_WIDTH, LANES), const),
                  pl.BlockSpec((A_KV_WIDTH, LANES), const),
                  pl.BlockSpec((A_KV_WIDTH, LANES), const)] + [whole(t) for t in rwkv_params],
        out_specs=[tile_spec(A_HEAD_DIM, A_HPG * TQ),
                   k_spec(LANES), tile_spec(V_ROWS, TQ), k_spec(LANES), tile_spec(V_ROWS, TQ),
                   tile_spec(GATE_ROWS, TQ),
                   pl.BlockSpec((1, tm, 2 * A_KV_WIDTH), lambda b, i: (b, i, 0)),
                   pl.BlockSpec((1, tm, FIN_COLS), lambda b, i: (b, i, 0)),
                   pl.BlockSpec((1, tm, SCAN_BF16_COLS), lambda b, i: (b, i, 0)),
                   pl.BlockSpec((1, tm, SCAN_F32_COLS), lambda b, i: (b, i, 0))],
        out_shape=[tile_shape(A_HEAD_DIM, A_HPG * TQ, BF16),
                   k_shape(LANES), tile_shape(V_ROWS, TQ, BF16), k_shape(LANES), tile_shape(V_ROWS, TQ, BF16),
                   tile_shape(GATE_ROWS, TQ, F32),
                   jax.ShapeDtypeStruct((bsz, s, 2 * A_KV_WIDTH), F32),
                   jax.ShapeDtypeStruct((bsz, s, FIN_COLS), BF16),
                   jax.ShapeDtypeStruct((bsz, s, SCAN_BF16_COLS), BF16),
                   jax.ShapeDtypeStruct((bsz, s, SCAN_F32_COLS), F32)],
        scratch_shapes=[pltpu.VMEM((8, RWKV_COLS), F32)],
        compiler_params=pltpu.CompilerParams(dimension_semantics=("parallel", "arbitrary"),
                                             vmem_limit_bytes=VMEM_LIMIT),
        name="proj",
    )(x, mod.reshape(bsz, 1, 3 * D_MODEL), norm_gain.reshape(1, D_MODEL), w_mix, w_fin,
      col(qg), col(ksg), col(kwg), *rwkv_params)


def _compress_kernel(ck_ref, cv_ref, pk_ref, pv_ref, w1k_ref, w2k_ref, w1v_ref, w2v_ref, kg_ref, kc_ref, vc_ref):
    n16 = ck_ref.shape[1] // CMP_STRIDE

    def rows16(ref):
        return jnp.concatenate([ref[0, pl.ds(p, n16, stride=CMP_STRIDE), :] for p in range(CMP_STRIDE)], axis=1)

    def hidden(z, pos_ref, w1_ref, g):
        top = _dot(z + pos_ref[0:1, :], w1_ref[g, 0])
        bot = _dot(z + pos_ref[1:2, :], w1_ref[g, 1])
        return jax.nn.gelu(top + pltpu.roll(bot, n16 - 1, axis=0), approximate=True)

    zk = rows16(ck_ref)
    zv = rows16(cv_ref)
    for g in range(A_KV_GROUPS):
        kc = _dot(hidden(zk, pk_ref, w1k_ref, g), w2k_ref[...])
        ms = jnp.mean(kc * kc, axis=-1, keepdims=True)
        kc_ref[0, g] = (kc * lax.rsqrt(ms + NORM_EPS) * kg_ref[...]).astype(BF16)
        vc_ref[0, g] = _dot_nt(w2v_ref[...], hidden(zv, pv_ref, w1v_ref, g)).astype(BF16)


def _expand_cmp_w1(w1):
    w = w1.reshape(2, CMP_STRIDE, 1, A_HEAD_DIM, CMP_HIDDEN)
    per_group = []
    for g in range(A_KV_GROUPS):
        pad = [(0, 0), (0, 0), (g, A_KV_GROUPS - 1 - g), (0, 0), (0, 0)]
        per_group.append(jnp.pad(w, pad).reshape(2, CMP_STRIDE * A_KV_WIDTH, CMP_HIDDEN))
    return jnp.stack(per_group).astype(BF16)


def _expand_cmp_pos(pos):
    p = jnp.broadcast_to(pos.reshape(2, CMP_STRIDE, 1, A_HEAD_DIM), (2, CMP_STRIDE, A_KV_GROUPS, A_HEAD_DIM))
    return p.reshape(2, CMP_STRIDE * A_KV_WIDTH)


def _compress(ck, pk, pv, w1k, w2k, w1v, w2v_t, kg):
    bsz, s, _ = ck.shape
    n16 = s // CMP_STRIDE
    zw = CMP_STRIDE * A_KV_WIDTH
    const = lambda b: (0, 0)
    const4 = lambda b: (0, 0, 0, 0)
    return pl.pallas_call(
        _compress_kernel,
        grid=(bsz,),
        in_specs=[pl.BlockSpec((1, s, A_KV_WIDTH), lambda b: (b, 0, 0)),
                  pl.BlockSpec((1, s, A_KV_WIDTH), lambda b: (b, 0, 1)),
                  pl.BlockSpec((2, zw), const), pl.BlockSpec((2, zw), const),
                  pl.BlockSpec((A_KV_GROUPS, 2, zw, CMP_HIDDEN), const4), pl.BlockSpec((CMP_HIDDEN, A_HEAD_DIM), const),
                  pl.BlockSpec((A_KV_GROUPS, 2, zw, CMP_HIDDEN), const4), pl.BlockSpec((A_HEAD_DIM, CMP_HIDDEN), const),
                  pl.BlockSpec((1, A_HEAD_DIM), const)],
        out_specs=[pl.BlockSpec((1, A_KV_GROUPS, n16, A_HEAD_DIM), lambda b: (b, 0, 0, 0)),
                   pl.BlockSpec((1, A_KV_GROUPS, A_HEAD_DIM, n16), lambda b: (b, 0, 0, 0))],
        out_shape=[jax.ShapeDtypeStruct((bsz, A_KV_GROUPS, n16, A_HEAD_DIM), BF16),
                   jax.ShapeDtypeStruct((bsz, A_KV_GROUPS, A_HEAD_DIM, n16), BF16)],
        compiler_params=pltpu.CompilerParams(dimension_semantics=("parallel",)),
        name="compress",
    )(ck, ck, pk, pv, w1k, w2k, w1v, w2v_t, kg)


TILE_FAR, TILE_EDGE, N_BIAS_TILES = 2, 3, 4
SUB = 4
ATT_TILES = 4


def _bias_cmp_kernel(tbl_ref, o_ref):
    i = pl.program_id(0)
    g = pl.program_id(1)
    n_cmp = o_ref.shape[2]
    n = lax.broadcasted_iota(jnp.int32, (n_cmp, TQ), 0)
    q = lax.broadcasted_iota(jnp.int32, (n_cmp, TQ), 1)
    dist = i * TQ + q - (n * CMP_STRIDE + CMP_BLOCK - 1)
    for h in range(A_HPG):
        bias = _bias_from_dist(dist, tbl_ref, g * A_HPG + h)
        o_ref[0, 0, :, h * TQ:(h + 1) * TQ] = jnp.where(dist >= 0, bias * LOG2E, NEG_INF)


def _bias_toeplitz_kernel(tbl_ref, o_ref):
    g = pl.program_id(0)
    r = pl.program_id(1)
    off = jnp.where(r == TILE_EDGE, WINDOW // TQ, r)
    k = lax.broadcasted_iota(jnp.int32, (TQ, TQ), 0)
    q = lax.broadcasted_iota(jnp.int32, (TQ, TQ), 1)
    dist = off * TQ + q - k
    valid = (dist >= 0) & (dist < WINDOW)
    for h in range(A_HPG):
        head = g * A_HPG + h
        bias = _bias_from_dist(dist, tbl_ref, head) - tbl_ref[REL_BUCKETS - 1, head]
        o_ref[0, 0, :, h * TQ:(h + 1) * TQ] = jnp.where(valid, bias * LOG2E, NEG_INF)


def _bias_tables(rel_bias, s, n_cmp):
    smem = pl.BlockSpec(memory_space=pltpu.SMEM)
    nt = s // TQ
    bias_c = pl.pallas_call(
        _bias_cmp_kernel,
        grid=(nt, A_KV_GROUPS),
        in_specs=[smem],
        out_specs=pl.BlockSpec((1, 1, n_cmp, A_HPG * TQ), lambda i, g: (i, g, 0, 0)),
        out_shape=jax.ShapeDtypeStruct((nt, A_KV_GROUPS, n_cmp, A_HPG * TQ), F32),
        name="bias_cmp",
    )(rel_bias)
    assert _BUCKET_TH[REL_BUCKETS - 1] <= TQ + 1 and WINDOW // TQ >= 3
    bias_d = pl.pallas_call(
        _bias_toeplitz_kernel,
        grid=(A_KV_GROUPS, N_BIAS_TILES),
        in_specs=[smem],
        out_specs=pl.BlockSpec((1, 1, TQ, A_HPG * TQ), lambda g, r: (g, r, 0, 0)),
        out_shape=jax.ShapeDtypeStruct((A_KV_GROUPS, N_BIAS_TILES, TQ, A_HPG * TQ), F32),
        name="bias_toeplitz",
    )(rel_bias)
    return bias_c, bias_d


def _attn_kernel(*refs, j):
    q_refs, refs = refs[:ATT_TILES], refs[ATT_TILES:]
    kc_ref, vc_ref, ks_ref, vs_ref, kw_ref, vw_ref = refs[:6]
    bc_refs, bd_ref, gt_refs, o_ref = refs[6:6 + ATT_TILES], refs[6 + ATT_TILES], refs[7 + ATT_TILES:-1], refs[-1]
    tq = TQ
    n_cmp = kc_ref.shape[2]
    n_slc = ks_ref.shape[2] // SLC_BLOCK
    wt = WINDOW // tq
    dh = A_HEAD_DIM
    round8 = lambda n: -(-n // 8) * 8
    tiles = [dict(i=j + t * SUB, last=(j + t * SUB) // SUB, n_tok=(j + t * SUB + 1) * tq, q=q_refs[t][0, 0, 0],
                  bias_c=bc_refs[t], gates=gt_refs[t][0, 0, 0]) for t in range(ATT_TILES)]
    zero_rows = jnp.zeros((LANES - dh, A_HPG * tq), BF16)

    def scores(k_ref, q_mat, i, first_tile, n_sub, tile_index):
        s = jnp.dot(k_ref[0, 0, first_tile * tq:(first_tile + n_sub) * tq, :], q_mat,
                    preferred_element_type=F32)
        idx = [tile_index(i - (first_tile + t)) for t in range(n_sub)]
        parts = [s[t * tq:(t + 1) * tq] if idx[t] == TILE_FAR else s[t * tq:(t + 1) * tq] + bd_ref[0, idx[t]]
                 for t in range(n_sub)]
        return jnp.concatenate(parts, axis=0)

    def values_t(v_ref, first_tile, n_sub):
        return jnp.concatenate([v_ref[0, 0, first_tile + t] for t in range(n_sub)], axis=1)

    win_tile = lambda r: TILE_EDGE if r == wt else min(r, TILE_FAR)
    for t in tiles:
        t["first_w"] = max(t["i"] - wt, 0)
        t["n_w"] = t["i"] - t["first_w"] + 1
        q_pad = jnp.concatenate([t["q"], zero_rows], axis=0)
        t["s_w"] = scores(kw_ref, q_pad, t["i"], t["first_w"], t["n_w"], win_tile)
    for t in tiles:
        t["n_cmp"] = min(n_cmp, round8(t["n_tok"] // CMP_STRIDE))
        t["n_slc"] = min(n_slc, round8(t["n_tok"] // SLC_BLOCK))
        bias = t["bias_c"][0, 0, 0:t["n_cmp"], :]
        t["valid_c"] = bias > 0.5 * NEG_INF
        t["s_c"] = jnp.dot(kc_ref[0, 0, 0:t["n_cmp"], :], t["q"], preferred_element_type=F32) + bias

    r1, r2 = SLC_BLOCK // CMP_STRIDE, CMP_BLOCK // CMP_STRIDE
    jj = lax.broadcasted_iota(jnp.int32, (n_slc, n_cmp), 0)
    nn = lax.broadcasted_iota(jnp.int32, (n_slc, n_cmp), 1)
    d = nn - r1 * jj
    cnt = jnp.zeros((n_slc, n_cmp), F32)
    for a in range(r1):
        for c in range(r2):
            cnt = cnt + jnp.where(d == a - c, 1.0, 0.0)
    cnt = cnt.astype(BF16)
    for t in tiles:
        s = t["s_c"]
        e = jnp.where(t["valid_c"], jnp.exp2(s - jnp.max(s, axis=0, keepdims=True)), 0.0)
        l = jnp.sum(e, axis=0, keepdims=True)
        p = e * (1.0 / jnp.where(l > 0.0, l, 1.0))
        rest = n_cmp - t["n_cmp"]
        pad = (lambda a: jnp.concatenate([a, jnp.zeros((rest, a.shape[1]), a.dtype)], axis=0)) if rest else (lambda a: a)
        t["out_c"] = jnp.dot(vc_ref[0, 0], pad(p).astype(BF16), preferred_element_type=F32)
        p_grp = pad(sum(p[:, h * tq:(h + 1) * tq] for h in range(A_HPG)))
        t["imp"] = sum(jnp.dot(cnt[0:t["n_slc"]], part, preferred_element_type=F32)
                       for part in _split3(p_grp))

    for t in tiles:
        s = t["s_w"]
        t["p_w"] = jnp.exp2(s - jnp.max(s, axis=0, keepdims=True)).astype(BF16)
    for t in tiles:
        acc = jnp.dot(values_t(vw_ref, t["first_w"], t["n_w"]), t["p_w"], preferred_element_type=F32)
        out_w = acc[:dh] * (1.0 / acc[dh:dh + 1])
        gates = t["gates"]
        t["part"] = [gates[h:h + 1, :] * t["out_c"][:, h * tq:(h + 1) * tq]
                     + gates[2 * A_HPG + h:2 * A_HPG + h + 1, :] * out_w[:, h * tq:(h + 1) * tq]
                     for h in range(A_HPG)]

    for t in tiles:
        nb = t["n_slc"]
        blk = lax.broadcasted_iota(jnp.int32, (nb, tq), 0)
        tpos = t["i"] * tq + lax.broadcasted_iota(jnp.int32, (nb, tq), 1)
        cur = tpos // SLC_BLOCK
        forced = (blk == 0) | (blk == cur) | (blk == cur - 1)
        causal = blk * SLC_BLOCK <= tpos
        imp = jnp.where(forced, FORCE_SCORE, jnp.where(causal, t["imp"], NEG_INF))
        rank = jnp.zeros((nb, tq), F32)
        for c in range(nb):
            row = imp[c:c + 1, :]
            ahead = (row > imp) | ((row == imp) & (blk > c))
            rank = rank + jnp.where(ahead, 1.0, 0.0)
        pen = jnp.where(rank < float(min(SLC_TOPN, n_slc)), 0.0, -FORCE_SCORE)
        pen = jnp.concatenate([pen, jnp.zeros((LANES - dh - nb, tq), F32)], axis=0)
        t["q_aug"] = jnp.concatenate([t["q"], jnp.concatenate([pen] * A_HPG, axis=1).astype(BF16)], axis=0)

    sel_tile = lambda r: min(r, TILE_FAR)

    jobs = [(t, c, min(SUB, t["i"] - c * SUB + 1)) for t in tiles for c in range(t["last"], -1, -1)]
    ss = [scores(ks_ref, t["q_aug"], t["i"], c * SUB, n, sel_tile) for t, c, n in jobs]
    ms = [jnp.max(s, axis=0, keepdims=True) for s in ss]
    ps = [jnp.exp2(s - m).astype(BF16) for s, m in zip(ss, ms)]
    accs = [jnp.dot(values_t(vs_ref, c * SUB, n), p, preferred_element_type=F32) for (t, c, n), p in zip(jobs, ps)]

    for k, t in enumerate(tiles):
        mine = [n for n, job in enumerate(jobs) if job[0] is t]
        m = functools.reduce(jnp.maximum, [ms[n] for n in mine])
        acc = sum(jnp.exp2(ms[n] - m) * accs[n] for n in mine)
        out_s = acc[:dh] * (1.0 / acc[dh:dh + 1])
        gates = t["gates"]
        blocks = [t["part"][h] + gates[A_HPG + h:A_HPG + h + 1, :] * out_s[:, h * tq:(h + 1) * tq]
                  for h in range(A_HPG)]
        o_ref[0, k] = jnp.concatenate(blocks, axis=0).T.astype(BF16)


def _attention(q_t, kc, vc_t, ks, vs_t, kw, vw_t, bias_c, bias_d, gates_t):
    bsz, _, nt, _, _ = q_t.shape
    s = ks.shape[2]
    n_cmp = kc.shape[2]
    assert nt == ATT_TILES * SUB and WINDOW // TQ + 1 <= nt
    k_spec = pl.BlockSpec((1, 1, s, LANES), lambda b, g: (b, g, 0, 0))
    vt_spec = pl.BlockSpec((1, 1, nt, V_ROWS, TQ), lambda b, g: (b, g, 0, 0, 0))
    outs = []
    for j in range(SUB):
        per_tile = lambda spec: [spec(j + t * SUB) for t in range(ATT_TILES)]
        q_spec = lambda i: pl.BlockSpec((1, 1, 1, A_HEAD_DIM, A_HPG * TQ), lambda b, g: (b, g, i, 0, 0))
        bc_spec = lambda i: pl.BlockSpec((1, 1, n_cmp, A_HPG * TQ), lambda b, g: (i, g, 0, 0))
        gt_spec = lambda i: pl.BlockSpec((1, 1, 1, GATE_ROWS, TQ), lambda b, g: (b, g, i, 0, 0))
        outs.append(pl.pallas_call(
            functools.partial(_attn_kernel, j=j),
            grid=(bsz, A_KV_GROUPS),
            in_specs=(per_tile(q_spec)
                      + [pl.BlockSpec((1, 1, n_cmp, A_HEAD_DIM), lambda b, g: (b, g, 0, 0)),
                         pl.BlockSpec((1, 1, A_HEAD_DIM, n_cmp), lambda b, g: (b, g, 0, 0)),
                         k_spec, vt_spec, k_spec, vt_spec]
                      + per_tile(bc_spec)
                      + [pl.BlockSpec((1, N_BIAS_TILES, TQ, A_HPG * TQ), lambda b, g: (g, 0, 0, 0))]
                      + per_tile(gt_spec)),
            out_specs=pl.BlockSpec((1, ATT_TILES, TQ, A_HPG * A_HEAD_DIM), lambda b, g: (b, 0, 0, g)),
            out_shape=jax.ShapeDtypeStruct((bsz, ATT_TILES, TQ, A_WIDTH), BF16),
            compiler_params=pltpu.CompilerParams(dimension_semantics=("parallel", "parallel"),
                                                 vmem_limit_bytes=VMEM_LIMIT),
            name=f"attn{j}",
        )(*([q_t] * ATT_TILES), kc, vc_t, ks, vs_t, kw, vw_t, *([bias_c] * ATT_TILES), bias_d,
          *([gates_t] * ATT_TILES)))
    return outs


def _rwkv_kernel(rw_ref, ld_ref, lw_ref, lb_ref, o_ref, state_ref):
    cc = pl.program_id(1)
    n = B_HEAD_DIM
    nb, csz = rw_ref.shape[0], rw_ref.shape[1]

    @pl.when(cc == 0)
    def _():
        state_ref[...] = jnp.zeros(state_ref.shape, F32)

    ti = lax.broadcasted_iota(jnp.int32, (csz, LANES), 0)
    si = lax.broadcasted_iota(jnp.int32, (csz, LANES), 1) % n
    lower = si <= ti
    strict = si < ti
    eye = jnp.where(si == ti, 1.0, 0.0)
    tri = jnp.where(lax.broadcasted_iota(jnp.int32, (csz, csz), 1) <= lax.broadcasted_iota(jnp.int32, (csz, csz), 0),
                    1.0, 0.0).astype(BF16)
    n_pairs = B_WIDTH // LANES
    left =lax.broadcasted_iota(jnp.int32, (csz, LANES), 1) < n
    row_left = lax.broadcasted_iota(jnp.int32, (LANES, LANES), 0) < n
    same_head = row_left == (lax.broadcasted_iota(jnp.int32, (LANES, LANES), 1) < n)

    def blockdiag(y):
        zero = jnp.zeros_like(y)
        return jnp.concatenate([jnp.where(left, y, zero), jnp.where(left, zero, y)], axis=0)

    chains = []
    for bi in range(nb):
        r, kk, b, k_mod, v = (rw_ref[bi, :, m * B_WIDTH:(m + 1) * B_WIDTH].astype(F32) for m in range(5))
        ld = ld_ref[bi, :, 0:B_WIDTH]
        rsum = ld_ref[bi, :, B_WIDTH:2 * B_WIDTH]

        ld_hi, ld_lo = _split2(ld)
        cum = jnp.dot(tri, ld_hi, preferred_element_type=F32) + jnp.dot(tri, ld_lo, preferred_element_type=F32)
        g_inc = jnp.exp(cum)
        g_exc = jnp.exp(cum - ld)
        g_inv = jnp.exp(-cum)
        g_end = jnp.exp(cum[csz - 1:csz, :] - cum)
        g_all = g_inc[csz - 1:csz, :]

        for pr in range(n_pairs):
            sl = slice(pr * LANES, (pr + 1) * LANES)
            kk_p = kk[:, sl]
            b_p = b[:, sl]
            bt = (b_p * g_inv[:, sl]).astype(BF16)
            kt = (k_mod[:, sl] * g_inv[:, sl]).astype(BF16)
            ch = dict(
                idx=bi * n_pairs + pr,
                v=v[:, sl],
                lhs=jnp.concatenate([-kk_p * g_exc[:, sl], r[:, sl] * g_inc[:, sl]], axis=0).astype(BF16),
                rhs=jnp.concatenate([blockdiag(bt), blockdiag(kt)], axis=0),
                bk=jnp.concatenate([b_p * g_end[:, sl], k_mod[:, sl] * g_end[:, sl]], axis=0).astype(BF16),
                g_all=g_all[:, sl],
                bonus=rsum[:, sl] * v[:, sl],
            )
            chains.append(ch)

    for ch in chains:
        x = _dot_nt(ch["lhs"], ch["rhs"])
        xb, xk = x[:, :LANES], x[:, LANES:]
        ch["a_ab"] = jnp.where(strict, xb[:csz], 0.0)
        a_ak = jnp.where(strict, xk[:csz], 0.0)
        m_rk = jnp.where(lower, xk[csz:], 0.0)
        ch["ak_rk"] = jnp.concatenate([a_ak, m_rk], axis=0).astype(BF16)
        ch["m_rb"] = jnp.where(lower, xb[csz:], 0.0).astype(BF16)
    for ch in chains:
        akv = _dot(ch["ak_rk"], blockdiag(ch["v"].astype(BF16)))
        ch["akv"], ch["mrkv"] = akv[:csz], akv[csz:]
        ch["tinv"] = eye + ch["a_ab"]
        ch["pw"] = ch["a_ab"].astype(BF16)
    n_sq = int(math.log2(csz)) - 1
    for ch in chains:
        ch["pw"] = _dot(ch["pw"], blockdiag(ch["pw"])).astype(BF16)
    for step in range(n_sq):
        for ch in chains:
            if step + 1 < n_sq:
                both = _dot(jnp.concatenate([ch["pw"], ch["tinv"].astype(BF16)], axis=0), blockdiag(ch["pw"]))
                ch["tinv"] = ch["tinv"] + both[csz:]
                ch["pw"] = both[:csz].astype(BF16)
            else:
                ch["tinv"] = ch["tinv"] + _dot(ch["tinv"], blockdiag(ch["pw"]))
    for ch in chains:
        ch["s0"] = state_ref[ch["idx"]]
        ch["as0"] = _dot_nt(ch["lhs"], ch["s0"])
    for ch in chains:
        w = (ch["as0"][:csz] + ch["akv"]).astype(BF16)
        ch["u"] = _dot(ch["tinv"], blockdiag(w))
    outs = []
    for ch in chains:
        u = ch["u"]
        y = ch["as0"][csz:] + _dot(ch["m_rb"], blockdiag(u.astype(BF16))) + ch["mrkv"]
        uv = jnp.concatenate([u, ch["v"]], axis=0)
        state_ref[ch["idx"]] = ch["s0"] * ch["g_all"] + jnp.where(same_head, _dot_tn(uv, ch["bk"]), 0.0)
        yc = y - _head_sum(y) * (1.0 / n)
        var = _head_sum(yc * yc) * (1.0 / n)
        outs.append(yc * lax.rsqrt(var + LNX_EPS))
    for bi in range(nb):
        yn = jnp.concatenate(outs[bi * n_pairs:(bi + 1) * n_pairs], axis=-1)
        bonus = jnp.concatenate([ch["bonus"] for ch in chains[bi * n_pairs:(bi + 1) * n_pairs]], axis=-1)
        o_ref[bi] = (yn * lw_ref[...] + lb_ref[...] + bonus).astype(BF16)


RWKV_NB = 8


def _rwkv(scan_bf16, scan_f32, ln_w, ln_b):
    bsz, s, _ = scan_bf16.shape
    nb = RWKV_NB if bsz % RWKV_NB == 0 else 1
    const = lambda b, c: (0, 0)
    vec = pl.BlockSpec((1, B_WIDTH), const)
    return pl.pallas_call(
        _rwkv_kernel,
        grid=(bsz // nb, s // CHUNK),
        in_specs=[pl.BlockSpec((nb, CHUNK, SCAN_BF16_COLS), lambda b, c: (b, c, 0)),
                  pl.BlockSpec((nb, CHUNK, SCAN_F32_COLS), lambda b, c: (b, c, 0)),
                  vec, vec],
        out_specs=pl.BlockSpec((nb, CHUNK, B_WIDTH), lambda b, c: (b, c, 0)),
        out_shape=jax.ShapeDtypeStruct((bsz, s, B_WIDTH), BF16),
        scratch_shapes=[pltpu.VMEM((nb * B_WIDTH // LANES, LANES, LANES), F32)],
        compiler_params=pltpu.CompilerParams(dimension_semantics=("parallel", "arbitrary")),
        name="rwkv",
    )(scan_bf16, scan_f32, ln_w, ln_b)


def _final_kernel(x_ref, *refs):
    ya_refs, (yb_ref, cf_ref, gate_ref, wa_ref, wb_ref, wo_ref, o_ref) = refs[:SUB], refs[SUB:]
    a_silu = cf_ref[0, :, 0:A_WIDTH].astype(F32)
    b_silu = cf_ref[0, :, A_WIDTH:A_WIDTH + B_WIDTH].astype(F32)
    merge_a = cf_ref[0, :, A_WIDTH + B_WIDTH:A_WIDTH + B_WIDTH + D_MODEL].astype(F32)
    merge_b = cf_ref[0, :, A_WIDTH + B_WIDTH + D_MODEL:A_WIDTH + B_WIDTH + 2 * D_MODEL].astype(F32)
    y_a = jnp.concatenate([r[0, 0] for r in ya_refs], axis=0)
    ya = y_a.astype(F32) * (a_silu * _sigmoid(a_silu))
    yb = yb_ref[0].astype(F32) * (b_silu * _sigmoid(b_silu))
    merged = _sigmoid(merge_a) * _dot(ya, wa_ref[...]) + _sigmoid(merge_b) * _dot(yb, wb_ref[...])
    o_ref[0] = x_ref[0] + gate_ref[0] * _dot(merged, wo_ref[...])


def _final(x, y_a, y_b, cols_fin, gate, w_out_a, w_out_b, w_o):
    bsz, s, _ = x.shape
    tm = SUB * TQ
    const = lambda b, i: (0, 0)
    row = lambda w: pl.BlockSpec((1, tm, w), lambda b, i: (b, i, 0))
    ya_spec = pl.BlockSpec((1, 1, TQ, A_WIDTH), lambda b, i: (b, i, 0, 0))
    return pl.pallas_call(
        _final_kernel,
        grid=(bsz, s // tm),
        in_specs=[row(D_MODEL)] + [ya_spec] * SUB + [row(B_WIDTH), row(FIN_COLS),
                  pl.BlockSpec((1, 1, D_MODEL), lambda b, i: (b, 0, 0)),
                  pl.BlockSpec((A_WIDTH, D_MODEL), const),
                  pl.BlockSpec((B_WIDTH, D_MODEL), const),
                  pl.BlockSpec((D_MODEL, D_MODEL), const)],
        out_specs=row(D_MODEL),
        out_shape=jax.ShapeDtypeStruct((bsz, s, D_MODEL), F32),
        compiler_params=pltpu.CompilerParams(dimension_semantics=("parallel", "parallel"),
                                             vmem_limit_bytes=VMEM_LIMIT),
        name="final",
    )(x, *y_a, y_b, cols_fin, gate, w_out_a, w_out_b, w_o)


def _split_w_in(w_in):
    nsa_in = 2 * A_WIDTH + 6 * A_KV_WIDTH + 3 * A_HEADS
    o_gate = A_WIDTH + 6 * A_KV_WIDTH
    o_asilu = o_gate + 3 * A_HEADS
    o_shift = nsa_in
    o_rest = nsa_in + RWKV_COLS
    gate_w = w_in[:, o_gate:o_asilu].reshape(D_MODEL, 3, A_KV_GROUPS, A_HPG)
    gate_w = gate_w.transpose(0, 2, 1, 3).reshape(D_MODEL, A_KV_GROUPS, 3 * A_HPG)
    gate_w = jnp.pad(gate_w, ((0, 0), (0, 0), (0, A_HEAD_DIM - 3 * A_HPG))).reshape(D_MODEL, GATE_PAD)
    w_nsa = jnp.concatenate([w_in[:, :o_gate], gate_w], axis=1)
    w_fin = jnp.concatenate([w_in[:, o_asilu:o_shift], w_in[:, o_rest:]], axis=1)
    w_mix = jnp.concatenate([w_in[:, o_shift:o_rest], w_nsa], axis=1)
    return w_mix.astype(BF16), w_fin.astype(BF16)


def _layer(x, c, rel_bias, w_ada, b_ada, norm_gain, w_in, q_norm_gain, k_norm_gain,
           cmp_pos_k, cmp_pos_v, cmp_k_w1, cmp_k_w2, cmp_v_w1, cmp_v_w2,
           shift_mu, w0, w_lora_up, a0, a_lora_up, k_k, k_a, r_k, ln_x_w, ln_x_b,
           w_out_a, w_out_b, w_o):
    bsz, s, _ = x.shape
    assert s % (2 * TQ) == 0 and s // CMP_STRIDE == LANES
    n16 = s // CMP_STRIDE
    mod = _ada(c, w_ada, b_ada)
    w_mix, w_fin = _split_w_in(w_in)
    scale = A_HEAD_DIM ** -0.5 * LOG2E
    qg = jnp.tile(q_norm_gain, A_HEADS) * scale
    ksg = jnp.tile(k_norm_gain[1], A_KV_GROUPS)
    kwg = jnp.tile(k_norm_gain[2], A_KV_GROUPS)
    vec = lambda t: t.reshape(1, -1)
    rwkv_params = (vec(shift_mu), vec(w0), w_lora_up.astype(BF16), vec(a0), a_lora_up.astype(BF16),
                   vec(k_k), vec(k_a), vec(r_k))
    q_t, ks, vs_t, kw, vw_t, gates_t, ck, cols_fin, scan_bf16, scan_f32 = _proj(
        x, mod, norm_gain, w_mix, w_fin, qg, ksg, kwg, rwkv_params)

    kc, vc_t = _compress(ck, _expand_cmp_pos(cmp_pos_k), _expand_cmp_pos(cmp_pos_v),
                         _expand_cmp_w1(cmp_k_w1), cmp_k_w2.astype(BF16),
                         _expand_cmp_w1(cmp_v_w1), cmp_v_w2.T.astype(BF16),
                         k_norm_gain[0].reshape(1, A_HEAD_DIM))
    bias_c, bias_d = _bias_tables(rel_bias, s, n16)
    y_a = _attention(q_t, kc, vc_t, ks, vs_t, kw, vw_t, bias_c, bias_d, gates_t)

    y_b = _rwkv(scan_bf16, scan_f32, vec(ln_x_w), vec(ln_x_b))

    gate = mod[:, 2 * D_MODEL:].reshape(bsz, 1, D_MODEL)
    return _final(x, y_a, y_b, cols_fin, gate, w_out_a.astype(BF16), w_out_b.astype(BF16), w_o.astype(BF16))


def kernel(x, c, w_ada, b_ada, norm_gain, w_in, q_norm_gain, k_norm_gain, cmp_pos_k, cmp_pos_v, cmp_k_w1, cmp_k_w2, cmp_v_w1, cmp_v_w2, rel_bias, shift_mu, w0, w_lora_up, a0, a_lora_up, k_k, k_a, r_k, ln_x_w, ln_x_b, w_out_a, w_out_b, w_o):
    for l in range(w_in.shape[0]):
        x = _layer(x, c, rel_bias, w_ada[l], b_ada[l], norm_gain[l], w_in[l], q_norm_gain[l], k_norm_gain[l],
                   cmp_pos_k[l], cmp_pos_v[l], cmp_k_w1[l], cmp_k_w2[l], cmp_v_w1[l], cmp_v_w2[l],
                   shift_mu[l], w0[l], w_lora_up[l], a0[l], a_lora_up[l], k_k[l], k_a[l], r_k[l],
                   ln_x_w[l], ln_x_b[l], w_out_a[l], w_out_b[l], w_o[l])
    return x
```

```python
import functools
import math

import numpy as np
import jax
import jax.numpy as jnp
from jax import lax
from jax.experimental import pallas as pl
from jax.experimental.pallas import tpu as pltpu

F32 = jnp.float32
BF16 = jnp.bfloat16

D_MODEL = 1024
A_HEADS = 8
A_HEAD_DIM = 64
A_KV_GROUPS = 2
A_HPG = A_HEADS // A_KV_GROUPS
A_WIDTH = A_HEADS * A_HEAD_DIM
A_KV_WIDTH = A_KV_GROUPS * A_HEAD_DIM
CMP_BLOCK = 32
CMP_STRIDE = 16
CMP_HIDDEN = 256
SLC_BLOCK = 64
SLC_TOPN = 16
WINDOW = 512
B_HEADS = 8
B_HEAD_DIM = 64
B_WIDTH = B_HEADS * B_HEAD_DIM
DECAY_LORA = 64
ICLR_LORA = 64
LNX_EPS = 64e-5
REL_BUCKETS = 32
REL_MAX_EXACT = 16
REL_MAX_DIST = 128
NORM_EPS = 1e-6
NEG_INF = -1e30
FORCE_SCORE = 1e30

LANES = 128
TQ = 128
CHUNK = 64
GATE_PAD = LANES
LOG2E = math.log2(math.e)
V_ROWS = A_HEAD_DIM + 16
GATE_ROWS = 16
NSA_COLS = A_WIDTH + 6 * A_KV_WIDTH + GATE_PAD
FIN_COLS = A_WIDTH + B_WIDTH + 2 * D_MODEL
RWKV_COLS = 3 * B_WIDTH + DECAY_LORA + ICLR_LORA
SCAN_BF16_COLS = 5 * B_WIDTH
SCAN_F32_COLS = 2 * B_WIDTH
VMEM_LIMIT = 56 * 1024 * 1024


def _dot(a, b):
    return jnp.dot(a.astype(BF16), b.astype(BF16), preferred_element_type=F32)


def _dot_nt(a, b):
    return lax.dot_general(a.astype(BF16), b.astype(BF16), (((1,), (1,)), ((), ())),
                           preferred_element_type=F32)


def _dot_tn(a, b):
    return lax.dot_general(a.astype(BF16), b.astype(BF16), (((0,), (0,)), ((), ())),
                           preferred_element_type=F32)


def _split2(x):
    hi = x.astype(BF16)
    lo = (x - hi.astype(F32)).astype(BF16)
    return hi, lo


def _split3(x):
    h1 = x.astype(BF16)
    r1 = x - h1.astype(F32)
    h2 = r1.astype(BF16)
    h3 = (r1 - h2.astype(F32)).astype(BF16)
    return h1, h2, h3


def _sigmoid(x):
    return 1.0 / (1.0 + jnp.exp(-x))


def _bucket_thresholds():
    n = np.arange(0, 4096)
    nf = np.maximum(n, REL_MAX_EXACT).astype(np.float64)
    val = np.log(nf / REL_MAX_EXACT) / math.log(REL_MAX_DIST / REL_MAX_EXACT) * (REL_BUCKETS - REL_MAX_EXACT)
    frac = np.abs(val - np.round(val))
    assert np.all((frac > 1e-4) | (n <= REL_MAX_EXACT) | (n >= REL_MAX_DIST))
    large = REL_MAX_EXACT + np.floor(val + 1e-9).astype(np.int64)
    bucket = np.where(n < REL_MAX_EXACT, n, np.minimum(large, REL_BUCKETS - 1))
    return [int(np.argmax(bucket >= j)) for j in range(REL_BUCKETS)]


_BUCKET_TH = _bucket_thresholds()


def _bias_from_dist(dist, tbl_ref, head):
    val = jnp.full(dist.shape, tbl_ref[0, head], F32)
    for j in range(1, REL_BUCKETS):
        val = jnp.where(dist >= _BUCKET_TH[j], tbl_ref[j, head], val)
    return val


def _ada_kernel(c_ref, w_ref, b_ref, o_ref):
    c = c_ref[...]
    o_ref[...] = _dot(c * _sigmoid(c), w_ref[...]) + b_ref[...]


def _ada(c, w_ada, b_ada):
    bsz = c.shape[0]
    return pl.pallas_call(
        _ada_kernel,
        grid=(3,),
        in_specs=[pl.BlockSpec((bsz, D_MODEL), lambda j: (0, 0)),
                  pl.BlockSpec((D_MODEL, D_MODEL), lambda j: (0, j)),
                  pl.BlockSpec((1, D_MODEL), lambda j: (0, j))],
        out_specs=pl.BlockSpec((bsz, D_MODEL), lambda j: (0, j)),
        out_shape=jax.ShapeDtypeStruct((bsz, 3 * D_MODEL), F32),
        name="ada",
    )(c, w_ada, b_ada.reshape(1, 3 * D_MODEL))


def _norm_rows(x_t, gain_col, n_seg):
    out = []
    for seg in range(n_seg):
        blk = x_t[seg * A_HEAD_DIM:(seg + 1) * A_HEAD_DIM, :]
        ms = jnp.mean(blk * blk, axis=0, keepdims=True)
        out.append(blk * lax.rsqrt(ms + NORM_EPS) * gain_col[seg * A_HEAD_DIM:(seg + 1) * A_HEAD_DIM, :])
    return out


def _head_sum(x):
    left = lax.broadcasted_iota(jnp.int32, x.shape, x.ndim - 1) < B_HEAD_DIM
    lo = jnp.sum(jnp.where(left, x, 0.0), axis=-1, keepdims=True)
    hi = jnp.sum(jnp.where(left, 0.0, x), axis=-1, keepdims=True)
    return jnp.where(left, lo, hi)


def _proj_kernel(x_ref, mod_ref, g_ref, wm_ref, wf_ref, qg_ref, ksg_ref, kwg_ref,
                 mu_ref, w0_ref, wl_ref, a0_ref, al_ref, kk_ref, ka_ref, rk_ref,
                 q_ref, ks_ref, vs_ref, kw_ref, vw_ref, gt_ref, ck_ref, of_ref, rw_ref, ld_ref, prev_ref):
    tm = x_ref.shape[1]

    @pl.when(pl.program_id(1) == 0)
    def _():
        prev_ref[...] = jnp.zeros(prev_ref.shape, F32)

    x = x_ref[0]
    ms = jnp.mean(x * x, axis=-1, keepdims=True)
    y = x * lax.rsqrt(ms + NORM_EPS) * g_ref[...]
    mod = mod_ref[0]
    h = (y * (1.0 + mod[:, D_MODEL:2 * D_MODEL]) + mod[:, :D_MODEL]).astype(BF16)
    crn = jnp.dot(h, wm_ref[...], preferred_element_type=F32)
    cr = crn[:, 0:RWKV_COLS]
    cn = crn[:, RWKV_COLS:RWKV_COLS + NSA_COLS]
    ck_ref[0] = cn[:, A_WIDTH:A_WIDTH + 2 * A_KV_WIDTH]

    rolled = pltpu.roll(cr, 1, axis=0)
    row8 = lax.broadcasted_iota(jnp.int32, (8, RWKV_COLS), 0)
    prev = jnp.concatenate([jnp.where(row8 == 0, prev_ref[0:1, :], rolled[0:8]), rolled[8:]], axis=0)
    prev_ref[0:1, :] = cr[tm - 1:tm, :]
    xs = cr + (prev - cr) * mu_ref[...]
    r = xs[:, 0:B_WIDTH]
    k = xs[:, B_WIDTH:2 * B_WIDTH]
    v = xs[:, 2 * B_WIDTH:3 * B_WIDTH]
    wd = xs[:, 3 * B_WIDTH:3 * B_WIDTH + DECAY_LORA]
    ad = xs[:, 3 * B_WIDTH + DECAY_LORA:3 * B_WIDTH + DECAY_LORA + ICLR_LORA]
    ld = -math.exp(-0.5) * _sigmoid(w0_ref[...] + _dot(jnp.tanh(wd), wl_ref[...]))
    a = _sigmoid(a0_ref[...] + _dot(ad, al_ref[...]))
    kk = k * kk_ref[...]
    k_mod = k * (1.0 + (a - 1.0) * ka_ref[...])
    rkr = r * k_mod * rk_ref[...]
    ld_ref[0, :, 0:B_WIDTH] = ld
    rw_ref[0, :, 0:B_WIDTH] = r.astype(BF16)
    rw_ref[0, :, 3 * B_WIDTH:4 * B_WIDTH] = k_mod.astype(BF16)
    rw_ref[0, :, 4 * B_WIDTH:5 * B_WIDTH] = v.astype(BF16)
    for pr in range(B_WIDTH // LANES):
        sl = slice(pr * LANES, (pr + 1) * LANES)
        kk_p = kk[:, sl]
        kk_p = kk_p * lax.rsqrt(jnp.maximum(_head_sum(kk_p * kk_p), 1e-24))
        rw_ref[0, :, B_WIDTH + pr * LANES:B_WIDTH + (pr + 1) * LANES] = kk_p.astype(BF16)
        rw_ref[0, :, 2 * B_WIDTH + pr * LANES:2 * B_WIDTH + (pr + 1) * LANES] = (kk_p * a[:, sl]).astype(BF16)
        ld_ref[0, :, B_WIDTH + pr * LANES:B_WIDTH + (pr + 1) * LANES] = _head_sum(rkr[:, sl])

    lane = lax.broadcasted_iota(jnp.int32, (TQ, LANES), 1)
    row = lax.broadcasted_iota(jnp.int32, (TQ, LANES), 0)
    ones_rows = (lax.broadcasted_iota(jnp.int32, (V_ROWS - A_HEAD_DIM, TQ), 0) == 0).astype(BF16)
    off = A_WIDTH + 2 * A_KV_WIDTH
    for sub in range(tm // TQ):
        c = cn[sub * TQ:(sub + 1) * TQ]
        q_heads = _norm_rows(c[:, 0:A_WIDTH].T, qg_ref[...], A_HEADS)
        ks_t = jnp.concatenate(_norm_rows(c[:, off:off + A_KV_WIDTH].T, ksg_ref[...], A_KV_GROUPS), axis=0)
        kw_t = jnp.concatenate(_norm_rows(c[:, off + 2 * A_KV_WIDTH:off + 3 * A_KV_WIDTH].T, kwg_ref[...],
                                          A_KV_GROUPS), axis=0)
        ksn = ks_t.T
        kwn = kw_t.T
        vs_t = c[:, off + A_KV_WIDTH:off + 2 * A_KV_WIDTH].T.astype(BF16)
        vw_t = c[:, off + 3 * A_KV_WIDTH:off + 4 * A_KV_WIDTH].T.astype(BF16)
        gates_t = _sigmoid(c[:, off + 4 * A_KV_WIDTH:off + 5 * A_KV_WIDTH]).T
        blk = (pl.program_id(1) * tm + sub * TQ + row) // SLC_BLOCK
        onehot = jnp.where(lane - A_HEAD_DIM == blk, 1.0, 0.0)
        for g in range(A_KV_GROUPS):
            q_ref[0, g, sub] = jnp.concatenate(q_heads[g * A_HPG:(g + 1) * A_HPG], axis=1).astype(BF16)
            sl = slice(g * A_HEAD_DIM, (g + 1) * A_HEAD_DIM)
            k_g = ksn if g == 0 else pltpu.roll(ksn, A_HEAD_DIM, axis=1)
            ks_ref[0, g, sub * TQ:(sub + 1) * TQ, :] = jnp.where(lane < A_HEAD_DIM, k_g, onehot).astype(BF16)
            kw_g = kwn if g == 0 else pltpu.roll(kwn, A_HEAD_DIM, axis=1)
            kw_ref[0, g, sub * TQ:(sub + 1) * TQ, :] = jnp.where(lane < A_HEAD_DIM, kw_g, 0.0).astype(BF16)
            vs_ref[0, g, sub] = jnp.concatenate([vs_t[sl, :], ones_rows], axis=0)
            vw_ref[0, g, sub] = jnp.concatenate([vw_t[sl, :], ones_rows], axis=0)
            gt_ref[0, g, sub] = gates_t[g * A_HEAD_DIM:g * A_HEAD_DIM + GATE_ROWS, :]
    of_ref[0] = jnp.dot(h, wf_ref[...], preferred_element_type=F32).astype(BF16)


def _proj(x, mod, norm_gain, w_mix, w_fin, qg, ksg, kwg, rwkv_params, tm=512):
    bsz, s, _ = x.shape
    nt, nsub = s // TQ, tm // TQ
    assert A_HEAD_DIM + s // SLC_BLOCK <= LANES and A_KV_WIDTH == LANES and 2 * B_HEAD_DIM == LANES
    const = lambda b, i: (0, 0)
    weight = lambda cols: pl.BlockSpec((D_MODEL, cols), const, pipeline_mode=pl.Buffered(1))
    whole = lambda t: pl.BlockSpec(t.shape, const)
    col = lambda t: jnp.broadcast_to(t.reshape(-1, 1), (t.size, LANES))
    k_spec = lambda width: pl.BlockSpec((1, A_KV_GROUPS, tm, width), lambda b, i: (b, 0, i, 0))
    k_shape = lambda width: jax.ShapeDtypeStruct((bsz, A_KV_GROUPS, s, width), BF16)
    tile_spec = lambda r, c: pl.BlockSpec((1, A_KV_GROUPS, nsub, r, c), lambda b, i: (b, 0, i, 0, 0))
    tile_shape = lambda r, c, dt: jax.ShapeDtypeStruct((bsz, A_KV_GROUPS, nt, r, c), dt)
    return pl.pallas_call(
        _proj_kernel,
        grid=(bsz, s // tm),
        in_specs=[pl.BlockSpec((1, tm, D_MODEL), lambda b, i: (b, i, 0)),
                  pl.BlockSpec((1, 1, 3 * D_MODEL), lambda b, i: (b, 0, 0)),
                  pl.BlockSpec((1, D_MODEL), const),
                  weight(RWKV_COLS + NSA_COLS), weight(FIN_COLS),
                  pl.BlockSpec((A_WIDTH, LANES), const),
                  pl.BlockSpec((A_KV_WIDTH, LANES), const),
                  pl.BlockSpec((A_KV_WIDTH, LANES), const)] + [whole(t) for t in rwkv_params],
        out_specs=[tile_spec(A_HEAD_DIM, A_HPG * TQ),
                   k_spec(LANES), tile_spec(V_ROWS, TQ), k_spec(LANES), tile_spec(V_ROWS, TQ),
                   tile_spec(GATE_ROWS, TQ),
                   pl.BlockSpec((1, tm, 2 * A_KV_WIDTH), lambda b, i: (b, i, 0)),
                   pl.BlockSpec((1, tm, FIN_COLS), lambda b, i: (b, i, 0)),
                   pl.BlockSpec((1, tm, SCAN_BF16_COLS), lambda b, i: (b, i, 0)),
                   pl.BlockSpec((1, tm, SCAN_F32_COLS), lambda b, i: (b, i, 0))],
        out_shape=[tile_shape(A_HEAD_DIM, A_HPG * TQ, BF16),
                   k_shape(LANES), tile_shape(V_ROWS, TQ, BF16), k_shape(LANES), tile_shape(V_ROWS, TQ, BF16),
                   tile_shape(GATE_ROWS, TQ, F32),
                   jax.ShapeDtypeStruct((bsz, s, 2 * A_KV_WIDTH), F32),
                   jax.ShapeDtypeStruct((bsz, s, FIN_COLS), BF16),
                   jax.ShapeDtypeStruct((bsz, s, SCAN_BF16_COLS), BF16),
                   jax.ShapeDtypeStruct((bsz, s, SCAN_F32_COLS), F32)],
        scratch_shapes=[pltpu.VMEM((8, RWKV_COLS), F32)],
        compiler_params=pltpu.CompilerParams(dimension_semantics=("parallel", "arbitrary"),
                                             vmem_limit_bytes=VMEM_LIMIT),
        name="proj",
    )(x, mod.reshape(bsz, 1, 3 * D_MODEL), norm_gain.reshape(1, D_MODEL), w_mix, w_fin,
      col(qg), col(ksg), col(kwg), *rwkv_params)


def _compress_kernel(ck_ref, cv_ref, pk_ref, pv_ref, w1k_ref, w2k_ref, w1v_ref, w2v_ref, kg_ref, kc_ref, vc_ref):
    nb, n16 = ck_ref.shape[0], ck_ref.shape[1] // CMP_STRIDE

    def rows16(ref):
        return jnp.concatenate(
            [jnp.concatenate([ref[b, pl.ds(p, n16, stride=CMP_STRIDE), :] for p in range(CMP_STRIDE)], axis=1)
             for b in range(nb)], axis=0)

    def hidden(z, pos_ref, w1_ref, g):
        top = _dot(z + pos_ref[0:1, :], w1_ref[g, 0])
        bot = _dot(z + pos_ref[1:2, :], w1_ref[g, 1])
        bot = jnp.concatenate([pltpu.roll(bot[b * n16:(b + 1) * n16], n16 - 1, axis=0) for b in range(nb)], axis=0)
        return jax.nn.gelu(top + bot, approximate=True)

    zk = rows16(ck_ref)
    zv = rows16(cv_ref)
    for g in range(A_KV_GROUPS):
        kc = _dot(hidden(zk, pk_ref, w1k_ref, g), w2k_ref[...])
        ms = jnp.mean(kc * kc, axis=-1, keepdims=True)
        kc = (kc * lax.rsqrt(ms + NORM_EPS) * kg_ref[...]).astype(BF16)
        vc_t = _dot_nt(w2v_ref[...], hidden(zv, pv_ref, w1v_ref, g)).astype(BF16)
        for b in range(nb):
            kc_ref[b, g] = kc[b * n16:(b + 1) * n16]
            vc_ref[b, g] = vc_t[:, b * n16:(b + 1) * n16]


CMP_NB = 4


def _expand_cmp_w1(w1):
    w = w1.reshape(2, CMP_STRIDE, 1, A_HEAD_DIM, CMP_HIDDEN)
    per_group = []
    for g in range(A_KV_GROUPS):
        pad = [(0, 0), (0, 0), (g, A_KV_GROUPS - 1 - g), (0, 0), (0, 0)]
        per_group.append(jnp.pad(w, pad).reshape(2, CMP_STRIDE * A_KV_WIDTH, CMP_HIDDEN))
    return jnp.stack(per_group).astype(BF16)


def _expand_cmp_pos(pos):
    p = jnp.broadcast_to(pos.reshape(2, CMP_STRIDE, 1, A_HEAD_DIM), (2, CMP_STRIDE, A_KV_GROUPS, A_HEAD_DIM))
    return p.reshape(2, CMP_STRIDE * A_KV_WIDTH)


def _compress(ck, pk, pv, w1k, w2k, w1v, w2v_t, kg):
    bsz, s, _ = ck.shape
    n16 = s // CMP_STRIDE
    zw = CMP_STRIDE * A_KV_WIDTH
    const = lambda b: (0, 0)
    const4 = lambda b: (0, 0, 0, 0)
    nb = CMP_NB if bsz % CMP_NB == 0 else 1
    return pl.pallas_call(
        _compress_kernel,
        grid=(bsz // nb,),
        in_specs=[pl.BlockSpec((nb, s, A_KV_WIDTH), lambda b: (b, 0, 0)),
                  pl.BlockSpec((nb, s, A_KV_WIDTH), lambda b: (b, 0, 1)),
                  pl.BlockSpec((2, zw), const), pl.BlockSpec((2, zw), const),
                  pl.BlockSpec((A_KV_GROUPS, 2, zw, CMP_HIDDEN), const4), pl.BlockSpec((CMP_HIDDEN, A_HEAD_DIM), const),
                  pl.BlockSpec((A_KV_GROUPS, 2, zw, CMP_HIDDEN), const4), pl.BlockSpec((A_HEAD_DIM, CMP_HIDDEN), const),
                  pl.BlockSpec((1, A_HEAD_DIM), const)],
        out_specs=[pl.BlockSpec((nb, A_KV_GROUPS, n16, A_HEAD_DIM), lambda b: (b, 0, 0, 0)),
                   pl.BlockSpec((nb, A_KV_GROUPS, A_HEAD_DIM, n16), lambda b: (b, 0, 0, 0))],
        out_shape=[jax.ShapeDtypeStruct((bsz, A_KV_GROUPS, n16, A_HEAD_DIM), BF16),
                   jax.ShapeDtypeStruct((bsz, A_KV_GROUPS, A_HEAD_DIM, n16), BF16)],
        compiler_params=pltpu.CompilerParams(dimension_semantics=("parallel",)),
        name="compress",
    )(ck, ck, pk, pv, w1k, w2k, w1v, w2v_t, kg)


TILE_FAR, TILE_EDGE, N_BIAS_TILES = 2, 3, 4
SUB = 4
ATT_TILES = 4


def _bias_cmp_kernel(tbl_ref, o_ref):
    i = pl.program_id(0)
    g = pl.program_id(1)
    n_cmp = o_ref.shape[2]
    n = lax.broadcasted_iota(jnp.int32, (n_cmp, TQ), 0)
    q = lax.broadcasted_iota(jnp.int32, (n_cmp, TQ), 1)
    dist = i * TQ + q - (n * CMP_STRIDE + CMP_BLOCK - 1)
    for h in range(A_HPG):
        bias = _bias_from_dist(dist, tbl_ref, g * A_HPG + h)
        o_ref[0, 0, :, h * TQ:(h + 1) * TQ] = jnp.where(dist >= 0, bias * LOG2E, NEG_INF)


def _bias_toeplitz_kernel(tbl_ref, o_ref):
    g = pl.program_id(0)
    r = pl.program_id(1)
    off = jnp.where(r == TILE_EDGE, WINDOW // TQ, r)
    k = lax.broadcasted_iota(jnp.int32, (TQ, TQ), 0)
    q = lax.broadcasted_iota(jnp.int32, (TQ, TQ), 1)
    dist = off * TQ + q - k
    valid = (dist >= 0) & (dist < WINDOW)
    for h in range(A_HPG):
        head = g * A_HPG + h
        bias = _bias_from_dist(dist, tbl_ref, head) - tbl_ref[REL_BUCKETS - 1, head]
        o_ref[0, 0, :, h * TQ:(h + 1) * TQ] = jnp.where(valid, bias * LOG2E, NEG_INF)


def _bias_tables(rel_bias, s, n_cmp):
    smem = pl.BlockSpec(memory_space=pltpu.SMEM)
    nt = s // TQ
    bias_c = pl.pallas_call(
        _bias_cmp_kernel,
        grid=(nt, A_KV_GROUPS),
        in_specs=[smem],
        out_specs=pl.BlockSpec((1, 1, n_cmp, A_HPG * TQ), lambda i, g: (i, g, 0, 0)),
        out_shape=jax.ShapeDtypeStruct((nt, A_KV_GROUPS, n_cmp, A_HPG * TQ), F32),
        name="bias_cmp",
    )(rel_bias)
    assert _BUCKET_TH[REL_BUCKETS - 1] <= TQ + 1 and WINDOW // TQ >= 3
    bias_d = pl.pallas_call(
        _bias_toeplitz_kernel,
        grid=(A_KV_GROUPS, N_BIAS_TILES),
        in_specs=[smem],
        out_specs=pl.BlockSpec((1, 1, TQ, A_HPG * TQ), lambda g, r: (g, r, 0, 0)),
        out_shape=jax.ShapeDtypeStruct((A_KV_GROUPS, N_BIAS_TILES, TQ, A_HPG * TQ), F32),
        name="bias_toeplitz",
    )(rel_bias)
    return bias_c, bias_d


def _attn_kernel(*refs, j):
    q_refs, refs = refs[:ATT_TILES], refs[ATT_TILES:]
    kc_ref, vc_ref, ks_ref, vs_ref, kw_ref, vw_ref = refs[:6]
    bc_refs, bd_ref, gt_refs, o_ref = refs[6:6 + ATT_TILES], refs[6 + ATT_TILES], refs[7 + ATT_TILES:-1], refs[-1]
    tq = TQ
    n_cmp = kc_ref.shape[2]
    n_slc = ks_ref.shape[2] // SLC_BLOCK
    wt = WINDOW // tq
    dh = A_HEAD_DIM
    round8 = lambda n: -(-n // 8) * 8
    tiles = [dict(i=j + t * SUB, last=(j + t * SUB) // SUB, n_tok=(j + t * SUB + 1) * tq, q=q_refs[t][0, 0, 0],
                  bias_c=bc_refs[t], gates=gt_refs[t][0, 0, 0]) for t in range(ATT_TILES)]
    zero_rows = jnp.zeros((LANES - dh, A_HPG * tq), BF16)

    def scores(k_ref, q_mat, i, first_tile, n_sub, tile_index):
        s = jnp.dot(k_ref[0, 0, first_tile * tq:(first_tile + n_sub) * tq, :], q_mat,
                    preferred_element_type=F32)
        idx = [tile_index(i - (first_tile + t)) for t in range(n_sub)]
        parts = [s[t * tq:(t + 1) * tq] if idx[t] == TILE_FAR else s[t * tq:(t + 1) * tq] + bd_ref[0, idx[t]]
                 for t in range(n_sub)]
        return jnp.concatenate(parts, axis=0)

    def values_t(v_ref, first_tile, n_sub):
        return jnp.concatenate([v_ref[0, 0, first_tile + t] for t in range(n_sub)], axis=1)

    win_tile = lambda r: TILE_EDGE if r == wt else min(r, TILE_FAR)
    for t in tiles:
        t["first_w"] = max(t["i"] - wt, 0)
        t["n_w"] = t["i"] - t["first_w"] + 1
        q_pad = jnp.concatenate([t["q"], zero_rows], axis=0)
        t["s_w"] = scores(kw_ref, q_pad, t["i"], t["first_w"], t["n_w"], win_tile)
    for t in tiles:
        t["n_cmp"] = min(n_cmp, round8(t["n_tok"] // CMP_STRIDE))
        t["n_slc"] = min(n_slc, round8(t["n_tok"] // SLC_BLOCK))
        bias = t["bias_c"][0, 0, 0:t["n_cmp"], :]
        t["valid_c"] = bias > 0.5 * NEG_INF
        t["s_c"] = jnp.dot(kc_ref[0, 0, 0:t["n_cmp"], :], t["q"], preferred_element_type=F32) + bias

    r1, r2 = SLC_BLOCK // CMP_STRIDE, CMP_BLOCK // CMP_STRIDE
    jj = lax.broadcasted_iota(jnp.int32, (n_slc, n_cmp), 0)
    nn = lax.broadcasted_iota(jnp.int32, (n_slc, n_cmp), 1)
    d = nn - r1 * jj
    cnt = jnp.zeros((n_slc, n_cmp), F32)
    for a in range(r1):
        for c in range(r2):
            cnt = cnt + jnp.where(d == a - c, 1.0, 0.0)
    cnt = cnt.astype(BF16)
    for t in tiles:
        s = t["s_c"]
        e = jnp.where(t["valid_c"], jnp.exp2(s - jnp.max(s, axis=0, keepdims=True)), 0.0)
        l = jnp.sum(e, axis=0, keepdims=True)
        p = e * (1.0 / jnp.where(l > 0.0, l, 1.0))
        rest = n_cmp - t["n_cmp"]
        pad = (lambda a: jnp.concatenate([a, jnp.zeros((rest, a.shape[1]), a.dtype)], axis=0)) if rest else (lambda a: a)
        t["out_c"] = jnp.dot(vc_ref[0, 0], pad(p).astype(BF16), preferred_element_type=F32)
        p_grp = pad(sum(p[:, h * tq:(h + 1) * tq] for h in range(A_HPG)))
        t["imp"] = sum(jnp.dot(cnt[0:t["n_slc"]], part, preferred_element_type=F32)
                       for part in _split3(p_grp))

    for t in tiles:
        s = t["s_w"]
        t["p_w"] = jnp.exp2(s - jnp.max(s, axis=0, keepdims=True)).astype(BF16)
    for t in tiles:
        acc = jnp.dot(values_t(vw_ref, t["first_w"], t["n_w"]), t["p_w"], preferred_element_type=F32)
        out_w = acc[:dh] * (1.0 / acc[dh:dh + 1])
        gates = t["gates"]
        t["part"] = [gates[h:h + 1, :] * t["out_c"][:, h * tq:(h + 1) * tq]
                     + gates[2 * A_HPG + h:2 * A_HPG + h + 1, :] * out_w[:, h * tq:(h + 1) * tq]
                     for h in range(A_HPG)]

    for t in tiles:
        nb = t["n_slc"]
        blk = lax.broadcasted_iota(jnp.int32, (nb, tq), 0)
        tpos = t["i"] * tq + lax.broadcasted_iota(jnp.int32, (nb, tq), 1)
        cur = tpos // SLC_BLOCK
        forced = (blk == 0) | (blk == cur) | (blk == cur - 1)
        causal = blk * SLC_BLOCK <= tpos
        imp = jnp.where(forced, FORCE_SCORE, jnp.where(causal, t["imp"], NEG_INF))
        rank = jnp.zeros((nb, tq), F32)
        for c in range(nb):
            row = imp[c:c + 1, :]
            ahead = (row > imp) | ((row == imp) & (blk > c))
            rank = rank + jnp.where(ahead, 1.0, 0.0)
        pen = jnp.where(rank < float(min(SLC_TOPN, n_slc)), 0.0, -FORCE_SCORE)
        pen = jnp.concatenate([pen, jnp.zeros((LANES - dh - nb, tq), F32)], axis=0)
        t["q_aug"] = jnp.concatenate([t["q"], jnp.concatenate([pen] * A_HPG, axis=1).astype(BF16)], axis=0)

    sel_tile = lambda r: min(r, TILE_FAR)

    jobs = [(t, c, min(SUB, t["i"] - c * SUB + 1)) for t in tiles for c in range(t["last"], -1, -1)]
    ss = [scores(ks_ref, t["q_aug"], t["i"], c * SUB, n, sel_tile) for t, c, n in jobs]
    ms = [jnp.max(s, axis=0, keepdims=True) for s in ss]
    ps = [jnp.exp2(s - m).astype(BF16) for s, m in zip(ss, ms)]
    accs = [jnp.dot(values_t(vs_ref, c * SUB, n), p, preferred_element_type=F32) for (t, c, n), p in zip(jobs, ps)]

    for k, t in enumerate(tiles):
        mine = [n for n, job in enumerate(jobs) if job[0] is t]
        m = functools.reduce(jnp.maximum, [ms[n] for n in mine])
        acc = sum(jnp.exp2(ms[n] - m) * accs[n] for n in mine)
        out_s = acc[:dh] * (1.0 / acc[dh:dh + 1])
        gates = t["gates"]
        blocks = [t["part"][h] + gates[A_HPG + h:A_HPG + h + 1, :] * out_s[:, h * tq:(h + 1) * tq]
                  for h in range(A_HPG)]
        o_ref[0, k] = jnp.concatenate(blocks, axis=0).T.astype(BF16)


def _attention(q_t, kc, vc_t, ks, vs_t, kw, vw_t, bias_c, bias_d, gates_t):
    bsz, _, nt, _, _ = q_t.shape
    s = ks.shape[2]
    n_cmp = kc.shape[2]
    assert nt == ATT_TILES * SUB and WINDOW // TQ + 1 <= nt
    k_spec = pl.BlockSpec((1, 1, s, LANES), lambda b, g: (b, g, 0, 0))
    vt_spec = pl.BlockSpec((1, 1, nt, V_ROWS, TQ), lambda b, g: (b, g, 0, 0, 0))
    outs = []
    for j in range(SUB):
        per_tile = lambda spec: [spec(j + t * SUB) for t in range(ATT_TILES)]
        q_spec = lambda i: pl.BlockSpec((1, 1, 1, A_HEAD_DIM, A_HPG * TQ), lambda b, g: (b, g, i, 0, 0))
        bc_spec = lambda i: pl.BlockSpec((1, 1, n_cmp, A_HPG * TQ), lambda b, g: (i, g, 0, 0))
        gt_spec = lambda i: pl.BlockSpec((1, 1, 1, GATE_ROWS, TQ), lambda b, g: (b, g, i, 0, 0))
        outs.append(pl.pallas_call(
            functools.partial(_attn_kernel, j=j),
            grid=(bsz, A_KV_GROUPS),
            in_specs=(per_tile(q_spec)
                      + [pl.BlockSpec((1, 1, n_cmp, A_HEAD_DIM), lambda b, g: (b, g, 0, 0)),
                         pl.BlockSpec((1, 1, A_HEAD_DIM, n_cmp), lambda b, g: (b, g, 0, 0)),
                         k_spec, vt_spec, k_spec, vt_spec]
                      + per_tile(bc_spec)
                      + [pl.BlockSpec((1, N_BIAS_TILES, TQ, A_HPG * TQ), lambda b, g: (g, 0, 0, 0))]
                      + per_tile(gt_spec)),
            out_specs=pl.BlockSpec((1, ATT_TILES, TQ, A_HPG * A_HEAD_DIM), lambda b, g: (b, 0, 0, g)),
            out_shape=jax.ShapeDtypeStruct((bsz, ATT_TILES, TQ, A_WIDTH), BF16),
            compiler_params=pltpu.CompilerParams(dimension_semantics=("parallel", "parallel"),
                                                 vmem_limit_bytes=VMEM_LIMIT),
            name=f"attn{j}",
        )(*([q_t] * ATT_TILES), kc, vc_t, ks, vs_t, kw, vw_t, *([bias_c] * ATT_TILES), bias_d,
          *([gates_t] * ATT_TILES)))
    return outs


def _rwkv_kernel(rw_ref, ld_ref, lw_ref, lb_ref, o_ref, state_ref):
    cc = pl.program_id(1)
    n = B_HEAD_DIM
    nb, csz = rw_ref.shape[0], rw_ref.shape[1]

    @pl.when(cc == 0)
    def _():
        state_ref[...] = jnp.zeros(state_ref.shape, F32)

    ti = lax.broadcasted_iota(jnp.int32, (csz, LANES), 0)
    si = lax.broadcasted_iota(jnp.int32, (csz, LANES), 1) % n
    lower = si <= ti
    strict = si < ti
    eye = jnp.where(si == ti, 1.0, 0.0)
    tri = jnp.where(lax.broadcasted_iota(jnp.int32, (csz, csz), 1) <= lax.broadcasted_iota(jnp.int32, (csz, csz), 0),
                    1.0, 0.0).astype(BF16)
    n_pairs = B_WIDTH // LANES
    left =lax.broadcasted_iota(jnp.int32, (csz, LANES), 1) < n
    row_left = lax.broadcasted_iota(jnp.int32, (LANES, LANES), 0) < n
    same_head = row_left == (lax.broadcasted_iota(jnp.int32, (LANES, LANES), 1) < n)

    def blockdiag(y):
        zero = jnp.zeros_like(y)
        return jnp.concatenate([jnp.where(left, y, zero), jnp.where(left, zero, y)], axis=0)

    chains = []
    for bi in range(nb):
        r, kk, b, k_mod, v = (rw_ref[bi, :, m * B_WIDTH:(m + 1) * B_WIDTH].astype(F32) for m in range(5))
        ld = ld_ref[bi, :, 0:B_WIDTH]
        rsum = ld_ref[bi, :, B_WIDTH:2 * B_WIDTH]

        ld_hi, ld_lo = _split2(ld)
        cum = jnp.dot(tri, ld_hi, preferred_element_type=F32) + jnp.dot(tri, ld_lo, preferred_element_type=F32)
        g_inc = jnp.exp(cum)
        g_exc = jnp.exp(cum - ld)
        g_inv = jnp.exp(-cum)
        g_end = jnp.exp(cum[csz - 1:csz, :] - cum)
        g_all = g_inc[csz - 1:csz, :]

        for pr in range(n_pairs):
            sl = slice(pr * LANES, (pr + 1) * LANES)
            kk_p = kk[:, sl]
            b_p = b[:, sl]
            bt = (b_p * g_inv[:, sl]).astype(BF16)
            kt = (k_mod[:, sl] * g_inv[:, sl]).astype(BF16)
            ch = dict(
                idx=bi * n_pairs + pr,
                v=v[:, sl],
                lhs=jnp.concatenate([-kk_p * g_exc[:, sl], r[:, sl] * g_inc[:, sl]], axis=0).astype(BF16),
                rhs=jnp.concatenate([blockdiag(bt), blockdiag(kt)], axis=0),
                bk=jnp.concatenate([b_p * g_end[:, sl], k_mod[:, sl] * g_end[:, sl]], axis=0).astype(BF16),
                g_all=g_all[:, sl],
                bonus=rsum[:, sl] * v[:, sl],
            )
            chains.append(ch)

    for ch in chains:
        x = _dot_nt(ch["lhs"], ch["rhs"])
        xb, xk = x[:, :LANES], x[:, LANES:]
        ch["a_ab"] = jnp.where(strict, xb[:csz], 0.0)
        a_ak = jnp.where(strict, xk[:csz], 0.0)
        m_rk = jnp.where(lower, xk[csz:], 0.0)
        ch["ak_rk"] = jnp.concatenate([a_ak, m_rk], axis=0).astype(BF16)
        ch["m_rb"] = jnp.where(lower, xb[csz:], 0.0).astype(BF16)
    for ch in chains:
        akv = _dot(ch["ak_rk"], blockdiag(ch["v"].astype(BF16)))
        ch["akv"], ch["mrkv"] = akv[:csz], akv[csz:]
        ch["tinv"] = eye + ch["a_ab"]
        ch["pw"] = ch["a_ab"].astype(BF16)
    n_sq = int(math.log2(csz)) - 1
    for ch in chains:
        ch["pw"] = _dot(ch["pw"], blockdiag(ch["pw"])).astype(BF16)
    for step in range(n_sq):
        for ch in chains:
            if step + 1 < n_sq:
                both = _dot(jnp.concatenate([ch["pw"], ch["tinv"].astype(BF16)], axis=0), blockdiag(ch["pw"]))
                ch["tinv"] = ch["tinv"] + both[csz:]
                ch["pw"] = both[:csz].astype(BF16)
            else:
                ch["tinv"] = ch["tinv"] + _dot(ch["tinv"], blockdiag(ch["pw"]))
    for ch in chains:
        ch["s0"] = state_ref[ch["idx"]]
        ch["as0"] = _dot_nt(ch["lhs"], ch["s0"])
    for ch in chains:
        w = (ch["as0"][:csz] + ch["akv"]).astype(BF16)
        ch["u"] = _dot(ch["tinv"], blockdiag(w))
    outs = []
    for ch in chains:
        u = ch["u"]
        y = ch["as0"][csz:] + _dot(ch["m_rb"], blockdiag(u.astype(BF16))) + ch["mrkv"]
        uv = jnp.concatenate([u, ch["v"]], axis=0)
        state_ref[ch["idx"]] = ch["s0"] * ch["g_all"] + jnp.where(same_head, _dot_tn(uv, ch["bk"]), 0.0)
        yc = y - _head_sum(y) * (1.0 / n)
        var = _head_sum(yc * yc) * (1.0 / n)
        outs.append(yc * lax.rsqrt(var + LNX_EPS))
    for bi in range(nb):
        yn = jnp.concatenate(outs[bi * n_pairs:(bi + 1) * n_pairs], axis=-1)
        bonus = jnp.concatenate([ch["bonus"] for ch in chains[bi * n_pairs:(bi + 1) * n_pairs]], axis=-1)
        o_ref[bi] = (yn * lw_ref[...] + lb_ref[...] + bonus).astype(BF16)


RWKV_NB = 8


def _rwkv(scan_bf16, scan_f32, ln_w, ln_b):
    bsz, s, _ = scan_bf16.shape
    nb = RWKV_NB if bsz % RWKV_NB == 0 else 1
    const = lambda b, c: (0, 0)
    vec = pl.BlockSpec((1, B_WIDTH), const)
    return pl.pallas_call(
        _rwkv_kernel,
        grid=(bsz // nb, s // CHUNK),
        in_specs=[pl.BlockSpec((nb, CHUNK, SCAN_BF16_COLS), lambda b, c: (b, c, 0)),
                  pl.BlockSpec((nb, CHUNK, SCAN_F32_COLS), lambda b, c: (b, c, 0)),
                  vec, vec],
        out_specs=pl.BlockSpec((nb, CHUNK, B_WIDTH), lambda b, c: (b, c, 0)),
        out_shape=jax.ShapeDtypeStruct((bsz, s, B_WIDTH), BF16),
        scratch_shapes=[pltpu.VMEM((nb * B_WIDTH // LANES, LANES, LANES), F32)],
        compiler_params=pltpu.CompilerParams(dimension_semantics=("parallel", "arbitrary")),
        name="rwkv",
    )(scan_bf16, scan_f32, ln_w, ln_b)


def _final_kernel(x_ref, *refs):
    ya_refs, (yb_ref, cf_ref, gate_ref, wa_ref, wb_ref, wo_ref, o_ref) = refs[:SUB], refs[SUB:]
    a_silu = cf_ref[0, :, 0:A_WIDTH].astype(F32)
    b_silu = cf_ref[0, :, A_WIDTH:A_WIDTH + B_WIDTH].astype(F32)
    merge_a = cf_ref[0, :, A_WIDTH + B_WIDTH:A_WIDTH + B_WIDTH + D_MODEL].astype(F32)
    merge_b = cf_ref[0, :, A_WIDTH + B_WIDTH + D_MODEL:A_WIDTH + B_WIDTH + 2 * D_MODEL].astype(F32)
    y_a = jnp.concatenate([r[0, 0] for r in ya_refs], axis=0)
    ya = y_a.astype(F32) * (a_silu * _sigmoid(a_silu))
    yb = yb_ref[0].astype(F32) * (b_silu * _sigmoid(b_silu))
    merged = _sigmoid(merge_a) * _dot(ya, wa_ref[...]) + _sigmoid(merge_b) * _dot(yb, wb_ref[...])
    o_ref[0] = x_ref[0] + gate_ref[0] * _dot(merged, wo_ref[...])


def _final(x, y_a, y_b, cols_fin, gate, w_out_a, w_out_b, w_o):
    bsz, s, _ = x.shape
    tm = SUB * TQ
    const = lambda b, i: (0, 0)
    row = lambda w: pl.BlockSpec((1, tm, w), lambda b, i: (b, i, 0))
    ya_spec = pl.BlockSpec((1, 1, TQ, A_WIDTH), lambda b, i: (b, i, 0, 0))
    return pl.pallas_call(
        _final_kernel,
        grid=(bsz, s // tm),
        in_specs=[row(D_MODEL)] + [ya_spec] * SUB + [row(B_WIDTH), row(FIN_COLS),
                  pl.BlockSpec((1, 1, D_MODEL), lambda b, i: (b, 0, 0)),
                  pl.BlockSpec((A_WIDTH, D_MODEL), const),
                  pl.BlockSpec((B_WIDTH, D_MODEL), const),
                  pl.BlockSpec((D_MODEL, D_MODEL), const)],
        out_specs=row(D_MODEL),
        out_shape=jax.ShapeDtypeStruct((bsz, s, D_MODEL), F32),
        compiler_params=pltpu.CompilerParams(dimension_semantics=("parallel", "parallel"),
                                             vmem_limit_bytes=VMEM_LIMIT),
        name="final",
    )(x, *y_a, y_b, cols_fin, gate, w_out_a, w_out_b, w_o)


def _split_w_in(w_in):
    nsa_in = 2 * A_WIDTH + 6 * A_KV_WIDTH + 3 * A_HEADS
    o_gate = A_WIDTH + 6 * A_KV_WIDTH
    o_asilu = o_gate + 3 * A_HEADS
    o_shift = nsa_in
    o_rest = nsa_in + RWKV_COLS
    gate_w = w_in[:, o_gate:o_asilu].reshape(D_MODEL, 3, A_KV_GROUPS, A_HPG)
    gate_w = gate_w.transpose(0, 2, 1, 3).reshape(D_MODEL, A_KV_GROUPS, 3 * A_HPG)
    gate_w = jnp.pad(gate_w, ((0, 0), (0, 0), (0, A_HEAD_DIM - 3 * A_HPG))).reshape(D_MODEL, GATE_PAD)
    w_nsa = jnp.concatenate([w_in[:, :o_gate], gate_w], axis=1)
    w_fin = jnp.concatenate([w_in[:, o_asilu:o_shift], w_in[:, o_rest:]], axis=1)
    w_mix = jnp.concatenate([w_in[:, o_shift:o_rest], w_nsa], axis=1)
    return w_mix.astype(BF16), w_fin.astype(BF16)


def _layer(x, c, rel_bias, w_ada, b_ada, norm_gain, w_in, q_norm_gain, k_norm_gain,
           cmp_pos_k, cmp_pos_v, cmp_k_w1, cmp_k_w2, cmp_v_w1, cmp_v_w2,
           shift_mu, w0, w_lora_up, a0, a_lora_up, k_k, k_a, r_k, ln_x_w, ln_x_b,
           w_out_a, w_out_b, w_o):
    bsz, s, _ = x.shape
    assert s % (2 * TQ) == 0 and s // CMP_STRIDE == LANES
    n16 = s // CMP_STRIDE
    mod = _ada(c, w_ada, b_ada)
    w_mix, w_fin = _split_w_in(w_in)
    scale = A_HEAD_DIM ** -0.5 * LOG2E
    qg = jnp.tile(q_norm_gain, A_HEADS) * scale
    ksg = jnp.tile(k_norm_gain[1], A_KV_GROUPS)
    kwg = jnp.tile(k_norm_gain[2], A_KV_GROUPS)
    vec = lambda t: t.reshape(1, -1)
    rwkv_params = (vec(shift_mu), vec(w0), w_lora_up.astype(BF16), vec(a0), a_lora_up.astype(BF16),
                   vec(k_k), vec(k_a), vec(r_k))
    q_t, ks, vs_t, kw, vw_t, gates_t, ck, cols_fin, scan_bf16, scan_f32 = _proj(
        x, mod, norm_gain, w_mix, w_fin, qg, ksg, kwg, rwkv_params)

    kc, vc_t = _compress(ck, _expand_cmp_pos(cmp_pos_k), _expand_cmp_pos(cmp_pos_v),
                         _expand_cmp_w1(cmp_k_w1), cmp_k_w2.astype(BF16),
                         _expand_cmp_w1(cmp_v_w1), cmp_v_w2.T.astype(BF16),
                         k_norm_gain[0].reshape(1, A_HEAD_DIM))
    bias_c, bias_d = _bias_tables(rel_bias, s, n16)
    y_a = _attention(q_t, kc, vc_t, ks, vs_t, kw, vw_t, bias_c, bias_d, gates_t)

    y_b = _rwkv(scan_bf16, scan_f32, vec(ln_x_w), vec(ln_x_b))

    gate = mod[:, 2 * D_MODEL:].reshape(bsz, 1, D_MODEL)
    return _final(x, y_a, y_b, cols_fin, gate, w_out_a.astype(BF16), w_out_b.astype(BF16), w_o.astype(BF16))


def kernel(x, c, w_ada, b_ada, norm_gain, w_in, q_norm_gain, k_norm_gain, cmp_pos_k, cmp_pos_v, cmp_k_w1, cmp_k_w2, cmp_v_w1, cmp_v_w2, rel_bias, shift_mu, w0, w_lora_up, a0, a_lora_up, k_k, k_a, r_k, ln_x_w, ln_x_b, w_out_a, w_out_b, w_o):
    for l in range(w_in.shape[0]):
        x = _layer(x, c, rel_bias, w_ada[l], b_ada[l], norm_gain[l], w_in[l], q_norm_gain[l], k_norm_gain[l],
                   cmp_pos_k[l], cmp_pos_v[l], cmp_k_w1[l], cmp_k_w2[l], cmp_v_w1[l], cmp_v_w2[l],
                   shift_mu[l], w0[l], w_lora_up[l], a0[l], a_lora_up[l], k_k[l], k_a[l], r_k[l],
                   ln_x_w[l], ln_x_b[l], w_out_a[l], w_out_b[l], w_o[l])
    return x
```

```python
import functools
import math

import numpy as np
import jax
import jax.numpy as jnp
from jax import lax
from jax.experimental import pallas as pl
from jax.experimental.pallas import tpu as pltpu

F32 = jnp.float32
BF16 = jnp.bfloat16

D_MODEL = 1024
A_HEADS = 8
A_HEAD_DIM = 64
A_KV_GROUPS = 2
A_HPG = A_HEADS // A_KV_GROUPS
A_WIDTH = A_HEADS * A_HEAD_DIM
A_KV_WIDTH = A_KV_GROUPS * A_HEAD_DIM
CMP_BLOCK = 32
CMP_STRIDE = 16
CMP_HIDDEN = 256
SLC_BLOCK = 64
SLC_TOPN = 16
WINDOW = 512
B_HEADS = 8
B_HEAD_DIM = 64
B_WIDTH = B_HEADS * B_HEAD_DIM
DECAY_LORA = 64
ICLR_LORA = 64
LNX_EPS = 64e-5
REL_BUCKETS = 32
REL_MAX_EXACT = 16
REL_MAX_DIST = 128
NORM_EPS = 1e-6
NEG_INF = -1e30
FORCE_SCORE = 1e30

LANES = 128
TQ = 128
CHUNK = 64
GATE_PAD = LANES
LOG2E = math.log2(math.e)
V_ROWS = A_HEAD_DIM + 16
GATE_ROWS = 16
NSA_COLS = A_WIDTH + 6 * A_KV_WIDTH + GATE_PAD
FIN_COLS = A_WIDTH + B_WIDTH + 2 * D_MODEL
RWKV_COLS = 3 * B_WIDTH + DECAY_LORA + ICLR_LORA
SCAN_BF16_COLS = 5 * B_WIDTH
SCAN_F32_COLS = 2 * B_WIDTH
VMEM_LIMIT = 56 * 1024 * 1024


def _dot(a, b):
    return jnp.dot(a.astype(BF16), b.astype(BF16), preferred_element_type=F32)


def _dot_nt(a, b):
    return lax.dot_general(a.astype(BF16), b.astype(BF16), (((1,), (1,)), ((), ())),
                           preferred_element_type=F32)


def _dot_tn(a, b):
    return lax.dot_general(a.astype(BF16), b.astype(BF16), (((0,), (0,)), ((), ())),
                           preferred_element_type=F32)


def _split2(x):
    hi = x.astype(BF16)
    lo = (x - hi.astype(F32)).astype(BF16)
    return hi, lo


def _split3(x):
    h1 = x.astype(BF16)
    r1 = x - h1.astype(F32)
    h2 = r1.astype(BF16)
    h3 = (r1 - h2.astype(F32)).astype(BF16)
    return h1, h2, h3


def _sigmoid(x):
    return 1.0 / (1.0 + jnp.exp(-x))


def _bucket_thresholds():
    n = np.arange(0, 4096)
    nf = np.maximum(n, REL_MAX_EXACT).astype(np.float64)
    val = np.log(nf / REL_MAX_EXACT) / math.log(REL_MAX_DIST / REL_MAX_EXACT) * (REL_BUCKETS - REL_MAX_EXACT)
    frac = np.abs(val - np.round(val))
    assert np.all((frac > 1e-4) | (n <= REL_MAX_EXACT) | (n >= REL_MAX_DIST))
    large = REL_MAX_EXACT + np.floor(val + 1e-9).astype(np.int64)
    bucket = np.where(n < REL_MAX_EXACT, n, np.minimum(large, REL_BUCKETS - 1))
    return [int(np.argmax(bucket >= j)) for j in range(REL_BUCKETS)]


_BUCKET_TH = _bucket_thresholds()


def _bias_from_dist(dist, tbl_ref, head):
    val = jnp.full(dist.shape, tbl_ref[0, head], F32)
    for j in range(1, REL_BUCKETS):
        val = jnp.where(dist >= _BUCKET_TH[j], tbl_ref[j, head], val)
    return val


def _ada_kernel(c_ref, w_ref, b_ref, o_ref):
    c = c_ref[...]
    o_ref[...] = _dot(c * _sigmoid(c), w_ref[...]) + b_ref[...]


def _ada(c, w_ada, b_ada):
    bsz = c.shape[0]
    return pl.pallas_call(
        _ada_kernel,
        grid=(3,),
        in_specs=[pl.BlockSpec((bsz, D_MODEL), lambda j: (0, 0)),
                  pl.BlockSpec((D_MODEL, D_MODEL), lambda j: (0, j)),
                  pl.BlockSpec((1, D_MODEL), lambda j: (0, j))],
        out_specs=pl.BlockSpec((bsz, D_MODEL), lambda j: (0, j)),
        out_shape=jax.ShapeDtypeStruct((bsz, 3 * D_MODEL), F32),
        name="ada",
    )(c, w_ada, b_ada.reshape(1, 3 * D_MODEL))


def _norm_rows(x_t, gain_col, n_seg):
    out = []
    for seg in range(n_seg):
        blk = x_t[seg * A_HEAD_DIM:(seg + 1) * A_HEAD_DIM, :]
        ms = jnp.mean(blk * blk, axis=0, keepdims=True)
        out.append(blk * lax.rsqrt(ms + NORM_EPS) * gain_col[seg * A_HEAD_DIM:(seg + 1) * A_HEAD_DIM, :])
    return out


def _head_sum(x):
    left = lax.broadcasted_iota(jnp.int32, x.shape, x.ndim - 1) < B_HEAD_DIM
    lo = jnp.sum(jnp.where(left, x, 0.0), axis=-1, keepdims=True)
    hi = jnp.sum(jnp.where(left, 0.0, x), axis=-1, keepdims=True)
    return jnp.where(left, lo, hi)


def _proj_kernel(x_ref, mod_ref, g_ref, wm_ref, wf_ref, qg_ref, ksg_ref, kwg_ref,
                 mu_ref, w0_ref, wl_ref, a0_ref, al_ref, kk_ref, ka_ref, rk_ref,
                 q_ref, ks_ref, vs_ref, kw_ref, vw_ref, gt_ref, ck_ref, of_ref, rw_ref, ld_ref, prev_ref):
    tm = x_ref.shape[1]

    @pl.when(pl.program_id(1) == 0)
    def _():
        prev_ref[...] = jnp.zeros(prev_ref.shape, F32)

    x = x_ref[0]
    ms = jnp.mean(x * x, axis=-1, keepdims=True)
    y = x * lax.rsqrt(ms + NORM_EPS) * g_ref[...]
    mod = mod_ref[0]
    h = (y * (1.0 + mod[:, D_MODEL:2 * D_MODEL]) + mod[:, :D_MODEL]).astype(BF16)
    crn = jnp.dot(h, wm_ref[...], preferred_element_type=F32)
    cr = crn[:, 0:RWKV_COLS]
    cn = crn[:, RWKV_COLS:RWKV_COLS + NSA_COLS]
    ck_ref[0] = cn[:, A_WIDTH:A_WIDTH + 2 * A_KV_WIDTH]

    rolled = pltpu.roll(cr, 1, axis=0)
    row8 = lax.broadcasted_iota(jnp.int32, (8, RWKV_COLS), 0)
    prev = jnp.concatenate([jnp.where(row8 == 0, prev_ref[0:1, :], rolled[0:8]), rolled[8:]], axis=0)
    prev_ref[0:1, :] = cr[tm - 1:tm, :]
    xs = cr + (prev - cr) * mu_ref[...]
    r = xs[:, 0:B_WIDTH]
    k = xs[:, B_WIDTH:2 * B_WIDTH]
    v = xs[:, 2 * B_WIDTH:3 * B_WIDTH]
    wd = xs[:, 3 * B_WIDTH:3 * B_WIDTH + DECAY_LORA]
    ad = xs[:, 3 * B_WIDTH + DECAY_LORA:3 * B_WIDTH + DECAY_LORA + ICLR_LORA]
    ld = -math.exp(-0.5) * _sigmoid(w0_ref[...] + _dot(jnp.tanh(wd), wl_ref[...]))
    a = _sigmoid(a0_ref[...] + _dot(ad, al_ref[...]))
    kk = k * kk_ref[...]
    k_mod = k * (1.0 + (a - 1.0) * ka_ref[...])
    rkr = r * k_mod * rk_ref[...]
    ld_ref[0, :, 0:B_WIDTH] = ld
    rw_ref[0, :, 0:B_WIDTH] = r.astype(BF16)
    rw_ref[0, :, 3 * B_WIDTH:4 * B_WIDTH] = k_mod.astype(BF16)
    rw_ref[0, :, 4 * B_WIDTH:5 * B_WIDTH] = v.astype(BF16)
    for pr in range(B_WIDTH // LANES):
        sl = slice(pr * LANES, (pr + 1) * LANES)
        kk_p = kk[:, sl]
        kk_p = kk_p * lax.rsqrt(jnp.maximum(_head_sum(kk_p * kk_p), 1e-24))
        rw_ref[0, :, B_WIDTH + pr * LANES:B_WIDTH + (pr + 1) * LANES] = kk_p.astype(BF16)
        rw_ref[0, :, 2 * B_WIDTH + pr * LANES:2 * B_WIDTH + (pr + 1) * LANES] = (kk_p * a[:, sl]).astype(BF16)
        ld_ref[0, :, B_WIDTH + pr * LANES:B_WIDTH + (pr + 1) * LANES] = _head_sum(rkr[:, sl])

    lane = lax.broadcasted_iota(jnp.int32, (TQ, LANES), 1)
    row = lax.broadcasted_iota(jnp.int32, (TQ, LANES), 0)
    ones_rows = (lax.broadcasted_iota(jnp.int32, (V_ROWS - A_HEAD_DIM, TQ), 0) == 0).astype(BF16)
    off = A_WIDTH + 2 * A_KV_WIDTH
    for sub in range(tm // TQ):
        c = cn[sub * TQ:(sub + 1) * TQ]
        q_heads = _norm_rows(c[:, 0:A_WIDTH].T, qg_ref[...], A_HEADS)
        ks_t = jnp.concatenate(_norm_rows(c[:, off:off + A_KV_WIDTH].T, ksg_ref[...], A_KV_GROUPS), axis=0)
        kw_t = jnp.concatenate(_norm_rows(c[:, off + 2 * A_KV_WIDTH:off + 3 * A_KV_WIDTH].T, kwg_ref[...],
                                          A_KV_GROUPS), axis=0)
        ksn = ks_t.T
        kwn = kw_t.T
        vs_t = c[:, off + A_KV_WIDTH:off + 2 * A_KV_WIDTH].T.astype(BF16)
        vw_t = c[:, off + 3 * A_KV_WIDTH:off + 4 * A_KV_WIDTH].T.astype(BF16)
        gates_t = _sigmoid(c[:, off + 4 * A_KV_WIDTH:off + 5 * A_KV_WIDTH]).T
        blk = (pl.program_id(1) * tm + sub * TQ + row) // SLC_BLOCK
        onehot = jnp.where(lane - A_HEAD_DIM == blk, 1.0, 0.0)
        for g in range(A_KV_GROUPS):
            q_ref[0, g, sub] = jnp.concatenate(q_heads[g * A_HPG:(g + 1) * A_HPG], axis=1).astype(BF16)
            sl = slice(g * A_HEAD_DIM, (g + 1) * A_HEAD_DIM)
            k_g = ksn if g == 0 else pltpu.roll(ksn, A_HEAD_DIM, axis=1)
            ks_ref[0, g, sub * TQ:(sub + 1) * TQ, :] = jnp.where(lane < A_HEAD_DIM, k_g, onehot).astype(BF16)
            kw_g = kwn if g == 0 else pltpu.roll(kwn, A_HEAD_DIM, axis=1)
            kw_ref[0, g, sub * TQ:(sub + 1) * TQ, :] = jnp.where(lane < A_HEAD_DIM, kw_g, 0.0).astype(BF16)
            vs_ref[0, g, sub] = jnp.concatenate([vs_t[sl, :], ones_rows], axis=0)
            vw_ref[0, g, sub] = jnp.concatenate([vw_t[sl, :], ones_rows], axis=0)
            gt_ref[0, g, sub] = gates_t[g * A_HEAD_DIM:g * A_HEAD_DIM + GATE_ROWS, :]
    of_ref[0] = jnp.dot(h, wf_ref[...], preferred_element_type=F32).astype(BF16)


def _proj(x, mod, norm_gain, w_mix, w_fin, qg, ksg, kwg, rwkv_params, tm=512):
    bsz, s, _ = x.shape
    nt, nsub = s // TQ, tm // TQ
    assert A_HEAD_DIM + s // SLC_BLOCK <= LANES and A_KV_WIDTH == LANES and 2 * B_HEAD_DIM == LANES
    const = lambda b, i: (0, 0)
    weight = lambda cols: pl.BlockSpec((D_MODEL, cols), const, pipeline_mode=pl.Buffered(1))
    whole = lambda t: pl.BlockSpec(t.shape, const)
    col = lambda t: jnp.broadcast_to(t.reshape(-1, 1), (t.size, LANES))
    k_spec = lambda width: pl.BlockSpec((1, A_KV_GROUPS, tm, width), lambda b, i: (b, 0, i, 0))
    k_shape = lambda width: jax.ShapeDtypeStruct((bsz, A_KV_GROUPS, s, width), BF16)
    tile_spec = lambda r, c: pl.BlockSpec((1, A_KV_GROUPS, nsub, r, c), lambda b, i: (b, 0, i, 0, 0))
    tile_shape = lambda r, c, dt: jax.ShapeDtypeStruct((bsz, A_KV_GROUPS, nt, r, c), dt)
    return pl.pallas_call(
        _proj_kernel,
        grid=(bsz, s // tm),
        in_specs=[pl.BlockSpec((1, tm, D_MODEL), lambda b, i: (b, i, 0)),
                  pl.BlockSpec((1, 1, 3 * D_MODEL), lambda b, i: (b, 0, 0)),
                  pl.BlockSpec((1, D_MODEL), const),
                  weight(RWKV_COLS + NSA_COLS), weight(FIN_COLS),
                  pl.BlockSpec((A_WIDTH, LANES), const),
                  pl.BlockSpec((A_KV_WIDTH, LANES), const),
                  pl.BlockSpec((A_KV_WIDTH, LANES), const)] + [whole(t) for t in rwkv_params],
        out_specs=[tile_spec(A_HEAD_DIM, A_HPG * TQ),
                   k_spec(LANES), tile_spec(V_ROWS, TQ), k_spec(LANES), tile_spec(V_ROWS, TQ),
                   tile_spec(GATE_ROWS, TQ),
                   pl.BlockSpec((1, tm, 2 * A_KV_WIDTH), lambda b, i: (b, i, 0)),
                   pl.BlockSpec((1, tm, FIN_COLS), lambda b, i: (b, i, 0)),
                   pl.BlockSpec((1, tm, SCAN_BF16_COLS), lambda b, i: (b, i, 0)),
                   pl.BlockSpec((1, tm, SCAN_F32_COLS), lambda b, i: (b, i, 0))],
        out_shape=[tile_shape(A_HEAD_DIM, A_HPG * TQ, BF16),
                   k_shape(LANES), tile_shape(V_ROWS, TQ, BF16), k_shape(LANES), tile_shape(V_ROWS, TQ, BF16),
                   tile_shape(GATE_ROWS, TQ, F32),
                   jax.ShapeDtypeStruct((bsz, s, 2 * A_KV_WIDTH), F32),
                   jax.ShapeDtypeStruct((bsz, s, FIN_COLS), BF16),
                   jax.ShapeDtypeStruct((bsz, s, SCAN_BF16_COLS), BF16),
                   jax.ShapeDtypeStruct((bsz, s, SCAN_F32_COLS), F32)],
        scratch_shapes=[pltpu.VMEM((8, RWKV_COLS), F32)],
        compiler_params=pltpu.CompilerParams(dimension_semantics=("parallel", "arbitrary"),
                                             vmem_limit_bytes=VMEM_LIMIT),
        name="proj",
    )(x, mod.reshape(bsz, 1, 3 * D_MODEL), norm_gain.reshape(1, D_MODEL), w_mix, w_fin,
      col(qg), col(ksg), col(kwg), *rwkv_params)


def _compress_kernel(ck_ref, cv_ref, pk_ref, pv_ref, w1k_ref, w2k_ref, w1v_ref, w2v_ref, kg_ref, kc_ref, vc_ref):
    nb, n16 = ck_ref.shape[0], ck_ref.shape[1] // CMP_STRIDE

    def rows16(ref):
        return jnp.concatenate(
            [jnp.concatenate([ref[b, pl.ds(p, n16, stride=CMP_STRIDE), :] for p in range(CMP_STRIDE)], axis=1)
             for b in range(nb)], axis=0)

    def hidden(z, pos_ref, w1_ref, g):
        top = _dot(z + pos_ref[0:1, :], w1_ref[g, 0])
        bot = _dot(z + pos_ref[1:2, :], w1_ref[g, 1])
        bot = jnp.concatenate([pltpu.roll(bot[b * n16:(b + 1) * n16], n16 - 1, axis=0) for b in range(nb)], axis=0)
        return jax.nn.gelu(top + bot, approximate=True)

    zk = rows16(ck_ref)
    zv = rows16(cv_ref)
    for g in range(A_KV_GROUPS):
        kc = _dot(hidden(zk, pk_ref, w1k_ref, g), w2k_ref[...])
        ms = jnp.mean(kc * kc, axis=-1, keepdims=True)
        kc = (kc * lax.rsqrt(ms + NORM_EPS) * kg_ref[...]).astype(BF16)
        vc_t = _dot_nt(w2v_ref[...], hidden(zv, pv_ref, w1v_ref, g)).astype(BF16)
        for b in range(nb):
            kc_ref[b, g] = kc[b * n16:(b + 1) * n16]
            vc_ref[b, g] = vc_t[:, b * n16:(b + 1) * n16]


CMP_NB = 4


def _expand_cmp_w1(w1):
    w = w1.reshape(2, CMP_STRIDE, 1, A_HEAD_DIM, CMP_HIDDEN)
    per_group = []
    for g in range(A_KV_GROUPS):
        pad = [(0, 0), (0, 0), (g, A_KV_GROUPS - 1 - g), (0, 0), (0, 0)]
        per_group.append(jnp.pad(w, pad).reshape(2, CMP_STRIDE * A_KV_WIDTH, CMP_HIDDEN))
    return jnp.stack(per_group).astype(BF16)


def _expand_cmp_pos(pos):
    p = jnp.broadcast_to(pos.reshape(2, CMP_STRIDE, 1, A_HEAD_DIM), (2, CMP_STRIDE, A_KV_GROUPS, A_HEAD_DIM))
    return p.reshape(2, CMP_STRIDE * A_KV_WIDTH)


def _compress(ck, pk, pv, w1k, w2k, w1v, w2v_t, kg):
    bsz, s, _ = ck.shape
    n16 = s // CMP_STRIDE
    zw = CMP_STRIDE * A_KV_WIDTH
    const = lambda b: (0, 0)
    const4 = lambda b: (0, 0, 0, 0)
    nb = CMP_NB if bsz % CMP_NB == 0 else 1
    return pl.pallas_call(
        _compress_kernel,
        grid=(bsz // nb,),
        in_specs=[pl.BlockSpec((nb, s, A_KV_WIDTH), lambda b: (b, 0, 0)),
                  pl.BlockSpec((nb, s, A_KV_WIDTH), lambda b: (b, 0, 1)),
                  pl.BlockSpec((2, zw), const), pl.BlockSpec((2, zw), const),
                  pl.BlockSpec((A_KV_GROUPS, 2, zw, CMP_HIDDEN), const4), pl.BlockSpec((CMP_HIDDEN, A_HEAD_DIM), const),
                  pl.BlockSpec((A_KV_GROUPS, 2, zw, CMP_HIDDEN), const4), pl.BlockSpec((A_HEAD_DIM, CMP_HIDDEN), const),
                  pl.BlockSpec((1, A_HEAD_DIM), const)],
        out_specs=[pl.BlockSpec((nb, A_KV_GROUPS, n16, A_HEAD_DIM), lambda b: (b, 0, 0, 0)),
                   pl.BlockSpec((nb, A_KV_GROUPS, A_HEAD_DIM, n16), lambda b: (b, 0, 0, 0))],
        out_shape=[jax.ShapeDtypeStruct((bsz, A_KV_GROUPS, n16, A_HEAD_DIM), BF16),
                   jax.ShapeDtypeStruct((bsz, A_KV_GROUPS, A_HEAD_DIM, n16), BF16)],
        compiler_params=pltpu.CompilerParams(dimension_semantics=("parallel",)),
        name="compress",
    )(ck, ck, pk, pv, w1k, w2k, w1v, w2v_t, kg)


TILE_FAR, TILE_EDGE, N_BIAS_TILES = 2, 3, 4
SUB = 4
ATT_CALLS = 1
ATT_TILES = 16


def _bias_cmp_kernel(tbl_ref, o_ref):
    i = pl.program_id(0)
    g = pl.program_id(1)
    n_cmp = o_ref.shape[2]
    n = lax.broadcasted_iota(jnp.int32, (n_cmp, TQ), 0)
    q = lax.broadcasted_iota(jnp.int32, (n_cmp, TQ), 1)
    dist = i * TQ + q - (n * CMP_STRIDE + CMP_BLOCK - 1)
    for h in range(A_HPG):
        bias = _bias_from_dist(dist, tbl_ref, g * A_HPG + h)
        o_ref[0, 0, :, h * TQ:(h + 1) * TQ] = jnp.where(dist >= 0, bias * LOG2E, NEG_INF)


def _bias_toeplitz_kernel(tbl_ref, o_ref):
    g = pl.program_id(0)
    r = pl.program_id(1)
    off = jnp.where(r == TILE_EDGE, WINDOW // TQ, r)
    k = lax.broadcasted_iota(jnp.int32, (TQ, TQ), 0)
    q = lax.broadcasted_iota(jnp.int32, (TQ, TQ), 1)
    dist = off * TQ + q - k
    valid = (dist >= 0) & (dist < WINDOW)
    for h in range(A_HPG):
        head = g * A_HPG + h
        bias = _bias_from_dist(dist, tbl_ref, head) - tbl_ref[REL_BUCKETS - 1, head]
        o_ref[0, 0, :, h * TQ:(h + 1) * TQ] = jnp.where(valid, bias * LOG2E, NEG_INF)


def _bias_tables(rel_bias, s, n_cmp):
    smem = pl.BlockSpec(memory_space=pltpu.SMEM)
    nt = s // TQ
    bias_c = pl.pallas_call(
        _bias_cmp_kernel,
        grid=(nt, A_KV_GROUPS),
        in_specs=[smem],
        out_specs=pl.BlockSpec((1, 1, n_cmp, A_HPG * TQ), lambda i, g: (i, g, 0, 0)),
        out_shape=jax.ShapeDtypeStruct((nt, A_KV_GROUPS, n_cmp, A_HPG * TQ), F32),
        name="bias_cmp",
    )(rel_bias)
    assert _BUCKET_TH[REL_BUCKETS - 1] <= TQ + 1 and WINDOW // TQ >= 3
    bias_d = pl.pallas_call(
        _bias_toeplitz_kernel,
        grid=(A_KV_GROUPS, N_BIAS_TILES),
        in_specs=[smem],
        out_specs=pl.BlockSpec((1, 1, TQ, A_HPG * TQ), lambda g, r: (g, r, 0, 0)),
        out_shape=jax.ShapeDtypeStruct((A_KV_GROUPS, N_BIAS_TILES, TQ, A_HPG * TQ), F32),
        name="bias_toeplitz",
    )(rel_bias)
    return bias_c, bias_d


def _attn_kernel(*refs, j):
    q_refs, refs = refs[:ATT_TILES], refs[ATT_TILES:]
    kc_ref, vc_ref, ks_ref, vs_ref, kw_ref, vw_ref = refs[:6]
    bc_refs, bd_ref, gt_refs, o_ref = refs[6:6 + ATT_TILES], refs[6 + ATT_TILES], refs[7 + ATT_TILES:-1], refs[-1]
    tq = TQ
    n_cmp = kc_ref.shape[2]
    n_slc = ks_ref.shape[2] // SLC_BLOCK
    wt = WINDOW // tq
    dh = A_HEAD_DIM
    round8 = lambda n: -(-n // 8) * 8
    tiles = [dict(i=j + t * ATT_CALLS, last=(j + t * ATT_CALLS) // SUB, n_tok=(j + t * ATT_CALLS + 1) * tq,
                  q=q_refs[t][0, 0, 0], bias_c=bc_refs[t], gates=gt_refs[t][0, 0, 0]) for t in range(ATT_TILES)]
    zero_rows = jnp.zeros((LANES - dh, A_HPG * tq), BF16)

    def scores(k_ref, q_mat, i, first_tile, n_sub, tile_index):
        s = jnp.dot(k_ref[0, 0, first_tile * tq:(first_tile + n_sub) * tq, :], q_mat,
                    preferred_element_type=F32)
        idx = [tile_index(i - (first_tile + t)) for t in range(n_sub)]
        parts = [s[t * tq:(t + 1) * tq] if idx[t] == TILE_FAR else s[t * tq:(t + 1) * tq] + bd_ref[0, idx[t]]
                 for t in range(n_sub)]
        return jnp.concatenate(parts, axis=0)

    def values_t(v_ref, first_tile, n_sub):
        return jnp.concatenate([v_ref[0, 0, first_tile + t] for t in range(n_sub)], axis=1)

    win_tile = lambda r: TILE_EDGE if r == wt else min(r, TILE_FAR)
    for t in tiles:
        t["first_w"] = max(t["i"] - wt, 0)
        t["n_w"] = t["i"] - t["first_w"] + 1
        q_pad = jnp.concatenate([t["q"], zero_rows], axis=0)
        t["s_w"] = scores(kw_ref, q_pad, t["i"], t["first_w"], t["n_w"], win_tile)
    for t in tiles:
        t["n_cmp"] = min(n_cmp, round8(t["n_tok"] // CMP_STRIDE))
        t["n_slc"] = min(n_slc, round8(t["n_tok"] // SLC_BLOCK))
        bias = t["bias_c"][0, 0, 0:t["n_cmp"], :]
        t["valid_c"] = bias > 0.5 * NEG_INF
        t["s_c"] = jnp.dot(kc_ref[0, 0, 0:t["n_cmp"], :], t["q"], preferred_element_type=F32) + bias

    r1, r2 = SLC_BLOCK // CMP_STRIDE, CMP_BLOCK // CMP_STRIDE
    jj = lax.broadcasted_iota(jnp.int32, (n_slc, n_cmp), 0)
    nn = lax.broadcasted_iota(jnp.int32, (n_slc, n_cmp), 1)
    d = nn - r1 * jj
    cnt = jnp.zeros((n_slc, n_cmp), F32)
    for a in range(r1):
        for c in range(r2):
            cnt = cnt + jnp.where(d == a - c, 1.0, 0.0)
    cnt = cnt.astype(BF16)
    for t in tiles:
        s = t["s_c"]
        e = jnp.where(t["valid_c"], jnp.exp2(s - jnp.max(s, axis=0, keepdims=True)), 0.0)
        l = jnp.sum(e, axis=0, keepdims=True)
        p = e * (1.0 / jnp.where(l > 0.0, l, 1.0))
        rest = n_cmp - t["n_cmp"]
        pad = (lambda a: jnp.concatenate([a, jnp.zeros((rest, a.shape[1]), a.dtype)], axis=0)) if rest else (lambda a: a)
        t["out_c"] = jnp.dot(vc_ref[0, 0], pad(p).astype(BF16), preferred_element_type=F32)
        p_grp = pad(sum(p[:, h * tq:(h + 1) * tq] for h in range(A_HPG)))
        t["imp"] = sum(jnp.dot(cnt[0:t["n_slc"]], part, preferred_element_type=F32)
                       for part in _split3(p_grp))

    for t in tiles:
        s = t["s_w"]
        t["p_w"] = jnp.exp2(s - jnp.max(s, axis=0, keepdims=True)).astype(BF16)
    for t in tiles:
        acc = jnp.dot(values_t(vw_ref, t["first_w"], t["n_w"]), t["p_w"], preferred_element_type=F32)
        out_w = acc[:dh] * (1.0 / acc[dh:dh + 1])
        gates = t["gates"]
        t["part"] = [gates[h:h + 1, :] * t["out_c"][:, h * tq:(h + 1) * tq]
                     + gates[2 * A_HPG + h:2 * A_HPG + h + 1, :] * out_w[:, h * tq:(h + 1) * tq]
                     for h in range(A_HPG)]

    for t in tiles:
        nb = t["n_slc"]
        blk = lax.broadcasted_iota(jnp.int32, (nb, tq), 0)
        tpos = t["i"] * tq + lax.broadcasted_iota(jnp.int32, (nb, tq), 1)
        cur = tpos // SLC_BLOCK
        forced = (blk == 0) | (blk == cur) | (blk == cur - 1)
        causal = blk * SLC_BLOCK <= tpos
        imp = jnp.where(forced, FORCE_SCORE, jnp.where(causal, t["imp"], NEG_INF))
        rank = jnp.zeros((nb, tq), F32)
        for c in range(nb):
            row = imp[c:c + 1, :]
            ahead = (row > imp) | ((row == imp) & (blk > c))
            rank = rank + jnp.where(ahead, 1.0, 0.0)
        pen = jnp.where(rank < float(min(SLC_TOPN, n_slc)), 0.0, -FORCE_SCORE)
        pen = jnp.concatenate([pen, jnp.zeros((LANES - dh - nb, tq), F32)], axis=0)
        t["q_aug"] = jnp.concatenate([t["q"], jnp.concatenate([pen] * A_HPG, axis=1).astype(BF16)], axis=0)

    sel_tile = lambda r: min(r, TILE_FAR)

    jobs = [(t, c, min(SUB, t["i"] - c * SUB + 1)) for t in tiles for c in range(t["last"], -1, -1)]
    ss = [scores(ks_ref, t["q_aug"], t["i"], c * SUB, n, sel_tile) for t, c, n in jobs]
    ms = [jnp.max(s, axis=0, keepdims=True) for s in ss]
    ps = [jnp.exp2(s - m).astype(BF16) for s, m in zip(ss, ms)]
    accs = [jnp.dot(values_t(vs_ref, c * SUB, n), p, preferred_element_type=F32) for (t, c, n), p in zip(jobs, ps)]

    for k, t in enumerate(tiles):
        mine = [n for n, job in enumerate(jobs) if job[0] is t]
        m = functools.reduce(jnp.maximum, [ms[n] for n in mine])
        acc = sum(jnp.exp2(ms[n] - m) * accs[n] for n in mine)
        out_s = acc[:dh] * (1.0 / acc[dh:dh + 1])
        gates = t["gates"]
        blocks = [t["part"][h] + gates[A_HPG + h:A_HPG + h + 1, :] * out_s[:, h * tq:(h + 1) * tq]
                  for h in range(A_HPG)]
        o_ref[0, k] = jnp.concatenate(blocks, axis=0).T.astype(BF16)


def _attention(q_t, kc, vc_t, ks, vs_t, kw, vw_t, bias_c, bias_d, gates_t):
    bsz, _, nt, _, _ = q_t.shape
    s = ks.shape[2]
    n_cmp = kc.shape[2]
    assert nt == ATT_TILES * ATT_CALLS and WINDOW // TQ + 1 <= nt
    k_spec = pl.BlockSpec((1, 1, s, LANES), lambda b, g: (b, g, 0, 0))
    vt_spec = pl.BlockSpec((1, 1, nt, V_ROWS, TQ), lambda b, g: (b, g, 0, 0, 0))
    outs = []
    for j in range(ATT_CALLS):
        per_tile = lambda spec: [spec(j + t * ATT_CALLS) for t in range(ATT_TILES)]
        q_spec = lambda i: pl.BlockSpec((1, 1, 1, A_HEAD_DIM, A_HPG * TQ), lambda b, g: (b, g, i, 0, 0))
        bc_spec = lambda i: pl.BlockSpec((1, 1, n_cmp, A_HPG * TQ), lambda b, g: (i, g, 0, 0))
        gt_spec = lambda i: pl.BlockSpec((1, 1, 1, GATE_ROWS, TQ), lambda b, g: (b, g, i, 0, 0))
        outs.append(pl.pallas_call(
            functools.partial(_attn_kernel, j=j),
            grid=(bsz, A_KV_GROUPS),
            in_specs=(per_tile(q_spec)
                      + [pl.BlockSpec((1, 1, n_cmp, A_HEAD_DIM), lambda b, g: (b, g, 0, 0)),
                         pl.BlockSpec((1, 1, A_HEAD_DIM, n_cmp), lambda b, g: (b, g, 0, 0)),
                         k_spec, vt_spec, k_spec, vt_spec]
                      + per_tile(bc_spec)
                      + [pl.BlockSpec((1, N_BIAS_TILES, TQ, A_HPG * TQ), lambda b, g: (g, 0, 0, 0))]
                      + per_tile(gt_spec)),
            out_specs=pl.BlockSpec((1, ATT_TILES, TQ, A_HPG * A_HEAD_DIM), lambda b, g: (b, 0, 0, g)),
            out_shape=jax.ShapeDtypeStruct((bsz, ATT_TILES, TQ, A_WIDTH), BF16),
            compiler_params=pltpu.CompilerParams(dimension_semantics=("parallel", "parallel"),
                                                 vmem_limit_bytes=VMEM_LIMIT),
            name=f"attn{j}",
        )(*([q_t] * ATT_TILES), kc, vc_t, ks, vs_t, kw, vw_t, *([bias_c] * ATT_TILES), bias_d,
          *([gates_t] * ATT_TILES)))
    return outs


def _rwkv_kernel(rw_ref, ld_ref, lw_ref, lb_ref, o_ref, state_ref):
    cc = pl.program_id(1)
    n = B_HEAD_DIM
    nb, csz = rw_ref.shape[0], rw_ref.shape[1]

    @pl.when(cc == 0)
    def _():
        state_ref[...] = jnp.zeros(state_ref.shape, F32)

    ti = lax.broadcasted_iota(jnp.int32, (csz, LANES), 0)
    si = lax.broadcasted_iota(jnp.int32, (csz, LANES), 1) % n
    lower = si <= ti
    strict = si < ti
    eye = jnp.where(si == ti, 1.0, 0.0)
    tri = jnp.where(lax.broadcasted_iota(jnp.int32, (csz, csz), 1) <= lax.broadcasted_iota(jnp.int32, (csz, csz), 0),
                    1.0, 0.0).astype(BF16)
    n_pairs = B_WIDTH // LANES
    left =lax.broadcasted_iota(jnp.int32, (csz, LANES), 1) < n
    row_left = lax.broadcasted_iota(jnp.int32, (LANES, LANES), 0) < n
    same_head = row_left == (lax.broadcasted_iota(jnp.int32, (LANES, LANES), 1) < n)

    def blockdiag(y):
        zero = jnp.zeros_like(y)
        return jnp.concatenate([jnp.where(left, y, zero), jnp.where(left, zero, y)], axis=0)

    chains = []
    for bi in range(nb):
        r, kk, b, k_mod, v = (rw_ref[bi, :, m * B_WIDTH:(m + 1) * B_WIDTH].astype(F32) for m in range(5))
        ld = ld_ref[bi, :, 0:B_WIDTH]
        rsum = ld_ref[bi, :, B_WIDTH:2 * B_WIDTH]

        ld_hi, ld_lo = _split2(ld)
        cum = jnp.dot(tri, ld_hi, preferred_element_type=F32) + jnp.dot(tri, ld_lo, preferred_element_type=F32)
        g_inc = jnp.exp(cum)
        g_exc = jnp.exp(cum - ld)
        g_inv = jnp.exp(-cum)
        g_end = jnp.exp(cum[csz - 1:csz, :] - cum)
        g_all = g_inc[csz - 1:csz, :]

        for pr in range(n_pairs):
            sl = slice(pr * LANES, (pr + 1) * LANES)
            kk_p = kk[:, sl]
            b_p = b[:, sl]
            bt = (b_p * g_inv[:, sl]).astype(BF16)
            kt = (k_mod[:, sl] * g_inv[:, sl]).astype(BF16)
            ch = dict(
                idx=bi * n_pairs + pr,
                v=v[:, sl],
                lhs=jnp.concatenate([-kk_p * g_exc[:, sl], r[:, sl] * g_inc[:, sl]], axis=0).astype(BF16),
                rhs=jnp.concatenate([blockdiag(bt), blockdiag(kt)], axis=0),
                bk=jnp.concatenate([b_p * g_end[:, sl], k_mod[:, sl] * g_end[:, sl]], axis=0).astype(BF16),
                g_all=g_all[:, sl],
                bonus=rsum[:, sl] * v[:, sl],
            )
            chains.append(ch)

    for ch in chains:
        x = _dot_nt(ch["lhs"], ch["rhs"])
        xb, xk = x[:, :LANES], x[:, LANES:]
        ch["a_ab"] = jnp.where(strict, xb[:csz], 0.0)
        a_ak = jnp.where(strict, xk[:csz], 0.0)
        m_rk = jnp.where(lower, xk[csz:], 0.0)
        ch["ak_rk"] = jnp.concatenate([a_ak, m_rk], axis=0).astype(BF16)
        ch["m_rb"] = jnp.where(lower, xb[csz:], 0.0).astype(BF16)
    for ch in chains:
        akv = _dot(ch["ak_rk"], blockdiag(ch["v"].astype(BF16)))
        ch["akv"], ch["mrkv"] = akv[:csz], akv[csz:]
        ch["tinv"] = eye + ch["a_ab"]
        ch["pw"] = ch["a_ab"].astype(BF16)
    n_sq = int(math.log2(csz)) - 1
    for ch in chains:
        ch["pw"] = _dot(ch["pw"], blockdiag(ch["pw"])).astype(BF16)
    for step in range(n_sq):
        for ch in chains:
            if step + 1 < n_sq:
                both = _dot(jnp.concatenate([ch["pw"], ch["tinv"].astype(BF16)], axis=0), blockdiag(ch["pw"]))
                ch["tinv"] = ch["tinv"] + both[csz:]
                ch["pw"] = both[:csz].astype(BF16)
            else:
                ch["tinv"] = ch["tinv"] + _dot(ch["tinv"], blockdiag(ch["pw"]))
    for ch in chains:
        ch["s0"] = state_ref[ch["idx"]]
        ch["as0"] = _dot_nt(ch["lhs"], ch["s0"])
    for ch in chains:
        w = (ch["as0"][:csz] + ch["akv"]).astype(BF16)
        ch["u"] = _dot(ch["tinv"], blockdiag(w))
    outs = []
    for ch in chains:
        u = ch["u"]
        y = ch["as0"][csz:] + _dot(ch["m_rb"], blockdiag(u.astype(BF16))) + ch["mrkv"]
        uv = jnp.concatenate([u, ch["v"]], axis=0)
        state_ref[ch["idx"]] = ch["s0"] * ch["g_all"] + jnp.where(same_head, _dot_tn(uv, ch["bk"]), 0.0)
        yc = y - _head_sum(y) * (1.0 / n)
        var = _head_sum(yc * yc) * (1.0 / n)
        outs.append(yc * lax.rsqrt(var + LNX_EPS))
    for bi in range(nb):
        yn = jnp.concatenate(outs[bi * n_pairs:(bi + 1) * n_pairs], axis=-1)
        bonus = jnp.concatenate([ch["bonus"] for ch in chains[bi * n_pairs:(bi + 1) * n_pairs]], axis=-1)
        o_ref[bi] = (yn * lw_ref[...] + lb_ref[...] + bonus).astype(BF16)


RWKV_NB = 8


def _rwkv(scan_bf16, scan_f32, ln_w, ln_b):
    bsz, s, _ = scan_bf16.shape
    nb = RWKV_NB if bsz % RWKV_NB == 0 else 1
    const = lambda b, c: (0, 0)
    vec = pl.BlockSpec((1, B_WIDTH), const)
    return pl.pallas_call(
        _rwkv_kernel,
        grid=(bsz // nb, s // CHUNK),
        in_specs=[pl.BlockSpec((nb, CHUNK, SCAN_BF16_COLS), lambda b, c: (b, c, 0)),
                  pl.BlockSpec((nb, CHUNK, SCAN_F32_COLS), lambda b, c: (b, c, 0)),
                  vec, vec],
        out_specs=pl.BlockSpec((nb, CHUNK, B_WIDTH), lambda b, c: (b, c, 0)),
        out_shape=jax.ShapeDtypeStruct((bsz, s, B_WIDTH), BF16),
        scratch_shapes=[pltpu.VMEM((nb * B_WIDTH // LANES, LANES, LANES), F32)],
        compiler_params=pltpu.CompilerParams(dimension_semantics=("parallel", "arbitrary")),
        name="rwkv",
    )(scan_bf16, scan_f32, ln_w, ln_b)


def _final_kernel(x_ref, *refs):
    ya_refs, (yb_ref, cf_ref, gate_ref, wa_ref, wb_ref, wo_ref, o_ref) = refs[:ATT_CALLS], refs[ATT_CALLS:]
    a_silu = cf_ref[0, :, 0:A_WIDTH].astype(F32)
    b_silu = cf_ref[0, :, A_WIDTH:A_WIDTH + B_WIDTH].astype(F32)
    merge_a = cf_ref[0, :, A_WIDTH + B_WIDTH:A_WIDTH + B_WIDTH + D_MODEL].astype(F32)
    merge_b = cf_ref[0, :, A_WIDTH + B_WIDTH + D_MODEL:A_WIDTH + B_WIDTH + 2 * D_MODEL].astype(F32)
    n_tiles = x_ref.shape[1] // TQ
    y_a = jnp.concatenate([ya_refs[u % ATT_CALLS][0, u // ATT_CALLS] for u in range(n_tiles)], axis=0)
    ya = y_a.astype(F32) * (a_silu * _sigmoid(a_silu))
    yb = yb_ref[0].astype(F32) * (b_silu * _sigmoid(b_silu))
    merged = _sigmoid(merge_a) * _dot(ya, wa_ref[...]) + _sigmoid(merge_b) * _dot(yb, wb_ref[...])
    o_ref[0] = x_ref[0] + gate_ref[0] * _dot(merged, wo_ref[...])


def _final(x, y_a, y_b, cols_fin, gate, w_out_a, w_out_b, w_o):
    bsz, s, _ = x.shape
    tm = 4 * TQ
    assert (tm // TQ) % ATT_CALLS == 0
    const = lambda b, i: (0, 0)
    row = lambda w: pl.BlockSpec((1, tm, w), lambda b, i: (b, i, 0))
    ya_spec = pl.BlockSpec((1, tm // TQ // ATT_CALLS, TQ, A_WIDTH), lambda b, i: (b, i, 0, 0))
    return pl.pallas_call(
        _final_kernel,
        grid=(bsz, s // tm),
        in_specs=[row(D_MODEL)] + [ya_spec] * ATT_CALLS + [row(B_WIDTH), row(FIN_COLS),
                  pl.BlockSpec((1, 1, D_MODEL), lambda b, i: (b, 0, 0)),
                  pl.BlockSpec((A_WIDTH, D_MODEL), const),
                  pl.BlockSpec((B_WIDTH, D_MODEL), const),
                  pl.BlockSpec((D_MODEL, D_MODEL), const)],
        out_specs=row(D_MODEL),
        out_shape=jax.ShapeDtypeStruct((bsz, s, D_MODEL), F32),
        compiler_params=pltpu.CompilerParams(dimension_semantics=("parallel", "parallel"),
                                             vmem_limit_bytes=VMEM_LIMIT),
        name="final",
    )(x, *y_a, y_b, cols_fin, gate, w_out_a, w_out_b, w_o)


def _split_w_in(w_in):
    nsa_in = 2 * A_WIDTH + 6 * A_KV_WIDTH + 3 * A_HEADS
    o_gate = A_WIDTH + 6 * A_KV_WIDTH
    o_asilu = o_gate + 3 * A_HEADS
    o_shift = nsa_in
    o_rest = nsa_in + RWKV_COLS
    gate_w = w_in[:, o_gate:o_asilu].reshape(D_MODEL, 3, A_KV_GROUPS, A_HPG)
    gate_w = gate_w.transpose(0, 2, 1, 3).reshape(D_MODEL, A_KV_GROUPS, 3 * A_HPG)
    gate_w = jnp.pad(gate_w, ((0, 0), (0, 0), (0, A_HEAD_DIM - 3 * A_HPG))).reshape(D_MODEL, GATE_PAD)
    w_nsa = jnp.concatenate([w_in[:, :o_gate], gate_w], axis=1)
    w_fin = jnp.concatenate([w_in[:, o_asilu:o_shift], w_in[:, o_rest:]], axis=1)
    w_mix = jnp.concatenate([w_in[:, o_shift:o_rest], w_nsa], axis=1)
    return w_mix.astype(BF16), w_fin.astype(BF16)


def _layer(x, c, rel_bias, w_ada, b_ada, norm_gain, w_in, q_norm_gain, k_norm_gain,
           cmp_pos_k, cmp_pos_v, cmp_k_w1, cmp_k_w2, cmp_v_w1, cmp_v_w2,
           shift_mu, w0, w_lora_up, a0, a_lora_up, k_k, k_a, r_k, ln_x_w, ln_x_b,
           w_out_a, w_out_b, w_o):
    bsz, s, _ = x.shape
    assert s % (2 * TQ) == 0 and s // CMP_STRIDE == LANES
    n16 = s // CMP_STRIDE
    mod = _ada(c, w_ada, b_ada)
    w_mix, w_fin = _split_w_in(w_in)
    scale = A_HEAD_DIM ** -0.5 * LOG2E
    qg = jnp.tile(q_norm_gain, A_HEADS) * scale
    ksg = jnp.tile(k_norm_gain[1], A_KV_GROUPS)
    kwg = jnp.tile(k_norm_gain[2], A_KV_GROUPS)
    vec = lambda t: t.reshape(1, -1)
    rwkv_params = (vec(shift_mu), vec(w0), w_lora_up.astype(BF16), vec(a0), a_lora_up.astype(BF16),
                   vec(k_k), vec(k_a), vec(r_k))
    q_t, ks, vs_t, kw, vw_t, gates_t, ck, cols_fin, scan_bf16, scan_f32 = _proj(
        x, mod, norm_gain, w_mix, w_fin, qg, ksg, kwg, rwkv_params)

    kc, vc_t = _compress(ck, _expand_cmp_pos(cmp_pos_k), _expand_cmp_pos(cmp_pos_v),
                         _expand_cmp_w1(cmp_k_w1), cmp_k_w2.astype(BF16),
                         _expand_cmp_w1(cmp_v_w1), cmp_v_w2.T.astype(BF16),
                         k_norm_gain[0].reshape(1, A_HEAD_DIM))
    bias_c, bias_d = _bias_tables(rel_bias, s, n16)
    y_a = _attention(q_t, kc, vc_t, ks, vs_t, kw, vw_t, bias_c, bias_d, gates_t)

    y_b = _rwkv(scan_bf16, scan_f32, vec(ln_x_w), vec(ln_x_b))

    gate = mod[:, 2 * D_MODEL:].reshape(bsz, 1, D_MODEL)
    return _final(x, y_a, y_b, cols_fin, gate, w_out_a.astype(BF16), w_out_b.astype(BF16), w_o.astype(BF16))


def kernel(x, c, w_ada, b_ada, norm_gain, w_in, q_norm_gain, k_norm_gain, cmp_pos_k, cmp_pos_v, cmp_k_w1, cmp_k_w2, cmp_v_w1, cmp_v_w2, rel_bias, shift_mu, w0, w_lora_up, a0, a_lora_up, k_k, k_a, r_k, ln_x_w, ln_x_b, w_out_a, w_out_b, w_o):
    for l in range(w_in.shape[0]):
        x = _layer(x, c, rel_bias, w_ada[l], b_ada[l], norm_gain[l], w_in[l], q_norm_gain[l], k_norm_gain[l],
                   cmp_pos_k[l], cmp_pos_v[l], cmp_k_w1[l], cmp_k_w2[l], cmp_v_w1[l], cmp_v_w2[l],
                   shift_mu[l], w0[l], w_lora_up[l], a0[l], a_lora_up[l], k_k[l], k_a[l], r_k[l],
                   ln_x_w[l], ln_x_b[l], w_out_a[l], w_out_b[l], w_o[l])
    return x
```

```python
import functools
import math

import numpy as np
import jax
import jax.numpy as jnp
from jax import lax
from jax.experimental import pallas as pl
from jax.experimental.pallas import tpu as pltpu

F32 = jnp.float32
BF16 = jnp.bfloat16

D_MODEL = 1024
A_HEADS = 8
A_HEAD_DIM = 64
A_KV_GROUPS = 2
A_HPG = A_HEADS // A_KV_GROUPS
A_WIDTH = A_HEADS * A_HEAD_DIM
A_KV_WIDTH = A_KV_GROUPS * A_HEAD_DIM
CMP_BLOCK = 32
CMP_STRIDE = 16
CMP_HIDDEN = 256
SLC_BLOCK = 64
SLC_TOPN = 16
WINDOW = 512
B_HEADS = 8
B_HEAD_DIM = 64
B_WIDTH = B_HEADS * B_HEAD_DIM
DECAY_LORA = 64
ICLR_LORA = 64
LNX_EPS = 64e-5
REL_BUCKETS = 32
REL_MAX_EXACT = 16
REL_MAX_DIST = 128
NORM_EPS = 1e-6
NEG_INF = -1e30
FORCE_SCORE = 1e30

LANES = 128
TQ = 128
CHUNK = 64
GATE_PAD = LANES
LOG2E = math.log2(math.e)
V_ROWS = A_HEAD_DIM + 16
GATE_ROWS = 16
NSA_COLS = A_WIDTH + 6 * A_KV_WIDTH + GATE_PAD
FIN_COLS = A_WIDTH + B_WIDTH + 2 * D_MODEL
RWKV_COLS = 3 * B_WIDTH + DECAY_LORA + ICLR_LORA
SCAN_BF16_COLS = 5 * B_WIDTH
SCAN_F32_COLS = 2 * B_WIDTH
VMEM_LIMIT = 56 * 1024 * 1024


def _dot(a, b):
    return jnp.dot(a.astype(BF16), b.astype(BF16), preferred_element_type=F32)


def _dot_nt(a, b):
    return lax.dot_general(a.astype(BF16), b.astype(BF16), (((1,), (1,)), ((), ())),
                           preferred_element_type=F32)


def _dot_tn(a, b):
    return lax.dot_general(a.astype(BF16), b.astype(BF16), (((0,), (0,)), ((), ())),
                           preferred_element_type=F32)


def _split2(x):
    hi = x.astype(BF16)
    lo = (x - hi.astype(F32)).astype(BF16)
    return hi, lo


def _split3(x):
    h1 = x.astype(BF16)
    r1 = x - h1.astype(F32)
    h2 = r1.astype(BF16)
    h3 = (r1 - h2.astype(F32)).astype(BF16)
    return h1, h2, h3


def _sigmoid(x):
    return 1.0 / (1.0 + jnp.exp(-x))


def _bucket_thresholds():
    n = np.arange(0, 4096)
    nf = np.maximum(n, REL_MAX_EXACT).astype(np.float64)
    val = np.log(nf / REL_MAX_EXACT) / math.log(REL_MAX_DIST / REL_MAX_EXACT) * (REL_BUCKETS - REL_MAX_EXACT)
    frac = np.abs(val - np.round(val))
    assert np.all((frac > 1e-4) | (n <= REL_MAX_EXACT) | (n >= REL_MAX_DIST))
    large = REL_MAX_EXACT + np.floor(val + 1e-9).astype(np.int64)
    bucket = np.where(n < REL_MAX_EXACT, n, np.minimum(large, REL_BUCKETS - 1))
    return [int(np.argmax(bucket >= j)) for j in range(REL_BUCKETS)]


_BUCKET_TH = _bucket_thresholds()


def _bias_from_dist(dist, tbl_ref, head):
    val = jnp.full(dist.shape, tbl_ref[0, head], F32)
    for j in range(1, REL_BUCKETS):
        val = jnp.where(dist >= _BUCKET_TH[j], tbl_ref[j, head], val)
    return val


def _ada_kernel(c_ref, w_ref, b_ref, o_ref):
    c = c_ref[...]
    o_ref[...] = _dot(c * _sigmoid(c), w_ref[...]) + b_ref[...]


def _ada(c, w_ada, b_ada):
    bsz = c.shape[0]
    return pl.pallas_call(
        _ada_kernel,
        grid=(3,),
        in_specs=[pl.BlockSpec((bsz, D_MODEL), lambda j: (0, 0)),
                  pl.BlockSpec((D_MODEL, D_MODEL), lambda j: (0, j)),
                  pl.BlockSpec((1, D_MODEL), lambda j: (0, j))],
        out_specs=pl.BlockSpec((bsz, D_MODEL), lambda j: (0, j)),
        out_shape=jax.ShapeDtypeStruct((bsz, 3 * D_MODEL), F32),
        name="ada",
    )(c, w_ada, b_ada.reshape(1, 3 * D_MODEL))


def _norm_rows(x_t, gain_col, n_seg):
    out = []
    for seg in range(n_seg):
        blk = x_t[seg * A_HEAD_DIM:(seg + 1) * A_HEAD_DIM, :]
        ms = jnp.mean(blk * blk, axis=0, keepdims=True)
        out.append(blk * lax.rsqrt(ms + NORM_EPS) * gain_col[seg * A_HEAD_DIM:(seg + 1) * A_HEAD_DIM, :])
    return out


def _head_sum(x):
    left = lax.broadcasted_iota(jnp.int32, x.shape, x.ndim - 1) < B_HEAD_DIM
    lo = jnp.sum(jnp.where(left, x, 0.0), axis=-1, keepdims=True)
    hi = jnp.sum(jnp.where(left, 0.0, x), axis=-1, keepdims=True)
    return jnp.where(left, lo, hi)


def _proj_kernel(x_ref, mod_ref, g_ref, wm_ref, wf_ref, qg_ref, ksg_ref, kwg_ref,
                 mu_ref, w0_ref, wl_ref, a0_ref, al_ref, kk_ref, ka_ref, rk_ref,
                 q_ref, ks_ref, vs_ref, kw_ref, vw_ref, gt_ref, ck_ref, of_ref, rw_ref, ld_ref, prev_ref):
    tm = x_ref.shape[1]

    @pl.when(pl.program_id(1) == 0)
    def _():
        prev_ref[...] = jnp.zeros(prev_ref.shape, F32)

    x = x_ref[0]
    ms = jnp.mean(x * x, axis=-1, keepdims=True)
    y = x * lax.rsqrt(ms + NORM_EPS) * g_ref[...]
    mod = mod_ref[0]
    h = (y * (1.0 + mod[:, D_MODEL:2 * D_MODEL]) + mod[:, :D_MODEL]).astype(BF16)
    crn = jnp.dot(h, wm_ref[...], preferred_element_type=F32)
    cr = crn[:, 0:RWKV_COLS]
    cn = crn[:, RWKV_COLS:RWKV_COLS + NSA_COLS]
    ck_ref[0] = cn[:, A_WIDTH:A_WIDTH + 2 * A_KV_WIDTH]

    rolled = pltpu.roll(cr, 1, axis=0)
    row8 = lax.broadcasted_iota(jnp.int32, (8, RWKV_COLS), 0)
    prev = jnp.concatenate([jnp.where(row8 == 0, prev_ref[0:1, :], rolled[0:8]), rolled[8:]], axis=0)
    prev_ref[0:1, :] = cr[tm - 1:tm, :]
    xs = cr + (prev - cr) * mu_ref[...]
    r = xs[:, 0:B_WIDTH]
    k = xs[:, B_WIDTH:2 * B_WIDTH]
    v = xs[:, 2 * B_WIDTH:3 * B_WIDTH]
    wd = xs[:, 3 * B_WIDTH:3 * B_WIDTH + DECAY_LORA]
    ad = xs[:, 3 * B_WIDTH + DECAY_LORA:3 * B_WIDTH + DECAY_LORA + ICLR_LORA]
    ld = -math.exp(-0.5) * _sigmoid(w0_ref[...] + _dot(jnp.tanh(wd), wl_ref[...]))
    a = _sigmoid(a0_ref[...] + _dot(ad, al_ref[...]))
    kk = k * kk_ref[...]
    k_mod = k * (1.0 + (a - 1.0) * ka_ref[...])
    rkr = r * k_mod * rk_ref[...]
    ld_ref[0, :, 0:B_WIDTH] = ld
    rw_ref[0, :, 0:B_WIDTH] = r.astype(BF16)
    rw_ref[0, :, 3 * B_WIDTH:4 * B_WIDTH] = k_mod.astype(BF16)
    rw_ref[0, :, 4 * B_WIDTH:5 * B_WIDTH] = v.astype(BF16)
    for pr in range(B_WIDTH // LANES):
        sl = slice(pr * LANES, (pr + 1) * LANES)
        kk_p = kk[:, sl]
        kk_p = kk_p * lax.rsqrt(jnp.maximum(_head_sum(kk_p * kk_p), 1e-24))
        rw_ref[0, :, B_WIDTH + pr * LANES:B_WIDTH + (pr + 1) * LANES] = kk_p.astype(BF16)
        rw_ref[0, :, 2 * B_WIDTH + pr * LANES:2 * B_WIDTH + (pr + 1) * LANES] = (kk_p * a[:, sl]).astype(BF16)
        ld_ref[0, :, B_WIDTH + pr * LANES:B_WIDTH + (pr + 1) * LANES] = _head_sum(rkr[:, sl])

    lane = lax.broadcasted_iota(jnp.int32, (TQ, LANES), 1)
    row = lax.broadcasted_iota(jnp.int32, (TQ, LANES), 0)
    ones_rows = (lax.broadcasted_iota(jnp.int32, (V_ROWS - A_HEAD_DIM, TQ), 0) == 0).astype(BF16)
    off = A_WIDTH + 2 * A_KV_WIDTH
    for sub in range(tm // TQ):
        c = cn[sub * TQ:(sub + 1) * TQ]
        q_heads = _norm_rows(c[:, 0:A_WIDTH].T, qg_ref[...], A_HEADS)
        ks_t = jnp.concatenate(_norm_rows(c[:, off:off + A_KV_WIDTH].T, ksg_ref[...], A_KV_GROUPS), axis=0)
        kw_t = jnp.concatenate(_norm_rows(c[:, off + 2 * A_KV_WIDTH:off + 3 * A_KV_WIDTH].T, kwg_ref[...],
                                          A_KV_GROUPS), axis=0)
        ksn = ks_t.T
        kwn = kw_t.T
        vs_t = c[:, off + A_KV_WIDTH:off + 2 * A_KV_WIDTH].T.astype(BF16)
        vw_t = c[:, off + 3 * A_KV_WIDTH:off + 4 * A_KV_WIDTH].T.astype(BF16)
        gates_t = _sigmoid(c[:, off + 4 * A_KV_WIDTH:off + 5 * A_KV_WIDTH]).T
        blk = (pl.program_id(1) * tm + sub * TQ + row) // SLC_BLOCK
        onehot = jnp.where(lane - A_HEAD_DIM == blk, 1.0, 0.0)
        for g in range(A_KV_GROUPS):
            q_ref[0, g, sub] = jnp.concatenate(q_heads[g * A_HPG:(g + 1) * A_HPG], axis=1).astype(BF16)
            sl = slice(g * A_HEAD_DIM, (g + 1) * A_HEAD_DIM)
            k_g = ksn if g == 0 else pltpu.roll(ksn, A_HEAD_DIM, axis=1)
            ks_ref[0, g, sub * TQ:(sub + 1) * TQ, :] = jnp.where(lane < A_HEAD_DIM, k_g, onehot).astype(BF16)
            kw_g = kwn if g == 0 else pltpu.roll(kwn, A_HEAD_DIM, axis=1)
            kw_ref[0, g, sub * TQ:(sub + 1) * TQ, :] = jnp.where(lane < A_HEAD_DIM, kw_g, 0.0).astype(BF16)
            vs_ref[0, g, sub] = jnp.concatenate([vs_t[sl, :], ones_rows], axis=0)
            vw_ref[0, g, sub] = jnp.concatenate([vw_t[sl, :], ones_rows], axis=0)
            gt_ref[0, g, sub] = gates_t[g * A_HEAD_DIM:g * A_HEAD_DIM + GATE_ROWS, :]
    of_ref[0] = jnp.dot(h, wf_ref[...], preferred_element_type=F32).astype(BF16)


def _proj(x, mod, norm_gain, w_mix, w_fin, qg, ksg, kwg, rwkv_params, tm=512):
    bsz, s, _ = x.shape
    nt, nsub = s // TQ, tm // TQ
    assert A_HEAD_DIM + s // SLC_BLOCK <= LANES and A_KV_WIDTH == LANES and 2 * B_HEAD_DIM == LANES
    const = lambda b, i: (0, 0)
    weight = lambda cols: pl.BlockSpec((D_MODEL, cols), const, pipeline_mode=pl.Buffered(1))
    whole = lambda t: pl.BlockSpec(t.shape, const)
    col = lambda t: jnp.broadcast_to(t.reshape(-1, 1), (t.size, LANES))
    k_spec = lambda width: pl.BlockSpec((1, A_KV_GROUPS, tm, width), lambda b, i: (b, 0, i, 0))
    k_shape = lambda width: jax.ShapeDtypeStruct((bsz, A_KV_GROUPS, s, width), BF16)
    tile_spec = lambda r, c: pl.BlockSpec((1, A_KV_GROUPS, nsub, r, c), lambda b, i: (b, 0, i, 0, 0))
    tile_shape = lambda r, c, dt: jax.ShapeDtypeStruct((bsz, A_KV_GROUPS, nt, r, c), dt)
    return pl.pallas_call(
        _proj_kernel,
        grid=(bsz, s // tm),
        in_specs=[pl.BlockSpec((1, tm, D_MODEL), lambda b, i: (b, i, 0)),
                  pl.BlockSpec((1, 1, 3 * D_MODEL), lambda b, i: (b, 0, 0)),
                  pl.BlockSpec((1, D_MODEL), const),
                  weight(RWKV_COLS + NSA_COLS), weight(FIN_COLS),
                  pl.BlockSpec((A_WIDTH, LANES), const),
                  pl.BlockSpec((A_KV_WIDTH, LANES), const),
                  pl.BlockSpec((A_KV_WIDTH, LANES), const)] + [whole(t) for t in rwkv_params],
        out_specs=[tile_spec(A_HEAD_DIM, A_HPG * TQ),
                   k_spec(LANES), tile_spec(V_ROWS, TQ), k_spec(LANES), tile_spec(V_ROWS, TQ),
                   tile_spec(GATE_ROWS, TQ),
                   pl.BlockSpec((1, tm, 2 * A_KV_WIDTH), lambda b, i: (b, i, 0)),
                   pl.BlockSpec((1, tm, FIN_COLS), lambda b, i: (b, i, 0)),
                   pl.BlockSpec((1, tm, SCAN_BF16_COLS), lambda b, i: (b, i, 0)),
                   pl.BlockSpec((1, tm, SCAN_F32_COLS), lambda b, i: (b, i, 0))],
        out_shape=[tile_shape(A_HEAD_DIM, A_HPG * TQ, BF16),
                   k_shape(LANES), tile_shape(V_ROWS, TQ, BF16), k_shape(LANES), tile_shape(V_ROWS, TQ, BF16),
                   tile_shape(GATE_ROWS, TQ, F32),
                   jax.ShapeDtypeStruct((bsz, s, 2 * A_KV_WIDTH), F32),
                   jax.ShapeDtypeStruct((bsz, s, FIN_COLS), BF16),
                   jax.ShapeDtypeStruct((bsz, s, SCAN_BF16_COLS), BF16),
                   jax.ShapeDtypeStruct((bsz, s, SCAN_F32_COLS), F32)],
        scratch_shapes=[pltpu.VMEM((8, RWKV_COLS), F32)],
        compiler_params=pltpu.CompilerParams(dimension_semantics=("parallel", "arbitrary"),
                                             vmem_limit_bytes=VMEM_LIMIT),
        name="proj",
    )(x, mod.reshape(bsz, 1, 3 * D_MODEL), norm_gain.reshape(1, D_MODEL), w_mix, w_fin,
      col(qg), col(ksg), col(kwg), *rwkv_params)


def _compress_kernel(ck_ref, cv_ref, pk_ref, pv_ref, w1k_ref, w2k_ref, w1v_ref, w2v_ref, kg_ref, kc_ref, vc_ref):
    nb, n16 = ck_ref.shape[0], ck_ref.shape[1] // CMP_STRIDE

    def rows16(ref):
        return jnp.concatenate(
            [jnp.concatenate([ref[b, pl.ds(p, n16, stride=CMP_STRIDE), :] for p in range(CMP_STRIDE)], axis=1)
             for b in range(nb)], axis=0)

    def hidden(z, pos_ref, w1_ref, g):
        top = _dot(z + pos_ref[0:1, :], w1_ref[g, 0])
        bot = _dot(z + pos_ref[1:2, :], w1_ref[g, 1])
        bot = jnp.concatenate([pltpu.roll(bot[b * n16:(b + 1) * n16], n16 - 1, axis=0) for b in range(nb)], axis=0)
        return jax.nn.gelu(top + bot, approximate=True)

    zk = rows16(ck_ref)
    zv = rows16(cv_ref)
    for g in range(A_KV_GROUPS):
        kc = _dot(hidden(zk, pk_ref, w1k_ref, g), w2k_ref[...])
        ms = jnp.mean(kc * kc, axis=-1, keepdims=True)
        kc = (kc * lax.rsqrt(ms + NORM_EPS) * kg_ref[...]).astype(BF16)
        vc_t = _dot_nt(w2v_ref[...], hidden(zv, pv_ref, w1v_ref, g)).astype(BF16)
        for b in range(nb):
            kc_ref[b, g] = kc[b * n16:(b + 1) * n16]
            vc_ref[b, g] = vc_t[:, b * n16:(b + 1) * n16]


CMP_NB = 4


def _expand_cmp_w1(w1):
    w = w1.reshape(2, CMP_STRIDE, 1, A_HEAD_DIM, CMP_HIDDEN)
    per_group = []
    for g in range(A_KV_GROUPS):
        pad = [(0, 0), (0, 0), (g, A_KV_GROUPS - 1 - g), (0, 0), (0, 0)]
        per_group.append(jnp.pad(w, pad).reshape(2, CMP_STRIDE * A_KV_WIDTH, CMP_HIDDEN))
    return jnp.stack(per_group).astype(BF16)


def _expand_cmp_pos(pos):
    p = jnp.broadcast_to(pos.reshape(2, CMP_STRIDE, 1, A_HEAD_DIM), (2, CMP_STRIDE, A_KV_GROUPS, A_HEAD_DIM))
    return p.reshape(2, CMP_STRIDE * A_KV_WIDTH)


def _compress(ck, pk, pv, w1k, w2k, w1v, w2v_t, kg):
    bsz, s, _ = ck.shape
    n16 = s // CMP_STRIDE
    zw = CMP_STRIDE * A_KV_WIDTH
    const = lambda b: (0, 0)
    const4 = lambda b: (0, 0, 0, 0)
    nb = CMP_NB if bsz % CMP_NB == 0 else 1
    return pl.pallas_call(
        _compress_kernel,
        grid=(bsz // nb,),
        in_specs=[pl.BlockSpec((nb, s, A_KV_WIDTH), lambda b: (b, 0, 0)),
                  pl.BlockSpec((nb, s, A_KV_WIDTH), lambda b: (b, 0, 1)),
                  pl.BlockSpec((2, zw), const), pl.BlockSpec((2, zw), const),
                  pl.BlockSpec((A_KV_GROUPS, 2, zw, CMP_HIDDEN), const4), pl.BlockSpec((CMP_HIDDEN, A_HEAD_DIM), const),
                  pl.BlockSpec((A_KV_GROUPS, 2, zw, CMP_HIDDEN), const4), pl.BlockSpec((A_HEAD_DIM, CMP_HIDDEN), const),
                  pl.BlockSpec((1, A_HEAD_DIM), const)],
        out_specs=[pl.BlockSpec((nb, A_KV_GROUPS, n16, A_HEAD_DIM), lambda b: (b, 0, 0, 0)),
                   pl.BlockSpec((nb, A_KV_GROUPS, A_HEAD_DIM, n16), lambda b: (b, 0, 0, 0))],
        out_shape=[jax.ShapeDtypeStruct((bsz, A_KV_GROUPS, n16, A_HEAD_DIM), BF16),
                   jax.ShapeDtypeStruct((bsz, A_KV_GROUPS, A_HEAD_DIM, n16), BF16)],
        compiler_params=pltpu.CompilerParams(dimension_semantics=("parallel",)),
        name="compress",
    )(ck, ck, pk, pv, w1k, w2k, w1v, w2v_t, kg)


TILE_FAR, TILE_EDGE, N_BIAS_TILES = 2, 3, 4
SUB = 4
CMP_WIN = 16
ATT_CALLS = 1
ATT_TILES = 16


def _bias_cmp_kernel(tbl_ref, o_ref):
    i = pl.program_id(0)
    g = pl.program_id(1)
    n_cmp = o_ref.shape[2]
    first_n = pl.multiple_of(jnp.maximum(i * (TQ // CMP_STRIDE) - 8, 0), 8)
    n = lax.broadcasted_iota(jnp.int32, (n_cmp, TQ), 0)
    n_win = first_n + lax.broadcasted_iota(jnp.int32, (CMP_WIN, TQ), 0)
    q = lax.broadcasted_iota(jnp.int32, (CMP_WIN, TQ), 1)
    dist = i * TQ + q - (n_win * CMP_STRIDE + CMP_BLOCK - 1)
    for h in range(A_HPG):
        head = g * A_HPG + h
        o_ref[0, 0, :, h * TQ:(h + 1) * TQ] = jnp.where(n < first_n, tbl_ref[REL_BUCKETS - 1, head] * LOG2E, NEG_INF)
        bias = _bias_from_dist(dist, tbl_ref, head)
        o_ref[0, 0, pl.ds(first_n, CMP_WIN), h * TQ:(h + 1) * TQ] = jnp.where(dist >= 0, bias * LOG2E, NEG_INF)


def _bias_toeplitz_kernel(tbl_ref, o_ref):
    g = pl.program_id(0)
    r = pl.program_id(1)
    off = jnp.where(r == TILE_EDGE, WINDOW // TQ, r)
    k = lax.broadcasted_iota(jnp.int32, (TQ, TQ), 0)
    q = lax.broadcasted_iota(jnp.int32, (TQ, TQ), 1)
    dist = off * TQ + q - k
    valid = (dist >= 0) & (dist < WINDOW)
    for h in range(A_HPG):
        head = g * A_HPG + h
        bias = _bias_from_dist(dist, tbl_ref, head) - tbl_ref[REL_BUCKETS - 1, head]
        o_ref[0, 0, :, h * TQ:(h + 1) * TQ] = jnp.where(valid, bias * LOG2E, NEG_INF)


def _bias_tables(rel_bias, s, n_cmp):
    smem = pl.BlockSpec(memory_space=pltpu.SMEM)
    nt = s // TQ
    bias_c = pl.pallas_call(
        _bias_cmp_kernel,
        grid=(nt, A_KV_GROUPS),
        in_specs=[smem],
        out_specs=pl.BlockSpec((1, 1, n_cmp, A_HPG * TQ), lambda i, g: (i, g, 0, 0)),
        out_shape=jax.ShapeDtypeStruct((nt, A_KV_GROUPS, n_cmp, A_HPG * TQ), F32),
        name="bias_cmp",
    )(rel_bias)
    assert _BUCKET_TH[REL_BUCKETS - 1] <= TQ + 1 and WINDOW // TQ >= 3
    assert _BUCKET_TH[REL_BUCKETS - 1] <= 9 * CMP_STRIDE - CMP_BLOCK + 1
    assert (CMP_WIN - 8) * CMP_STRIDE + CMP_BLOCK >= TQ and n_cmp >= CMP_WIN
    bias_d = pl.pallas_call(
        _bias_toeplitz_kernel,
        grid=(A_KV_GROUPS, N_BIAS_TILES),
        in_specs=[smem],
        out_specs=pl.BlockSpec((1, 1, TQ, A_HPG * TQ), lambda g, r: (g, r, 0, 0)),
        out_shape=jax.ShapeDtypeStruct((A_KV_GROUPS, N_BIAS_TILES, TQ, A_HPG * TQ), F32),
        name="bias_toeplitz",
    )(rel_bias)
    return bias_c, bias_d


def _attn_kernel(*refs, j):
    q_refs, refs = refs[:ATT_TILES], refs[ATT_TILES:]
    kc_ref, vc_ref, ks_ref, vs_ref, kw_ref, vw_ref = refs[:6]
    bc_refs, bd_ref, gt_refs, o_ref = refs[6:6 + ATT_TILES], refs[6 + ATT_TILES], refs[7 + ATT_TILES:-1], refs[-1]
    tq = TQ
    n_cmp = kc_ref.shape[2]
    n_slc = ks_ref.shape[2] // SLC_BLOCK
    wt = WINDOW // tq
    dh = A_HEAD_DIM
    round8 = lambda n: -(-n // 8) * 8
    tiles = [dict(i=j + t * ATT_CALLS, last=(j + t * ATT_CALLS) // SUB, n_tok=(j + t * ATT_CALLS + 1) * tq,
                  q=q_refs[t][0, 0, 0], bias_c=bc_refs[t], gates=gt_refs[t][0, 0, 0]) for t in range(ATT_TILES)]
    zero_rows = jnp.zeros((LANES - dh, A_HPG * tq), BF16)

    def scores(k_ref, q_mat, i, first_tile, n_sub, tile_index):
        s = jnp.dot(k_ref[0, 0, first_tile * tq:(first_tile + n_sub) * tq, :], q_mat,
                    preferred_element_type=F32)
        idx = [tile_index(i - (first_tile + t)) for t in range(n_sub)]
        parts = [s[t * tq:(t + 1) * tq] if idx[t] == TILE_FAR else s[t * tq:(t + 1) * tq] + bd_ref[0, idx[t]]
                 for t in range(n_sub)]
        return jnp.concatenate(parts, axis=0)

    def values_t(v_ref, first_tile, n_sub):
        return jnp.concatenate([v_ref[0, 0, first_tile + t] for t in range(n_sub)], axis=1)

    win_tile = lambda r: TILE_EDGE if r == wt else min(r, TILE_FAR)
    for t in tiles:
        t["first_w"] = max(t["i"] - wt, 0)
        t["n_w"] = t["i"] - t["first_w"] + 1
        q_pad = jnp.concatenate([t["q"], zero_rows], axis=0)
        t["s_w"] = scores(kw_ref, q_pad, t["i"], t["first_w"], t["n_w"], win_tile)
    for t in tiles:
        t["n_cmp"] = min(n_cmp, round8(t["n_tok"] // CMP_STRIDE))
        t["n_slc"] = min(n_slc, round8(t["n_tok"] // SLC_BLOCK))
        bias = t["bias_c"][0, 0, 0:t["n_cmp"], :]
        t["valid_c"] = bias > 0.5 * NEG_INF
        t["s_c"] = jnp.dot(kc_ref[0, 0, 0:t["n_cmp"], :], t["q"], preferred_element_type=F32) + bias

    r1, r2 = SLC_BLOCK // CMP_STRIDE, CMP_BLOCK // CMP_STRIDE
    jj = lax.broadcasted_iota(jnp.int32, (n_slc, n_cmp), 0)
    nn = lax.broadcasted_iota(jnp.int32, (n_slc, n_cmp), 1)
    d = nn - r1 * jj
    cnt = jnp.zeros((n_slc, n_cmp), F32)
    for a in range(r1):
        for c in range(r2):
            cnt = cnt + jnp.where(d == a - c, 1.0, 0.0)
    cnt = cnt.astype(BF16)
    for t in tiles:
        s = t["s_c"]
        e = jnp.where(t["valid_c"], jnp.exp2(s - jnp.max(s, axis=0, keepdims=True)), 0.0)
        l = jnp.sum(e, axis=0, keepdims=True)
        p = e * (1.0 / jnp.where(l > 0.0, l, 1.0))
        rest = n_cmp - t["n_cmp"]
        pad = (lambda a: jnp.concatenate([a, jnp.zeros((rest, a.shape[1]), a.dtype)], axis=0)) if rest else (lambda a: a)
        t["out_c"] = jnp.dot(vc_ref[0, 0], pad(p).astype(BF16), preferred_element_type=F32)
        p_grp = pad(sum(p[:, h * tq:(h + 1) * tq] for h in range(A_HPG)))
        t["imp"] = sum(jnp.dot(cnt[0:t["n_slc"]], part, preferred_element_type=F32)
                       for part in _split3(p_grp))

    for t in tiles:
        s = t["s_w"]
        t["p_w"] = jnp.exp2(s - jnp.max(s, axis=0, keepdims=True)).astype(BF16)
    for t in tiles:
        acc = jnp.dot(values_t(vw_ref, t["first_w"], t["n_w"]), t["p_w"], preferred_element_type=F32)
        out_w = acc[:dh] * (1.0 / acc[dh:dh + 1])
        gates = t["gates"]
        t["part"] = [gates[h:h + 1, :] * t["out_c"][:, h * tq:(h + 1) * tq]
                     + gates[2 * A_HPG + h:2 * A_HPG + h + 1, :] * out_w[:, h * tq:(h + 1) * tq]
                     for h in range(A_HPG)]

    for t in tiles:
        nb = t["n_slc"]
        blk = lax.broadcasted_iota(jnp.int32, (nb, tq), 0)
        tpos = t["i"] * tq + lax.broadcasted_iota(jnp.int32, (nb, tq), 1)
        cur = tpos // SLC_BLOCK
        forced = (blk == 0) | (blk == cur) | (blk == cur - 1)
        causal = blk * SLC_BLOCK <= tpos
        imp = jnp.where(forced, FORCE_SCORE, jnp.where(causal, t["imp"], NEG_INF))
        rank = jnp.zeros((nb, tq), F32)
        for c in range(nb):
            row = imp[c:c + 1, :]
            ahead = (row > imp) | ((row == imp) & (blk > c))
            rank = rank + jnp.where(ahead, 1.0, 0.0)
        pen = jnp.where(rank < float(min(SLC_TOPN, n_slc)), 0.0, -FORCE_SCORE)
        pen = jnp.concatenate([pen, jnp.zeros((LANES - dh - nb, tq), F32)], axis=0)
        t["q_aug"] = jnp.concatenate([t["q"], jnp.concatenate([pen] * A_HPG, axis=1).astype(BF16)], axis=0)

    sel_tile = lambda r: min(r, TILE_FAR)

    jobs = [(t, c, min(SUB, t["i"] - c * SUB + 1)) for t in tiles for c in range(t["last"], -1, -1)]
    ss = [scores(ks_ref, t["q_aug"], t["i"], c * SUB, n, sel_tile) for t, c, n in jobs]
    ms = [jnp.max(s, axis=0, keepdims=True) for s in ss]
    ps = [jnp.exp2(s - m).astype(BF16) for s, m in zip(ss, ms)]
    accs = [jnp.dot(values_t(vs_ref, c * SUB, n), p, preferred_element_type=F32) for (t, c, n), p in zip(jobs, ps)]

    for k, t in enumerate(tiles):
        mine = [n for n, job in enumerate(jobs) if job[0] is t]
        m = functools.reduce(jnp.maximum, [ms[n] for n in mine])
        acc = sum(jnp.exp2(ms[n] - m) * accs[n] for n in mine)
        out_s = acc[:dh] * (1.0 / acc[dh:dh + 1])
        gates = t["gates"]
        blocks = [t["part"][h] + gates[A_HPG + h:A_HPG + h + 1, :] * out_s[:, h * tq:(h + 1) * tq]
                  for h in range(A_HPG)]
        o_ref[0, k] = jnp.concatenate(blocks, axis=0).T.astype(BF16)


def _attention(q_t, kc, vc_t, ks, vs_t, kw, vw_t, bias_c, bias_d, gates_t):
    bsz, _, nt, _, _ = q_t.shape
    s = ks.shape[2]
    n_cmp = kc.shape[2]
    assert nt == ATT_TILES * ATT_CALLS and WINDOW // TQ + 1 <= nt
    k_spec = pl.BlockSpec((1, 1, s, LANES), lambda b, g: (b, g, 0, 0))
    vt_spec = pl.BlockSpec((1, 1, nt, V_ROWS, TQ), lambda b, g: (b, g, 0, 0, 0))
    outs = []
    for j in range(ATT_CALLS):
        per_tile = lambda spec: [spec(j + t * ATT_CALLS) for t in range(ATT_TILES)]
        q_spec = lambda i: pl.BlockSpec((1, 1, 1, A_HEAD_DIM, A_HPG * TQ), lambda b, g: (b, g, i, 0, 0))
        bc_spec = lambda i: pl.BlockSpec((1, 1, n_cmp, A_HPG * TQ), lambda b, g: (i, g, 0, 0))
        gt_spec = lambda i: pl.BlockSpec((1, 1, 1, GATE_ROWS, TQ), lambda b, g: (b, g, i, 0, 0))
        outs.append(pl.pallas_call(
            functools.partial(_attn_kernel, j=j),
            grid=(bsz, A_KV_GROUPS),
            in_specs=(per_tile(q_spec)
                      + [pl.BlockSpec((1, 1, n_cmp, A_HEAD_DIM), lambda b, g: (b, g, 0, 0)),
                         pl.BlockSpec((1, 1, A_HEAD_DIM, n_cmp), lambda b, g: (b, g, 0, 0)),
                         k_spec, vt_spec, k_spec, vt_spec]
                      + per_tile(bc_spec)
                      + [pl.BlockSpec((1, N_BIAS_TILES, TQ, A_HPG * TQ), lambda b, g: (g, 0, 0, 0))]
                      + per_tile(gt_spec)),
            out_specs=pl.BlockSpec((1, ATT_TILES, TQ, A_HPG * A_HEAD_DIM), lambda b, g: (b, 0, 0, g)),
            out_shape=jax.ShapeDtypeStruct((bsz, ATT_TILES, TQ, A_WIDTH), BF16),
            compiler_params=pltpu.CompilerParams(dimension_semantics=("parallel", "parallel"),
                                                 vmem_limit_bytes=VMEM_LIMIT),
            name=f"attn{j}",
        )(*([q_t] * ATT_TILES), kc, vc_t, ks, vs_t, kw, vw_t, *([bias_c] * ATT_TILES), bias_d,
          *([gates_t] * ATT_TILES)))
    return outs


def _rwkv_kernel(rw_ref, ld_ref, lw_ref, lb_ref, o_ref, state_ref):
    cc = pl.program_id(1)
    n = B_HEAD_DIM
    nb, csz = rw_ref.shape[0], rw_ref.shape[1]

    @pl.when(cc == 0)
    def _():
        state_ref[...] = jnp.zeros(state_ref.shape, F32)

    ti = lax.broadcasted_iota(jnp.int32, (csz, LANES), 0)
    si = lax.broadcasted_iota(jnp.int32, (csz, LANES), 1) % n
    lower = si <= ti
    strict = si < ti
    eye = jnp.where(si == ti, 1.0, 0.0)
    tri = jnp.where(lax.broadcasted_iota(jnp.int32, (csz, csz), 1) <= lax.broadcasted_iota(jnp.int32, (csz, csz), 0),
                    1.0, 0.0).astype(BF16)
    n_pairs = B_WIDTH // LANES
    left =lax.broadcasted_iota(jnp.int32, (csz, LANES), 1) < n
    row_left = lax.broadcasted_iota(jnp.int32, (LANES, LANES), 0) < n
    same_head = row_left == (lax.broadcasted_iota(jnp.int32, (LANES, LANES), 1) < n)

    def blockdiag(y):
        zero = jnp.zeros_like(y)
        return jnp.concatenate([jnp.where(left, y, zero), jnp.where(left, zero, y)], axis=0)

    chains = []
    for bi in range(nb):
        r, kk, b, k_mod, v = (rw_ref[bi, :, m * B_WIDTH:(m + 1) * B_WIDTH].astype(F32) for m in range(5))
        ld = ld_ref[bi, :, 0:B_WIDTH]
        rsum = ld_ref[bi, :, B_WIDTH:2 * B_WIDTH]

        ld_hi, ld_lo = _split2(ld)
        cum = jnp.dot(tri, ld_hi, preferred_element_type=F32) + jnp.dot(tri, ld_lo, preferred_element_type=F32)
        g_inc = jnp.exp(cum)
        g_exc = jnp.exp(cum - ld)
        g_inv = jnp.exp(-cum)
        g_end = jnp.exp(cum[csz - 1:csz, :] - cum)
        g_all = g_inc[csz - 1:csz, :]

        for pr in range(n_pairs):
            sl = slice(pr * LANES, (pr + 1) * LANES)
            kk_p = kk[:, sl]
            b_p = b[:, sl]
            bt = (b_p * g_inv[:, sl]).astype(BF16)
            kt = (k_mod[:, sl] * g_inv[:, sl]).astype(BF16)
            ch = dict(
                idx=bi * n_pairs + pr,
                v=v[:, sl],
                lhs=jnp.concatenate([-kk_p * g_exc[:, sl], r[:, sl] * g_inc[:, sl]], axis=0).astype(BF16),
                rhs=jnp.concatenate([blockdiag(bt), blockdiag(kt)], axis=0),
                bk=jnp.concatenate([b_p * g_end[:, sl], k_mod[:, sl] * g_end[:, sl]], axis=0).astype(BF16),
                g_all=g_all[:, sl],
                bonus=rsum[:, sl] * v[:, sl],
            )
            chains.append(ch)

    for ch in chains:
        x = _dot_nt(ch["lhs"], ch["rhs"])
        xb, xk = x[:, :LANES], x[:, LANES:]
        ch["a_ab"] = jnp.where(strict, xb[:csz], 0.0)
        a_ak = jnp.where(strict, xk[:csz], 0.0)
        m_rk = jnp.where(lower, xk[csz:], 0.0)
        ch["ak_rk"] = jnp.concatenate([a_ak, m_rk], axis=0).astype(BF16)
        ch["m_rb"] = jnp.where(lower, xb[csz:], 0.0).astype(BF16)
    for ch in chains:
        akv = _dot(ch["ak_rk"], blockdiag(ch["v"].astype(BF16)))
        ch["akv"], ch["mrkv"] = akv[:csz], akv[csz:]
        ch["tinv"] = eye + ch["a_ab"]
        ch["pw"] = ch["a_ab"].astype(BF16)
    n_sq = int(math.log2(csz)) - 1
    for ch in chains:
        ch["pw"] = _dot(ch["pw"], blockdiag(ch["pw"])).astype(BF16)
    for step in range(n_sq):
        for ch in chains:
            if step + 1 < n_sq:
                both = _dot(jnp.concatenate([ch["pw"], ch["tinv"].astype(BF16)], axis=0), blockdiag(ch["pw"]))
                ch["tinv"] = ch["tinv"] + both[csz:]
                ch["pw"] = both[:csz].astype(BF16)
            else:
                ch["tinv"] = ch["tinv"] + _dot(ch["tinv"], blockdiag(ch["pw"]))
    for ch in chains:
        ch["s0"] = state_ref[ch["idx"]]
        ch["as0"] = _dot_nt(ch["lhs"], ch["s0"])
    for ch in chains:
        w = (ch["as0"][:csz] + ch["akv"]).astype(BF16)
        ch["u"] = _dot(ch["tinv"], blockdiag(w))
    outs = []
    for ch in chains:
        u = ch["u"]
        y = ch["as0"][csz:] + _dot(ch["m_rb"], blockdiag(u.astype(BF16))) + ch["mrkv"]
        uv = jnp.concatenate([u, ch["v"]], axis=0)
        state_ref[ch["idx"]] = ch["s0"] * ch["g_all"] + jnp.where(same_head, _dot_tn(uv, ch["bk"]), 0.0)
        yc = y - _head_sum(y) * (1.0 / n)
        var = _head_sum(yc * yc) * (1.0 / n)
        outs.append(yc * lax.rsqrt(var + LNX_EPS))
    for bi in range(nb):
        yn = jnp.concatenate(outs[bi * n_pairs:(bi + 1) * n_pairs], axis=-1)
        bonus = jnp.concatenate([ch["bonus"] for ch in chains[bi * n_pairs:(bi + 1) * n_pairs]], axis=-1)
        o_ref[bi] = (yn * lw_ref[...] + lb_ref[...] + bonus).astype(BF16)


RWKV_NB = 8


def _rwkv(scan_bf16, scan_f32, ln_w, ln_b):
    bsz, s, _ = scan_bf16.shape
    nb = RWKV_NB if bsz % RWKV_NB == 0 else 1
    const = lambda b, c: (0, 0)
    vec = pl.BlockSpec((1, B_WIDTH), const)
    return pl.pallas_call(
        _rwkv_kernel,
        grid=(bsz // nb, s // CHUNK),
        in_specs=[pl.BlockSpec((nb, CHUNK, SCAN_BF16_COLS), lambda b, c: (b, c, 0)),
                  pl.BlockSpec((nb, CHUNK, SCAN_F32_COLS), lambda b, c: (b, c, 0)),
                  vec, vec],
        out_specs=pl.BlockSpec((nb, CHUNK, B_WIDTH), lambda b, c: (b, c, 0)),
        out_shape=jax.ShapeDtypeStruct((bsz, s, B_WIDTH), BF16),
        scratch_shapes=[pltpu.VMEM((nb * B_WIDTH // LANES, LANES, LANES), F32)],
        compiler_params=pltpu.CompilerParams(dimension_semantics=("parallel", "arbitrary")),
        name="rwkv",
    )(scan_bf16, scan_f32, ln_w, ln_b)


def _final_kernel(x_ref, *refs):
    ya_refs, (yb_ref, cf_ref, gate_ref, wa_ref, wb_ref, wo_ref, o_ref) = refs[:ATT_CALLS], refs[ATT_CALLS:]
    a_silu = cf_ref[0, :, 0:A_WIDTH].astype(F32)
    b_silu = cf_ref[0, :, A_WIDTH:A_WIDTH + B_WIDTH].astype(F32)
    merge_a = cf_ref[0, :, A_WIDTH + B_WIDTH:A_WIDTH + B_WIDTH + D_MODEL].astype(F32)
    merge_b = cf_ref[0, :, A_WIDTH + B_WIDTH + D_MODEL:A_WIDTH + B_WIDTH + 2 * D_MODEL].astype(F32)
    n_tiles = x_ref.shape[1] // TQ
    y_a = jnp.concatenate([ya_refs[u % ATT_CALLS][0, u // ATT_CALLS] for u in range(n_tiles)], axis=0)
    ya = y_a.astype(F32) * (a_silu * _sigmoid(a_silu))
    yb = yb_ref[0].astype(F32) * (b_silu * _sigmoid(b_silu))
    merged = _sigmoid(merge_a) * _dot(ya, wa_ref[...]) + _sigmoid(merge_b) * _dot(yb, wb_ref[...])
    o_ref[0] = x_ref[0] + gate_ref[0] * _dot(merged, wo_ref[...])


def _final(x, y_a, y_b, cols_fin, gate, w_out_a, w_out_b, w_o):
    bsz, s, _ = x.shape
    tm = 4 * TQ
    assert (tm // TQ) % ATT_CALLS == 0
    const = lambda b, i: (0, 0)
    row = lambda w: pl.BlockSpec((1, tm, w), lambda b, i: (b, i, 0))
    ya_spec = pl.BlockSpec((1, tm // TQ // ATT_CALLS, TQ, A_WIDTH), lambda b, i: (b, i, 0, 0))
    return pl.pallas_call(
        _final_kernel,
        grid=(bsz, s // tm),
        in_specs=[row(D_MODEL)] + [ya_spec] * ATT_CALLS + [row(B_WIDTH), row(FIN_COLS),
                  pl.BlockSpec((1, 1, D_MODEL), lambda b, i: (b, 0, 0)),
                  pl.BlockSpec((A_WIDTH, D_MODEL), const),
                  pl.BlockSpec((B_WIDTH, D_MODEL), const),
                  pl.BlockSpec((D_MODEL, D_MODEL), const)],
        out_specs=row(D_MODEL),
        out_shape=jax.ShapeDtypeStruct((bsz, s, D_MODEL), F32),
        compiler_params=pltpu.CompilerParams(dimension_semantics=("parallel", "parallel"),
                                             vmem_limit_bytes=VMEM_LIMIT),
        name="final",
    )(x, *y_a, y_b, cols_fin, gate, w_out_a, w_out_b, w_o)


def _split_w_in(w_in):
    nsa_in = 2 * A_WIDTH + 6 * A_KV_WIDTH + 3 * A_HEADS
    o_gate = A_WIDTH + 6 * A_KV_WIDTH
    o_asilu = o_gate + 3 * A_HEADS
    o_shift = nsa_in
    o_rest = nsa_in + RWKV_COLS
    gate_w = w_in[:, o_gate:o_asilu].reshape(D_MODEL, 3, A_KV_GROUPS, A_HPG)
    gate_w = gate_w.transpose(0, 2, 1, 3).reshape(D_MODEL, A_KV_GROUPS, 3 * A_HPG)
    gate_w = jnp.pad(gate_w, ((0, 0), (0, 0), (0, A_HEAD_DIM - 3 * A_HPG))).reshape(D_MODEL, GATE_PAD)
    w_nsa = jnp.concatenate([w_in[:, :o_gate], gate_w], axis=1)
    w_fin = jnp.concatenate([w_in[:, o_asilu:o_shift], w_in[:, o_rest:]], axis=1)
    w_mix = jnp.concatenate([w_in[:, o_shift:o_rest], w_nsa], axis=1)
    return w_mix.astype(BF16), w_fin.astype(BF16)


def _layer(x, c, rel_bias, w_ada, b_ada, norm_gain, w_in, q_norm_gain, k_norm_gain,
           cmp_pos_k, cmp_pos_v, cmp_k_w1, cmp_k_w2, cmp_v_w1, cmp_v_w2,
           shift_mu, w0, w_lora_up, a0, a_lora_up, k_k, k_a, r_k, ln_x_w, ln_x_b,
           w_out_a, w_out_b, w_o):
    bsz, s, _ = x.shape
    assert s % (2 * TQ) == 0 and s // CMP_STRIDE == LANES
    n16 = s // CMP_STRIDE
    mod = _ada(c, w_ada, b_ada)
    w_mix, w_fin = _split_w_in(w_in)
    scale = A_HEAD_DIM ** -0.5 * LOG2E
    qg = jnp.tile(q_norm_gain, A_HEADS) * scale
    ksg = jnp.tile(k_norm_gain[1], A_KV_GROUPS)
    kwg = jnp.tile(k_norm_gain[2], A_KV_GROUPS)
    vec = lambda t: t.reshape(1, -1)
    rwkv_params = (vec(shift_mu), vec(w0), w_lora_up.astype(BF16), vec(a0), a_lora_up.astype(BF16),
                   vec(k_k), vec(k_a), vec(r_k))
    q_t, ks, vs_t, kw, vw_t, gates_t, ck, cols_fin, scan_bf16, scan_f32 = _proj(
        x, mod, norm_gain, w_mix, w_fin, qg, ksg, kwg, rwkv_params)

    kc, vc_t = _compress(ck, _expand_cmp_pos(cmp_pos_k), _expand_cmp_pos(cmp_pos_v),
                         _expand_cmp_w1(cmp_k_w1), cmp_k_w2.astype(BF16),
                         _expand_cmp_w1(cmp_v_w1), cmp_v_w2.T.astype(BF16),
                         k_norm_gain[0].reshape(1, A_HEAD_DIM))
    bias_c, bias_d = _bias_tables(rel_bias, s, n16)
    y_a = _attention(q_t, kc, vc_t, ks, vs_t, kw, vw_t, bias_c, bias_d, gates_t)

    y_b = _rwkv(scan_bf16, scan_f32, vec(ln_x_w), vec(ln_x_b))

    gate = mod[:, 2 * D_MODEL:].reshape(bsz, 1, D_MODEL)
    return _final(x, y_a, y_b, cols_fin, gate, w_out_a.astype(BF16), w_out_b.astype(BF16), w_o.astype(BF16))


def kernel(x, c, w_ada, b_ada, norm_gain, w_in, q_norm_gain, k_norm_gain, cmp_pos_k, cmp_pos_v, cmp_k_w1, cmp_k_w2, cmp_v_w1, cmp_v_w2, rel_bias, shift_mu, w0, w_lora_up, a0, a_lora_up, k_k, k_a, r_k, ln_x_w, ln_x_b, w_out_a, w_out_b, w_o):
    for l in range(w_in.shape[0]):
        x = _layer(x, c, rel_bias, w_ada[l], b_ada[l], norm_gain[l], w_in[l], q_norm_gain[l], k_norm_gain[l],
                   cmp_pos_k[l], cmp_pos_v[l], cmp_k_w1[l], cmp_k_w2[l], cmp_v_w1[l], cmp_v_w2[l],
                   shift_mu[l], w0[l], w_lora_up[l], a0[l], a_lora_up[l], k_k[l], k_a[l], r_k[l],
                   ln_x_w[l], ln_x_b[l], w_out_a[l], w_out_b[l], w_o[l])
    return x
```

```python
import functools
import math

import numpy as np
import jax
import jax.numpy as jnp
from jax import lax
from jax.experimental import pallas as pl
from jax.experimental.pallas import tpu as pltpu

F32 = jnp.float32
BF16 = jnp.bfloat16

D_MODEL = 1024
A_HEADS = 8
A_HEAD_DIM = 64
A_KV_GROUPS = 2
A_HPG = A_HEADS // A_KV_GROUPS
A_WIDTH = A_HEADS * A_HEAD_DIM
A_KV_WIDTH = A_KV_GROUPS * A_HEAD_DIM
CMP_BLOCK = 32
CMP_STRIDE = 16
CMP_HIDDEN = 256
SLC_BLOCK = 64
SLC_TOPN = 16
WINDOW = 512
B_HEADS = 8
B_HEAD_DIM = 64
B_WIDTH = B_HEADS * B_HEAD_DIM
DECAY_LORA = 64
ICLR_LORA = 64
LNX_EPS = 64e-5
REL_BUCKETS = 32
REL_MAX_EXACT = 16
REL_MAX_DIST = 128
NORM_EPS = 1e-6
NEG_INF = -1e30
FORCE_SCORE = 1e30

LANES = 128
TQ = 128
CHUNK = 64
GATE_PAD = LANES
LOG2E = math.log2(math.e)
V_ROWS = A_HEAD_DIM + 16
GATE_ROWS = 16
NSA_COLS = A_WIDTH + 6 * A_KV_WIDTH + GATE_PAD
FIN_COLS = A_WIDTH + B_WIDTH + 2 * D_MODEL
RWKV_COLS = 3 * B_WIDTH + DECAY_LORA + ICLR_LORA
SCAN_BF16_COLS = 5 * B_WIDTH
SCAN_F32_COLS = 2 * B_WIDTH
VMEM_LIMIT = 56 * 1024 * 1024


def _dot(a, b):
    return jnp.dot(a.astype(BF16), b.astype(BF16), preferred_element_type=F32)


def _dot_nt(a, b):
    return lax.dot_general(a.astype(BF16), b.astype(BF16), (((1,), (1,)), ((), ())),
                           preferred_element_type=F32)


def _dot_tn(a, b):
    return lax.dot_general(a.astype(BF16), b.astype(BF16), (((0,), (0,)), ((), ())),
                           preferred_element_type=F32)


def _split2(x):
    hi = x.astype(BF16)
    lo = (x - hi.astype(F32)).astype(BF16)
    return hi, lo


def _split3(x):
    h1 = x.astype(BF16)
    r1 = x - h1.astype(F32)
    h2 = r1.astype(BF16)
    h3 = (r1 - h2.astype(F32)).astype(BF16)
    return h1, h2, h3


def _sigmoid(x):
    return 1.0 / (1.0 + jnp.exp(-x))


def _bucket_thresholds():
    n = np.arange(0, 4096)
    nf = np.maximum(n, REL_MAX_EXACT).astype(np.float64)
    val = np.log(nf / REL_MAX_EXACT) / math.log(REL_MAX_DIST / REL_MAX_EXACT) * (REL_BUCKETS - REL_MAX_EXACT)
    frac = np.abs(val - np.round(val))
    assert np.all((frac > 1e-4) | (n <= REL_MAX_EXACT) | (n >= REL_MAX_DIST))
    large = REL_MAX_EXACT + np.floor(val + 1e-9).astype(np.int64)
    bucket = np.where(n < REL_MAX_EXACT, n, np.minimum(large, REL_BUCKETS - 1))
    return [int(np.argmax(bucket >= j)) for j in range(REL_BUCKETS)]


_BUCKET_TH = _bucket_thresholds()


def _bias_from_dist(dist, tbl_ref, head):
    val = jnp.full(dist.shape, tbl_ref[0, head], F32)
    for j in range(1, REL_BUCKETS):
        val = jnp.where(dist >= _BUCKET_TH[j], tbl_ref[j, head], val)
    return val


def _ada_kernel(c_ref, w_ref, b_ref, o_ref):
    c = c_ref[...]
    o_ref[...] = _dot(c * _sigmoid(c), w_ref[...]) + b_ref[...]


def _ada(c, w_ada, b_ada):
    bsz = c.shape[0]
    return pl.pallas_call(
        _ada_kernel,
        grid=(3,),
        in_specs=[pl.BlockSpec((bsz, D_MODEL), lambda j: (0, 0)),
                  pl.BlockSpec((D_MODEL, D_MODEL), lambda j: (0, j)),
                  pl.BlockSpec((1, D_MODEL), lambda j: (0, j))],
        out_specs=pl.BlockSpec((bsz, D_MODEL), lambda j: (0, j)),
        out_shape=jax.ShapeDtypeStruct((bsz, 3 * D_MODEL), F32),
        name="ada",
    )(c, w_ada, b_ada.reshape(1, 3 * D_MODEL))


def _norm_rows(x_t, gain_col, n_seg):
    out = []
    for seg in range(n_seg):
        blk = x_t[seg * A_HEAD_DIM:(seg + 1) * A_HEAD_DIM, :]
        ms = jnp.mean(blk * blk, axis=0, keepdims=True)
        out.append(blk * lax.rsqrt(ms + NORM_EPS) * gain_col[seg * A_HEAD_DIM:(seg + 1) * A_HEAD_DIM, :])
    return out


def _head_sum(x):
    left = lax.broadcasted_iota(jnp.int32, x.shape, x.ndim - 1) < B_HEAD_DIM
    lo = jnp.sum(jnp.where(left, x, 0.0), axis=-1, keepdims=True)
    hi = jnp.sum(jnp.where(left, 0.0, x), axis=-1, keepdims=True)
    return jnp.where(left, lo, hi)


def _proj_kernel(x_ref, mod_ref, g_ref, wm_ref, wf_ref, qg_ref, ksg_ref, kwg_ref,
                 mu_ref, w0_ref, wl_ref, a0_ref, al_ref, kk_ref, ka_ref, rk_ref,
                 q_ref, ks_ref, vs_ref, kw_ref, vw_ref, gt_ref, ck_ref, of_ref, rw_ref, ld_ref, prev_ref):
    tm = x_ref.shape[1]

    @pl.when(pl.program_id(1) == 0)
    def _():
        prev_ref[...] = jnp.zeros(prev_ref.shape, F32)

    x = x_ref[0]
    ms = jnp.mean(x * x, axis=-1, keepdims=True)
    y = x * lax.rsqrt(ms + NORM_EPS) * g_ref[...]
    mod = mod_ref[0]
    h = (y * (1.0 + mod[:, D_MODEL:2 * D_MODEL]) + mod[:, :D_MODEL]).astype(BF16)
    crn = jnp.dot(h, wm_ref[...], preferred_element_type=F32)
    cr = crn[:, 0:RWKV_COLS]
    cn = crn[:, RWKV_COLS:RWKV_COLS + NSA_COLS]
    ck_ref[0] = cn[:, A_WIDTH:A_WIDTH + 2 * A_KV_WIDTH]

    rolled = pltpu.roll(cr, 1, axis=0)
    row8 = lax.broadcasted_iota(jnp.int32, (8, RWKV_COLS), 0)
    prev = jnp.concatenate([jnp.where(row8 == 0, prev_ref[0:1, :], rolled[0:8]), rolled[8:]], axis=0)
    prev_ref[0:1, :] = cr[tm - 1:tm, :]
    xs = cr + (prev - cr) * mu_ref[...]
    r = xs[:, 0:B_WIDTH]
    k = xs[:, B_WIDTH:2 * B_WIDTH]
    v = xs[:, 2 * B_WIDTH:3 * B_WIDTH]
    wd = xs[:, 3 * B_WIDTH:3 * B_WIDTH + DECAY_LORA]
    ad = xs[:, 3 * B_WIDTH + DECAY_LORA:3 * B_WIDTH + DECAY_LORA + ICLR_LORA]
    ld = -math.exp(-0.5) * _sigmoid(w0_ref[...] + _dot(jnp.tanh(wd), wl_ref[...]))
    a = _sigmoid(a0_ref[...] + _dot(ad, al_ref[...]))
    kk = k * kk_ref[...]
    k_mod = k * (1.0 + (a - 1.0) * ka_ref[...])
    rkr = r * k_mod * rk_ref[...]
    ld_ref[0, :, 0:B_WIDTH] = ld
    rw_ref[0, :, 0:B_WIDTH] = r.astype(BF16)
    rw_ref[0, :, 3 * B_WIDTH:4 * B_WIDTH] = k_mod.astype(BF16)
    rw_ref[0, :, 4 * B_WIDTH:5 * B_WIDTH] = v.astype(BF16)
    for pr in range(B_WIDTH // LANES):
        sl = slice(pr * LANES, (pr + 1) * LANES)
        kk_p = kk[:, sl]
        kk_p = kk_p * lax.rsqrt(jnp.maximum(_head_sum(kk_p * kk_p), 1e-24))
        rw_ref[0, :, B_WIDTH + pr * LANES:B_WIDTH + (pr + 1) * LANES] = kk_p.astype(BF16)
        rw_ref[0, :, 2 * B_WIDTH + pr * LANES:2 * B_WIDTH + (pr + 1) * LANES] = (kk_p * a[:, sl]).astype(BF16)
        ld_ref[0, :, B_WIDTH + pr * LANES:B_WIDTH + (pr + 1) * LANES] = _head_sum(rkr[:, sl])

    lane = lax.broadcasted_iota(jnp.int32, (TQ, LANES), 1)
    row = lax.broadcasted_iota(jnp.int32, (TQ, LANES), 0)
    ones_rows = (lax.broadcasted_iota(jnp.int32, (V_ROWS - A_HEAD_DIM, TQ), 0) == 0).astype(BF16)
    off = A_WIDTH + 2 * A_KV_WIDTH
    for sub in range(tm // TQ):
        c = cn[sub * TQ:(sub + 1) * TQ]
        q_heads = _norm_rows(c[:, 0:A_WIDTH].T, qg_ref[...], A_HEADS)
        ks_t = jnp.concatenate(_norm_rows(c[:, off:off + A_KV_WIDTH].T, ksg_ref[...], A_KV_GROUPS), axis=0)
        kw_t = jnp.concatenate(_norm_rows(c[:, off + 2 * A_KV_WIDTH:off + 3 * A_KV_WIDTH].T, kwg_ref[...],
                                          A_KV_GROUPS), axis=0)
        ksn = ks_t.T
        kwn = kw_t.T
        vs_t = c[:, off + A_KV_WIDTH:off + 2 * A_KV_WIDTH].T.astype(BF16)
        vw_t = c[:, off + 3 * A_KV_WIDTH:off + 4 * A_KV_WIDTH].T.astype(BF16)
        gates_t = _sigmoid(c[:, off + 4 * A_KV_WIDTH:off + 5 * A_KV_WIDTH]).T
        blk = (pl.program_id(1) * tm + sub * TQ + row) // SLC_BLOCK
        onehot = jnp.where(lane - A_HEAD_DIM == blk, 1.0, 0.0)
        for g in range(A_KV_GROUPS):
            q_ref[0, g, sub] = jnp.concatenate(q_heads[g * A_HPG:(g + 1) * A_HPG], axis=1).astype(BF16)
            sl = slice(g * A_HEAD_DIM, (g + 1) * A_HEAD_DIM)
            k_g = ksn if g == 0 else pltpu.roll(ksn, A_HEAD_DIM, axis=1)
            ks_ref[0, g, sub * TQ:(sub + 1) * TQ, :] = jnp.where(lane < A_HEAD_DIM, k_g, onehot).astype(BF16)
            kw_g = kwn if g == 0 else pltpu.roll(kwn, A_HEAD_DIM, axis=1)
            kw_ref[0, g, sub * TQ:(sub + 1) * TQ, :] = jnp.where(lane < A_HEAD_DIM, kw_g, 0.0).astype(BF16)
            vs_ref[0, g, sub] = jnp.concatenate([vs_t[sl, :], ones_rows], axis=0)
            vw_ref[0, g, sub] = jnp.concatenate([vw_t[sl, :], ones_rows], axis=0)
            gt_ref[0, g, sub] = gates_t[g * A_HEAD_DIM:g * A_HEAD_DIM + GATE_ROWS, :]
    of_ref[0] = jnp.dot(h, wf_ref[...], preferred_element_type=F32).astype(BF16)


def _proj(x, mod, norm_gain, w_mix, w_fin, qg, ksg, kwg, rwkv_params, tm=512):
    bsz, s, _ = x.shape
    nt, nsub = s // TQ, tm // TQ
    assert A_HEAD_DIM + s // SLC_BLOCK <= LANES and A_KV_WIDTH == LANES and 2 * B_HEAD_DIM == LANES
    const = lambda b, i: (0, 0)
    weight = lambda cols: pl.BlockSpec((D_MODEL, cols), const, pipeline_mode=pl.Buffered(1))
    whole = lambda t: pl.BlockSpec(t.shape, const)
    col = lambda t: jnp.broadcast_to(t.reshape(-1, 1), (t.size, LANES))
    k_spec = lambda width: pl.BlockSpec((1, A_KV_GROUPS, tm, width), lambda b, i: (b, 0, i, 0))
    k_shape = lambda width: jax.ShapeDtypeStruct((bsz, A_KV_GROUPS, s, width), BF16)
    tile_spec = lambda r, c: pl.BlockSpec((1, A_KV_GROUPS, nsub, r, c), lambda b, i: (b, 0, i, 0, 0))
    tile_shape = lambda r, c, dt: jax.ShapeDtypeStruct((bsz, A_KV_GROUPS, nt, r, c), dt)
    return pl.pallas_call(
        _proj_kernel,
        grid=(bsz, s // tm),
        in_specs=[pl.BlockSpec((1, tm, D_MODEL), lambda b, i: (b, i, 0)),
                  pl.BlockSpec((1, 1, 3 * D_MODEL), lambda b, i: (b, 0, 0)),
                  pl.BlockSpec((1, D_MODEL), const),
                  weight(RWKV_COLS + NSA_COLS), weight(FIN_COLS),
                  pl.BlockSpec((A_WIDTH, LANES), const),
                  pl.BlockSpec((A_KV_WIDTH, LANES), const),
                  pl.BlockSpec((A_KV_WIDTH, LANES), const)] + [whole(t) for t in rwkv_params],
        out_specs=[tile_spec(A_HEAD_DIM, A_HPG * TQ),
                   k_spec(LANES), tile_spec(V_ROWS, TQ), k_spec(LANES), tile_spec(V_ROWS, TQ),
                   tile_spec(GATE_ROWS, TQ),
                   pl.BlockSpec((1, tm, 2 * A_KV_WIDTH), lambda b, i: (b, i, 0)),
                   pl.BlockSpec((1, tm, FIN_COLS), lambda b, i: (b, i, 0)),
                   pl.BlockSpec((1, tm, SCAN_BF16_COLS), lambda b, i: (b, i, 0)),
                   pl.BlockSpec((1, tm, SCAN_F32_COLS), lambda b, i: (b, i, 0))],
        out_shape=[tile_shape(A_HEAD_DIM, A_HPG * TQ, BF16),
                   k_shape(LANES), tile_shape(V_ROWS, TQ, BF16), k_shape(LANES), tile_shape(V_ROWS, TQ, BF16),
                   tile_shape(GATE_ROWS, TQ, F32),
                   jax.ShapeDtypeStruct((bsz, s, 2 * A_KV_WIDTH), F32),
                   jax.ShapeDtypeStruct((bsz, s, FIN_COLS), BF16),
                   jax.ShapeDtypeStruct((bsz, s, SCAN_BF16_COLS), BF16),
                   jax.ShapeDtypeStruct((bsz, s, SCAN_F32_COLS), F32)],
        scratch_shapes=[pltpu.VMEM((8, RWKV_COLS), F32)],
        compiler_params=pltpu.CompilerParams(dimension_semantics=("parallel", "arbitrary"),
                                             vmem_limit_bytes=VMEM_LIMIT),
        name="proj",
    )(x, mod.reshape(bsz, 1, 3 * D_MODEL), norm_gain.reshape(1, D_MODEL), w_mix, w_fin,
      col(qg), col(ksg), col(kwg), *rwkv_params)


def _compress_kernel(ck_ref, cv_ref, pk_ref, pv_ref, w1k_ref, w2k_ref, w1v_ref, w2v_ref, kg_ref, kc_ref, vc_ref):
    nb, n16 = ck_ref.shape[0], ck_ref.shape[1] // CMP_STRIDE

    def rows16(ref):
        return jnp.concatenate(
            [jnp.concatenate([ref[b, pl.ds(p, n16, stride=CMP_STRIDE), :] for p in range(CMP_STRIDE)], axis=1)
             for b in range(nb)], axis=0)

    def hidden(z, pos_ref, w1_ref, g):
        top = _dot(z + pos_ref[0:1, :], w1_ref[g, 0])
        bot = _dot(z + pos_ref[1:2, :], w1_ref[g, 1])
        bot = jnp.concatenate([pltpu.roll(bot[b * n16:(b + 1) * n16], n16 - 1, axis=0) for b in range(nb)], axis=0)
        return jax.nn.gelu(top + bot, approximate=True)

    zk = rows16(ck_ref)
    zv = rows16(cv_ref)
    for g in range(A_KV_GROUPS):
        kc = _dot(hidden(zk, pk_ref, w1k_ref, g), w2k_ref[...])
        ms = jnp.mean(kc * kc, axis=-1, keepdims=True)
        kc = (kc * lax.rsqrt(ms + NORM_EPS) * kg_ref[...]).astype(BF16)
        vc_t = _dot_nt(w2v_ref[...], hidden(zv, pv_ref, w1v_ref, g)).astype(BF16)
        for b in range(nb):
            kc_ref[b, g] = kc[b * n16:(b + 1) * n16]
            vc_ref[b, g] = vc_t[:, b * n16:(b + 1) * n16]


CMP_NB = 4


def _expand_cmp_w1(w1):
    w = w1.reshape(2, CMP_STRIDE, 1, A_HEAD_DIM, CMP_HIDDEN)
    per_group = []
    for g in range(A_KV_GROUPS):
        pad = [(0, 0), (0, 0), (g, A_KV_GROUPS - 1 - g), (0, 0), (0, 0)]
        per_group.append(jnp.pad(w, pad).reshape(2, CMP_STRIDE * A_KV_WIDTH, CMP_HIDDEN))
    return jnp.stack(per_group).astype(BF16)


def _expand_cmp_pos(pos):
    p = jnp.broadcast_to(pos.reshape(2, CMP_STRIDE, 1, A_HEAD_DIM), (2, CMP_STRIDE, A_KV_GROUPS, A_HEAD_DIM))
    return p.reshape(2, CMP_STRIDE * A_KV_WIDTH)


def _compress(ck, pk, pv, w1k, w2k, w1v, w2v_t, kg):
    bsz, s, _ = ck.shape
    n16 = s // CMP_STRIDE
    zw = CMP_STRIDE * A_KV_WIDTH
    const = lambda b: (0, 0)
    const4 = lambda b: (0, 0, 0, 0)
    nb = CMP_NB if bsz % CMP_NB == 0 else 1
    return pl.pallas_call(
        _compress_kernel,
        grid=(bsz // nb,),
        in_specs=[pl.BlockSpec((nb, s, A_KV_WIDTH), lambda b: (b, 0, 0)),
                  pl.BlockSpec((nb, s, A_KV_WIDTH), lambda b: (b, 0, 1)),
                  pl.BlockSpec((2, zw), const), pl.BlockSpec((2, zw), const),
                  pl.BlockSpec((A_KV_GROUPS, 2, zw, CMP_HIDDEN), const4), pl.BlockSpec((CMP_HIDDEN, A_HEAD_DIM), const),
                  pl.BlockSpec((A_KV_GROUPS, 2, zw, CMP_HIDDEN), const4), pl.BlockSpec((A_HEAD_DIM, CMP_HIDDEN), const),
                  pl.BlockSpec((1, A_HEAD_DIM), const)],
        out_specs=[pl.BlockSpec((nb, A_KV_GROUPS, n16, A_HEAD_DIM), lambda b: (b, 0, 0, 0)),
                   pl.BlockSpec((nb, A_KV_GROUPS, A_HEAD_DIM, n16), lambda b: (b, 0, 0, 0))],
        out_shape=[jax.ShapeDtypeStruct((bsz, A_KV_GROUPS, n16, A_HEAD_DIM), BF16),
                   jax.ShapeDtypeStruct((bsz, A_KV_GROUPS, A_HEAD_DIM, n16), BF16)],
        compiler_params=pltpu.CompilerParams(dimension_semantics=("parallel",)),
        name="compress",
    )(ck, ck, pk, pv, w1k, w2k, w1v, w2v_t, kg)


TILE_FAR, TILE_EDGE, N_BIAS_TILES = 2, 3, 4
SUB = 4
CMP_WIN = 16
ATT_CALLS = 1
ATT_TILES = 16


def _bias_cmp_kernel(tbl_ref, o_ref):
    i = pl.program_id(0)
    n_cmp = o_ref.shape[2]
    first_n = pl.multiple_of(jnp.maximum(i * (TQ // CMP_STRIDE) - 8, 0), 8)
    n = lax.broadcasted_iota(jnp.int32, (n_cmp, TQ), 0)
    n_win = first_n + lax.broadcasted_iota(jnp.int32, (CMP_WIN, TQ), 0)
    q = lax.broadcasted_iota(jnp.int32, (CMP_WIN, TQ), 1)
    dist = i * TQ + q - (n_win * CMP_STRIDE + CMP_BLOCK - 1)
    for head in range(A_HEADS):
        g, h = divmod(head, A_HPG)
        o_ref[0, g, :, h * TQ:(h + 1) * TQ] = jnp.where(n < first_n, tbl_ref[REL_BUCKETS - 1, head] * LOG2E, NEG_INF)
        bias = _bias_from_dist(dist, tbl_ref, head)
        o_ref[0, g, pl.ds(first_n, CMP_WIN), h * TQ:(h + 1) * TQ] = jnp.where(dist >= 0, bias * LOG2E, NEG_INF)


def _bias_toeplitz_kernel(tbl_ref, o_ref):
    g = pl.program_id(0)
    k = lax.broadcasted_iota(jnp.int32, (TQ, TQ), 0)
    q = lax.broadcasted_iota(jnp.int32, (TQ, TQ), 1)
    for r in range(N_BIAS_TILES):
        dist = (WINDOW // TQ if r == TILE_EDGE else r) * TQ + q - k
        valid = (dist >= 0) & (dist < WINDOW)
        for h in range(A_HPG):
            head = g * A_HPG + h
            bias = _bias_from_dist(dist, tbl_ref, head) - tbl_ref[REL_BUCKETS - 1, head]
            o_ref[0, r, :, h * TQ:(h + 1) * TQ] = jnp.where(valid, bias * LOG2E, NEG_INF)


def _bias_tables(rel_bias, s, n_cmp):
    smem = pl.BlockSpec(memory_space=pltpu.SMEM)
    nt = s // TQ
    bias_c = pl.pallas_call(
        _bias_cmp_kernel,
        grid=(nt,),
        in_specs=[smem],
        out_specs=pl.BlockSpec((1, A_KV_GROUPS, n_cmp, A_HPG * TQ), lambda i: (i, 0, 0, 0)),
        out_shape=jax.ShapeDtypeStruct((nt, A_KV_GROUPS, n_cmp, A_HPG * TQ), F32),
        name="bias_cmp",
    )(rel_bias)
    assert _BUCKET_TH[REL_BUCKETS - 1] <= TQ + 1 and WINDOW // TQ >= 3
    assert _BUCKET_TH[REL_BUCKETS - 1] <= 9 * CMP_STRIDE - CMP_BLOCK + 1
    assert (CMP_WIN - 8) * CMP_STRIDE + CMP_BLOCK >= TQ and n_cmp >= CMP_WIN
    bias_d = pl.pallas_call(
        _bias_toeplitz_kernel,
        grid=(A_KV_GROUPS,),
        in_specs=[smem],
        out_specs=pl.BlockSpec((1, N_BIAS_TILES, TQ, A_HPG * TQ), lambda g: (g, 0, 0, 0)),
        out_shape=jax.ShapeDtypeStruct((A_KV_GROUPS, N_BIAS_TILES, TQ, A_HPG * TQ), F32),
        name="bias_toeplitz",
    )(rel_bias)
    return bias_c, bias_d


def _attn_kernel(*refs, j):
    q_refs, refs = refs[:ATT_TILES], refs[ATT_TILES:]
    kc_ref, vc_ref, ks_ref, vs_ref, kw_ref, vw_ref = refs[:6]
    bc_refs, bd_ref, gt_refs, o_ref = refs[6:6 + ATT_TILES], refs[6 + ATT_TILES], refs[7 + ATT_TILES:-1], refs[-1]
    tq = TQ
    n_cmp = kc_ref.shape[2]
    n_slc = ks_ref.shape[2] // SLC_BLOCK
    wt = WINDOW // tq
    dh = A_HEAD_DIM
    round8 = lambda n: -(-n // 8) * 8
    tiles = [dict(i=j + t * ATT_CALLS, last=(j + t * ATT_CALLS) // SUB, n_tok=(j + t * ATT_CALLS + 1) * tq,
                  q=q_refs[t][0, 0, 0], bias_c=bc_refs[t], gates=gt_refs[t][0, 0, 0]) for t in range(ATT_TILES)]
    zero_rows = jnp.zeros((LANES - dh, A_HPG * tq), BF16)

    def scores(k_ref, q_mat, i, first_tile, n_sub, tile_index):
        s = jnp.dot(k_ref[0, 0, first_tile * tq:(first_tile + n_sub) * tq, :], q_mat,
                    preferred_element_type=F32)
        idx = [tile_index(i - (first_tile + t)) for t in range(n_sub)]
        parts = [s[t * tq:(t + 1) * tq] if idx[t] == TILE_FAR else s[t * tq:(t + 1) * tq] + bd_ref[0, idx[t]]
                 for t in range(n_sub)]
        return jnp.concatenate(parts, axis=0)

    def values_t(v_ref, first_tile, n_sub):
        return jnp.concatenate([v_ref[0, 0, first_tile + t] for t in range(n_sub)], axis=1)

    win_tile = lambda r: TILE_EDGE if r == wt else min(r, TILE_FAR)
    for t in tiles:
        t["first_w"] = max(t["i"] - wt, 0)
        t["n_w"] = t["i"] - t["first_w"] + 1
        q_pad = jnp.concatenate([t["q"], zero_rows], axis=0)
        t["s_w"] = scores(kw_ref, q_pad, t["i"], t["first_w"], t["n_w"], win_tile)
    for t in tiles:
        t["n_cmp"] = min(n_cmp, round8(t["n_tok"] // CMP_STRIDE))
        t["n_slc"] = min(n_slc, round8(t["n_tok"] // SLC_BLOCK))
        bias = t["bias_c"][0, 0, 0:t["n_cmp"], :]
        t["valid_c"] = bias > 0.5 * NEG_INF
        t["s_c"] = jnp.dot(kc_ref[0, 0, 0:t["n_cmp"], :], t["q"], preferred_element_type=F32) + bias

    r1, r2 = SLC_BLOCK // CMP_STRIDE, CMP_BLOCK // CMP_STRIDE
    jj = lax.broadcasted_iota(jnp.int32, (n_slc, n_cmp), 0)
    nn = lax.broadcasted_iota(jnp.int32, (n_slc, n_cmp), 1)
    d = nn - r1 * jj
    cnt = jnp.zeros((n_slc, n_cmp), F32)
    for a in range(r1):
        for c in range(r2):
            cnt = cnt + jnp.where(d == a - c, 1.0, 0.0)
    cnt = cnt.astype(BF16)
    for t in tiles:
        s = t["s_c"]
        e = jnp.where(t["valid_c"], jnp.exp2(s - jnp.max(s, axis=0, keepdims=True)), 0.0)
        l = jnp.sum(e, axis=0, keepdims=True)
        p = e * (1.0 / jnp.where(l > 0.0, l, 1.0))
        rest = n_cmp - t["n_cmp"]
        pad = (lambda a: jnp.concatenate([a, jnp.zeros((rest, a.shape[1]), a.dtype)], axis=0)) if rest else (lambda a: a)
        t["out_c"] = jnp.dot(vc_ref[0, 0], pad(p).astype(BF16), preferred_element_type=F32)
        p_grp = pad(sum(p[:, h * tq:(h + 1) * tq] for h in range(A_HPG)))
        t["imp"] = sum(jnp.dot(cnt[0:t["n_slc"]], part, preferred_element_type=F32)
                       for part in _split3(p_grp))

    for t in tiles:
        s = t["s_w"]
        t["p_w"] = jnp.exp2(s - jnp.max(s, axis=0, keepdims=True)).astype(BF16)
    for t in tiles:
        acc = jnp.dot(values_t(vw_ref, t["first_w"], t["n_w"]), t["p_w"], preferred_element_type=F32)
        out_w = acc[:dh] * (1.0 / acc[dh:dh + 1])
        gates = t["gates"]
        t["part"] = [gates[h:h + 1, :] * t["out_c"][:, h * tq:(h + 1) * tq]
                     + gates[2 * A_HPG + h:2 * A_HPG + h + 1, :] * out_w[:, h * tq:(h + 1) * tq]
                     for h in range(A_HPG)]

    for t in tiles:
        nb = t["n_slc"]
        blk = lax.broadcasted_iota(jnp.int32, (nb, tq), 0)
        tpos = t["i"] * tq + lax.broadcasted_iota(jnp.int32, (nb, tq), 1)
        cur = tpos // SLC_BLOCK
        forced = (blk == 0) | (blk == cur) | (blk == cur - 1)
        causal = blk * SLC_BLOCK <= tpos
        imp = jnp.where(forced, FORCE_SCORE, jnp.where(causal, t["imp"], NEG_INF))
        rank = jnp.zeros((nb, tq), F32)
        for c in range(nb):
            row = imp[c:c + 1, :]
            ahead = (row > imp) | ((row == imp) & (blk > c))
            rank = rank + jnp.where(ahead, 1.0, 0.0)
        pen = jnp.where(rank < float(min(SLC_TOPN, n_slc)), 0.0, -FORCE_SCORE)
        pen = jnp.concatenate([pen, jnp.zeros((LANES - dh - nb, tq), F32)], axis=0)
        t["q_aug"] = jnp.concatenate([t["q"], jnp.concatenate([pen] * A_HPG, axis=1).astype(BF16)], axis=0)

    sel_tile = lambda r: min(r, TILE_FAR)

    jobs = [(t, c, min(SUB, t["i"] - c * SUB + 1)) for t in tiles for c in range(t["last"], -1, -1)]
    ss = [scores(ks_ref, t["q_aug"], t["i"], c * SUB, n, sel_tile) for t, c, n in jobs]
    ms = [jnp.max(s, axis=0, keepdims=True) for s in ss]
    ps = [jnp.exp2(s - m).astype(BF16) for s, m in zip(ss, ms)]
    accs = [jnp.dot(values_t(vs_ref, c * SUB, n), p, preferred_element_type=F32) for (t, c, n), p in zip(jobs, ps)]

    for k, t in enumerate(tiles):
        mine = [n for n, job in enumerate(jobs) if job[0] is t]
        m = functools.reduce(jnp.maximum, [ms[n] for n in mine])
        acc = sum(jnp.exp2(ms[n] - m) * accs[n] for n in mine)
        out_s = acc[:dh] * (1.0 / acc[dh:dh + 1])
        gates = t["gates"]
        blocks = [t["part"][h] + gates[A_HPG + h:A_HPG + h + 1, :] * out_s[:, h * tq:(h + 1) * tq]
                  for h in range(A_HPG)]
        o_ref[0, k] = jnp.concatenate(blocks, axis=0).T.astype(BF16)


def _attention(q_t, kc, vc_t, ks, vs_t, kw, vw_t, bias_c, bias_d, gates_t):
    bsz, _, nt, _, _ = q_t.shape
    s = ks.shape[2]
    n_cmp = kc.shape[2]
    assert nt == ATT_TILES * ATT_CALLS and WINDOW // TQ + 1 <= nt
    k_spec = pl.BlockSpec((1, 1, s, LANES), lambda b, g: (b, g, 0, 0))
    vt_spec = pl.BlockSpec((1, 1, nt, V_ROWS, TQ), lambda b, g: (b, g, 0, 0, 0))
    outs = []
    for j in range(ATT_CALLS):
        per_tile = lambda spec: [spec(j + t * ATT_CALLS) for t in range(ATT_TILES)]
        q_spec = lambda i: pl.BlockSpec((1, 1, 1, A_HEAD_DIM, A_HPG * TQ), lambda b, g: (b, g, i, 0, 0))
        bc_spec = lambda i: pl.BlockSpec((1, 1, n_cmp, A_HPG * TQ), lambda b, g: (i, g, 0, 0))
        gt_spec = lambda i: pl.BlockSpec((1, 1, 1, GATE_ROWS, TQ), lambda b, g: (b, g, i, 0, 0))
        outs.append(pl.pallas_call(
            functools.partial(_attn_kernel, j=j),
            grid=(bsz, A_KV_GROUPS),
            in_specs=(per_tile(q_spec)
                      + [pl.BlockSpec((1, 1, n_cmp, A_HEAD_DIM), lambda b, g: (b, g, 0, 0)),
                         pl.BlockSpec((1, 1, A_HEAD_DIM, n_cmp), lambda b, g: (b, g, 0, 0)),
                         k_spec, vt_spec, k_spec, vt_spec]
                      + per_tile(bc_spec)
                      + [pl.BlockSpec((1, N_BIAS_TILES, TQ, A_HPG * TQ), lambda b, g: (g, 0, 0, 0))]
                      + per_tile(gt_spec)),
            out_specs=pl.BlockSpec((1, ATT_TILES, TQ, A_HPG * A_HEAD_DIM), lambda b, g: (b, 0, 0, g)),
            out_shape=jax.ShapeDtypeStruct((bsz, ATT_TILES, TQ, A_WIDTH), BF16),
            compiler_params=pltpu.CompilerParams(dimension_semantics=("parallel", "parallel"),
                                                 vmem_limit_bytes=VMEM_LIMIT),
            name=f"attn{j}",
        )(*([q_t] * ATT_TILES), kc, vc_t, ks, vs_t, kw, vw_t, *([bias_c] * ATT_TILES), bias_d,
          *([gates_t] * ATT_TILES)))
    return outs


def _rwkv_kernel(rw_ref, ld_ref, lw_ref, lb_ref, o_ref, state_ref):
    cc = pl.program_id(1)
    n = B_HEAD_DIM
    nb, csz = rw_ref.shape[0], rw_ref.shape[1]

    @pl.when(cc == 0)
    def _():
        state_ref[...] = jnp.zeros(state_ref.shape, F32)

    ti = lax.broadcasted_iota(jnp.int32, (csz, LANES), 0)
    si = lax.broadcasted_iota(jnp.int32, (csz, LANES), 1) % n
    lower = si <= ti
    strict = si < ti
    eye = jnp.where(si == ti, 1.0, 0.0)
    tri = jnp.where(lax.broadcasted_iota(jnp.int32, (csz, csz), 1) <= lax.broadcasted_iota(jnp.int32, (csz, csz), 0),
                    1.0, 0.0).astype(BF16)
    n_pairs = B_WIDTH // LANES
    left =lax.broadcasted_iota(jnp.int32, (csz, LANES), 1) < n
    row_left = lax.broadcasted_iota(jnp.int32, (LANES, LANES), 0) < n
    same_head = row_left == (lax.broadcasted_iota(jnp.int32, (LANES, LANES), 1) < n)

    def blockdiag(y):
        zero = jnp.zeros_like(y)
        return jnp.concatenate([jnp.where(left, y, zero), jnp.where(left, zero, y)], axis=0)

    chains = []
    for bi in range(nb):
        r, kk, b, k_mod, v = (rw_ref[bi, :, m * B_WIDTH:(m + 1) * B_WIDTH].astype(F32) for m in range(5))
        ld = ld_ref[bi, :, 0:B_WIDTH]
        rsum = ld_ref[bi, :, B_WIDTH:2 * B_WIDTH]

        ld_hi, ld_lo = _split2(ld)
        cum = jnp.dot(tri, ld_hi, preferred_element_type=F32) + jnp.dot(tri, ld_lo, preferred_element_type=F32)
        g_inc = jnp.exp(cum)
        g_exc = jnp.exp(cum - ld)
        g_inv = jnp.exp(-cum)
        g_end = jnp.exp(cum[csz - 1:csz, :] - cum)
        g_all = g_inc[csz - 1:csz, :]

        for pr in range(n_pairs):
            sl = slice(pr * LANES, (pr + 1) * LANES)
            kk_p = kk[:, sl]
            b_p = b[:, sl]
            bt = (b_p * g_inv[:, sl]).astype(BF16)
            kt = (k_mod[:, sl] * g_inv[:, sl]).astype(BF16)
            ch = dict(
                idx=bi * n_pairs + pr,
                v=v[:, sl],
                lhs=jnp.concatenate([-kk_p * g_exc[:, sl], r[:, sl] * g_inc[:, sl]], axis=0).astype(BF16),
                rhs=jnp.concatenate([blockdiag(bt), blockdiag(kt)], axis=0),
                bk=jnp.concatenate([b_p * g_end[:, sl], k_mod[:, sl] * g_end[:, sl]], axis=0).astype(BF16),
                g_all=g_all[:, sl],
                bonus=rsum[:, sl] * v[:, sl],
            )
            chains.append(ch)

    for ch in chains:
        x = _dot_nt(ch["lhs"], ch["rhs"])
        xb, xk = x[:, :LANES], x[:, LANES:]
        ch["a_ab"] = jnp.where(strict, xb[:csz], 0.0)
        a_ak = jnp.where(strict, xk[:csz], 0.0)
        m_rk = jnp.where(lower, xk[csz:], 0.0)
        ch["ak_rk"] = jnp.concatenate([a_ak, m_rk], axis=0).astype(BF16)
        ch["m_rb"] = jnp.where(lower, xb[csz:], 0.0).astype(BF16)
    for ch in chains:
        akv = _dot(ch["ak_rk"], blockdiag(ch["v"].astype(BF16)))
        ch["akv"], ch["mrkv"] = akv[:csz], akv[csz:]
        ch["tinv"] = eye + ch["a_ab"]
        ch["pw"] = ch["a_ab"].astype(BF16)
    n_sq = int(math.log2(csz)) - 1
    for ch in chains:
        ch["pw"] = _dot(ch["pw"], blockdiag(ch["pw"])).astype(BF16)
    for step in range(n_sq):
        for ch in chains:
            if step + 1 < n_sq:
                both = _dot(jnp.concatenate([ch["pw"], ch["tinv"].astype(BF16)], axis=0), blockdiag(ch["pw"]))
                ch["tinv"] = ch["tinv"] + both[csz:]
                ch["pw"] = both[:csz].astype(BF16)
            else:
                ch["tinv"] = ch["tinv"] + _dot(ch["tinv"], blockdiag(ch["pw"]))
    for ch in chains:
        ch["s0"] = state_ref[ch["idx"]]
        ch["as0"] = _dot_nt(ch["lhs"], ch["s0"])
    for ch in chains:
        w = (ch["as0"][:csz] + ch["akv"]).astype(BF16)
        ch["u"] = _dot(ch["tinv"], blockdiag(w))
    outs = []
    for ch in chains:
        u = ch["u"]
        y = ch["as0"][csz:] + _dot(ch["m_rb"], blockdiag(u.astype(BF16))) + ch["mrkv"]
        uv = jnp.concatenate([u, ch["v"]], axis=0)
        state_ref[ch["idx"]] = ch["s0"] * ch["g_all"] + jnp.where(same_head, _dot_tn(uv, ch["bk"]), 0.0)
        yc = y - _head_sum(y) * (1.0 / n)
        var = _head_sum(yc * yc) * (1.0 / n)
        outs.append(yc * lax.rsqrt(var + LNX_EPS))
    for bi in range(nb):
        yn = jnp.concatenate(outs[bi * n_pairs:(bi + 1) * n_pairs], axis=-1)
        bonus = jnp.concatenate([ch["bonus"] for ch in chains[bi * n_pairs:(bi + 1) * n_pairs]], axis=-1)
        o_ref[bi] = (yn * lw_ref[...] + lb_ref[...] + bonus).astype(BF16)


RWKV_NB = 16


def _rwkv(scan_bf16, scan_f32, ln_w, ln_b):
    bsz, s, _ = scan_bf16.shape
    nb = RWKV_NB if bsz % RWKV_NB == 0 else 1
    const = lambda b, c: (0, 0)
    vec = pl.BlockSpec((1, B_WIDTH), const)
    return pl.pallas_call(
        _rwkv_kernel,
        grid=(bsz // nb, s // CHUNK),
        in_specs=[pl.BlockSpec((nb, CHUNK, SCAN_BF16_COLS), lambda b, c: (b, c, 0)),
                  pl.BlockSpec((nb, CHUNK, SCAN_F32_COLS), lambda b, c: (b, c, 0)),
                  vec, vec],
        out_specs=pl.BlockSpec((nb, CHUNK, B_WIDTH), lambda b, c: (b, c, 0)),
        out_shape=jax.ShapeDtypeStruct((bsz, s, B_WIDTH), BF16),
        scratch_shapes=[pltpu.VMEM((nb * B_WIDTH // LANES, LANES, LANES), F32)],
        compiler_params=pltpu.CompilerParams(dimension_semantics=("parallel", "arbitrary")),
        name="rwkv",
    )(scan_bf16, scan_f32, ln_w, ln_b)


def _final_kernel(x_ref, *refs):
    ya_refs, (yb_ref, cf_ref, gate_ref, wa_ref, wb_ref, wo_ref, o_ref) = refs[:ATT_CALLS], refs[ATT_CALLS:]
    a_silu = cf_ref[0, :, 0:A_WIDTH].astype(F32)
    b_silu = cf_ref[0, :, A_WIDTH:A_WIDTH + B_WIDTH].astype(F32)
    merge_a = cf_ref[0, :, A_WIDTH + B_WIDTH:A_WIDTH + B_WIDTH + D_MODEL].astype(F32)
    merge_b = cf_ref[0, :, A_WIDTH + B_WIDTH + D_MODEL:A_WIDTH + B_WIDTH + 2 * D_MODEL].astype(F32)
    n_tiles = x_ref.shape[1] // TQ
    y_a = jnp.concatenate([ya_refs[u % ATT_CALLS][0, u // ATT_CALLS] for u in range(n_tiles)], axis=0)
    ya = y_a.astype(F32) * (a_silu * _sigmoid(a_silu))
    yb = yb_ref[0].astype(F32) * (b_silu * _sigmoid(b_silu))
    merged = _sigmoid(merge_a) * _dot(ya, wa_ref[...]) + _sigmoid(merge_b) * _dot(yb, wb_ref[...])
    o_ref[0] = x_ref[0] + gate_ref[0] * _dot(merged, wo_ref[...])


def _final(x, y_a, y_b, cols_fin, gate, w_out_a, w_out_b, w_o):
    bsz, s, _ = x.shape
    tm = 4 * TQ
    assert (tm // TQ) % ATT_CALLS == 0
    const = lambda b, i: (0, 0)
    row = lambda w: pl.BlockSpec((1, tm, w), lambda b, i: (b, i, 0))
    ya_spec = pl.BlockSpec((1, tm // TQ // ATT_CALLS, TQ, A_WIDTH), lambda b, i: (b, i, 0, 0))
    return pl.pallas_call(
        _final_kernel,
        grid=(bsz, s // tm),
        in_specs=[row(D_MODEL)] + [ya_spec] * ATT_CALLS + [row(B_WIDTH), row(FIN_COLS),
                  pl.BlockSpec((1, 1, D_MODEL), lambda b, i: (b, 0, 0)),
                  pl.BlockSpec((A_WIDTH, D_MODEL), const),
                  pl.BlockSpec((B_WIDTH, D_MODEL), const),
                  pl.BlockSpec((D_MODEL, D_MODEL), const)],
        out_specs=row(D_MODEL),
        out_shape=jax.ShapeDtypeStruct((bsz, s, D_MODEL), F32),
        compiler_params=pltpu.CompilerParams(dimension_semantics=("parallel", "parallel"),
                                             vmem_limit_bytes=VMEM_LIMIT),
        name="final",
    )(x, *y_a, y_b, cols_fin, gate, w_out_a, w_out_b, w_o)


def _split_w_in(w_in):
    nsa_in = 2 * A_WIDTH + 6 * A_KV_WIDTH + 3 * A_HEADS
    o_gate = A_WIDTH + 6 * A_KV_WIDTH
    o_asilu = o_gate + 3 * A_HEADS
    o_shift = nsa_in
    o_rest = nsa_in + RWKV_COLS
    gate_w = w_in[:, o_gate:o_asilu].reshape(D_MODEL, 3, A_KV_GROUPS, A_HPG)
    gate_w = gate_w.transpose(0, 2, 1, 3).reshape(D_MODEL, A_KV_GROUPS, 3 * A_HPG)
    gate_w = jnp.pad(gate_w, ((0, 0), (0, 0), (0, A_HEAD_DIM - 3 * A_HPG))).reshape(D_MODEL, GATE_PAD)
    w_nsa = jnp.concatenate([w_in[:, :o_gate], gate_w], axis=1)
    w_fin = jnp.concatenate([w_in[:, o_asilu:o_shift], w_in[:, o_rest:]], axis=1)
    w_mix = jnp.concatenate([w_in[:, o_shift:o_rest], w_nsa], axis=1)
    return w_mix.astype(BF16), w_fin.astype(BF16)


def _layer(x, c, rel_bias, w_ada, b_ada, norm_gain, w_in, q_norm_gain, k_norm_gain,
           cmp_pos_k, cmp_pos_v, cmp_k_w1, cmp_k_w2, cmp_v_w1, cmp_v_w2,
           shift_mu, w0, w_lora_up, a0, a_lora_up, k_k, k_a, r_k, ln_x_w, ln_x_b,
           w_out_a, w_out_b, w_o):
    bsz, s, _ = x.shape
    assert s % (2 * TQ) == 0 and s // CMP_STRIDE == LANES
    n16 = s // CMP_STRIDE
    mod = _ada(c, w_ada, b_ada)
    w_mix, w_fin = _split_w_in(w_in)
    scale = A_HEAD_DIM ** -0.5 * LOG2E
    qg = jnp.tile(q_norm_gain, A_HEADS) * scale
    ksg = jnp.tile(k_norm_gain[1], A_KV_GROUPS)
    kwg = jnp.tile(k_norm_gain[2], A_KV_GROUPS)
    vec = lambda t: t.reshape(1, -1)
    rwkv_params = (vec(shift_mu), vec(w0), w_lora_up.astype(BF16), vec(a0), a_lora_up.astype(BF16),
                   vec(k_k), vec(k_a), vec(r_k))
    q_t, ks, vs_t, kw, vw_t, gates_t, ck, cols_fin, scan_bf16, scan_f32 = _proj(
        x, mod, norm_gain, w_mix, w_fin, qg, ksg, kwg, rwkv_params)

    kc, vc_t = _compress(ck, _expand_cmp_pos(cmp_pos_k), _expand_cmp_pos(cmp_pos_v),
                         _expand_cmp_w1(cmp_k_w1), cmp_k_w2.astype(BF16),
                         _expand_cmp_w1(cmp_v_w1), cmp_v_w2.T.astype(BF16),
                         k_norm_gain[0].reshape(1, A_HEAD_DIM))
    bias_c, bias_d = _bias_tables(rel_bias, s, n16)
    y_a = _attention(q_t, kc, vc_t, ks, vs_t, kw, vw_t, bias_c, bias_d, gates_t)

    y_b = _rwkv(scan_bf16, scan_f32, vec(ln_x_w), vec(ln_x_b))

    gate = mod[:, 2 * D_MODEL:].reshape(bsz, 1, D_MODEL)
    return _final(x, y_a, y_b, cols_fin, gate, w_out_a.astype(BF16), w_out_b.astype(BF16), w_o.astype(BF16))


def kernel(x, c, w_ada, b_ada, norm_gain, w_in, q_norm_gain, k_norm_gain, cmp_pos_k, cmp_pos_v, cmp_k_w1, cmp_k_w2, cmp_v_w1, cmp_v_w2, rel_bias, shift_mu, w0, w_lora_up, a0, a_lora_up, k_k, k_a, r_k, ln_x_w, ln_x_b, w_out_a, w_out_b, w_o):
    for l in range(w_in.shape[0]):
        x = _layer(x, c, rel_bias, w_ada[l], b_ada[l], norm_gain[l], w_in[l], q_norm_gain[l], k_norm_gain[l],
                   cmp_pos_k[l], cmp_pos_v[l], cmp_k_w1[l], cmp_k_w2[l], cmp_v_w1[l], cmp_v_w2[l],
                   shift_mu[l], w0[l], w_lora_up[l], a0[l], a_lora_up[l], k_k[l], k_a[l], r_k[l],
                   ln_x_w[l], ln_x_b[l], w_out_a[l], w_out_b[l], w_o[l])
    return x
```

```python
import functools
import math

import numpy as np
import jax
import jax.numpy as jnp
from jax import lax
from jax.experimental import pallas as pl
from jax.experimental.pallas import tpu as pltpu

F32 = jnp.float32
BF16 = jnp.bfloat16

D_MODEL = 1024
A_HEADS = 8
A_HEAD_DIM = 64
A_KV_GROUPS = 2
A_HPG = A_HEADS // A_KV_GROUPS
A_WIDTH = A_HEADS * A_HEAD_DIM
A_KV_WIDTH = A_KV_GROUPS * A_HEAD_DIM
CMP_BLOCK = 32
CMP_STRIDE = 16
CMP_HIDDEN = 256
SLC_BLOCK = 64
SLC_TOPN = 16
WINDOW = 512
B_HEADS = 8
B_HEAD_DIM = 64
B_WIDTH = B_HEADS * B_HEAD_DIM
DECAY_LORA = 64
ICLR_LORA = 64
LNX_EPS = 64e-5
REL_BUCKETS = 32
REL_MAX_EXACT = 16
REL_MAX_DIST = 128
NORM_EPS = 1e-6
NEG_INF = -1e30
FORCE_SCORE = 1e30

LANES = 128
TQ = 128
CHUNK = 64
GATE_PAD = LANES
LOG2E = math.log2(math.e)
V_ROWS = A_HEAD_DIM + 16
GATE_ROWS = 16
NSA_COLS = A_WIDTH + 6 * A_KV_WIDTH + GATE_PAD
FIN_COLS = A_WIDTH + B_WIDTH + 2 * D_MODEL
RWKV_COLS = 3 * B_WIDTH + DECAY_LORA + ICLR_LORA
SCAN_BF16_COLS = 5 * B_WIDTH
SCAN_F32_COLS = 2 * B_WIDTH
VMEM_LIMIT = 56 * 1024 * 1024


def _dot(a, b):
    return jnp.dot(a.astype(BF16), b.astype(BF16), preferred_element_type=F32)


def _dot_nt(a, b):
    return lax.dot_general(a.astype(BF16), b.astype(BF16), (((1,), (1,)), ((), ())),
                           preferred_element_type=F32)


def _dot_tn(a, b):
    return lax.dot_general(a.astype(BF16), b.astype(BF16), (((0,), (0,)), ((), ())),
                           preferred_element_type=F32)


def _split2(x):
    hi = x.astype(BF16)
    lo = (x - hi.astype(F32)).astype(BF16)
    return hi, lo


def _split3(x):
    h1 = x.astype(BF16)
    r1 = x - h1.astype(F32)
    h2 = r1.astype(BF16)
    h3 = (r1 - h2.astype(F32)).astype(BF16)
    return h1, h2, h3


def _sigmoid(x):
    return 1.0 / (1.0 + jnp.exp(-x))


def _bucket_thresholds():
    n = np.arange(0, 4096)
    nf = np.maximum(n, REL_MAX_EXACT).astype(np.float64)
    val = np.log(nf / REL_MAX_EXACT) / math.log(REL_MAX_DIST / REL_MAX_EXACT) * (REL_BUCKETS - REL_MAX_EXACT)
    frac = np.abs(val - np.round(val))
    assert np.all((frac > 1e-4) | (n <= REL_MAX_EXACT) | (n >= REL_MAX_DIST))
    large = REL_MAX_EXACT + np.floor(val + 1e-9).astype(np.int64)
    bucket = np.where(n < REL_MAX_EXACT, n, np.minimum(large, REL_BUCKETS - 1))
    return [int(np.argmax(bucket >= j)) for j in range(REL_BUCKETS)]


_BUCKET_TH = _bucket_thresholds()


def _bias_from_dist(dist, tbl_ref, head):
    val = jnp.full(dist.shape, tbl_ref[0, head], F32)
    for j in range(1, REL_BUCKETS):
        val = jnp.where(dist >= _BUCKET_TH[j], tbl_ref[j, head], val)
    return val


def _ada_kernel(c_ref, w_ref, b_ref, o_ref):
    c = c_ref[...]
    o_ref[...] = _dot(c * _sigmoid(c), w_ref[...]) + b_ref[...]


def _ada(c, w_ada, b_ada):
    bsz = c.shape[0]
    return pl.pallas_call(
        _ada_kernel,
        grid=(3,),
        in_specs=[pl.BlockSpec((bsz, D_MODEL), lambda j: (0, 0)),
                  pl.BlockSpec((D_MODEL, D_MODEL), lambda j: (0, j)),
                  pl.BlockSpec((1, D_MODEL), lambda j: (0, j))],
        out_specs=pl.BlockSpec((bsz, D_MODEL), lambda j: (0, j)),
        out_shape=jax.ShapeDtypeStruct((bsz, 3 * D_MODEL), F32),
        name="ada",
    )(c, w_ada, b_ada.reshape(1, 3 * D_MODEL))


def _norm_rows(x_t, gain_col, n_seg):
    out = []
    for seg in range(n_seg):
        blk = x_t[seg * A_HEAD_DIM:(seg + 1) * A_HEAD_DIM, :]
        ms = jnp.mean(blk * blk, axis=0, keepdims=True)
        out.append(blk * lax.rsqrt(ms + NORM_EPS) * gain_col[seg * A_HEAD_DIM:(seg + 1) * A_HEAD_DIM, :])
    return out


def _head_sum(x):
    left = lax.broadcasted_iota(jnp.int32, x.shape, x.ndim - 1) < B_HEAD_DIM
    lo = jnp.sum(jnp.where(left, x, 0.0), axis=-1, keepdims=True)
    hi = jnp.sum(jnp.where(left, 0.0, x), axis=-1, keepdims=True)
    return jnp.where(left, lo, hi)


def _proj_kernel(x_ref, mod_ref, g_ref, wm_ref, wf_ref, qg_ref, ksg_ref, kwg_ref,
                 mu_ref, w0_ref, wl_ref, a0_ref, al_ref, kk_ref, ka_ref, rk_ref,
                 q_ref, ks_ref, vs_ref, kw_ref, vw_ref, gt_ref, ck_ref, of_ref, rw_ref, ld_ref, prev_ref):
    tm = x_ref.shape[1]

    @pl.when(pl.program_id(1) == 0)
    def _():
        prev_ref[...] = jnp.zeros(prev_ref.shape, F32)

    x = x_ref[0]
    ms = jnp.mean(x * x, axis=-1, keepdims=True)
    y = x * lax.rsqrt(ms + NORM_EPS) * g_ref[...]
    mod = mod_ref[0]
    h = (y * (1.0 + mod[:, D_MODEL:2 * D_MODEL]) + mod[:, :D_MODEL]).astype(BF16)
    crn = jnp.dot(h, wm_ref[...], preferred_element_type=F32)
    cr = crn[:, 0:RWKV_COLS]
    cn = crn[:, RWKV_COLS:RWKV_COLS + NSA_COLS]
    ck_ref[0] = cn[:, A_WIDTH:A_WIDTH + 2 * A_KV_WIDTH]

    rolled = pltpu.roll(cr, 1, axis=0)
    row8 = lax.broadcasted_iota(jnp.int32, (8, RWKV_COLS), 0)
    prev = jnp.concatenate([jnp.where(row8 == 0, prev_ref[0:1, :], rolled[0:8]), rolled[8:]], axis=0)
    prev_ref[0:1, :] = cr[tm - 1:tm, :]
    xs = cr + (prev - cr) * mu_ref[...]
    r = xs[:, 0:B_WIDTH]
    k = xs[:, B_WIDTH:2 * B_WIDTH]
    v = xs[:, 2 * B_WIDTH:3 * B_WIDTH]
    wd = xs[:, 3 * B_WIDTH:3 * B_WIDTH + DECAY_LORA]
    ad = xs[:, 3 * B_WIDTH + DECAY_LORA:3 * B_WIDTH + DECAY_LORA + ICLR_LORA]
    ld = -math.exp(-0.5) * _sigmoid(w0_ref[...] + _dot(jnp.tanh(wd), wl_ref[...]))
    a = _sigmoid(a0_ref[...] + _dot(ad, al_ref[...]))
    kk = k * kk_ref[...]
    k_mod = k * (1.0 + (a - 1.0) * ka_ref[...])
    rkr = r * k_mod * rk_ref[...]
    ld_ref[0, :, 0:B_WIDTH] = ld
    rw_ref[0, :, 0:B_WIDTH] = r.astype(BF16)
    rw_ref[0, :, 3 * B_WIDTH:4 * B_WIDTH] = k_mod.astype(BF16)
    rw_ref[0, :, 4 * B_WIDTH:5 * B_WIDTH] = v.astype(BF16)
    for pr in range(B_WIDTH // LANES):
        sl = slice(pr * LANES, (pr + 1) * LANES)
        kk_p = kk[:, sl]
        kk_p = kk_p * lax.rsqrt(jnp.maximum(_head_sum(kk_p * kk_p), 1e-24))
        rw_ref[0, :, B_WIDTH + pr * LANES:B_WIDTH + (pr + 1) * LANES] = kk_p.astype(BF16)
        rw_ref[0, :, 2 * B_WIDTH + pr * LANES:2 * B_WIDTH + (pr + 1) * LANES] = (kk_p * a[:, sl]).astype(BF16)
        ld_ref[0, :, B_WIDTH + pr * LANES:B_WIDTH + (pr + 1) * LANES] = _head_sum(rkr[:, sl])

    lane = lax.broadcasted_iota(jnp.int32, (TQ, LANES), 1)
    row = lax.broadcasted_iota(jnp.int32, (TQ, LANES), 0)
    ones_rows = (lax.broadcasted_iota(jnp.int32, (V_ROWS - A_HEAD_DIM, TQ), 0) == 0).astype(BF16)
    off = A_WIDTH + 2 * A_KV_WIDTH
    for sub in range(tm // TQ):
        c = cn[sub * TQ:(sub + 1) * TQ]
        q_heads = _norm_rows(c[:, 0:A_WIDTH].T, qg_ref[...], A_HEADS)
        ks_t = jnp.concatenate(_norm_rows(c[:, off:off + A_KV_WIDTH].T, ksg_ref[...], A_KV_GROUPS), axis=0)
        kw_t = jnp.concatenate(_norm_rows(c[:, off + 2 * A_KV_WIDTH:off + 3 * A_KV_WIDTH].T, kwg_ref[...],
                                          A_KV_GROUPS), axis=0)
        ksn = ks_t.T
        kwn = kw_t.T
        vs_t = c[:, off + A_KV_WIDTH:off + 2 * A_KV_WIDTH].T.astype(BF16)
        vw_t = c[:, off + 3 * A_KV_WIDTH:off + 4 * A_KV_WIDTH].T.astype(BF16)
        gates_t = _sigmoid(c[:, off + 4 * A_KV_WIDTH:off + 5 * A_KV_WIDTH]).T
        blk = (pl.program_id(1) * tm + sub * TQ + row) // SLC_BLOCK
        onehot = jnp.where(lane - A_HEAD_DIM == blk, 1.0, 0.0)
        for g in range(A_KV_GROUPS):
            q_ref[0, g, sub] = jnp.concatenate(q_heads[g * A_HPG:(g + 1) * A_HPG], axis=1).astype(BF16)
            sl = slice(g * A_HEAD_DIM, (g + 1) * A_HEAD_DIM)
            k_g = ksn if g == 0 else pltpu.roll(ksn, A_HEAD_DIM, axis=1)
            ks_ref[0, g, sub * TQ:(sub + 1) * TQ, :] = jnp.where(lane < A_HEAD_DIM, k_g, onehot).astype(BF16)
            kw_g = kwn if g == 0 else pltpu.roll(kwn, A_HEAD_DIM, axis=1)
            kw_ref[0, g, sub * TQ:(sub + 1) * TQ, :] = jnp.where(lane < A_HEAD_DIM, kw_g, 0.0).astype(BF16)
            vs_ref[0, g, sub] = jnp.concatenate([vs_t[sl, :], ones_rows], axis=0)
            vw_ref[0, g, sub] = jnp.concatenate([vw_t[sl, :], ones_rows], axis=0)
            gt_ref[0, g, sub] = gates_t[g * A_HEAD_DIM:g * A_HEAD_DIM + GATE_ROWS, :]
    of_ref[0] = jnp.dot(h, wf_ref[...], preferred_element_type=F32).astype(BF16)


def _proj(x, mod, norm_gain, w_mix, w_fin, qg, ksg, kwg, rwkv_params, tm=512):
    bsz, s, _ = x.shape
    nt, nsub = s // TQ, tm // TQ
    assert A_HEAD_DIM + s // SLC_BLOCK <= LANES and A_KV_WIDTH == LANES and 2 * B_HEAD_DIM == LANES
    const = lambda b, i: (0, 0)
    weight = lambda cols: pl.BlockSpec((D_MODEL, cols), const, pipeline_mode=pl.Buffered(1))
    whole = lambda t: pl.BlockSpec(t.shape, const)
    col = lambda t: jnp.broadcast_to(t.reshape(-1, 1), (t.size, LANES))
    k_spec = lambda width: pl.BlockSpec((1, A_KV_GROUPS, tm, width), lambda b, i: (b, 0, i, 0))
    k_shape = lambda width: jax.ShapeDtypeStruct((bsz, A_KV_GROUPS, s, width), BF16)
    tile_spec = lambda r, c: pl.BlockSpec((1, A_KV_GROUPS, nsub, r, c), lambda b, i: (b, 0, i, 0, 0))
    tile_shape = lambda r, c, dt: jax.ShapeDtypeStruct((bsz, A_KV_GROUPS, nt, r, c), dt)
    return pl.pallas_call(
        _proj_kernel,
        grid=(bsz, s // tm),
        in_specs=[pl.BlockSpec((1, tm, D_MODEL), lambda b, i: (b, i, 0)),
                  pl.BlockSpec((1, 1, 3 * D_MODEL), lambda b, i: (b, 0, 0)),
                  pl.BlockSpec((1, D_MODEL), const),
                  weight(RWKV_COLS + NSA_COLS), weight(FIN_COLS),
                  pl.BlockSpec((A_WIDTH, LANES), const),
                  pl.BlockSpec((A_KV_WIDTH, LANES), const),
                  pl.BlockSpec((A_KV_WIDTH, LANES), const)] + [whole(t) for t in rwkv_params],
        out_specs=[tile_spec(A_HEAD_DIM, A_HPG * TQ),
                   k_spec(LANES), tile_spec(V_ROWS, TQ), k_spec(LANES), tile_spec(V_ROWS, TQ),
                   tile_spec(GATE_ROWS, TQ),
                   pl.BlockSpec((1, tm, 2 * A_KV_WIDTH), lambda b, i: (b, i, 0)),
                   pl.BlockSpec((1, tm, FIN_COLS), lambda b, i: (b, i, 0)),
                   pl.BlockSpec((1, tm, SCAN_BF16_COLS), lambda b, i: (b, i, 0)),
                   pl.BlockSpec((1, tm, SCAN_F32_COLS), lambda b, i: (b, i, 0))],
        out_shape=[tile_shape(A_HEAD_DIM, A_HPG * TQ, BF16),
                   k_shape(LANES), tile_shape(V_ROWS, TQ, BF16), k_shape(LANES), tile_shape(V_ROWS, TQ, BF16),
                   tile_shape(GATE_ROWS, TQ, F32),
                   jax.ShapeDtypeStruct((bsz, s, 2 * A_KV_WIDTH), F32),
                   jax.ShapeDtypeStruct((bsz, s, FIN_COLS), BF16),
                   jax.ShapeDtypeStruct((bsz, s, SCAN_BF16_COLS), BF16),
                   jax.ShapeDtypeStruct((bsz, s, SCAN_F32_COLS), F32)],
        scratch_shapes=[pltpu.VMEM((8, RWKV_COLS), F32)],
        compiler_params=pltpu.CompilerParams(dimension_semantics=("parallel", "arbitrary"),
                                             vmem_limit_bytes=VMEM_LIMIT),
        name="proj",
    )(x, mod.reshape(bsz, 1, 3 * D_MODEL), norm_gain.reshape(1, D_MODEL), w_mix, w_fin,
      col(qg), col(ksg), col(kwg), *rwkv_params)


def _compress_kernel(ck_ref, cv_ref, pk_ref, pv_ref, w1k_ref, w2k_ref, w1v_ref, w2v_ref, kg_ref, kc_ref, vc_ref):
    nb, n16 = ck_ref.shape[0], ck_ref.shape[1] // CMP_STRIDE

    def rows16(ref):
        return jnp.concatenate(
            [jnp.concatenate([ref[b, pl.ds(p, n16, stride=CMP_STRIDE), :] for p in range(CMP_STRIDE)], axis=1)
             for b in range(nb)], axis=0)

    def hidden(z, pos_ref, w1_ref, g):
        top = _dot(z + pos_ref[0:1, :], w1_ref[g, 0])
        bot = _dot(z + pos_ref[1:2, :], w1_ref[g, 1])
        bot = jnp.concatenate([pltpu.roll(bot[b * n16:(b + 1) * n16], n16 - 1, axis=0) for b in range(nb)], axis=0)
        return jax.nn.gelu(top + bot, approximate=True)

    zk = rows16(ck_ref)
    zv = rows16(cv_ref)
    for g in range(A_KV_GROUPS):
        kc = _dot(hidden(zk, pk_ref, w1k_ref, g), w2k_ref[...])
        ms = jnp.mean(kc * kc, axis=-1, keepdims=True)
        kc = (kc * lax.rsqrt(ms + NORM_EPS) * kg_ref[...]).astype(BF16)
        vc_t = _dot_nt(w2v_ref[...], hidden(zv, pv_ref, w1v_ref, g)).astype(BF16)
        for b in range(nb):
            kc_ref[b, g] = kc[b * n16:(b + 1) * n16]
            vc_ref[b, g] = vc_t[:, b * n16:(b + 1) * n16]


CMP_NB = 4


def _expand_cmp_w1(w1):
    w = w1.reshape(2, CMP_STRIDE, 1, A_HEAD_DIM, CMP_HIDDEN)
    per_group = []
    for g in range(A_KV_GROUPS):
        pad = [(0, 0), (0, 0), (g, A_KV_GROUPS - 1 - g), (0, 0), (0, 0)]
        per_group.append(jnp.pad(w, pad).reshape(2, CMP_STRIDE * A_KV_WIDTH, CMP_HIDDEN))
    return jnp.stack(per_group).astype(BF16)


def _expand_cmp_pos(pos):
    p = jnp.broadcast_to(pos.reshape(2, CMP_STRIDE, 1, A_HEAD_DIM), (2, CMP_STRIDE, A_KV_GROUPS, A_HEAD_DIM))
    return p.reshape(2, CMP_STRIDE * A_KV_WIDTH)


def _compress(ck, pk, pv, w1k, w2k, w1v, w2v_t, kg):
    bsz, s, _ = ck.shape
    n16 = s // CMP_STRIDE
    zw = CMP_STRIDE * A_KV_WIDTH
    const = lambda b: (0, 0)
    const4 = lambda b: (0, 0, 0, 0)
    nb = CMP_NB if bsz % CMP_NB == 0 else 1
    return pl.pallas_call(
        _compress_kernel,
        grid=(bsz // nb,),
        in_specs=[pl.BlockSpec((nb, s, A_KV_WIDTH), lambda b: (b, 0, 0)),
                  pl.BlockSpec((nb, s, A_KV_WIDTH), lambda b: (b, 0, 1)),
                  pl.BlockSpec((2, zw), const), pl.BlockSpec((2, zw), const),
                  pl.BlockSpec((A_KV_GROUPS, 2, zw, CMP_HIDDEN), const4), pl.BlockSpec((CMP_HIDDEN, A_HEAD_DIM), const),
                  pl.BlockSpec((A_KV_GROUPS, 2, zw, CMP_HIDDEN), const4), pl.BlockSpec((A_HEAD_DIM, CMP_HIDDEN), const),
                  pl.BlockSpec((1, A_HEAD_DIM), const)],
        out_specs=[pl.BlockSpec((nb, A_KV_GROUPS, n16, A_HEAD_DIM), lambda b: (b, 0, 0, 0)),
                   pl.BlockSpec((nb, A_KV_GROUPS, A_HEAD_DIM, n16), lambda b: (b, 0, 0, 0))],
        out_shape=[jax.ShapeDtypeStruct((bsz, A_KV_GROUPS, n16, A_HEAD_DIM), BF16),
                   jax.ShapeDtypeStruct((bsz, A_KV_GROUPS, A_HEAD_DIM, n16), BF16)],
        compiler_params=pltpu.CompilerParams(dimension_semantics=("parallel",)),
        name="compress",
    )(ck, ck, pk, pv, w1k, w2k, w1v, w2v_t, kg)


TILE_FAR, TILE_EDGE, N_BIAS_TILES = 2, 3, 4
SUB = 4
CMP_WIN = 16
ATT_CALLS = 1
ATT_TILES = 16


def _bias_cmp_kernel(tbl_ref, o_ref):
    i = pl.program_id(0)
    g = pl.program_id(1)
    n_cmp = o_ref.shape[2]
    first_n = pl.multiple_of(jnp.maximum(i * (TQ // CMP_STRIDE) - 8, 0), 8)
    n = lax.broadcasted_iota(jnp.int32, (n_cmp, TQ), 0)
    n_win = first_n + lax.broadcasted_iota(jnp.int32, (CMP_WIN, TQ), 0)
    q = lax.broadcasted_iota(jnp.int32, (CMP_WIN, TQ), 1)
    dist = i * TQ + q - (n_win * CMP_STRIDE + CMP_BLOCK - 1)
    for h in range(A_HPG):
        head = g * A_HPG + h
        o_ref[0, 0, :, h * TQ:(h + 1) * TQ] = jnp.where(n < first_n, tbl_ref[REL_BUCKETS - 1, head] * LOG2E, NEG_INF)
        bias = _bias_from_dist(dist, tbl_ref, head)
        o_ref[0, 0, pl.ds(first_n, CMP_WIN), h * TQ:(h + 1) * TQ] = jnp.where(dist >= 0, bias * LOG2E, NEG_INF)


def _bias_toeplitz_kernel(tbl_ref, o_ref):
    g = pl.program_id(0)
    r = pl.program_id(1)
    off = jnp.where(r == TILE_EDGE, WINDOW // TQ, r)
    k = lax.broadcasted_iota(jnp.int32, (TQ, TQ), 0)
    q = lax.broadcasted_iota(jnp.int32, (TQ, TQ), 1)
    dist = off * TQ + q - k
    valid = (dist >= 0) & (dist < WINDOW)
    for h in range(A_HPG):
        head = g * A_HPG + h
        bias = _bias_from_dist(dist, tbl_ref, head) - tbl_ref[REL_BUCKETS - 1, head]
        o_ref[0, 0, :, h * TQ:(h + 1) * TQ] = jnp.where(valid, bias * LOG2E, NEG_INF)


def _bias_tables(rel_bias, s, n_cmp):
    smem = pl.BlockSpec(memory_space=pltpu.SMEM)
    nt = s // TQ
    bias_c = pl.pallas_call(
        _bias_cmp_kernel,
        grid=(nt, A_KV_GROUPS),
        in_specs=[smem],
        out_specs=pl.BlockSpec((1, 1, n_cmp, A_HPG * TQ), lambda i, g: (i, g, 0, 0)),
        out_shape=jax.ShapeDtypeStruct((nt, A_KV_GROUPS, n_cmp, A_HPG * TQ), F32),
        name="bias_cmp",
    )(rel_bias)
    assert _BUCKET_TH[REL_BUCKETS - 1] <= TQ + 1 and WINDOW // TQ >= 3
    assert _BUCKET_TH[REL_BUCKETS - 1] <= 9 * CMP_STRIDE - CMP_BLOCK + 1
    assert (CMP_WIN - 8) * CMP_STRIDE + CMP_BLOCK >= TQ and n_cmp >= CMP_WIN
    bias_d = pl.pallas_call(
        _bias_toeplitz_kernel,
        grid=(A_KV_GROUPS, N_BIAS_TILES),
        in_specs=[smem],
        out_specs=pl.BlockSpec((1, 1, TQ, A_HPG * TQ), lambda g, r: (g, r, 0, 0)),
        out_shape=jax.ShapeDtypeStruct((A_KV_GROUPS, N_BIAS_TILES, TQ, A_HPG * TQ), F32),
        name="bias_toeplitz",
    )(rel_bias)
    return bias_c, bias_d


def _attn_kernel(*refs, j):
    q_refs, refs = refs[:ATT_TILES], refs[ATT_TILES:]
    kc_ref, vc_ref, ks_ref, vs_ref, kw_ref, vw_ref = refs[:6]
    bc_refs, bd_ref, gt_refs, o_ref = refs[6:6 + ATT_TILES], refs[6 + ATT_TILES], refs[7 + ATT_TILES:-1], refs[-1]
    tq = TQ
    n_cmp = kc_ref.shape[2]
    n_slc = ks_ref.shape[2] // SLC_BLOCK
    wt = WINDOW // tq
    dh = A_HEAD_DIM
    round8 = lambda n: -(-n // 8) * 8
    tiles = [dict(i=j + t * ATT_CALLS, last=(j + t * ATT_CALLS) // SUB, n_tok=(j + t * ATT_CALLS + 1) * tq,
                  q=q_refs[t][0, 0, 0], bias_c=bc_refs[t], gates=gt_refs[t][0, 0, 0]) for t in range(ATT_TILES)]
    zero_rows = jnp.zeros((LANES - dh, A_HPG * tq), BF16)

    def scores(k_ref, q_mat, i, first_tile, n_sub, tile_index):
        s = jnp.dot(k_ref[0, 0, first_tile * tq:(first_tile + n_sub) * tq, :], q_mat,
                    preferred_element_type=F32)
        idx = [tile_index(i - (first_tile + t)) for t in range(n_sub)]
        parts = [s[t * tq:(t + 1) * tq] if idx[t] == TILE_FAR else s[t * tq:(t + 1) * tq] + bd_ref[0, idx[t]]
                 for t in range(n_sub)]
        return jnp.concatenate(parts, axis=0)

    def values_t(v_ref, first_tile, n_sub):
        return jnp.concatenate([v_ref[0, 0, first_tile + t] for t in range(n_sub)], axis=1)

    win_tile = lambda r: TILE_EDGE if r == wt else min(r, TILE_FAR)
    for t in tiles:
        t["first_w"] = max(t["i"] - wt, 0)
        t["n_w"] = t["i"] - t["first_w"] + 1
        q_pad = jnp.concatenate([t["q"], zero_rows], axis=0)
        t["s_w"] = scores(kw_ref, q_pad, t["i"], t["first_w"], t["n_w"], win_tile)
    for t in tiles:
        t["n_cmp"] = min(n_cmp, round8(t["n_tok"] // CMP_STRIDE))
        t["n_slc"] = min(n_slc, round8(t["n_tok"] // SLC_BLOCK))
        bias = t["bias_c"][0, 0, 0:t["n_cmp"], :]
        t["valid_c"] = bias > 0.5 * NEG_INF
        t["s_c"] = jnp.dot(kc_ref[0, 0, 0:t["n_cmp"], :], t["q"], preferred_element_type=F32) + bias

    r1, r2 = SLC_BLOCK // CMP_STRIDE, CMP_BLOCK // CMP_STRIDE
    jj = lax.broadcasted_iota(jnp.int32, (n_slc, n_cmp), 0)
    nn = lax.broadcasted_iota(jnp.int32, (n_slc, n_cmp), 1)
    d = nn - r1 * jj
    cnt = jnp.zeros((n_slc, n_cmp), F32)
    for a in range(r1):
        for c in range(r2):
            cnt = cnt + jnp.where(d == a - c, 1.0, 0.0)
    cnt = cnt.astype(BF16)
    for t in tiles:
        s = t["s_c"]
        e = jnp.where(t["valid_c"], jnp.exp2(s - jnp.max(s, axis=0, keepdims=True)), 0.0)
        l = jnp.sum(e, axis=0, keepdims=True)
        p = e * (1.0 / jnp.where(l > 0.0, l, 1.0))
        rest = n_cmp - t["n_cmp"]
        pad = (lambda a: jnp.concatenate([a, jnp.zeros((rest, a.shape[1]), a.dtype)], axis=0)) if rest else (lambda a: a)
        t["out_c"] = jnp.dot(vc_ref[0, 0], pad(p).astype(BF16), preferred_element_type=F32)
        p_grp = pad(sum(p[:, h * tq:(h + 1) * tq] for h in range(A_HPG)))
        t["imp"] = sum(jnp.dot(cnt[0:t["n_slc"]], part, preferred_element_type=F32)
                       for part in _split3(p_grp))

    for t in tiles:
        s = t["s_w"]
        t["p_w"] = jnp.exp2(s - jnp.max(s, axis=0, keepdims=True)).astype(BF16)
    for t in tiles:
        acc = jnp.dot(values_t(vw_ref, t["first_w"], t["n_w"]), t["p_w"], preferred_element_type=F32)
        out_w = acc[:dh] * (1.0 / acc[dh:dh + 1])
        gates = t["gates"]
        t["part"] = [gates[h:h + 1, :] * t["out_c"][:, h * tq:(h + 1) * tq]
                     + gates[2 * A_HPG + h:2 * A_HPG + h + 1, :] * out_w[:, h * tq:(h + 1) * tq]
                     for h in range(A_HPG)]

    for t in tiles:
        nb = t["n_slc"]
        blk = lax.broadcasted_iota(jnp.int32, (nb, tq), 0)
        tpos = t["i"] * tq + lax.broadcasted_iota(jnp.int32, (nb, tq), 1)
        cur = tpos // SLC_BLOCK
        forced = (blk == 0) | (blk == cur) | (blk == cur - 1)
        causal = blk * SLC_BLOCK <= tpos
        imp = jnp.where(forced, FORCE_SCORE, jnp.where(causal, t["imp"], NEG_INF))
        rank = jnp.zeros((nb, tq), F32)
        for c in range(nb):
            row = imp[c:c + 1, :]
            ahead = (row > imp) | ((row == imp) & (blk > c))
            rank = rank + jnp.where(ahead, 1.0, 0.0)
        pen = jnp.where(rank < float(min(SLC_TOPN, n_slc)), 0.0, -FORCE_SCORE)
        pen = jnp.concatenate([pen, jnp.zeros((LANES - dh - nb, tq), F32)], axis=0)
        t["q_aug"] = jnp.concatenate([t["q"], jnp.concatenate([pen] * A_HPG, axis=1).astype(BF16)], axis=0)

    sel_tile = lambda r: min(r, TILE_FAR)

    jobs = [(t, c, min(SUB, t["i"] - c * SUB + 1)) for t in tiles for c in range(t["last"], -1, -1)]
    ss = [scores(ks_ref, t["q_aug"], t["i"], c * SUB, n, sel_tile) for t, c, n in jobs]
    ms = [jnp.max(s, axis=0, keepdims=True) for s in ss]
    ps = [jnp.exp2(s - m).astype(BF16) for s, m in zip(ss, ms)]
    accs = [jnp.dot(values_t(vs_ref, c * SUB, n), p, preferred_element_type=F32) for (t, c, n), p in zip(jobs, ps)]

    for k, t in enumerate(tiles):
        mine = [n for n, job in enumerate(jobs) if job[0] is t]
        m = functools.reduce(jnp.maximum, [ms[n] for n in mine])
        acc = sum(jnp.exp2(ms[n] - m) * accs[n] for n in mine)
        out_s = acc[:dh] * (1.0 / acc[dh:dh + 1])
        gates = t["gates"]
        blocks = [t["part"][h] + gates[A_HPG + h:A_HPG + h + 1, :] * out_s[:, h * tq:(h + 1) * tq]
                  for h in range(A_HPG)]
        o_ref[0, k] = jnp.concatenate(blocks, axis=0).T.astype(BF16)


def _attention(q_t, kc, vc_t, ks, vs_t, kw, vw_t, bias_c, bias_d, gates_t):
    bsz, _, nt, _, _ = q_t.shape
    s = ks.shape[2]
    n_cmp = kc.shape[2]
    assert nt == ATT_TILES * ATT_CALLS and WINDOW // TQ + 1 <= nt
    k_spec = pl.BlockSpec((1, 1, s, LANES), lambda b, g: (b, g, 0, 0))
    vt_spec = pl.BlockSpec((1, 1, nt, V_ROWS, TQ), lambda b, g: (b, g, 0, 0, 0))
    outs = []
    for j in range(ATT_CALLS):
        per_tile = lambda spec: [spec(j + t * ATT_CALLS) for t in range(ATT_TILES)]
        q_spec = lambda i: pl.BlockSpec((1, 1, 1, A_HEAD_DIM, A_HPG * TQ), lambda b, g: (b, g, i, 0, 0))
        bc_spec = lambda i: pl.BlockSpec((1, 1, n_cmp, A_HPG * TQ), lambda b, g: (i, g, 0, 0))
        gt_spec = lambda i: pl.BlockSpec((1, 1, 1, GATE_ROWS, TQ), lambda b, g: (b, g, i, 0, 0))
        outs.append(pl.pallas_call(
            functools.partial(_attn_kernel, j=j),
            grid=(bsz, A_KV_GROUPS),
            in_specs=(per_tile(q_spec)
                      + [pl.BlockSpec((1, 1, n_cmp, A_HEAD_DIM), lambda b, g: (b, g, 0, 0)),
                         pl.BlockSpec((1, 1, A_HEAD_DIM, n_cmp), lambda b, g: (b, g, 0, 0)),
                         k_spec, vt_spec, k_spec, vt_spec]
                      + per_tile(bc_spec)
                      + [pl.BlockSpec((1, N_BIAS_TILES, TQ, A_HPG * TQ), lambda b, g: (g, 0, 0, 0))]
                      + per_tile(gt_spec)),
            out_specs=pl.BlockSpec((1, ATT_TILES, TQ, A_HPG * A_HEAD_DIM), lambda b, g: (b, 0, 0, g)),
            out_shape=jax.ShapeDtypeStruct((bsz, ATT_TILES, TQ, A_WIDTH), BF16),
            compiler_params=pltpu.CompilerParams(dimension_semantics=("parallel", "parallel"),
                                                 vmem_limit_bytes=VMEM_LIMIT),
            name=f"attn{j}",
        )(*([q_t] * ATT_TILES), kc, vc_t, ks, vs_t, kw, vw_t, *([bias_c] * ATT_TILES), bias_d,
          *([gates_t] * ATT_TILES)))
    return outs


def _rwkv_kernel(rw_ref, ld_ref, lw_ref, lb_ref, o_ref, state_ref):
    cc = pl.program_id(1)
    n = B_HEAD_DIM
    nb, csz = rw_ref.shape[0], rw_ref.shape[1]

    @pl.when(cc == 0)
    def _():
        state_ref[...] = jnp.zeros(state_ref.shape, F32)

    ti = lax.broadcasted_iota(jnp.int32, (csz, LANES), 0)
    si = lax.broadcasted_iota(jnp.int32, (csz, LANES), 1) % n
    lower = si <= ti
    strict = si < ti
    eye = jnp.where(si == ti, 1.0, 0.0)
    tri = jnp.where(lax.broadcasted_iota(jnp.int32, (csz, csz), 1) <= lax.broadcasted_iota(jnp.int32, (csz, csz), 0),
                    1.0, 0.0).astype(BF16)
    n_pairs = B_WIDTH // LANES
    left =lax.broadcasted_iota(jnp.int32, (csz, LANES), 1) < n
    row_left = lax.broadcasted_iota(jnp.int32, (LANES, LANES), 0) < n
    same_head = row_left == (lax.broadcasted_iota(jnp.int32, (LANES, LANES), 1) < n)

    def blockdiag(y):
        zero = jnp.zeros_like(y)
        return jnp.concatenate([jnp.where(left, y, zero), jnp.where(left, zero, y)], axis=0)

    chains = []
    for bi in range(nb):
        r, kk, b, k_mod, v = (rw_ref[bi, :, m * B_WIDTH:(m + 1) * B_WIDTH].astype(F32) for m in range(5))
        ld = ld_ref[bi, :, 0:B_WIDTH]
        rsum = ld_ref[bi, :, B_WIDTH:2 * B_WIDTH]

        ld_hi, ld_lo = _split2(ld)
        cum = jnp.dot(tri, ld_hi, preferred_element_type=F32) + jnp.dot(tri, ld_lo, preferred_element_type=F32)
        g_inc = jnp.exp(cum)
        g_exc = jnp.exp(cum - ld)
        g_inv = jnp.exp(-cum)
        g_end = jnp.exp(cum[csz - 1:csz, :] - cum)
        g_all = g_inc[csz - 1:csz, :]

        for pr in range(n_pairs):
            sl = slice(pr * LANES, (pr + 1) * LANES)
            kk_p = kk[:, sl]
            b_p = b[:, sl]
            bt = (b_p * g_inv[:, sl]).astype(BF16)
            kt = (k_mod[:, sl] * g_inv[:, sl]).astype(BF16)
            ch = dict(
                idx=bi * n_pairs + pr,
                v=v[:, sl],
                lhs=jnp.concatenate([-kk_p * g_exc[:, sl], r[:, sl] * g_inc[:, sl]], axis=0).astype(BF16),
                rhs=jnp.concatenate([blockdiag(bt), blockdiag(kt)], axis=0),
                bk=jnp.concatenate([b_p * g_end[:, sl], k_mod[:, sl] * g_end[:, sl]], axis=0).astype(BF16),
                g_all=g_all[:, sl],
                bonus=rsum[:, sl] * v[:, sl],
            )
            chains.append(ch)

    for ch in chains:
        x = _dot_nt(ch["lhs"], ch["rhs"])
        xb, xk = x[:, :LANES], x[:, LANES:]
        ch["a_ab"] = jnp.where(strict, xb[:csz], 0.0)
        a_ak = jnp.where(strict, xk[:csz], 0.0)
        m_rk = jnp.where(lower, xk[csz:], 0.0)
        ch["ak_rk"] = jnp.concatenate([a_ak, m_rk], axis=0).astype(BF16)
        ch["m_rb"] = jnp.where(lower, xb[csz:], 0.0).astype(BF16)
    for ch in chains:
        akv = _dot(ch["ak_rk"], blockdiag(ch["v"].astype(BF16)))
        ch["akv"], ch["mrkv"] = akv[:csz], akv[csz:]
        ch["tinv"] = eye + ch["a_ab"]
        ch["pw"] = ch["a_ab"].astype(BF16)
    n_sq = int(math.log2(csz)) - 1
    for ch in chains:
        ch["pw"] = _dot(ch["pw"], blockdiag(ch["pw"])).astype(BF16)
    for step in range(n_sq):
        for ch in chains:
            if step + 1 < n_sq:
                both = _dot(jnp.concatenate([ch["pw"], ch["tinv"].astype(BF16)], axis=0), blockdiag(ch["pw"]))
                ch["tinv"] = ch["tinv"] + both[csz:]
                ch["pw"] = both[:csz].astype(BF16)
            else:
                ch["tinv"] = ch["tinv"] + _dot(ch["tinv"], blockdiag(ch["pw"]))
    for ch in chains:
        ch["s0"] = state_ref[ch["idx"]]
        ch["as0"] = _dot_nt(ch["lhs"], ch["s0"])
    for ch in chains:
        w = (ch["as0"][:csz] + ch["akv"]).astype(BF16)
        ch["u"] = _dot(ch["tinv"], blockdiag(w))
    outs = []
    for ch in chains:
        u = ch["u"]
        y = ch["as0"][csz:] + _dot(ch["m_rb"], blockdiag(u.astype(BF16))) + ch["mrkv"]
        uv = jnp.concatenate([u, ch["v"]], axis=0)
        state_ref[ch["idx"]] = ch["s0"] * ch["g_all"] + jnp.where(same_head, _dot_tn(uv, ch["bk"]), 0.0)
        yc = y - _head_sum(y) * (1.0 / n)
        var = _head_sum(yc * yc) * (1.0 / n)
        outs.append(yc * lax.rsqrt(var + LNX_EPS))
    for bi in range(nb):
        yn = jnp.concatenate(outs[bi * n_pairs:(bi + 1) * n_pairs], axis=-1)
        bonus = jnp.concatenate([ch["bonus"] for ch in chains[bi * n_pairs:(bi + 1) * n_pairs]], axis=-1)
        o_ref[bi] = (yn * lw_ref[...] + lb_ref[...] + bonus).astype(BF16)


RWKV_NB = 8


def _rwkv(scan_bf16, scan_f32, ln_w, ln_b):
    bsz, s, _ = scan_bf16.shape
    nb = RWKV_NB if bsz % RWKV_NB == 0 else 1
    const = lambda b, c: (0, 0)
    vec = pl.BlockSpec((1, B_WIDTH), const)
    return pl.pallas_call(
        _rwkv_kernel,
        grid=(bsz // nb, s // CHUNK),
        in_specs=[pl.BlockSpec((nb, CHUNK, SCAN_BF16_COLS), lambda b, c: (b, c, 0)),
                  pl.BlockSpec((nb, CHUNK, SCAN_F32_COLS), lambda b, c: (b, c, 0)),
                  vec, vec],
        out_specs=pl.BlockSpec((nb, CHUNK, B_WIDTH), lambda b, c: (b, c, 0)),
        out_shape=jax.ShapeDtypeStruct((bsz, s, B_WIDTH), BF16),
        scratch_shapes=[pltpu.VMEM((nb * B_WIDTH // LANES, LANES, LANES), F32)],
        compiler_params=pltpu.CompilerParams(dimension_semantics=("parallel", "arbitrary")),
        name="rwkv",
    )(scan_bf16, scan_f32, ln_w, ln_b)


def _final_kernel(x_ref, *refs):
    ya_refs, (yb_ref, cf_ref, gate_ref, wa_ref, wb_ref, wo_ref, o_ref) = refs[:ATT_CALLS], refs[ATT_CALLS:]
    a_silu = cf_ref[0, :, 0:A_WIDTH].astype(F32)
    b_silu = cf_ref[0, :, A_WIDTH:A_WIDTH + B_WIDTH].astype(F32)
    merge_a = cf_ref[0, :, A_WIDTH + B_WIDTH:A_WIDTH + B_WIDTH + D_MODEL].astype(F32)
    merge_b = cf_ref[0, :, A_WIDTH + B_WIDTH + D_MODEL:A_WIDTH + B_WIDTH + 2 * D_MODEL].astype(F32)
    n_tiles = x_ref.shape[1] // TQ
    y_a = jnp.concatenate([ya_refs[u % ATT_CALLS][0, u // ATT_CALLS] for u in range(n_tiles)], axis=0)
    ya = y_a.astype(F32) * (a_silu * _sigmoid(a_silu))
    yb = yb_ref[0].astype(F32) * (b_silu * _sigmoid(b_silu))
    gate_a = 1.0 / (1.0 + jnp.exp2(merge_a))
    gate_b = 1.0 / (1.0 + jnp.exp2(merge_b))
    merged = gate_a * _dot(ya, wa_ref[...]) + gate_b * _dot(yb, wb_ref[...])
    o_ref[0] = x_ref[0] + gate_ref[0] * _dot(merged, wo_ref[...])


def _final(x, y_a, y_b, cols_fin, gate, w_out_a, w_out_b, w_o):
    bsz, s, _ = x.shape
    tm = 4 * TQ
    assert (tm // TQ) % ATT_CALLS == 0
    const = lambda b, i: (0, 0)
    row = lambda w: pl.BlockSpec((1, tm, w), lambda b, i: (b, i, 0))
    ya_spec = pl.BlockSpec((1, tm // TQ // ATT_CALLS, TQ, A_WIDTH), lambda b, i: (b, i, 0, 0))
    return pl.pallas_call(
        _final_kernel,
        grid=(bsz, s // tm),
        in_specs=[row(D_MODEL)] + [ya_spec] * ATT_CALLS + [row(B_WIDTH), row(FIN_COLS),
                  pl.BlockSpec((1, 1, D_MODEL), lambda b, i: (b, 0, 0)),
                  pl.BlockSpec((A_WIDTH, D_MODEL), const),
                  pl.BlockSpec((B_WIDTH, D_MODEL), const),
                  pl.BlockSpec((D_MODEL, D_MODEL), const)],
        out_specs=row(D_MODEL),
        out_shape=jax.ShapeDtypeStruct((bsz, s, D_MODEL), F32),
        compiler_params=pltpu.CompilerParams(dimension_semantics=("parallel", "parallel"),
                                             vmem_limit_bytes=VMEM_LIMIT),
        name="final",
    )(x, *y_a, y_b, cols_fin, gate, w_out_a, w_out_b, w_o)


def _split_w_in(w_in):
    nsa_in = 2 * A_WIDTH + 6 * A_KV_WIDTH + 3 * A_HEADS
    o_gate = A_WIDTH + 6 * A_KV_WIDTH
    o_asilu = o_gate + 3 * A_HEADS
    o_shift = nsa_in
    o_rest = nsa_in + RWKV_COLS
    gate_w = w_in[:, o_gate:o_asilu].reshape(D_MODEL, 3, A_KV_GROUPS, A_HPG)
    gate_w = gate_w.transpose(0, 2, 1, 3).reshape(D_MODEL, A_KV_GROUPS, 3 * A_HPG)
    gate_w = jnp.pad(gate_w, ((0, 0), (0, 0), (0, A_HEAD_DIM - 3 * A_HPG))).reshape(D_MODEL, GATE_PAD)
    w_nsa = jnp.concatenate([w_in[:, :o_gate], gate_w], axis=1)
    w_fin = jnp.concatenate([w_in[:, o_asilu:o_shift], w_in[:, o_rest:o_rest + B_WIDTH],
                             w_in[:, o_rest + B_WIDTH:] * (-LOG2E)], axis=1)
    w_mix = jnp.concatenate([w_in[:, o_shift:o_rest], w_nsa], axis=1)
    return w_mix.astype(BF16), w_fin.astype(BF16)


def _layer(x, c, rel_bias, w_ada, b_ada, norm_gain, w_in, q_norm_gain, k_norm_gain,
           cmp_pos_k, cmp_pos_v, cmp_k_w1, cmp_k_w2, cmp_v_w1, cmp_v_w2,
           shift_mu, w0, w_lora_up, a0, a_lora_up, k_k, k_a, r_k, ln_x_w, ln_x_b,
           w_out_a, w_out_b, w_o):
    bsz, s, _ = x.shape
    assert s % (2 * TQ) == 0 and s // CMP_STRIDE == LANES
    n16 = s // CMP_STRIDE
    mod = _ada(c, w_ada, b_ada)
    w_mix, w_fin = _split_w_in(w_in)
    scale = A_HEAD_DIM ** -0.5 * LOG2E
    qg = jnp.tile(q_norm_gain, A_HEADS) * scale
    ksg = jnp.tile(k_norm_gain[1], A_KV_GROUPS)
    kwg = jnp.tile(k_norm_gain[2], A_KV_GROUPS)
    vec = lambda t: t.reshape(1, -1)
    rwkv_params = (vec(shift_mu), vec(w0), w_lora_up.astype(BF16), vec(a0), a_lora_up.astype(BF16),
                   vec(k_k), vec(k_a), vec(r_k))
    q_t, ks, vs_t, kw, vw_t, gates_t, ck, cols_fin, scan_bf16, scan_f32 = _proj(
        x, mod, norm_gain, w_mix, w_fin, qg, ksg, kwg, rwkv_params)

    kc, vc_t = _compress(ck, _expand_cmp_pos(cmp_pos_k), _expand_cmp_pos(cmp_pos_v),
                         _expand_cmp_w1(cmp_k_w1), cmp_k_w2.astype(BF16),
                         _expand_cmp_w1(cmp_v_w1), cmp_v_w2.T.astype(BF16),
                         k_norm_gain[0].reshape(1, A_HEAD_DIM))
    bias_c, bias_d = _bias_tables(rel_bias, s, n16)
    y_a = _attention(q_t, kc, vc_t, ks, vs_t, kw, vw_t, bias_c, bias_d, gates_t)

    y_b = _rwkv(scan_bf16, scan_f32, vec(ln_x_w), vec(ln_x_b))

    gate = mod[:, 2 * D_MODEL:].reshape(bsz, 1, D_MODEL)
    return _final(x, y_a, y_b, cols_fin, gate, w_out_a.astype(BF16), w_out_b.astype(BF16), w_o.astype(BF16))


def kernel(x, c, w_ada, b_ada, norm_gain, w_in, q_norm_gain, k_norm_gain, cmp_pos_k, cmp_pos_v, cmp_k_w1, cmp_k_w2, cmp_v_w1, cmp_v_w2, rel_bias, shift_mu, w0, w_lora_up, a0, a_lora_up, k_k, k_a, r_k, ln_x_w, ln_x_b, w_out_a, w_out_b, w_o):
    for l in range(w_in.shape[0]):
        x = _layer(x, c, rel_bias, w_ada[l], b_ada[l], norm_gain[l], w_in[l], q_norm_gain[l], k_norm_gain[l],
                   cmp_pos_k[l], cmp_pos_v[l], cmp_k_w1[l], cmp_k_w2[l], cmp_v_w1[l], cmp_v_w2[l],
                   shift_mu[l], w0[l], w_lora_up[l], a0[l], a_lora_up[l], k_k[l], k_a[l], r_k[l],
                   ln_x_w[l], ln_x_b[l], w_out_a[l], w_out_b[l], w_o[l])
    return x
```
